```python
import jax, jax.numpy as jnp
from jax import lax
import numpy as np

D_MODEL = 1024
BATCH = 4
SEQ = 4096
DEPTH = 1
DEC_BATCH = 32
DEC_SEQ = 32
PAST_LEN = 2048

CHUNK = 64
BAND_CHUNKS = 8
WINDOW_A = BAND_CHUNKS * CHUNK
H_A = 8
HD_A = 64
MAX_REL = 128
H_B = 4
DK_B = 64
DV_B = 128
GK_RANK = 16
GK_NORM = 16.0
GLA_BLOCK = 16
D_FF = 2816
CONV_W = 3
EPS = 1e-6
NEG_INF = -1e30

W_A = H_A * HD_A
W_BK = H_B * DK_B
W_BV = H_B * DV_B
IN_SIZES = (W_A, W_A, W_A, W_BK, W_BK, W_BV, W_BV, GK_RANK, D_MODEL, D_MODEL)
D_IN = 3 * W_A + 2 * W_BK + 2 * W_BV + GK_RANK + 2 * D_MODEL

kernel_name = 'hybrid_chunkband_gla_convffn_step'


def rmsnorm(x, g):
    xf = x.astype(jnp.float32)
    r = lax.rsqrt(jnp.mean(xf * xf, axis=-1, keepdims=True) + EPS)
    return (xf * r).astype(x.dtype) * g


def adaln_modulation(c, w_ada, b_ada):
    m = jax.nn.silu(c) @ w_ada + b_ada
    return jnp.split(m[:, None, :], 6, axis=-1)


def rel_bias_lookup(rel_bias, rel):
    return rel_bias[:, jnp.clip(rel, -MAX_REL, MAX_REL) + MAX_REL]


def band_attention_prompt(q, k, v, rel_bias):
    b, t, h, d = q.shape
    nc = t // CHUNK
    nband = BAND_CHUNKS + 1
    qc = q.reshape(b, nc, CHUNK, h, d)
    pad_k = jnp.zeros((b, BAND_CHUNKS, CHUNK, h, d), k.dtype)
    pad_v = jnp.zeros((b, BAND_CHUNKS, CHUNK, h, d), v.dtype)
    kc = jnp.concatenate([pad_k, k.reshape(b, nc, CHUNK, h, d)], axis=1)
    vc = jnp.concatenate([pad_v, v.reshape(b, nc, CHUNK, h, d)], axis=1)
    idx = jnp.arange(nc)[:, None] + jnp.arange(nband)[None, :]
    kb = kc[:, idx].reshape(b, nc, nband * CHUNK, h, d)
    vb = vc[:, idx].reshape(b, nc, nband * CHUNK, h, d)
    valid = jnp.repeat(idx >= BAND_CHUNKS, CHUNK, axis=1)
    qi = jnp.arange(CHUNK)
    kr = jnp.arange(nband * CHUNK) - BAND_CHUNKS * CHUNK
    bias = rel_bias_lookup(rel_bias, qi[:, None] - kr[None, :]).astype(jnp.float32)
    s = jnp.einsum('bcqhd,bckhd->bchqk', qc, kb).astype(jnp.float32) * (d ** -0.5) + bias
    s = jnp.where(valid[None, :, None, None, :], s, NEG_INF)
    p = jax.nn.softmax(s, axis=-1).astype(v.dtype)
    o = jnp.einsum('bchqk,bckhd->bcqhd', p, vb)
    return o.reshape(b, t, h * d)


def band_attention_sample(q, k_new, v_new, k_cache, v_cache, rel_bias):
    b, s_len, h, d = q.shape
    w = k_cache.shape[1]
    kk = jnp.concatenate([k_cache, k_new], axis=1)
    vv = jnp.concatenate([v_cache, v_new], axis=1)
    qpos = PAST_LEN + jnp.arange(s_len)
    kpos = jnp.concatenate([PAST_LEN - w + jnp.arange(w), qpos])
    bias = rel_bias_lookup(rel_bias, qpos[:, None] - kpos[None, :]).astype(jnp.float32)
    s = jnp.einsum('bqhd,bkhd->bhqk', q, kk).astype(jnp.float32) * (d ** -0.5) + bias
    p = jax.nn.softmax(s, axis=-1).astype(vv.dtype)
    o = jnp.einsum('bhqk,bkhd->bqhd', p, vv)
    return o.reshape(b, s_len, h * d)


def gla(q, k, v, log_a, s0):
    b, t, h, dk = q.shape
    dv = v.shape[-1]
    pad = (-t) % GLA_BLOCK
    if pad:
        cfg = ((0, 0), (0, pad), (0, 0), (0, 0))
        q, k, v, log_a = (jnp.pad(a, cfg) for a in (q, k, v, log_a))
    nb = (t + pad) // GLA_BLOCK
    rs = lambda a: a.reshape(b, nb, GLA_BLOCK, h, a.shape[-1]).astype(jnp.float32)
    qb, kb, vb, ab = rs(q), rs(k), rs(v), rs(log_a)
    cum = jnp.cumsum(ab, axis=2)
    cum_end = cum[:, :, -1:]
    q_t = qb * jnp.exp(cum) * (dk ** -0.5)
    k_t = kb * jnp.exp(-cum)
    k_end = kb * jnp.exp(cum_end - cum)
    mask = jnp.tril(jnp.ones((GLA_BLOCK, GLA_BLOCK), jnp.float32))
    att = jnp.einsum('bnthk,bnshk->bnhts', q_t, k_t) * mask
    o_intra = jnp.einsum('bnhts,bnshv->bnthv', att, vb)
    upd = jnp.einsum('bnshk,bnshv->bnhkv', k_end, vb)
    dec = jnp.exp(cum_end[:, :, 0])

    def step(state, xs):
        u, dcy = xs
        return dcy[..., None] * state + u, state

    s_fin, s_prev = lax.scan(step, s0.astype(jnp.float32),
                             (jnp.swapaxes(upd, 0, 1), jnp.swapaxes(dec, 0, 1)))
    s_prev = jnp.swapaxes(s_prev, 0, 1)
    o_inter = jnp.einsum('bnthk,bnhkv->bnthv', q_t, s_prev)
    o = (o_intra + o_inter).reshape(b, nb * GLA_BLOCK, h, dv)[:, :t]
    return o.astype(v.dtype), s_fin.astype(s0.dtype)


def conv_ffn(h, conv_state, w_up, w_dw, b_dw, w_down):
    t = h.shape[1]
    u = h @ w_up
    uu = jnp.concatenate([conv_state.astype(u.dtype), u], axis=1)
    y = b_dw
    for j in range(CONV_W):
        y = y + w_dw[j] * uu[:, j:j + t]
    a, g = jnp.split(y, 2, axis=-1)
    out = (jax.nn.gelu(a, approximate=True) * g) @ w_down
    return out, uu[:, -(CONV_W - 1):]


def trunk_layer(x, c, k_cache, v_cache, s_gla, s_conv, p, first_chunk):
    b, t, _ = x.shape
    shift_m, scale_m, gate_m, shift_f, scale_f, gate_f = adaln_modulation(c, p['w_ada'], p['b_ada'])

    h = rmsnorm(x, p['g_pre_mix']) * (1.0 + scale_m) + shift_m
    z = h @ p['w_in']
    qa, ka, va, qb, kb, vb, gb, gk_low, gate_a, gate_b = jnp.split(
        z, np.cumsum(IN_SIZES)[:-1].tolist(), axis=-1)
    qa = qa.reshape(b, t, H_A, HD_A)
    ka = ka.reshape(b, t, H_A, HD_A)
    va = va.reshape(b, t, H_A, HD_A)
    if first_chunk:
        ya = band_attention_prompt(qa, ka, va, p['rel_bias'])
        rows = min(WINDOW_A, t)
        k_keep, v_keep = ka[:, t - rows:], va[:, t - rows:]
        s_gla = jnp.zeros((b, H_B, DK_B, DV_B), x.dtype)
        s_conv = jnp.zeros((b, CONV_W - 1, 2 * D_FF), x.dtype)
    else:
        ya = band_attention_sample(qa, ka, va, k_cache, v_cache, p['rel_bias'])
        k_keep, v_keep = ka, va
    log_a = jax.nn.log_sigmoid((gk_low @ p['w_gk2'] + p['b_gk']).astype(jnp.float32)) / GK_NORM
    yb, s_gla_new = gla(qb.reshape(b, t, H_B, DK_B), kb.reshape(b, t, H_B, DK_B),
                        vb.reshape(b, t, H_B, DV_B), log_a.reshape(b, t, H_B, DK_B), s_gla)
    yb = rmsnorm(yb, p['g_gla']) * jax.nn.silu(gb.reshape(b, t, H_B, DV_B))
    yb = yb.reshape(b, t, W_BV)
    merged = jax.nn.sigmoid(gate_a) * (ya @ p['w_br_a']) + jax.nn.sigmoid(gate_b) * (yb @ p['w_br_b'])
    x = x + gate_m * rmsnorm(merged @ p['w_out'], p['g_post_mix'])

    h = rmsnorm(x, p['g_pre_ffn']) * (1.0 + scale_f) + shift_f
    yf, conv_new = conv_ffn(h, s_conv, p['w_up'], p['w_dw'], p['b_dw'], p['w_down'])
    x = x + gate_f * rmsnorm(yf, p['g_post_ffn'])
    return x, k_keep, v_keep, s_gla_new, conv_new


def setup_inputs(seed: int = 0) -> dict:
    key = jax.random.key(seed)
    ks = jax.random.split(key, 26)
    kv_rows = min(WINDOW_A, PAST_LEN)

    def nrm(k, shape, scale):
        return jax.random.normal(k, shape, jnp.float32) * scale

    return {
        'x_prompt': nrm(ks[0], (BATCH, SEQ, D_MODEL), 1.0),
        'x_sample': nrm(ks[1], (DEC_BATCH, DEC_SEQ, D_MODEL), 1.0),
        'cache_k_a': nrm(ks[2], (DEPTH, DEC_BATCH, kv_rows, H_A, HD_A), 1.0),
        'cache_v_a': nrm(ks[3], (DEPTH, DEC_BATCH, kv_rows, H_A, HD_A), 1.0),
        'state_gla': nrm(ks[4], (DEPTH, DEC_BATCH, H_B, DK_B, DV_B), 1.0),
        'state_conv': nrm(ks[5], (DEPTH, DEC_BATCH, CONV_W - 1, 2 * D_FF), 1.0),
        'c_prompt': nrm(ks[6], (BATCH, D_MODEL), 1.0),
        'c_sample': nrm(ks[7], (DEC_BATCH, D_MODEL), 1.0),
        'w_ada': nrm(ks[8], (DEPTH, D_MODEL, 6 * D_MODEL), 0.5 * D_MODEL ** -0.5),
        'b_ada': nrm(ks[9], (DEPTH, 6 * D_MODEL), 0.01),
        'g_pre_mix': 1.0 + nrm(ks[10], (DEPTH, D_MODEL), 0.05),
        'g_post_mix': 1.0 + nrm(ks[11], (DEPTH, D_MODEL), 0.05),
        'g_pre_ffn': 1.0 + nrm(ks[12], (DEPTH, D_MODEL), 0.05),
        'g_post_ffn': 1.0 + nrm(ks[13], (DEPTH, D_MODEL), 0.05),
        'w_in': nrm(ks[14], (DEPTH, D_MODEL, D_IN), D_MODEL ** -0.5),
        'w_gk2': nrm(ks[15], (DEPTH, GK_RANK, W_BK), GK_RANK ** -0.5),
        'b_gk': nrm(ks[16], (DEPTH, W_BK), 0.1),
        'rel_bias': nrm(ks[17], (DEPTH, H_A, 2 * MAX_REL + 1), 0.5),
        'g_gla': 1.0 + nrm(ks[18], (DEPTH, DV_B), 0.05),
        'w_br_a': nrm(ks[19], (DEPTH, W_A, D_MODEL), W_A ** -0.5),
        'w_br_b': nrm(ks[20], (DEPTH, W_BV, D_MODEL), W_BV ** -0.5),
        'w_out': nrm(ks[21], (DEPTH, D_MODEL, D_MODEL), D_MODEL ** -0.5),
        'w_up': nrm(ks[22], (DEPTH, D_MODEL, 2 * D_FF), D_MODEL ** -0.5),
        'w_dw': nrm(ks[23], (DEPTH, CONV_W, 2 * D_FF), CONV_W ** -0.5),
        'b_dw': nrm(ks[24], (DEPTH, 2 * D_FF), 0.01),
        'w_down': nrm(ks[25], (DEPTH, D_FF, D_MODEL), D_FF ** -0.5),
    }


def reference(x_prompt, x_sample, cache_k_a, cache_v_a, state_gla, state_conv, c_prompt, c_sample,
              w_ada, b_ada, g_pre_mix, g_post_mix, g_pre_ffn, g_post_ffn, w_in, w_gk2, b_gk,
              rel_bias, g_gla, w_br_a, w_br_b, w_out, w_up, w_dw, b_dw, w_down):
    yp, ys = x_prompt, x_sample
    kp_l, vp_l, gp_l, cp_l, ks_l, vs_l, gs_l, cs_l = [], [], [], [], [], [], [], []
    for l in range(DEPTH):
        p = {'w_ada': w_ada[l], 'b_ada': b_ada[l], 'g_pre_mix': g_pre_mix[l],
             'g_post_mix': g_post_mix[l], 'g_pre_ffn': g_pre_ffn[l], 'g_post_ffn': g_post_ffn[l],
             'w_in': w_in[l], 'w_gk2': w_gk2[l], 'b_gk': b_gk[l], 'rel_bias': rel_bias[l],
             'g_gla': g_gla[l], 'w_br_a': w_br_a[l], 'w_br_b': w_br_b[l], 'w_out': w_out[l],
             'w_up': w_up[l], 'w_dw': w_dw[l], 'b_dw': b_dw[l], 'w_down': w_down[l]}
        yp, kp, vp, gp, cp = trunk_layer(yp, c_prompt, None, None, None, None, p, True)
        ys, kn, vn, gn, cn = trunk_layer(ys, c_sample, cache_k_a[l], cache_v_a[l],
                                         state_gla[l], state_conv[l], p, False)
        kp_l.append(kp); vp_l.append(vp); gp_l.append(gp); cp_l.append(cp)
        ks_l.append(kn); vs_l.append(vn); gs_l.append(gn); cs_l.append(cn)
    k_a_prompt = jnp.stack(kp_l)
    v_a_prompt = jnp.stack(vp_l)
    gla_prompt = jnp.stack(gp_l)
    conv_prompt = jnp.stack(cp_l)
    k_a_sample = jnp.stack(ks_l)
    v_a_sample = jnp.stack(vs_l)
    gla_sample = jnp.stack(gs_l)
    conv_sample = jnp.stack(cs_l)
    return (yp, ys, k_a_prompt, v_a_prompt, gla_prompt, conv_prompt,
            k_a_sample, v_a_sample, gla_sample, conv_sample)
```

```python
import functools

import jax
import jax.numpy as jnp
import numpy as np
from jax import lax
from jax.experimental import pallas as pl
from jax.experimental.pallas import tpu as pltpu

D_MODEL = 1024
CHUNK = 64
BAND_CHUNKS = 8
WINDOW_A = BAND_CHUNKS * CHUNK
H_A = 8
HD_A = 64
MAX_REL = 128
H_B = 4
DK_B = 64
DV_B = 128
GK_RANK = 16
GK_NORM = 16.0
GLA_SUB = 16
D_FF = 2816
CONV_W = 3
EPS = 1e-6
NEG_INF = -1e30
PAST_LEN = 2048

W_A = H_A * HD_A
W_BK = H_B * DK_B
W_BV = H_B * DV_B

LANES = 128
SUBLANES = 8
VMEM_LIMIT = 56 * 1024 * 1024

ATTN_QB = 256
ATTN_KB = 3 * ATTN_QB
FFN_FT = 256
FFN_NF = D_FF // FFN_FT

BF16 = jnp.bfloat16
F32 = jnp.float32


def _params(sem):
    return pltpu.CompilerParams(dimension_semantics=sem, vmem_limit_bytes=VMEM_LIMIT)


def _full_spec(shape):
    nd = len(shape)
    return pl.BlockSpec(shape, lambda *_: (0,) * nd, pipeline_mode=pl.Buffered(1))


def _dot(a, b):
    return jnp.dot(a, b, preferred_element_type=F32)


def _dot_nt(a, b):
    return lax.dot_general(a, b, (((1,), (1,)), ((), ())), preferred_element_type=F32)


def _dot_tn(a, b):
    return lax.dot_general(a, b, (((0,), (0,)), ((), ())), preferred_element_type=F32)


def _sigmoid(x):
    return 1.0 / (1.0 + jnp.exp(-x))


def _rms_scale(x):
    return lax.rsqrt(jnp.mean(x * x, axis=-1, keepdims=True) + EPS)


def _adaln_kernel(c_ref, w_ref, b_ref, o_ref):
    c = c_ref[...]
    s = (c * _sigmoid(c)).astype(BF16)
    o_ref[...] = _dot(s, w_ref[...].astype(BF16)) + b_ref[...]


def _adaln(c_all, w_ada, b_ada):
    rows = c_all.shape[0]
    n = w_ada.shape[1]
    tn = D_MODEL
    return pl.pallas_call(
        _adaln_kernel,
        grid=(n // tn,),
        in_specs=[
            pl.BlockSpec((rows, D_MODEL), lambda j: (0, 0)),
            pl.BlockSpec((D_MODEL, tn), lambda j: (0, j)),
            pl.BlockSpec((1, tn), lambda j: (0, j)),
        ],
        out_specs=pl.BlockSpec((rows, tn), lambda j: (0, j)),
        out_shape=jax.ShapeDtypeStruct((rows, n), F32),
        compiler_params=_params(("parallel",)),
        name="adaln",
    )(c_all, w_ada, b_ada.reshape(1, n))


_IN_GROUPS = (W_A, W_A, W_A, W_BK, W_BK, W_BV, W_BV, D_MODEL, D_MODEL)
_IN_MAIN = sum(_IN_GROUPS)


def _inproj_kernel(x_ref, shift_ref, scale_ref, g_ref, wm_ref, wg1_ref, wg2_ref, bgk_ref,
                   qa_ref, ka_ref, va_ref, qb_ref, kb_ref, vb_ref, gb_ref, ga_ref, gtb_ref, la_ref):
    nb, tt, _ = x_ref.shape
    x = x_ref[...]
    h = (x * _rms_scale(x)) * g_ref[...] * (1.0 + scale_ref[...]) + shift_ref[...]
    hb = h.reshape(nb * tt, D_MODEL).astype(BF16)
    outs = (qa_ref, ka_ref, va_ref, qb_ref, kb_ref, vb_ref, gb_ref, ga_ref, gtb_ref)
    lo = 0
    for o_ref, w in zip(outs, _IN_GROUPS):
        z = _dot(hb, wm_ref[:, lo:lo + w])
        o_ref[...] = z.reshape(nb, tt, w).astype(o_ref.dtype)
        lo += w
    gk_low = _dot(hb, wg1_ref[...]).astype(BF16)
    gk = _dot(gk_low, wg2_ref[...]) + bgk_ref[...]
    log_a = (jnp.minimum(gk, 0.0) - jnp.log1p(jnp.exp(-jnp.abs(gk)))) / GK_NORM
    la_ref[...] = log_a.reshape(nb, tt, W_BK)


def _inproj(x, mod, g_pre, wm, wg1, wg2, bgk, nb, tt):
    nbt, t, _ = x.shape
    grid = (nbt // nb, t // tt)
    tok = lambda w: pl.BlockSpec((nb, tt, w), lambda b, i: (b, i, 0))
    mod_spec = lambda col: pl.BlockSpec((nb, 1, D_MODEL), lambda b, i, col=col: (b, 0, col))
    widths = _IN_GROUPS + (W_BK,)
    dtypes = (BF16,) * len(_IN_GROUPS) + (F32,)
    return pl.pallas_call(
        _inproj_kernel,
        grid=grid,
        in_specs=[tok(D_MODEL), mod_spec(0), mod_spec(1), _full_spec((1, D_MODEL)),
                  _full_spec(wm.shape), _full_spec(wg1.shape), _full_spec(wg2.shape),
                  _full_spec((1, W_BK))],
        out_specs=[tok(w) for w in widths],
        out_shape=[jax.ShapeDtypeStruct((nbt, t, w), dt) for w, dt in zip(widths, dtypes)],
        compiler_params=_params(("parallel", "parallel")),
        name="inproj",
    )(x, mod, mod, g_pre, wm, wg1, wg2, bgk)


def _head_masks():
    lane = lax.broadcasted_iota(jnp.int32, (1, LANES), 1)
    first = lane < HD_A
    return first, jnp.logical_not(first)


def _attn_prompt_kernel(q_ref, k0_ref, k1_ref, k2_ref, v0_ref, v1_ref, v2_ref, bias_ref, o_ref):
    i = pl.program_id(1)
    col = lax.broadcasted_iota(jnp.int32, (1, ATTN_KB), 1)
    pen = jnp.where(col < (2 - i) * ATTN_QB, NEG_INF, 0.0).astype(F32)
    masks = _head_masks()
    for p in range(H_A // 2):
        sl = slice(p * LANES, (p + 1) * LANES)
        qp = q_ref[0, :, sl] * BF16(HD_A ** -0.5)
        kp = jnp.concatenate([k0_ref[0, :, sl], k1_ref[0, :, sl], k2_ref[0, :, sl]], axis=0)
        vp = jnp.concatenate([v0_ref[0, :, sl], v1_ref[0, :, sl], v2_ref[0, :, sl]], axis=0)
        o_heads = []
        for hh in range(2):
            qm = jnp.where(masks[hh], qp, jnp.zeros_like(qp))
            s = _dot_nt(qm, kp) + bias_ref[2 * p + hh] + pen
            m = jnp.max(s, axis=-1, keepdims=True)
            e = jnp.exp(s - m)
            l = jnp.sum(e, axis=-1, keepdims=True)
            o_heads.append(_dot(e.astype(BF16), vp) / l)
        o_ref[0, :, sl] = jnp.where(masks[0], o_heads[0], o_heads[1]).astype(o_ref.dtype)


def _attn_prompt(q, k, v, bias):
    b, t, _ = q.shape
    blk = lambda off: pl.BlockSpec(
        (1, ATTN_QB, W_A), lambda bb, i, off=off: (bb, jnp.maximum(i - off, 0), 0))
    return pl.pallas_call(
        _attn_prompt_kernel,
        grid=(b, t // ATTN_QB),
        in_specs=[blk(0), blk(2), blk(1), blk(0), blk(2), blk(1), blk(0), _full_spec(bias.shape)],
        out_specs=blk(0),
        out_shape=jax.ShapeDtypeStruct((b, t, W_A), BF16),
        compiler_params=_params(("parallel", "parallel")),
        name="attn_prompt",
    )(q, k, k, k, v, v, v, bias)


def _attn_sample_kernel(q_ref, kn_ref, vn_ref, kc_ref, vc_ref, bc_ref, bn_ref, o_ref):
    nb = q_ref.shape[0]
    masks = _head_masks()
    for b in range(nb):
        for p in range(H_A // 2):
            sl = slice(p * LANES, (p + 1) * LANES)
            qp = q_ref[b, :, sl] * BF16(HD_A ** -0.5)
            kc = kc_ref[b, :, sl].astype(BF16)
            vc = vc_ref[b, :, sl].astype(BF16)
            kn = kn_ref[b, :, sl]
            vn = vn_ref[b, :, sl]
            o_heads = []
            for hh in range(2):
                qm = jnp.where(masks[hh], qp, jnp.zeros_like(qp))
                sc = _dot_nt(qm, kc) + bc_ref[2 * p + hh]
                sn = _dot_nt(qm, kn) + bn_ref[2 * p + hh]
                m = jnp.maximum(jnp.max(sc, axis=-1, keepdims=True),
                                jnp.max(sn, axis=-1, keepdims=True))
                ec = jnp.exp(sc - m)
                en = jnp.exp(sn - m)
                l = jnp.sum(ec, axis=-1, keepdims=True) + jnp.sum(en, axis=-1, keepdims=True)
                o_heads.append((_dot(ec.astype(BF16), vc) + _dot(en.astype(BF16), vn)) / l)
            o_ref[b, :, sl] = jnp.where(masks[0], o_heads[0], o_heads[1]).astype(o_ref.dtype)


def _attn_sample(q, kn, vn, kc, vc, bias_c, bias_n, nb):
    b, s, _ = q.shape
    w = kc.shape[1]
    new = pl.BlockSpec((nb, s, W_A), lambda i: (i, 0, 0))
    cache = pl.BlockSpec((nb, w, W_A), lambda i: (i, 0, 0))
    return pl.pallas_call(
        _attn_sample_kernel,
        grid=(b // nb,),
        in_specs=[new, new, new, cache, cache, _full_spec(bias_c.shape), _full_spec(bias_n.shape)],
        out_specs=new,
        out_shape=jax.ShapeDtypeStruct((b, s, W_A), BF16),
        compiler_params=_params(("parallel",)),
        name="attn_sample",
    )(q, kn, vn, kc, vc, bias_c, bias_n)


def _prefix_sum_rows(x, row, period):
    d = 1
    while d < period:
        x = x + jnp.where((row & (period - 1)) >= d, pltpu.roll(x, d, 0), 0.0)
        d *= 2
    return x


def _gla_kernel(*refs, has_init):
    if has_init:
        q_ref, k_ref, v_ref, g_ref, la_ref, gg_ref, s0_ref, y_ref, so_ref, st_ref = refs
    else:
        q_ref, k_ref, v_ref, g_ref, la_ref, gg_ref, y_ref, so_ref, st_ref = refs
    nb, c, _ = q_ref.shape
    nsub = c // GLA_SUB
    j = pl.program_id(1)

    @pl.when(j == 0)
    def _():
        if has_init:
            for b in range(nb):
                for p in range(H_B // 2):
                    st_ref[b, p] = s0_ref[b, p].T
        else:
            st_ref[...] = jnp.zeros_like(st_ref)

    row = lax.broadcasted_iota(jnp.int32, (c, W_BK), 0)
    rowp = row[:, :LANES]
    colp = lax.broadcasted_iota(jnp.int32, (c, c), 1)
    tril = colp <= lax.broadcasted_iota(jnp.int32, (c, c), 0)
    masks = _head_masks()
    scale = DK_B ** -0.5

    for b in range(nb):
        la = la_ref[b]
        cum = _prefix_sum_rows(la, row, c)
        cum_sub = _prefix_sum_rows(la, row, GLA_SUB)
        cum_end = cum[c - 1:c, :]
        qf = q_ref[b].astype(F32)
        kf = k_ref[b].astype(F32)
        q_sub = qf * jnp.exp(cum_sub) * scale
        q_in = (qf * jnp.exp(cum) * scale).astype(BF16)
        k_end = (kf * jnp.exp(cum_end - cum)).astype(BF16)
        k_sub = []
        for i in range(nsub):
            ref_i = cum[i * GLA_SUB - 1:i * GLA_SUB, :] if i else jnp.zeros((1, W_BK), F32)
            k_i = jnp.where(row < (i + 1) * GLA_SUB, kf * jnp.exp(ref_i - cum), 0.0)
            k_sub.append(k_i.astype(BF16))
        for p in range(H_B // 2):
            sl = slice(p * LANES, (p + 1) * LANES)
            k_stack = jnp.concatenate([k_i[:, sl] for k_i in k_sub], axis=1)
            st_p = st_ref[b, p]
            st_pb = st_p.astype(BF16)
            upd = jnp.zeros((DV_B, LANES), F32)
            for hh in range(2):
                h = 2 * p + hh
                hs = slice(h * DV_B, (h + 1) * DV_B)
                q_m = jnp.where(masks[hh], q_sub[:, sl], 0.0)
                q_stack = jnp.concatenate(
                    [jnp.where((rowp >= i * GLA_SUB) & (rowp < (i + 1) * GLA_SUB), q_m, 0.0)
                     for i in range(nsub)], axis=1).astype(BF16)
                att = jnp.where(tril, _dot_nt(q_stack, k_stack), 0.0)
                v_h = v_ref[b, :, hs]
                q_im = jnp.where(masks[hh], q_in[:, sl], jnp.zeros((), BF16))
                o = _dot(att.astype(BF16), v_h) + _dot_nt(q_im, st_pb)
                gate = g_ref[b, :, hs].astype(F32)
                y = (o * _rms_scale(o)) * gg_ref[...] * (gate * _sigmoid(gate))
                y_ref[b, :, hs] = y.astype(y_ref.dtype)
                k_em = jnp.where(masks[hh], k_end[:, sl], jnp.zeros((), BF16))
                upd = upd + _dot_tn(v_h, k_em)
            st_ref[b, p] = st_p * jnp.exp(cum_end[:, sl]) + upd

    @pl.when(j == pl.num_programs(1) - 1)
    def _():
        for b in range(nb):
            for p in range(H_B // 2):
                so_ref[b, p] = st_ref[b, p].T


def _gla(q, k, v, g, la, g_gla, s0, nb, c):
    nbt, t, _ = q.shape
    tok = lambda w: pl.BlockSpec((nb, c, w), lambda b, j: (b, j, 0))
    st_spec = pl.BlockSpec((nb, H_B // 2, 2 * DK_B, DV_B), lambda b, j: (b, 0, 0, 0))
    in_specs = [tok(W_BK), tok(W_BK), tok(W_BV), tok(W_BV), tok(W_BK), _full_spec((1, DV_B))]
    args = [q, k, v, g, la, g_gla]
    if s0 is not None:
        in_specs.append(st_spec)
        args.append(s0)
    return pl.pallas_call(
        functools.partial(_gla_kernel, has_init=s0 is not None),
        grid=(nbt // nb, t // c),
        in_specs=in_specs,
        out_specs=[tok(W_BV), st_spec],
        out_shape=[jax.ShapeDtypeStruct((nbt, t, W_BV), BF16),
                   jax.ShapeDtypeStruct((nbt, H_B // 2, 2 * DK_B, DV_B), F32)],
        scratch_shapes=[pltpu.VMEM((nb, H_B // 2, DV_B, 2 * DK_B), F32)],
        compiler_params=_params(("parallel", "arbitrary")),
        name="gla",
    )(*args)


def _mixout_kernel(x_ref, ya_ref, yb_ref, ga_ref, gb_ref, gm_ref, gp_ref, wa_ref, wb_ref, wo_ref,
                   o_ref):
    nb, tt, _ = x_ref.shape
    m = nb * tt
    a = _dot(ya_ref[...].reshape(m, W_A), wa_ref[...])
    b = _dot(yb_ref[...].reshape(m, W_BV), wb_ref[...])
    ga = _sigmoid(ga_ref[...].reshape(m, D_MODEL).astype(F32))
    gb = _sigmoid(gb_ref[...].reshape(m, D_MODEL).astype(F32))
    merged = (ga * a + gb * b).astype(BF16)
    mo = _dot(merged, wo_ref[...])
    n = ((mo * _rms_scale(mo)) * gp_ref[...]).reshape(nb, tt, D_MODEL)
    o_ref[...] = x_ref[...] + gm_ref[...] * n


def _mixout(x, ya, yb, ga, gb, mod, g_post, wa, wb, wo, nb, tt):
    nbt, t, _ = x.shape
    tok = lambda w: pl.BlockSpec((nb, tt, w), lambda b, i: (b, i, 0))
    return pl.pallas_call(
        _mixout_kernel,
        grid=(nbt // nb, t // tt),
        in_specs=[tok(D_MODEL), tok(W_A), tok(W_BV), tok(D_MODEL), tok(D_MODEL),
                  pl.BlockSpec((nb, 1, D_MODEL), lambda b, i: (b, 0, 2)),
                  _full_spec((1, D_MODEL)), _full_spec(wa.shape), _full_spec(wb.shape),
                  _full_spec(wo.shape)],
        out_specs=tok(D_MODEL),
        out_shape=jax.ShapeDtypeStruct(x.shape, F32),
        compiler_params=_params(("parallel", "parallel")),
        name="mixout",
    )(x, ya, yb, ga, gb, mod, g_post, wa, wb, wo)


def _gelu_tanh(x):
    c = np.sqrt(2.0 / np.pi).astype(np.float32)
    return x * (0.5 * (1.0 + jnp.tanh(c * (x + 0.044715 * (x * x * x)))))


def _ffn_kernel(*refs, has_state):
    if has_state:
        (x_ref, shift_ref, scale_ref, gate_ref, gpre_ref, gpost_ref, wua_ref, wug_ref, wd_ref,
         cwa_ref, cwg_ref, st_ref, o_ref, tail_ref, h_ref, acc_ref) = refs
        prev_ref = st_ref
    else:
        (x_ref, shift_ref, scale_ref, gate_ref, gpre_ref, gpost_ref, wua_ref, wug_ref, wd_ref,
         cwa_ref, cwg_ref, o_ref, tail_ref, h_ref, acc_ref, carry_ref) = refs
        prev_ref = carry_ref

        @pl.when(pl.program_id(1) == 0)
        def _():
            carry_ref[...] = jnp.zeros_like(carry_ref)

    nb, tt, _ = x_ref.shape
    m = nb * tt
    x = x_ref[...]
    h = (x * _rms_scale(x)) * gpre_ref[...] * (1.0 + scale_ref[...]) + shift_ref[...]
    h_ref[...] = h.reshape(m, D_MODEL).astype(BF16)
    acc_ref[...] = jnp.zeros_like(acc_ref)
    ridx = lax.broadcasted_iota(jnp.int32, (nb, tt, FFN_FT), 1)

    def conv(u, prev, cw):
        u3 = u.reshape(nb, tt, FFN_FT)
        p1 = prev[:, SUBLANES - 1:SUBLANES, :]
        p2 = prev[:, SUBLANES - 2:SUBLANES - 1, :]
        r1 = pltpu.roll(u, 1, 0).reshape(nb, tt, FFN_FT)
        r2 = pltpu.roll(u, 2, 0).reshape(nb, tt, FFN_FT)
        u_m1 = jnp.where(ridx == 0, p1, r1)
        u_m2 = jnp.where(ridx == 0, p2, jnp.where(ridx == 1, p1, r2))
        y = cw[3:4, :] + cw[0:1, :] * u_m2
        y = y + cw[1:2, :] * u_m1
        y = y + cw[2:3, :] * u3
        return y.reshape(m, FFN_FT), u3[:, tt - SUBLANES:, :]

    def body(f, carry):
        hb = h_ref[...]
        ua = _dot(hb, wua_ref[f])
        ug = _dot(hb, wug_ref[f])
        ya, ta = conv(ua, prev_ref[0, f], cwa_ref[f])
        yg, tg = conv(ug, prev_ref[1, f], cwg_ref[f])
        act = (_gelu_tanh(ya) * yg).astype(BF16)
        acc_ref[...] += _dot(act, wd_ref[f])
        tail_ref[0, f] = ta
        tail_ref[1, f] = tg
        if not has_state:
            carry_ref[0, f] = ta
            carry_ref[1, f] = tg
        return carry

    lax.fori_loop(0, FFN_NF, body, 0)
    yf = acc_ref[...]
    n = ((yf * _rms_scale(yf)) * gpost_ref[...]).reshape(nb, tt, D_MODEL)
    o_ref[...] = x_ref[...] + gate_ref[...] * n


def _ffn(x, mod, g_pre, g_post, wua, wug, wd, cwa, cwg, state, nb, tt):
    nbt, t, _ = x.shape
    m = nb * tt
    tok = pl.BlockSpec((nb, tt, D_MODEL), lambda b, i: (b, i, 0))
    mod_spec = lambda col: pl.BlockSpec((nb, 1, D_MODEL), lambda b, i, col=col: (b, 0, col))
    tail_spec = pl.BlockSpec((2, FFN_NF, nb, SUBLANES, FFN_FT), lambda b, i: (0, 0, b, 0, 0))
    in_specs = [tok, mod_spec(3), mod_spec(4), mod_spec(5), _full_spec((1, D_MODEL)),
                _full_spec((1, D_MODEL)), _full_spec(wua.shape), _full_spec(wug.shape),
                _full_spec(wd.shape), _full_spec(cwa.shape), _full_spec(cwg.shape)]
    args = [x, mod, mod, mod, g_pre, g_post, wua, wug, wd, cwa, cwg]
    scratch = [pltpu.VMEM((m, D_MODEL), BF16), pltpu.VMEM((m, D_MODEL), F32)]
    if state is not None:
        in_specs.append(tail_spec)
        args.append(state)
    else:
        scratch.append(pltpu.VMEM((2, FFN_NF, nb, SUBLANES, FFN_FT), F32))
    return pl.pallas_call(
        functools.partial(_ffn_kernel, has_state=state is not None),
        grid=(nbt // nb, t // tt),
        in_specs=in_specs,
        out_specs=[tok, tail_spec],
        out_shape=[jax.ShapeDtypeStruct(x.shape, F32),
                   jax.ShapeDtypeStruct((2, FFN_NF, nbt, SUBLANES, FFN_FT), F32)],
        scratch_shapes=scratch,
        compiler_params=_params(("parallel", "arbitrary")),
        name="ffn",
    )(*args)


def _tail_to_state(tail):
    nbt = tail.shape[2]
    t = jnp.transpose(tail, (2, 3, 0, 1, 4)).reshape(nbt, SUBLANES, 2 * D_FF)
    return t[:, SUBLANES - (CONV_W - 1):, :]


def _state_to_prev(state):
    nbt = state.shape[0]
    s = jnp.pad(state, ((0, 0), (SUBLANES - (CONV_W - 1), 0), (0, 0)))
    s = s.reshape(nbt, SUBLANES, 2, FFN_NF, FFN_FT)
    return jnp.transpose(s, (2, 3, 0, 1, 4))


def _prompt_bias(rel_bias):
    q = np.arange(ATTN_QB)[:, None]
    kr = np.arange(ATTN_KB)[None, :] - 2 * ATTN_QB
    qc = q // CHUNK
    kc = np.floor_divide(kr, CHUNK)
    valid = (kc <= qc) & (kc >= qc - BAND_CHUNKS)
    idx = np.clip(q - kr, -MAX_REL, MAX_REL) + MAX_REL
    return jnp.where(valid[None], rel_bias[:, idx], NEG_INF).astype(F32)


def _sample_bias(rel_bias, s_len, w):
    q = np.arange(s_len)[:, None]
    idx_c = np.clip(q + w - np.arange(w)[None, :], -MAX_REL, MAX_REL) + MAX_REL
    idx_n = np.clip(q - np.arange(s_len)[None, :], -MAX_REL, MAX_REL) + MAX_REL
    return rel_bias[:, idx_c].astype(F32), rel_bias[:, idx_n].astype(F32)


def _layer(x, mod, cache, s_gla, s_conv, w, first_chunk):
    nbt, t, _ = x.shape
    if first_chunk:
        nb, tt = 1, 512
    else:
        nb, tt = 512 // t, t
    qa, ka, va, qb, kb, vb, gb, gate_a, gate_b, log_a = _inproj(
        x, mod, w['g_pre_mix'], w['w_main'], w['w_gk1'], w['w_gk2'], w['b_gk'], nb, tt)
    if first_chunk:
        ya = _attn_prompt(qa, ka, va, w['bias_p'])
        rows = min(WINDOW_A, t)
        k_keep, v_keep = ka[:, t - rows:], va[:, t - rows:]
        yb, s_new = _gla(qb, kb, vb, gb, log_a, w['g_gla'], None, nbt, CHUNK)
    else:
        k_cache, v_cache = cache
        ya = _attn_sample(qa, ka, va, k_cache, v_cache, w['bias_c'], w['bias_n'], 4)
        k_keep, v_keep = ka, va
        yb, s_new = _gla(qb, kb, vb, gb, log_a, w['g_gla'], s_gla, 4, t)
    x1 = _mixout(x, ya, yb, gate_a, gate_b, mod, w['g_post_mix'], w['w_br_a'], w['w_br_b'],
                 w['w_out'], nb, tt)
    prev = None if first_chunk else _state_to_prev(s_conv)
    y, tail = _ffn(x1, mod, w['g_pre_ffn'], w['g_post_ffn'], w['w_up_a'], w['w_up_g'], w['w_down'],
                   w['cw_a'], w['cw_g'], prev, nb, tt)
    heads = lambda a: a.astype(F32).reshape(nbt, a.shape[1], H_A, HD_A)
    return (y, heads(k_keep), heads(v_keep), s_new.reshape(nbt, H_B, DK_B, DV_B),
            _tail_to_state(tail))


def _prep_weights(w_in, w_gk2, b_gk, rel_bias, g_gla, w_br_a, w_br_b, w_out, w_up, w_dw, b_dw,
                  w_down, g_pre_mix, g_post_mix, g_pre_ffn, g_post_ffn, s_len, cache_rows):
    lo = 3 * W_A + 2 * W_BK + 2 * W_BV
    w_main = jnp.concatenate([w_in[:, :lo], w_in[:, lo + GK_RANK:]], axis=1).astype(BF16)
    w_gk1 = jnp.pad(w_in[:, lo:lo + GK_RANK], ((0, 0), (0, LANES - GK_RANK))).astype(BF16)
    w_gk2p = jnp.pad(w_gk2, ((0, LANES - GK_RANK), (0, 0))).astype(BF16)
    split_cols = lambda a: jnp.transpose(a.reshape(a.shape[0], FFN_NF, FFN_FT), (1, 0, 2))
    conv_w = lambda wd, bd: jnp.pad(split_cols(jnp.concatenate([wd, bd[None]], axis=0)),
                                    ((0, 0), (0, SUBLANES - CONV_W - 1), (0, 0)))
    bias_c, bias_n = _sample_bias(rel_bias, s_len, cache_rows)
    row = lambda a: a.reshape(1, -1)
    return {
        'w_main': w_main, 'w_gk1': w_gk1, 'w_gk2': w_gk2p, 'b_gk': row(b_gk),
        'bias_p': _prompt_bias(rel_bias), 'bias_c': bias_c, 'bias_n': bias_n,
        'g_gla': row(g_gla), 'w_br_a': w_br_a.astype(BF16), 'w_br_b': w_br_b.astype(BF16),
        'w_out': w_out.astype(BF16),
        'w_up_a': split_cols(w_up[:, :D_FF]).astype(BF16),
        'w_up_g': split_cols(w_up[:, D_FF:]).astype(BF16),
        'w_down': w_down.reshape(FFN_NF, FFN_FT, D_MODEL).astype(BF16),
        'cw_a': conv_w(w_dw[:, :D_FF], b_dw[:D_FF]), 'cw_g': conv_w(w_dw[:, D_FF:], b_dw[D_FF:]),
        'g_pre_mix': row(g_pre_mix), 'g_post_mix': row(g_post_mix),
        'g_pre_ffn': row(g_pre_ffn), 'g_post_ffn': row(g_post_ffn),
    }


def kernel(x_prompt, x_sample, cache_k_a, cache_v_a, state_gla, state_conv, c_prompt, c_sample, w_ada, b_ada, g_pre_mix, g_post_mix, g_pre_ffn, g_post_ffn, w_in, w_gk2, b_gk, rel_bias, g_gla, w_br_a, w_br_b, w_out, w_up, w_dw, b_dw, w_down):
    depth = w_ada.shape[0]
    assert depth == 1
    bp, bs = x_prompt.shape[0], x_sample.shape[0]
    s_len = x_sample.shape[1]
    cache_rows = cache_k_a.shape[2]
    yp, ys = x_prompt, x_sample
    outs = [[] for _ in range(8)]
    for l in range(depth):
        w = _prep_weights(w_in[l], w_gk2[l], b_gk[l], rel_bias[l], g_gla[l], w_br_a[l], w_br_b[l],
                          w_out[l], w_up[l], w_dw[l], b_dw[l], w_down[l], g_pre_mix[l],
                          g_post_mix[l], g_pre_ffn[l], g_post_ffn[l], s_len, cache_rows)
        c_all = jnp.concatenate([c_prompt, c_sample], axis=0)
        pad = (-c_all.shape[0]) % SUBLANES
        mod = _adaln(jnp.pad(c_all, ((0, pad), (0, 0))), w_ada[l], b_ada[l])
        mod_p = mod[:bp].reshape(bp, 1, 6 * D_MODEL)
        mod_s = mod[bp:bp + bs].reshape(bs, 1, 6 * D_MODEL)
        yp, kp, vp, gp, cp = _layer(yp, mod_p, None, None, None, w, True)
        cache = (cache_k_a[l].reshape(bs, cache_rows, W_A), cache_v_a[l].reshape(bs, cache_rows, W_A))
        s0 = state_gla[l].reshape(bs, H_B // 2, 2 * DK_B, DV_B)
        ys, kn, vn, gn, cn = _layer(ys, mod_s, cache, s0, state_conv[l], w, False)
        for lst, a in zip(outs, (kp, vp, gp, cp, kn, vn, gn, cn)):
            lst.append(a)
    return (yp, ys) + tuple(jnp.stack(lst) for lst in outs)
```

```python
import functools

import jax
import jax.numpy as jnp
import numpy as np
from jax import lax
from jax.experimental import pallas as pl
from jax.experimental.pallas import tpu as pltpu

D_MODEL = 1024
CHUNK = 64
BAND_CHUNKS = 8
WINDOW_A = BAND_CHUNKS * CHUNK
H_A = 8
HD_A = 64
MAX_REL = 128
H_B = 4
DK_B = 64
DV_B = 128
GK_RANK = 16
GK_NORM = 16.0
GLA_SUB = 16
D_FF = 2816
CONV_W = 3
EPS = 1e-6
NEG_INF = -1e30
PAST_LEN = 2048

W_A = H_A * HD_A
W_BK = H_B * DK_B
W_BV = H_B * DV_B

LANES = 128
SUBLANES = 8
VMEM_LIMIT = 56 * 1024 * 1024

ATTN_QB = 256
ATTN_KB = 3 * ATTN_QB
BIAS_ROW = 1024
FFN_FT = 256
FFN_NF = D_FF // FFN_FT

BF16 = jnp.bfloat16
F32 = jnp.float32


def _params(sem):
    return pltpu.CompilerParams(dimension_semantics=sem, vmem_limit_bytes=VMEM_LIMIT)


def _full_spec(shape):
    nd = len(shape)
    return pl.BlockSpec(shape, lambda *_: (0,) * nd, pipeline_mode=pl.Buffered(1))


def _dot(a, b):
    return jnp.dot(a, b, preferred_element_type=F32)


def _dot_nt(a, b):
    return lax.dot_general(a, b, (((1,), (1,)), ((), ())), preferred_element_type=F32)


def _dot_tn(a, b):
    return lax.dot_general(a, b, (((0,), (0,)), ((), ())), preferred_element_type=F32)


def _sigmoid(x):
    return 1.0 / (1.0 + jnp.exp(-x))


def _rms_scale(x):
    return lax.rsqrt(jnp.mean(x * x, axis=-1, keepdims=True) + EPS)


def _adaln_kernel(c_ref, w_ref, b_ref, o_ref):
    c = c_ref[...]
    s = (c * _sigmoid(c)).astype(BF16)
    o_ref[...] = _dot(s, w_ref[...].astype(BF16)) + b_ref[...]


def _adaln(c_all, w_ada, b_ada):
    rows = c_all.shape[0]
    n = w_ada.shape[1]
    tn = D_MODEL
    return pl.pallas_call(
        _adaln_kernel,
        grid=(n // tn,),
        in_specs=[
            pl.BlockSpec((rows, D_MODEL), lambda j: (0, 0)),
            pl.BlockSpec((D_MODEL, tn), lambda j: (0, j)),
            pl.BlockSpec((1, tn), lambda j: (0, j)),
        ],
        out_specs=pl.BlockSpec((rows, tn), lambda j: (0, j)),
        out_shape=jax.ShapeDtypeStruct((rows, n), F32),
        compiler_params=_params(("parallel",)),
        name="adaln",
    )(c_all, w_ada, b_ada.reshape(1, n))


_IN_GROUPS = (W_A, W_A, W_A, W_BK, W_BK, W_BV, W_BV, D_MODEL, D_MODEL)
_IN_MAIN = sum(_IN_GROUPS)


def _inproj_kernel(x_ref, shift_ref, scale_ref, g_ref, wm_ref, wg1_ref, wg2_ref, bgk_ref,
                   qa_ref, ka_ref, va_ref, qb_ref, kb_ref, vb_ref, gb_ref, ga_ref, gtb_ref, la_ref):
    nb, tt, _ = x_ref.shape
    x = x_ref[...]
    h = (x * _rms_scale(x)) * g_ref[...] * (1.0 + scale_ref[...]) + shift_ref[...]
    hb = h.reshape(nb * tt, D_MODEL).astype(BF16)
    outs = (qa_ref, ka_ref, va_ref, qb_ref, kb_ref, vb_ref, gb_ref, ga_ref, gtb_ref)
    lo = 0
    for o_ref, w in zip(outs, _IN_GROUPS):
        z = _dot(hb, wm_ref[:, lo:lo + w])
        o_ref[...] = z.reshape(nb, tt, w).astype(o_ref.dtype)
        lo += w
    gk_low = _dot(hb, wg1_ref[...]).astype(BF16)
    gk = _dot(gk_low, wg2_ref[...]) + bgk_ref[...]
    log_a = (jnp.minimum(gk, 0.0) - jnp.log1p(jnp.exp(-jnp.abs(gk)))) / GK_NORM
    la_ref[...] = log_a.reshape(nb, tt, W_BK)


def _inproj(x, mod, g_pre, wm, wg1, wg2, bgk, nb, tt):
    nbt, t, _ = x.shape
    grid = (nbt // nb, t // tt)
    tok = lambda w: pl.BlockSpec((nb, tt, w), lambda b, i: (b, i, 0))
    mod_spec = lambda col: pl.BlockSpec((nb, 1, D_MODEL), lambda b, i, col=col: (b, 0, col))
    widths = _IN_GROUPS + (W_BK,)
    dtypes = (BF16,) * len(_IN_GROUPS) + (F32,)
    return pl.pallas_call(
        _inproj_kernel,
        grid=grid,
        in_specs=[tok(D_MODEL), mod_spec(0), mod_spec(1), _full_spec((1, D_MODEL)),
                  _full_spec(wm.shape), _full_spec(wg1.shape), _full_spec(wg2.shape),
                  _full_spec((1, W_BK))],
        out_specs=[tok(w) for w in widths],
        out_shape=[jax.ShapeDtypeStruct((nbt, t, w), dt) for w, dt in zip(widths, dtypes)],
        compiler_params=_params(("parallel", "parallel")),
        name="inproj",
    )(x, mod, mod, g_pre, wm, wg1, wg2, bgk)


def _head_masks():
    lane = lax.broadcasted_iota(jnp.int32, (1, LANES), 1)
    first = lane < HD_A
    return first, jnp.logical_not(first)


def _toeplitz_bias(row_ref, h, rows):
    rb = jnp.broadcast_to(row_ref[h], (rows, BIAS_ROW))
    return pltpu.roll(rb, 0, 1, stride=1, stride_axis=0)


def _attn_prompt_kernel(q_ref, k0_ref, k1_ref, k2_ref, v0_ref, v1_ref, v2_ref, row_ref, o_ref,
                        bias_ref):
    i = pl.program_id(1)

    @pl.when((pl.program_id(0) == 0) & (i == 0))
    def _():
        qc = lax.broadcasted_iota(jnp.int32, (ATTN_QB, ATTN_KB), 0) // CHUNK
        kc = lax.broadcasted_iota(jnp.int32, (ATTN_QB, ATTN_KB), 1) // CHUNK - BAND_CHUNKS
        valid = (kc <= qc) & (kc >= qc - BAND_CHUNKS)
        for h in range(H_A):
            t = _toeplitz_bias(row_ref, h, ATTN_QB)
            bias_ref[h] = jnp.where(valid, t[:, :ATTN_KB], NEG_INF)

    col = lax.broadcasted_iota(jnp.int32, (1, ATTN_KB), 1)
    pen = jnp.where(col < (2 - i) * ATTN_QB, NEG_INF, 0.0).astype(F32)
    masks = _head_masks()
    for p in range(H_A // 2):
        sl = slice(p * LANES, (p + 1) * LANES)
        qp = q_ref[0, :, sl] * BF16(HD_A ** -0.5)
        kp = jnp.concatenate([k0_ref[0, :, sl], k1_ref[0, :, sl], k2_ref[0, :, sl]], axis=0)
        vp = jnp.concatenate([v0_ref[0, :, sl], v1_ref[0, :, sl], v2_ref[0, :, sl]], axis=0)
        o_heads = []
        for hh in range(2):
            qm = jnp.where(masks[hh], qp, jnp.zeros_like(qp))
            s = _dot_nt(qm, kp) + bias_ref[2 * p + hh] + pen
            m = jnp.max(s, axis=-1, keepdims=True)
            e = jnp.exp(s - m)
            l = jnp.sum(e, axis=-1, keepdims=True)
            o_heads.append(_dot(e.astype(BF16), vp) / l)
        o_ref[0, :, sl] = jnp.where(masks[0], o_heads[0], o_heads[1]).astype(o_ref.dtype)


def _attn_prompt(q, k, v, bias_rows):
    b, t, _ = q.shape
    blk = lambda off: pl.BlockSpec(
        (1, ATTN_QB, W_A), lambda bb, i, off=off: (bb, jnp.maximum(i - off, 0), 0))
    return pl.pallas_call(
        _attn_prompt_kernel,
        grid=(b, t // ATTN_QB),
        in_specs=[blk(0), blk(2), blk(1), blk(0), blk(2), blk(1), blk(0),
                  _full_spec(bias_rows.shape)],
        out_specs=blk(0),
        out_shape=jax.ShapeDtypeStruct((b, t, W_A), BF16),
        scratch_shapes=[pltpu.VMEM((H_A, ATTN_QB, ATTN_KB), F32)],
        compiler_params=_params(("arbitrary", "arbitrary")),
        name="attn_prompt",
    )(q, k, k, k, v, v, v, bias_rows)


def _attn_sample_kernel(q_ref, kn_ref, vn_ref, kc_ref, vc_ref, row_ref, o_ref, bc_ref, bn_ref):
    nb, s_len, _ = q_ref.shape
    w = kc_ref.shape[1]

    @pl.when(pl.program_id(0) == 0)
    def _():
        for h in range(H_A):
            t = _toeplitz_bias(row_ref, h, s_len)
            bc_ref[h] = t[:, :w]
            bn_ref[h] = t[:, w:w + s_len]

    masks = _head_masks()
    for b in range(nb):
        for p in range(H_A // 2):
            sl = slice(p * LANES, (p + 1) * LANES)
            qp = q_ref[b, :, sl] * BF16(HD_A ** -0.5)
            kc = kc_ref[b, :, sl].astype(BF16)
            vc = vc_ref[b, :, sl].astype(BF16)
            kn = kn_ref[b, :, sl]
            vn = vn_ref[b, :, sl]
            o_heads = []
            for hh in range(2):
                qm = jnp.where(masks[hh], qp, jnp.zeros_like(qp))
                sc = _dot_nt(qm, kc) + bc_ref[2 * p + hh]
                sn = _dot_nt(qm, kn) + bn_ref[2 * p + hh]
                m = jnp.maximum(jnp.max(sc, axis=-1, keepdims=True),
                                jnp.max(sn, axis=-1, keepdims=True))
                ec = jnp.exp(sc - m)
                en = jnp.exp(sn - m)
                l = jnp.sum(ec, axis=-1, keepdims=True) + jnp.sum(en, axis=-1, keepdims=True)
                o_heads.append((_dot(ec.astype(BF16), vc) + _dot(en.astype(BF16), vn)) / l)
            o_ref[b, :, sl] = jnp.where(masks[0], o_heads[0], o_heads[1]).astype(o_ref.dtype)


def _attn_sample(q, kn, vn, kc, vc, bias_rows, nb):
    b, s, _ = q.shape
    w = kc.shape[1]
    assert w == WINDOW_A
    new = pl.BlockSpec((nb, s, W_A), lambda i: (i, 0, 0))
    cache = pl.BlockSpec((nb, w, W_A), lambda i: (i, 0, 0))
    return pl.pallas_call(
        _attn_sample_kernel,
        grid=(b // nb,),
        in_specs=[new, new, new, cache, cache, _full_spec(bias_rows.shape)],
        out_specs=new,
        out_shape=jax.ShapeDtypeStruct((b, s, W_A), BF16),
        scratch_shapes=[pltpu.VMEM((H_A, s, w), F32), pltpu.VMEM((H_A, s, s), F32)],
        compiler_params=_params(("arbitrary",)),
        name="attn_sample",
    )(q, kn, vn, kc, vc, bias_rows)


def _prefix_sum_rows(x, row, period):
    d = 1
    while d < period:
        x = x + jnp.where((row & (period - 1)) >= d, pltpu.roll(x, d, 0), 0.0)
        d *= 2
    return x


def _gla_kernel(*refs, has_init):
    if has_init:
        q_ref, k_ref, v_ref, g_ref, la_ref, gg_ref, s0_ref, y_ref, so_ref, st_ref = refs
    else:
        q_ref, k_ref, v_ref, g_ref, la_ref, gg_ref, y_ref, so_ref, st_ref = refs
    nb, c, _ = q_ref.shape
    nsub = c // GLA_SUB
    j = pl.program_id(1)

    @pl.when(j == 0)
    def _():
        if has_init:
            for b in range(nb):
                for p in range(H_B // 2):
                    st_ref[b, p] = s0_ref[b, p].T
        else:
            st_ref[...] = jnp.zeros_like(st_ref)

    row = lax.broadcasted_iota(jnp.int32, (c, W_BK), 0)
    rowp = row[:, :LANES]
    colp = lax.broadcasted_iota(jnp.int32, (c, c), 1)
    tril = colp <= lax.broadcasted_iota(jnp.int32, (c, c), 0)
    masks = _head_masks()
    scale = DK_B ** -0.5

    for b in range(nb):
        la = la_ref[b]
        cum = _prefix_sum_rows(la, row, c)
        cum_sub = _prefix_sum_rows(la, row, GLA_SUB)
        cum_end = cum[c - 1:c, :]
        qf = q_ref[b].astype(F32)
        kf = k_ref[b].astype(F32)
        q_sub = qf * jnp.exp(cum_sub) * scale
        q_in = (qf * jnp.exp(cum) * scale).astype(BF16)
        k_end = (kf * jnp.exp(cum_end - cum)).astype(BF16)
        k_sub = []
        for i in range(nsub):
            ref_i = cum[i * GLA_SUB - 1:i * GLA_SUB, :] if i else jnp.zeros((1, W_BK), F32)
            k_i = jnp.where(row < (i + 1) * GLA_SUB, kf * jnp.exp(ref_i - cum), 0.0)
            k_sub.append(k_i.astype(BF16))
        for p in range(H_B // 2):
            sl = slice(p * LANES, (p + 1) * LANES)
            k_stack = jnp.concatenate([k_i[:, sl] for k_i in k_sub], axis=1)
            st_p = st_ref[b, p]
            st_pb = st_p.astype(BF16)
            upd = jnp.zeros((DV_B, LANES), F32)
            for hh in range(2):
                h = 2 * p + hh
                hs = slice(h * DV_B, (h + 1) * DV_B)
                q_m = jnp.where(masks[hh], q_sub[:, sl], 0.0)
                q_stack = jnp.concatenate(
                    [jnp.where((rowp >= i * GLA_SUB) & (rowp < (i + 1) * GLA_SUB), q_m, 0.0)
                     for i in range(nsub)], axis=1).astype(BF16)
                att = jnp.where(tril, _dot_nt(q_stack, k_stack), 0.0)
                v_h = v_ref[b, :, hs]
                q_im = jnp.where(masks[hh], q_in[:, sl], jnp.zeros((), BF16))
                o = _dot(att.astype(BF16), v_h) + _dot_nt(q_im, st_pb)
                gate = g_ref[b, :, hs].astype(F32)
                y = (o * _rms_scale(o)) * gg_ref[...] * (gate * _sigmoid(gate))
                y_ref[b, :, hs] = y.astype(y_ref.dtype)
                k_em = jnp.where(masks[hh], k_end[:, sl], jnp.zeros((), BF16))
                upd = upd + _dot_tn(v_h, k_em)
            st_ref[b, p] = st_p * jnp.exp(cum_end[:, sl]) + upd

    @pl.when(j == pl.num_programs(1) - 1)
    def _():
        for b in range(nb):
            for p in range(H_B // 2):
                so_ref[b, p] = st_ref[b, p].T


def _gla(q, k, v, g, la, g_gla, s0, nb, c):
    nbt, t, _ = q.shape
    tok = lambda w: pl.BlockSpec((nb, c, w), lambda b, j: (b, j, 0))
    st_spec = pl.BlockSpec((nb, H_B // 2, 2 * DK_B, DV_B), lambda b, j: (b, 0, 0, 0))
    in_specs = [tok(W_BK), tok(W_BK), tok(W_BV), tok(W_BV), tok(W_BK), _full_spec((1, DV_B))]
    args = [q, k, v, g, la, g_gla]
    if s0 is not None:
        in_specs.append(st_spec)
        args.append(s0)
    return pl.pallas_call(
        functools.partial(_gla_kernel, has_init=s0 is not None),
        grid=(nbt // nb, t // c),
        in_specs=in_specs,
        out_specs=[tok(W_BV), st_spec],
        out_shape=[jax.ShapeDtypeStruct((nbt, t, W_BV), BF16),
                   jax.ShapeDtypeStruct((nbt, H_B // 2, 2 * DK_B, DV_B), F32)],
        scratch_shapes=[pltpu.VMEM((nb, H_B // 2, DV_B, 2 * DK_B), F32)],
        compiler_params=_params(("parallel", "arbitrary")),
        name="gla",
    )(*args)


def _mixout_kernel(x_ref, ya_ref, yb_ref, ga_ref, gb_ref, gm_ref, gp_ref, wa_ref, wb_ref, wo_ref,
                   o_ref):
    nb, tt, _ = x_ref.shape
    m = nb * tt
    a = _dot(ya_ref[...].reshape(m, W_A), wa_ref[...])
    b = _dot(yb_ref[...].reshape(m, W_BV), wb_ref[...])
    ga = _sigmoid(ga_ref[...].reshape(m, D_MODEL).astype(F32))
    gb = _sigmoid(gb_ref[...].reshape(m, D_MODEL).astype(F32))
    merged = (ga * a + gb * b).astype(BF16)
    mo = _dot(merged, wo_ref[...])
    n = ((mo * _rms_scale(mo)) * gp_ref[...]).reshape(nb, tt, D_MODEL)
    o_ref[...] = x_ref[...] + gm_ref[...] * n


def _mixout(x, ya, yb, ga, gb, mod, g_post, wa, wb, wo, nb, tt):
    nbt, t, _ = x.shape
    tok = lambda w: pl.BlockSpec((nb, tt, w), lambda b, i: (b, i, 0))
    return pl.pallas_call(
        _mixout_kernel,
        grid=(nbt // nb, t // tt),
        in_specs=[tok(D_MODEL), tok(W_A), tok(W_BV), tok(D_MODEL), tok(D_MODEL),
                  pl.BlockSpec((nb, 1, D_MODEL), lambda b, i: (b, 0, 2)),
                  _full_spec((1, D_MODEL)), _full_spec(wa.shape), _full_spec(wb.shape),
                  _full_spec(wo.shape)],
        out_specs=tok(D_MODEL),
        out_shape=jax.ShapeDtypeStruct(x.shape, F32),
        compiler_params=_params(("parallel", "parallel")),
        name="mixout",
    )(x, ya, yb, ga, gb, mod, g_post, wa, wb, wo)


def _gelu_tanh(x):
    c = np.sqrt(2.0 / np.pi).astype(np.float32)
    return x * (0.5 * (1.0 + jnp.tanh(c * (x + 0.044715 * (x * x * x)))))


def _ffn_kernel(*refs, has_state):
    if has_state:
        (x_ref, shift_ref, scale_ref, gate_ref, gpre_ref, gpost_ref, wua_ref, wug_ref, wd_ref,
         cwa_ref, cwg_ref, st_ref, o_ref, tail_ref, h_ref, acc_ref) = refs
        prev_ref = st_ref
    else:
        (x_ref, shift_ref, scale_ref, gate_ref, gpre_ref, gpost_ref, wua_ref, wug_ref, wd_ref,
         cwa_ref, cwg_ref, o_ref, tail_ref, h_ref, acc_ref, carry_ref) = refs
        prev_ref = carry_ref

        @pl.when(pl.program_id(1) == 0)
        def _():
            carry_ref[...] = jnp.zeros_like(carry_ref)

    nb, tt, _ = x_ref.shape
    m = nb * tt
    x = x_ref[...]
    h = (x * _rms_scale(x)) * gpre_ref[...] * (1.0 + scale_ref[...]) + shift_ref[...]
    h_ref[...] = h.reshape(m, D_MODEL).astype(BF16)
    acc_ref[...] = jnp.zeros_like(acc_ref)
    ridx = lax.broadcasted_iota(jnp.int32, (nb, tt, FFN_FT), 1)

    def conv(u, prev, cw):
        u3 = u.reshape(nb, tt, FFN_FT)
        p1 = prev[:, SUBLANES - 1:SUBLANES, :]
        p2 = prev[:, SUBLANES - 2:SUBLANES - 1, :]
        r1 = pltpu.roll(u, 1, 0).reshape(nb, tt, FFN_FT)
        r2 = pltpu.roll(u, 2, 0).reshape(nb, tt, FFN_FT)
        u_m1 = jnp.where(ridx == 0, p1, r1)
        u_m2 = jnp.where(ridx == 0, p2, jnp.where(ridx == 1, p1, r2))
        y = cw[3:4, :] + cw[0:1, :] * u_m2
        y = y + cw[1:2, :] * u_m1
        y = y + cw[2:3, :] * u3
        return y.reshape(m, FFN_FT), u3[:, tt - SUBLANES:, :]

    def body(f, carry):
        hb = h_ref[...]
        ua = _dot(hb, wua_ref[f])
        ug = _dot(hb, wug_ref[f])
        ya, ta = conv(ua, prev_ref[0, f], cwa_ref[f])
        yg, tg = conv(ug, prev_ref[1, f], cwg_ref[f])
        act = (_gelu_tanh(ya) * yg).astype(BF16)
        acc_ref[...] += _dot(act, wd_ref[f])
        tail_ref[0, f] = ta
        tail_ref[1, f] = tg
        if not has_state:
            carry_ref[0, f] = ta
            carry_ref[1, f] = tg
        return carry

    lax.fori_loop(0, FFN_NF, body, 0)
    yf = acc_ref[...]
    n = ((yf * _rms_scale(yf)) * gpost_ref[...]).reshape(nb, tt, D_MODEL)
    o_ref[...] = x_ref[...] + gate_ref[...] * n


def _ffn(x, mod, g_pre, g_post, wua, wug, wd, cwa, cwg, state, nb, tt):
    nbt, t, _ = x.shape
    m = nb * tt
    tok = pl.BlockSpec((nb, tt, D_MODEL), lambda b, i: (b, i, 0))
    mod_spec = lambda col: pl.BlockSpec((nb, 1, D_MODEL), lambda b, i, col=col: (b, 0, col))
    tail_spec = pl.BlockSpec((2, FFN_NF, nb, SUBLANES, FFN_FT), lambda b, i: (0, 0, b, 0, 0))
    in_specs = [tok, mod_spec(3), mod_spec(4), mod_spec(5), _full_spec((1, D_MODEL)),
                _full_spec((1, D_MODEL)), _full_spec(wua.shape), _full_spec(wug.shape),
                _full_spec(wd.shape), _full_spec(cwa.shape), _full_spec(cwg.shape)]
    args = [x, mod, mod, mod, g_pre, g_post, wua, wug, wd, cwa, cwg]
    scratch = [pltpu.VMEM((m, D_MODEL), BF16), pltpu.VMEM((m, D_MODEL), F32)]
    if state is not None:
        in_specs.append(tail_spec)
        args.append(state)
    else:
        scratch.append(pltpu.VMEM((2, FFN_NF, nb, SUBLANES, FFN_FT), F32))
    return pl.pallas_call(
        functools.partial(_ffn_kernel, has_state=state is not None),
        grid=(nbt // nb, t // tt),
        in_specs=in_specs,
        out_specs=[tok, tail_spec],
        out_shape=[jax.ShapeDtypeStruct(x.shape, F32),
                   jax.ShapeDtypeStruct((2, FFN_NF, nbt, SUBLANES, FFN_FT), F32)],
        scratch_shapes=scratch,
        compiler_params=_params(("parallel", "arbitrary")),
        name="ffn",
    )(*args)


def _tail_to_state(tail):
    nbt = tail.shape[2]
    t = jnp.transpose(tail, (2, 3, 0, 1, 4)).reshape(nbt, SUBLANES, 2 * D_FF)
    return t[:, SUBLANES - (CONV_W - 1):, :]


def _state_to_prev(state):
    nbt = state.shape[0]
    s = jnp.pad(state, ((0, 0), (SUBLANES - (CONV_W - 1), 0), (0, 0)))
    s = s.reshape(nbt, SUBLANES, 2, FFN_NF, FFN_FT)
    return jnp.transpose(s, (2, 3, 0, 1, 4))


def _bias_rows(rel_bias):
    assert BIAS_ROW >= ATTN_KB + ATTN_QB - 1 and WINDOW_A == ATTN_KB - ATTN_QB
    far_pos = jnp.broadcast_to(rel_bias[:, -1:], (H_A, BIAS_ROW))
    far_neg = jnp.broadcast_to(rel_bias[:, :1], (H_A, BIAS_ROW))
    n_mid = 2 * MAX_REL + 1
    n_lo = ATTN_KB + 1 - (WINDOW_A - MAX_REL) - n_mid
    rows = jnp.concatenate([far_pos[:, :WINDOW_A - MAX_REL], rel_bias[:, ::-1], far_neg[:, :n_lo],
                            far_pos[:, :BIAS_ROW - ATTN_KB - 1]], axis=1)
    return rows.astype(F32).reshape(H_A, 1, BIAS_ROW)


def _layer(x, mod, cache, s_gla, s_conv, w, first_chunk):
    nbt, t, _ = x.shape
    if first_chunk:
        nb, tt = 1, 512
    else:
        nb, tt = 512 // t, t
    qa, ka, va, qb, kb, vb, gb, gate_a, gate_b, log_a = _inproj(
        x, mod, w['g_pre_mix'], w['w_main'], w['w_gk1'], w['w_gk2'], w['b_gk'], nb, tt)
    if first_chunk:
        ya = _attn_prompt(qa, ka, va, w['bias_rows'])
        rows = min(WINDOW_A, t)
        k_keep, v_keep = ka[:, t - rows:], va[:, t - rows:]
        yb, s_new = _gla(qb, kb, vb, gb, log_a, w['g_gla'], None, nbt, CHUNK)
    else:
        k_cache, v_cache = cache
        ya = _attn_sample(qa, ka, va, k_cache, v_cache, w['bias_rows'], 4)
        k_keep, v_keep = ka, va
        yb, s_new = _gla(qb, kb, vb, gb, log_a, w['g_gla'], s_gla, 4, t)
    x1 = _mixout(x, ya, yb, gate_a, gate_b, mod, w['g_post_mix'], w['w_br_a'], w['w_br_b'],
                 w['w_out'], nb, tt)
    prev = None if first_chunk else _state_to_prev(s_conv)
    y, tail = _ffn(x1, mod, w['g_pre_ffn'], w['g_post_ffn'], w['w_up_a'], w['w_up_g'], w['w_down'],
                   w['cw_a'], w['cw_g'], prev, nb, tt)
    heads = lambda a: a.astype(F32).reshape(nbt, a.shape[1], H_A, HD_A)
    return (y, heads(k_keep), heads(v_keep), s_new.reshape(nbt, H_B, DK_B, DV_B),
            _tail_to_state(tail))


def _prep_weights(w_in, w_gk2, b_gk, rel_bias, g_gla, w_br_a, w_br_b, w_out, w_up, w_dw, b_dw,
                  w_down, g_pre_mix, g_post_mix, g_pre_ffn, g_post_ffn, s_len, cache_rows):
    lo = 3 * W_A + 2 * W_BK + 2 * W_BV
    w_main = jnp.concatenate([w_in[:, :lo], w_in[:, lo + GK_RANK:]], axis=1).astype(BF16)
    w_gk1 = jnp.pad(w_in[:, lo:lo + GK_RANK], ((0, 0), (0, LANES - GK_RANK))).astype(BF16)
    w_gk2p = jnp.pad(w_gk2, ((0, LANES - GK_RANK), (0, 0))).astype(BF16)
    split_cols = lambda a: jnp.transpose(a.reshape(a.shape[0], FFN_NF, FFN_FT), (1, 0, 2))
    conv_w = lambda wd, bd: jnp.pad(split_cols(jnp.concatenate([wd, bd[None]], axis=0)),
                                    ((0, 0), (0, SUBLANES - CONV_W - 1), (0, 0)))
    row = lambda a: a.reshape(1, -1)
    return {
        'w_main': w_main, 'w_gk1': w_gk1, 'w_gk2': w_gk2p, 'b_gk': row(b_gk),
        'bias_rows': _bias_rows(rel_bias),
        'g_gla': row(g_gla), 'w_br_a': w_br_a.astype(BF16), 'w_br_b': w_br_b.astype(BF16),
        'w_out': w_out.astype(BF16),
        'w_up_a': split_cols(w_up[:, :D_FF]).astype(BF16),
        'w_up_g': split_cols(w_up[:, D_FF:]).astype(BF16),
        'w_down': w_down.reshape(FFN_NF, FFN_FT, D_MODEL).astype(BF16),
        'cw_a': conv_w(w_dw[:, :D_FF], b_dw[:D_FF]), 'cw_g': conv_w(w_dw[:, D_FF:], b_dw[D_FF:]),
        'g_pre_mix': row(g_pre_mix), 'g_post_mix': row(g_post_mix),
        'g_pre_ffn': row(g_pre_ffn), 'g_post_ffn': row(g_post_ffn),
    }


def kernel(x_prompt, x_sample, cache_k_a, cache_v_a, state_gla, state_conv, c_prompt, c_sample, w_ada, b_ada, g_pre_mix, g_post_mix, g_pre_ffn, g_post_ffn, w_in, w_gk2, b_gk, rel_bias, g_gla, w_br_a, w_br_b, w_out, w_up, w_dw, b_dw, w_down):
    depth = w_ada.shape[0]
    assert depth == 1
    bp, bs = x_prompt.shape[0], x_sample.shape[0]
    s_len = x_sample.shape[1]
    cache_rows = cache_k_a.shape[2]
    yp, ys = x_prompt, x_sample
    outs = [[] for _ in range(8)]
    for l in range(depth):
        w = _prep_weights(w_in[l], w_gk2[l], b_gk[l], rel_bias[l], g_gla[l], w_br_a[l], w_br_b[l],
                          w_out[l], w_up[l], w_dw[l], b_dw[l], w_down[l], g_pre_mix[l],
                          g_post_mix[l], g_pre_ffn[l], g_post_ffn[l], s_len, cache_rows)
        c_all = jnp.concatenate([c_prompt, c_sample], axis=0)
        pad = (-c_all.shape[0]) % SUBLANES
        mod = _adaln(jnp.pad(c_all, ((0, pad), (0, 0))), w_ada[l], b_ada[l])
        mod_p = mod[:bp].reshape(bp, 1, 6 * D_MODEL)
        mod_s = mod[bp:bp + bs].reshape(bs, 1, 6 * D_MODEL)
        yp, kp, vp, gp, cp = _layer(yp, mod_p, None, None, None, w, True)
        cache = (cache_k_a[l].reshape(bs, cache_rows, W_A), cache_v_a[l].reshape(bs, cache_rows, W_A))
        s0 = state_gla[l].reshape(bs, H_B // 2, 2 * DK_B, DV_B)
        ys, kn, vn, gn, cn = _layer(ys, mod_s, cache, s0, state_conv[l], w, False)
        for lst, a in zip(outs, (kp, vp, gp, cp, kn, vn, gn, cn)):
            lst.append(a)
    return (yp, ys) + tuple(jnp.stack(lst) for lst in outs)
```

```python
import functools

import jax
import jax.numpy as jnp
import numpy as np
from jax import lax
from jax.experimental import pallas as pl
from jax.experimental.pallas import tpu as pltpu

D_MODEL = 1024
CHUNK = 64
BAND_CHUNKS = 8
WINDOW_A = BAND_CHUNKS * CHUNK
H_A = 8
HD_A = 64
MAX_REL = 128
H_B = 4
DK_B = 64
DV_B = 128
GK_RANK = 16
GK_NORM = 16.0
GLA_SUB = 16
D_FF = 2816
CONV_W = 3
EPS = 1e-6
NEG_INF = -1e30
PAST_LEN = 2048

W_A = H_A * HD_A
W_BK = H_B * DK_B
W_BV = H_B * DV_B

LANES = 128
SUBLANES = 8
VMEM_LIMIT = 56 * 1024 * 1024

ATTN_QB = 256
ATTN_KB = 3 * ATTN_QB
BIAS_ROW = 1024
FFN_FT = 256
FFN_NF = D_FF // FFN_FT

BF16 = jnp.bfloat16
F32 = jnp.float32


def _params(sem):
    return pltpu.CompilerParams(dimension_semantics=sem, vmem_limit_bytes=VMEM_LIMIT)


def _full_spec(shape):
    nd = len(shape)
    return pl.BlockSpec(shape, lambda *_: (0,) * nd, pipeline_mode=pl.Buffered(1))


def _dot(a, b):
    return jnp.dot(a, b, preferred_element_type=F32)


def _dot_nt(a, b):
    return lax.dot_general(a, b, (((1,), (1,)), ((), ())), preferred_element_type=F32)


def _dot_tn(a, b):
    return lax.dot_general(a, b, (((0,), (0,)), ((), ())), preferred_element_type=F32)


def _sigmoid(x):
    return 1.0 / (1.0 + jnp.exp(-x))


def _rms_scale(x):
    return lax.rsqrt(jnp.mean(x * x, axis=-1, keepdims=True) + EPS)


def _cast_kernel(*refs):
    n = len(refs) // 2
    for w_ref, o_ref in zip(refs[:n], refs[n:]):
        o_ref[...] = w_ref[...].astype(o_ref.dtype).reshape(o_ref.shape)


def _cast_rows(ws, steps):
    specs = [pl.BlockSpec((w.shape[0] // steps, w.shape[1]), lambda i: (i, 0)) for w in ws]
    return pl.pallas_call(
        _cast_kernel,
        grid=(steps,),
        in_specs=specs,
        out_specs=specs,
        out_shape=[jax.ShapeDtypeStruct(w.shape, BF16) for w in ws],
        compiler_params=_params(("parallel",)),
        name="cast_rows",
    )(*ws)


def _cast_col_blocks(w, tn):
    k, n = w.shape
    return pl.pallas_call(
        _cast_kernel,
        grid=(n // tn,),
        in_specs=[pl.BlockSpec((k, tn), lambda j: (0, j))],
        out_specs=pl.BlockSpec((1, k, tn), lambda j: (j, 0, 0)),
        out_shape=jax.ShapeDtypeStruct((n // tn, k, tn), BF16),
        compiler_params=_params(("parallel",)),
        name="cast_col_blocks",
    )(w)


_GK_LO = 3 * W_A + 2 * W_BK + 2 * W_BV


def _split_w_in_kernel(w_ref, main_ref, gk_ref):
    main_ref[:, :_GK_LO] = w_ref[:, :_GK_LO].astype(BF16)
    main_ref[:, _GK_LO:] = w_ref[:, _GK_LO + GK_RANK:].astype(BF16)
    lane = lax.broadcasted_iota(jnp.int32, (1, LANES), 1)
    gk_ref[...] = jnp.where(lane < GK_RANK, w_ref[:, _GK_LO:_GK_LO + LANES], 0.0).astype(BF16)


def _split_w_in(w_in, rows):
    d, d_in = w_in.shape
    return pl.pallas_call(
        _split_w_in_kernel,
        grid=(d // rows,),
        in_specs=[pl.BlockSpec((rows, d_in), lambda i: (i, 0))],
        out_specs=[pl.BlockSpec((rows, d_in - GK_RANK), lambda i: (i, 0)),
                   pl.BlockSpec((rows, LANES), lambda i: (i, 0))],
        out_shape=[jax.ShapeDtypeStruct((d, d_in - GK_RANK), BF16),
                   jax.ShapeDtypeStruct((d, LANES), BF16)],
        compiler_params=_params(("parallel",)),
        name="split_w_in",
    )(w_in)


def _adaln_kernel(c_ref, w_ref, b_ref, o_ref):
    c = c_ref[...]
    s = (c * _sigmoid(c)).astype(BF16)
    o_ref[...] = _dot(s, w_ref[...].astype(BF16)) + b_ref[...]


def _adaln(c_all, w_ada, b_ada):
    rows = c_all.shape[0]
    n = w_ada.shape[1]
    tn = D_MODEL
    return pl.pallas_call(
        _adaln_kernel,
        grid=(n // tn,),
        in_specs=[
            pl.BlockSpec((rows, D_MODEL), lambda j: (0, 0)),
            pl.BlockSpec((D_MODEL, tn), lambda j: (0, j)),
            pl.BlockSpec((1, tn), lambda j: (0, j)),
        ],
        out_specs=pl.BlockSpec((rows, tn), lambda j: (0, j)),
        out_shape=jax.ShapeDtypeStruct((rows, n), F32),
        compiler_params=_params(("parallel",)),
        name="adaln",
    )(c_all, w_ada, b_ada.reshape(1, n))


_IN_GROUPS = (W_A, W_A, W_A, W_BK, W_BK, W_BV, W_BV, D_MODEL, D_MODEL)
_IN_MAIN = sum(_IN_GROUPS)


def _inproj_kernel(x_ref, shift_ref, scale_ref, g_ref, wm_ref, wg1_ref, wg2_ref, bgk_ref,
                   qa_ref, ka_ref, va_ref, qb_ref, kb_ref, vb_ref, gb_ref, ga_ref, gtb_ref, la_ref):
    nb, tt, _ = x_ref.shape
    x = x_ref[...]
    h = (x * _rms_scale(x)) * g_ref[...] * (1.0 + scale_ref[...]) + shift_ref[...]
    hb = h.reshape(nb * tt, D_MODEL).astype(BF16)
    outs = (qa_ref, ka_ref, va_ref, qb_ref, kb_ref, vb_ref, gb_ref, ga_ref, gtb_ref)
    lo = 0
    for o_ref, w in zip(outs, _IN_GROUPS):
        z = _dot(hb, wm_ref[:, lo:lo + w])
        o_ref[...] = z.reshape(nb, tt, w).astype(o_ref.dtype)
        lo += w
    gk_low = _dot(hb, wg1_ref[...]).astype(BF16)
    gk = _dot(gk_low, wg2_ref[...]) + bgk_ref[...]
    log_a = (jnp.minimum(gk, 0.0) - jnp.log1p(jnp.exp(-jnp.abs(gk)))) / GK_NORM
    la_ref[...] = log_a.reshape(nb, tt, W_BK)


def _inproj(x, mod, g_pre, wm, wg1, wg2, bgk, nb, tt):
    nbt, t, _ = x.shape
    grid = (nbt // nb, t // tt)
    tok = lambda w: pl.BlockSpec((nb, tt, w), lambda b, i: (b, i, 0))
    mod_spec = lambda col: pl.BlockSpec((nb, 1, D_MODEL), lambda b, i, col=col: (b, 0, col))
    widths = _IN_GROUPS + (W_BK,)
    dtypes = (BF16,) * len(_IN_GROUPS) + (F32,)
    return pl.pallas_call(
        _inproj_kernel,
        grid=grid,
        in_specs=[tok(D_MODEL), mod_spec(0), mod_spec(1), _full_spec((1, D_MODEL)),
                  _full_spec(wm.shape), _full_spec(wg1.shape), _full_spec(wg2.shape),
                  _full_spec((1, W_BK))],
        out_specs=[tok(w) for w in widths],
        out_shape=[jax.ShapeDtypeStruct((nbt, t, w), dt) for w, dt in zip(widths, dtypes)],
        compiler_params=_params(("parallel", "parallel")),
        name="inproj",
    )(x, mod, mod, g_pre, wm, wg1, wg2, bgk)


def _head_masks():
    lane = lax.broadcasted_iota(jnp.int32, (1, LANES), 1)
    first = lane < HD_A
    return first, jnp.logical_not(first)


def _toeplitz_bias(row_ref, h, rows):
    rb = jnp.broadcast_to(row_ref[h], (rows, BIAS_ROW))
    return pltpu.roll(rb, 0, 1, stride=1, stride_axis=0)


def _attn_prompt_kernel(q_ref, k0_ref, k1_ref, k2_ref, v0_ref, v1_ref, v2_ref, row_ref, o_ref,
                        bias_ref):
    i = pl.program_id(1)

    @pl.when((pl.program_id(0) == 0) & (i == 0))
    def _():
        qc = lax.broadcasted_iota(jnp.int32, (ATTN_QB, ATTN_KB), 0) // CHUNK
        kc = lax.broadcasted_iota(jnp.int32, (ATTN_QB, ATTN_KB), 1) // CHUNK - BAND_CHUNKS
        valid = (kc <= qc) & (kc >= qc - BAND_CHUNKS)
        for h in range(H_A):
            t = _toeplitz_bias(row_ref, h, ATTN_QB)
            bias_ref[h] = jnp.where(valid, t[:, :ATTN_KB], NEG_INF)

    col = lax.broadcasted_iota(jnp.int32, (1, ATTN_KB), 1)
    pen = jnp.where(col < (2 - i) * ATTN_QB, NEG_INF, 0.0).astype(F32)
    masks = _head_masks()
    for p in range(H_A // 2):
        sl = slice(p * LANES, (p + 1) * LANES)
        qp = q_ref[0, :, sl] * BF16(HD_A ** -0.5)
        kp = jnp.concatenate([k0_ref[0, :, sl], k1_ref[0, :, sl], k2_ref[0, :, sl]], axis=0)
        vp = jnp.concatenate([v0_ref[0, :, sl], v1_ref[0, :, sl], v2_ref[0, :, sl]], axis=0)
        o_heads = []
        for hh in range(2):
            qm = jnp.where(masks[hh], qp, jnp.zeros_like(qp))
            s = _dot_nt(qm, kp) + bias_ref[2 * p + hh] + pen
            m = jnp.max(s, axis=-1, keepdims=True)
            e = jnp.exp(s - m)
            l = jnp.sum(e, axis=-1, keepdims=True)
            o_heads.append(_dot(e.astype(BF16), vp) / l)
        o_ref[0, :, sl] = jnp.where(masks[0], o_heads[0], o_heads[1]).astype(o_ref.dtype)


def _attn_prompt(q, k, v, bias_rows):
    b, t, _ = q.shape
    blk = lambda off: pl.BlockSpec(
        (1, ATTN_QB, W_A), lambda bb, i, off=off: (bb, jnp.maximum(i - off, 0), 0))
    return pl.pallas_call(
        _attn_prompt_kernel,
        grid=(b, t // ATTN_QB),
        in_specs=[blk(0), blk(2), blk(1), blk(0), blk(2), blk(1), blk(0),
                  _full_spec(bias_rows.shape)],
        out_specs=blk(0),
        out_shape=jax.ShapeDtypeStruct((b, t, W_A), BF16),
        scratch_shapes=[pltpu.VMEM((H_A, ATTN_QB, ATTN_KB), F32)],
        compiler_params=_params(("arbitrary", "arbitrary")),
        name="attn_prompt",
    )(q, k, k, k, v, v, v, bias_rows)


def _attn_sample_kernel(q_ref, kn_ref, vn_ref, kc_ref, vc_ref, row_ref, o_ref, bc_ref, bn_ref):
    nb, s_len, _ = q_ref.shape
    w = kc_ref.shape[1] // H_A

    @pl.when(pl.program_id(0) == 0)
    def _():
        for h in range(H_A):
            t = _toeplitz_bias(row_ref, h, s_len)
            bc_ref[h] = t[:, :w]
            bn_ref[h] = t[:, w:w + s_len]

    for b in range(nb):
        o_heads = []
        for h in range(H_A):
            sl = slice(h * HD_A, (h + 1) * HD_A)
            qh = q_ref[b, :, sl] * BF16(HD_A ** -0.5)
            kc = kc_ref[b, pl.ds(h, w, stride=H_A), :].astype(BF16)
            vc = vc_ref[b, pl.ds(h, w, stride=H_A), :].astype(BF16)
            sc = _dot_nt(qh, kc) + bc_ref[h]
            sn = _dot_nt(qh, kn_ref[b, :, sl]) + bn_ref[h]
            m = jnp.maximum(jnp.max(sc, axis=-1, keepdims=True),
                            jnp.max(sn, axis=-1, keepdims=True))
            ec = jnp.exp(sc - m)
            en = jnp.exp(sn - m)
            l = jnp.sum(ec, axis=-1, keepdims=True) + jnp.sum(en, axis=-1, keepdims=True)
            o_heads.append((_dot(ec.astype(BF16), vc) + _dot(en.astype(BF16), vn_ref[b, :, sl])) / l)
        o_ref[b] = jnp.concatenate(o_heads, axis=1).astype(o_ref.dtype)


def _attn_sample(q, kn, vn, kc, vc, bias_rows, nb):
    b, s, _ = q.shape
    w = kc.shape[1] // H_A
    assert w == WINDOW_A
    new = pl.BlockSpec((nb, s, W_A), lambda i: (i, 0, 0))
    cache = pl.BlockSpec((nb, w * H_A, HD_A), lambda i: (i, 0, 0))
    return pl.pallas_call(
        _attn_sample_kernel,
        grid=(b // nb,),
        in_specs=[new, new, new, cache, cache, _full_spec(bias_rows.shape)],
        out_specs=new,
        out_shape=jax.ShapeDtypeStruct((b, s, W_A), BF16),
        scratch_shapes=[pltpu.VMEM((H_A, s, w), F32), pltpu.VMEM((H_A, s, s), F32)],
        compiler_params=_params(("arbitrary",)),
        name="attn_sample",
    )(q, kn, vn, kc, vc, bias_rows)


def _prefix_sum_rows(x, row, period):
    d = 1
    while d < period:
        x = x + jnp.where((row & (period - 1)) >= d, pltpu.roll(x, d, 0), 0.0)
        d *= 2
    return x


def _gla_kernel(*refs, has_init):
    if has_init:
        q_ref, k_ref, v_ref, g_ref, la_ref, gg_ref, s0_ref, y_ref, so_ref, st_ref = refs
    else:
        q_ref, k_ref, v_ref, g_ref, la_ref, gg_ref, y_ref, so_ref, st_ref = refs
    nb, c, _ = q_ref.shape
    nsub = c // GLA_SUB
    j = pl.program_id(1)

    @pl.when(j == 0)
    def _():
        if has_init:
            for b in range(nb):
                for p in range(H_B // 2):
                    st_ref[b, p] = s0_ref[b, p].T
        else:
            st_ref[...] = jnp.zeros_like(st_ref)

    row = lax.broadcasted_iota(jnp.int32, (c, W_BK), 0)
    rowp = row[:, :LANES]
    colp = lax.broadcasted_iota(jnp.int32, (c, c), 1)
    tril = colp <= lax.broadcasted_iota(jnp.int32, (c, c), 0)
    masks = _head_masks()
    scale = DK_B ** -0.5

    for b in range(nb):
        la = la_ref[b]
        cum = _prefix_sum_rows(la, row, c)
        cum_sub = _prefix_sum_rows(la, row, GLA_SUB)
        cum_end = cum[c - 1:c, :]
        qf = q_ref[b].astype(F32)
        kf = k_ref[b].astype(F32)
        q_sub = qf * jnp.exp(cum_sub) * scale
        q_in = (qf * jnp.exp(cum) * scale).astype(BF16)
        k_end = (kf * jnp.exp(cum_end - cum)).astype(BF16)
        k_sub = []
        for i in range(nsub):
            ref_i = cum[i * GLA_SUB - 1:i * GLA_SUB, :] if i else jnp.zeros((1, W_BK), F32)
            k_i = jnp.where(row < (i + 1) * GLA_SUB, kf * jnp.exp(ref_i - cum), 0.0)
            k_sub.append(k_i.astype(BF16))
        for p in range(H_B // 2):
            sl = slice(p * LANES, (p + 1) * LANES)
            k_stack = jnp.concatenate([k_i[:, sl] for k_i in k_sub], axis=1)
            st_p = st_ref[b, p]
            st_pb = st_p.astype(BF16)
            upd = jnp.zeros((DV_B, LANES), F32)
            for hh in range(2):
                h = 2 * p + hh
                hs = slice(h * DV_B, (h + 1) * DV_B)
                q_m = jnp.where(masks[hh], q_sub[:, sl], 0.0)
                q_stack = jnp.concatenate(
                    [jnp.where((rowp >= i * GLA_SUB) & (rowp < (i + 1) * GLA_SUB), q_m, 0.0)
                     for i in range(nsub)], axis=1).astype(BF16)
                att = jnp.where(tril, _dot_nt(q_stack, k_stack), 0.0)
                v_h = v_ref[b, :, hs]
                q_im = jnp.where(masks[hh], q_in[:, sl], jnp.zeros((), BF16))
                o = _dot(att.astype(BF16), v_h) + _dot_nt(q_im, st_pb)
                gate = g_ref[b, :, hs].astype(F32)
                y = (o * _rms_scale(o)) * gg_ref[...] * (gate * _sigmoid(gate))
                y_ref[b, :, hs] = y.astype(y_ref.dtype)
                k_em = jnp.where(masks[hh], k_end[:, sl], jnp.zeros((), BF16))
                upd = upd + _dot_tn(v_h, k_em)
            st_ref[b, p] = st_p * jnp.exp(cum_end[:, sl]) + upd

    @pl.when(j == pl.num_programs(1) - 1)
    def _():
        for b in range(nb):
            for p in range(H_B // 2):
                so_ref[b, p] = st_ref[b, p].T


def _gla(q, k, v, g, la, g_gla, s0, nb, c):
    nbt, t, _ = q.shape
    tok = lambda w: pl.BlockSpec((nb, c, w), lambda b, j: (b, j, 0))
    st_spec = pl.BlockSpec((nb, H_B // 2, 2 * DK_B, DV_B), lambda b, j: (b, 0, 0, 0))
    in_specs = [tok(W_BK), tok(W_BK), tok(W_BV), tok(W_BV), tok(W_BK), _full_spec((1, DV_B))]
    args = [q, k, v, g, la, g_gla]
    if s0 is not None:
        in_specs.append(st_spec)
        args.append(s0)
    return pl.pallas_call(
        functools.partial(_gla_kernel, has_init=s0 is not None),
        grid=(nbt // nb, t // c),
        in_specs=in_specs,
        out_specs=[tok(W_BV), st_spec],
        out_shape=[jax.ShapeDtypeStruct((nbt, t, W_BV), BF16),
                   jax.ShapeDtypeStruct((nbt, H_B // 2, 2 * DK_B, DV_B), F32)],
        scratch_shapes=[pltpu.VMEM((nb, H_B // 2, DV_B, 2 * DK_B), F32)],
        compiler_params=_params(("parallel", "arbitrary")),
        name="gla",
    )(*args)


def _mixout_kernel(x_ref, ya_ref, yb_ref, ga_ref, gb_ref, gm_ref, gp_ref, wa_ref, wb_ref, wo_ref,
                   o_ref):
    nb, tt, _ = x_ref.shape
    m = nb * tt
    a = _dot(ya_ref[...].reshape(m, W_A), wa_ref[...])
    b = _dot(yb_ref[...].reshape(m, W_BV), wb_ref[...])
    ga = _sigmoid(ga_ref[...].reshape(m, D_MODEL).astype(F32))
    gb = _sigmoid(gb_ref[...].reshape(m, D_MODEL).astype(F32))
    merged = (ga * a + gb * b).astype(BF16)
    mo = _dot(merged, wo_ref[...])
    n = ((mo * _rms_scale(mo)) * gp_ref[...]).reshape(nb, tt, D_MODEL)
    o_ref[...] = x_ref[...] + gm_ref[...] * n


def _mixout(x, ya, yb, ga, gb, mod, g_post, wa, wb, wo, nb, tt):
    nbt, t, _ = x.shape
    tok = lambda w: pl.BlockSpec((nb, tt, w), lambda b, i: (b, i, 0))
    return pl.pallas_call(
        _mixout_kernel,
        grid=(nbt // nb, t // tt),
        in_specs=[tok(D_MODEL), tok(W_A), tok(W_BV), tok(D_MODEL), tok(D_MODEL),
                  pl.BlockSpec((nb, 1, D_MODEL), lambda b, i: (b, 0, 2)),
                  _full_spec((1, D_MODEL)), _full_spec(wa.shape), _full_spec(wb.shape),
                  _full_spec(wo.shape)],
        out_specs=tok(D_MODEL),
        out_shape=jax.ShapeDtypeStruct(x.shape, F32),
        compiler_params=_params(("parallel", "parallel")),
        name="mixout",
    )(x, ya, yb, ga, gb, mod, g_post, wa, wb, wo)


def _gelu_tanh(x):
    c = float(np.sqrt(2.0 / np.pi))
    half = 0.5 * x
    return half + half * jnp.tanh(x * (c + (0.044715 * c) * (x * x)))


def _ffn_kernel(*refs, has_state):
    if has_state:
        (x_ref, shift_ref, scale_ref, gate_ref, gpre_ref, gpost_ref, wu_ref, wd_ref,
         cwa_ref, cwg_ref, st_ref, o_ref, tail_ref, h_ref, act_ref) = refs
        prev_ref = st_ref
    else:
        (x_ref, shift_ref, scale_ref, gate_ref, gpre_ref, gpost_ref, wu_ref, wd_ref,
         cwa_ref, cwg_ref, o_ref, tail_ref, h_ref, act_ref, carry_ref) = refs
        prev_ref = carry_ref

        @pl.when(pl.program_id(1) == 0)
        def _():
            carry_ref[...] = jnp.zeros_like(carry_ref)

    nb, tt, _ = x_ref.shape
    m = nb * tt
    x = x_ref[...]
    h = (x * _rms_scale(x)) * gpre_ref[...] * (1.0 + scale_ref[...]) + shift_ref[...]
    h_ref[...] = h.reshape(m, D_MODEL).astype(BF16)
    ridx = lax.broadcasted_iota(jnp.int32, (nb, SUBLANES, FFN_FT), 1)

    def conv(u, prev, cw):
        u3 = u.reshape(nb, tt, FFN_FT)
        r1 = pltpu.roll(u, 1, 0).reshape(nb, tt, FFN_FT)
        r2 = pltpu.roll(u, 2, 0).reshape(nb, tt, FFN_FT)

        def taps(u_m2, u_m1, u_0):
            y = cw[3:4, :] + cw[0:1, :] * u_m2
            y = y + cw[1:2, :] * u_m1
            return y + cw[2:3, :] * u_0

        p1 = prev[:, SUBLANES - 1:SUBLANES, :]
        p2 = prev[:, SUBLANES - 2:SUBLANES - 1, :]
        h_m1 = jnp.where(ridx == 0, p1, r1[:, :SUBLANES])
        h_m2 = jnp.where(ridx == 0, p2, jnp.where(ridx == 1, p1, r2[:, :SUBLANES]))
        y = jnp.concatenate([taps(h_m2, h_m1, u3[:, :SUBLANES]),
                             taps(r2[:, SUBLANES:], r1[:, SUBLANES:], u3[:, SUBLANES:])], axis=1)
        return y.reshape(m, FFN_FT), u3[:, tt - SUBLANES:, :]

    def up(f):
        hb = h_ref[...]
        return _dot(hb, wu_ref[f]), _dot(hb, wu_ref[FFN_NF + f])

    ua, ug = up(0)
    for f in range(FFN_NF):
        if f + 1 < FFN_NF:
            ua_next, ug_next = up(f + 1)
        ya, ta = conv(ua, prev_ref[0, f], cwa_ref[f])
        yg, tg = conv(ug, prev_ref[1, f], cwg_ref[f])
        act_ref[:, f * FFN_FT:(f + 1) * FFN_FT] = (_gelu_tanh(ya) * yg).astype(BF16)
        tail_ref[0, f] = ta
        tail_ref[1, f] = tg
        if not has_state:
            carry_ref[0, f] = ta
            carry_ref[1, f] = tg
        if f + 1 < FFN_NF:
            ua, ug = ua_next, ug_next
    yf = _dot(act_ref[...], wd_ref[...])
    n = ((yf * _rms_scale(yf)) * gpost_ref[...]).reshape(nb, tt, D_MODEL)
    o_ref[...] = x_ref[...] + gate_ref[...] * n


def _ffn(x, mod, g_pre, g_post, wu, wd, cwa, cwg, state, nb, tt):
    nbt, t, _ = x.shape
    m = nb * tt
    tok = pl.BlockSpec((nb, tt, D_MODEL), lambda b, i: (b, i, 0))
    mod_spec = lambda col: pl.BlockSpec((nb, 1, D_MODEL), lambda b, i, col=col: (b, 0, col))
    tail_spec = pl.BlockSpec((2, FFN_NF, nb, SUBLANES, FFN_FT), lambda b, i: (0, 0, b, 0, 0))
    in_specs = [tok, mod_spec(3), mod_spec(4), mod_spec(5), _full_spec((1, D_MODEL)),
                _full_spec((1, D_MODEL)), _full_spec(wu.shape),
                _full_spec(wd.shape), _full_spec(cwa.shape), _full_spec(cwg.shape)]
    args = [x, mod, mod, mod, g_pre, g_post, wu, wd, cwa, cwg]
    scratch = [pltpu.VMEM((m, D_MODEL), BF16), pltpu.VMEM((m, D_FF), BF16)]
    if state is not None:
        in_specs.append(tail_spec)
        args.append(state)
    else:
        scratch.append(pltpu.VMEM((2, FFN_NF, nb, SUBLANES, FFN_FT), F32))
    return pl.pallas_call(
        functools.partial(_ffn_kernel, has_state=state is not None),
        grid=(nbt // nb, t // tt),
        in_specs=in_specs,
        out_specs=[tok, tail_spec],
        out_shape=[jax.ShapeDtypeStruct(x.shape, F32),
                   jax.ShapeDtypeStruct((2, FFN_NF, nbt, SUBLANES, FFN_FT), F32)],
        scratch_shapes=scratch,
        compiler_params=_params(("parallel", "arbitrary")),
        name="ffn",
    )(*args)


def _tail_to_state(tail):
    nbt = tail.shape[2]
    t = jnp.transpose(tail, (2, 3, 0, 1, 4)).reshape(nbt, SUBLANES, 2 * D_FF)
    return t[:, SUBLANES - (CONV_W - 1):, :]


def _state_to_prev(state):
    nbt = state.shape[0]
    s = jnp.pad(state, ((0, 0), (SUBLANES - (CONV_W - 1), 0), (0, 0)))
    s = s.reshape(nbt, SUBLANES, 2, FFN_NF, FFN_FT)
    return jnp.transpose(s, (2, 3, 0, 1, 4))


def _bias_rows(rel_bias):
    assert BIAS_ROW >= ATTN_KB + ATTN_QB - 1 and WINDOW_A == ATTN_KB - ATTN_QB
    far_pos = jnp.broadcast_to(rel_bias[:, -1:], (H_A, BIAS_ROW))
    far_neg = jnp.broadcast_to(rel_bias[:, :1], (H_A, BIAS_ROW))
    n_mid = 2 * MAX_REL + 1
    n_lo = ATTN_KB + 1 - (WINDOW_A - MAX_REL) - n_mid
    rows = jnp.concatenate([far_pos[:, :WINDOW_A - MAX_REL], rel_bias[:, ::-1], far_neg[:, :n_lo],
                            far_pos[:, :BIAS_ROW - ATTN_KB - 1]], axis=1)
    return rows.astype(F32).reshape(H_A, 1, BIAS_ROW)


def _layer(x, mod, cache, s_gla, s_conv, w, first_chunk):
    nbt, t, _ = x.shape
    if first_chunk:
        nb, tt = 1, 512
    else:
        nb, tt = 512 // t, t
    qa, ka, va, qb, kb, vb, gb, gate_a, gate_b, log_a = _inproj(
        x, mod, w['g_pre_mix'], w['w_main'], w['w_gk1'], w['w_gk2'], w['b_gk'], nb, tt)
    if first_chunk:
        ya = _attn_prompt(qa, ka, va, w['bias_rows'])
        rows = min(WINDOW_A, t)
        k_keep, v_keep = ka[:, t - rows:], va[:, t - rows:]
        yb, s_new = _gla(qb, kb, vb, gb, log_a, w['g_gla'], None, nbt, CHUNK)
    else:
        k_cache, v_cache = cache
        ya = _attn_sample(qa, ka, va, k_cache, v_cache, w['bias_rows'], 2)
        k_keep, v_keep = ka, va
        yb, s_new = _gla(qb, kb, vb, gb, log_a, w['g_gla'], s_gla, 4, t)
    x1 = _mixout(x, ya, yb, gate_a, gate_b, mod, w['g_post_mix'], w['w_br_a'], w['w_br_b'],
                 w['w_out'], nb, tt)
    prev = None if first_chunk else _state_to_prev(s_conv)
    y, tail = _ffn(x1, mod, w['g_pre_ffn'], w['g_post_ffn'], w['w_up'], w['w_down'],
                   w['cw_a'], w['cw_g'], prev, nb, tt)
    heads = lambda a: a.astype(F32).reshape(nbt, a.shape[1], H_A, HD_A)
    return (y, heads(k_keep), heads(v_keep), s_new.reshape(nbt, H_B, DK_B, DV_B),
            _tail_to_state(tail))


def _prep_weights(w_in, w_gk2, b_gk, rel_bias, g_gla, w_br_a, w_br_b, w_out, w_up, w_dw, b_dw,
                  w_down, g_pre_mix, g_post_mix, g_pre_ffn, g_post_ffn):
    w_main, w_gk1 = _split_w_in(w_in, 128)
    w_gk2p = jnp.pad(w_gk2, ((0, LANES - GK_RANK), (0, 0))).astype(BF16)
    w_br_a16, w_br_b16, w_out16, w_down16 = _cast_rows((w_br_a, w_br_b, w_out, w_down), 4)
    split_cols = lambda a: jnp.transpose(a.reshape(a.shape[0], FFN_NF, FFN_FT), (1, 0, 2))
    conv_w = lambda wd, bd: jnp.pad(split_cols(jnp.concatenate([wd, bd[None]], axis=0)),
                                    ((0, 0), (0, SUBLANES - CONV_W - 1), (0, 0)))
    row = lambda a: a.reshape(1, -1)
    return {
        'w_main': w_main, 'w_gk1': w_gk1, 'w_gk2': w_gk2p, 'b_gk': row(b_gk),
        'bias_rows': _bias_rows(rel_bias),
        'g_gla': row(g_gla), 'w_br_a': w_br_a16, 'w_br_b': w_br_b16, 'w_out': w_out16,
        'w_up': _cast_col_blocks(w_up, FFN_FT), 'w_down': w_down16,
        'cw_a': conv_w(w_dw[:, :D_FF], b_dw[:D_FF]), 'cw_g': conv_w(w_dw[:, D_FF:], b_dw[D_FF:]),
        'g_pre_mix': row(g_pre_mix), 'g_post_mix': row(g_post_mix),
        'g_pre_ffn': row(g_pre_ffn), 'g_post_ffn': row(g_post_ffn),
    }


def kernel(x_prompt, x_sample, cache_k_a, cache_v_a, state_gla, state_conv, c_prompt, c_sample, w_ada, b_ada, g_pre_mix, g_post_mix, g_pre_ffn, g_post_ffn, w_in, w_gk2, b_gk, rel_bias, g_gla, w_br_a, w_br_b, w_out, w_up, w_dw, b_dw, w_down):
    depth = w_ada.shape[0]
    assert depth == 1
    bp, bs = x_prompt.shape[0], x_sample.shape[0]
    s_len = x_sample.shape[1]
    cache_rows = cache_k_a.shape[2]
    yp, ys = x_prompt, x_sample
    outs = [[] for _ in range(8)]
    for l in range(depth):
        w = _prep_weights(w_in[l], w_gk2[l], b_gk[l], rel_bias[l], g_gla[l], w_br_a[l], w_br_b[l],
                          w_out[l], w_up[l], w_dw[l], b_dw[l], w_down[l], g_pre_mix[l],
                          g_post_mix[l], g_pre_ffn[l], g_post_ffn[l])
        c_all = jnp.concatenate([c_prompt, c_sample], axis=0)
        pad = (-c_all.shape[0]) % SUBLANES
        mod = _adaln(jnp.pad(c_all, ((0, pad), (0, 0))), w_ada[l], b_ada[l])
        mod_p = mod[:bp].reshape(bp, 1, 6 * D_MODEL)
        mod_s = mod[bp:bp + bs].reshape(bs, 1, 6 * D_MODEL)
        yp, kp, vp, gp, cp = _layer(yp, mod_p, None, None, None, w, True)
        cache = (cache_k_a[l].reshape(bs, cache_rows * H_A, HD_A),
                 cache_v_a[l].reshape(bs, cache_rows * H_A, HD_A))
        s0 = state_gla[l].reshape(bs, H_B // 2, 2 * DK_B, DV_B)
        ys, kn, vn, gn, cn = _layer(ys, mod_s, cache, s0, state_conv[l], w, False)
        for lst, a in zip(outs, (kp, vp, gp, cp, kn, vn, gn, cn)):
            lst.append(a)
    return (yp, ys) + tuple(jnp.stack(lst) for lst in outs)
```

```python
import functools

import jax
import jax.numpy as jnp
import numpy as np
from jax import lax
from jax.experimental import pallas as pl
from jax.experimental.pallas import tpu as pltpu

D_MODEL = 1024
CHUNK = 64
BAND_CHUNKS = 8
WINDOW_A = BAND_CHUNKS * CHUNK
H_A = 8
HD_A = 64
MAX_REL = 128
H_B = 4
DK_B = 64
DV_B = 128
GK_RANK = 16
GK_NORM = 16.0
GLA_SUB = 16
D_FF = 2816
CONV_W = 3
EPS = 1e-6
NEG_INF = -1e30
PAST_LEN = 2048

W_A = H_A * HD_A
W_BK = H_B * DK_B
W_BV = H_B * DV_B

LANES = 128
SUBLANES = 8
VMEM_LIMIT = 56 * 1024 * 1024

ATTN_QB = 256
ATTN_KB = 3 * ATTN_QB
BIAS_ROW = 1024
FFN_FT = 256
FFN_NF = D_FF // FFN_FT

BF16 = jnp.bfloat16
F32 = jnp.float32


def _params(sem):
    return pltpu.CompilerParams(dimension_semantics=sem, vmem_limit_bytes=VMEM_LIMIT)


def _full_spec(shape):
    nd = len(shape)
    return pl.BlockSpec(shape, lambda *_: (0,) * nd, pipeline_mode=pl.Buffered(1))


def _dot(a, b):
    return jnp.dot(a, b, preferred_element_type=F32)


def _dot_nt(a, b):
    return lax.dot_general(a, b, (((1,), (1,)), ((), ())), preferred_element_type=F32)


def _dot_tn(a, b):
    return lax.dot_general(a, b, (((0,), (0,)), ((), ())), preferred_element_type=F32)


def _sigmoid(x):
    return 1.0 / (1.0 + jnp.exp(-x))


def _rms_scale(x):
    return lax.rsqrt(jnp.mean(x * x, axis=-1, keepdims=True) + EPS)


def _cast_kernel(*refs):
    n = len(refs) // 2
    for w_ref, o_ref in zip(refs[:n], refs[n:]):
        o_ref[...] = w_ref[...].astype(o_ref.dtype).reshape(o_ref.shape)


def _cast_rows(ws, steps):
    specs = [pl.BlockSpec((w.shape[0] // steps, w.shape[1]), lambda i: (i, 0)) for w in ws]
    return pl.pallas_call(
        _cast_kernel,
        grid=(steps,),
        in_specs=specs,
        out_specs=specs,
        out_shape=[jax.ShapeDtypeStruct(w.shape, BF16) for w in ws],
        compiler_params=_params(("parallel",)),
        name="cast_rows",
    )(*ws)


def _cast_col_blocks(w, tn):
    k, n = w.shape
    return pl.pallas_call(
        _cast_kernel,
        grid=(n // tn,),
        in_specs=[pl.BlockSpec((k, tn), lambda j: (0, j))],
        out_specs=pl.BlockSpec((1, k, tn), lambda j: (j, 0, 0)),
        out_shape=jax.ShapeDtypeStruct((n // tn, k, tn), BF16),
        compiler_params=_params(("parallel",)),
        name="cast_col_blocks",
    )(w)


_GK_LO = 3 * W_A + 2 * W_BK + 2 * W_BV


def _split_w_in_kernel(w_ref, main_ref, gk_ref):
    main_ref[:, :_GK_LO] = w_ref[:, :_GK_LO].astype(BF16)
    main_ref[:, _GK_LO:] = w_ref[:, _GK_LO + GK_RANK:].astype(BF16)
    lane = lax.broadcasted_iota(jnp.int32, (1, LANES), 1)
    gk_ref[...] = jnp.where(lane < GK_RANK, w_ref[:, _GK_LO:_GK_LO + LANES], 0.0).astype(BF16)


def _split_w_in(w_in, rows):
    d, d_in = w_in.shape
    return pl.pallas_call(
        _split_w_in_kernel,
        grid=(d // rows,),
        in_specs=[pl.BlockSpec((rows, d_in), lambda i: (i, 0))],
        out_specs=[pl.BlockSpec((rows, d_in - GK_RANK), lambda i: (i, 0)),
                   pl.BlockSpec((rows, LANES), lambda i: (i, 0))],
        out_shape=[jax.ShapeDtypeStruct((d, d_in - GK_RANK), BF16),
                   jax.ShapeDtypeStruct((d, LANES), BF16)],
        compiler_params=_params(("parallel",)),
        name="split_w_in",
    )(w_in)


def _adaln_kernel(c_ref, w_ref, b_ref, o_ref):
    c = c_ref[...]
    s = (c * _sigmoid(c)).astype(BF16)
    o_ref[...] = _dot(s, w_ref[...].astype(BF16)) + b_ref[...]


def _adaln(c_all, w_ada, b_ada):
    rows = c_all.shape[0]
    n = w_ada.shape[1]
    tn = D_MODEL
    return pl.pallas_call(
        _adaln_kernel,
        grid=(n // tn,),
        in_specs=[
            pl.BlockSpec((rows, D_MODEL), lambda j: (0, 0)),
            pl.BlockSpec((D_MODEL, tn), lambda j: (0, j)),
            pl.BlockSpec((1, tn), lambda j: (0, j)),
        ],
        out_specs=pl.BlockSpec((rows, tn), lambda j: (0, j)),
        out_shape=jax.ShapeDtypeStruct((rows, n), F32),
        compiler_params=_params(("parallel",)),
        name="adaln",
    )(c_all, w_ada, b_ada.reshape(1, n))


_IN_GROUPS = (W_A, W_A, W_A, W_BK, W_BK, W_BV, W_BV, D_MODEL, D_MODEL)
_IN_MAIN = sum(_IN_GROUPS)


def _inproj_kernel(x_ref, shift_ref, scale_ref, g_ref, wm_ref, wg1_ref, wg2_ref, bgk_ref,
                   qa_ref, ka_ref, va_ref, qb_ref, kb_ref, vb_ref, gb_ref, ga_ref, gtb_ref, la_ref):
    nb, tt, _ = x_ref.shape
    x = x_ref[...]
    h = (x * _rms_scale(x)) * g_ref[...] * (1.0 + scale_ref[...]) + shift_ref[...]
    hb = h.reshape(nb * tt, D_MODEL).astype(BF16)
    outs = (qa_ref, ka_ref, va_ref, qb_ref, kb_ref, vb_ref, gb_ref, ga_ref, gtb_ref)
    lo = 0
    for o_ref, w in zip(outs, _IN_GROUPS):
        z = _dot(hb, wm_ref[:, lo:lo + w])
        o_ref[...] = z.reshape(nb, tt, w).astype(o_ref.dtype)
        lo += w
    gk_low = _dot(hb, wg1_ref[...]).astype(BF16)
    gk = _dot(gk_low, wg2_ref[...]) + bgk_ref[...]
    log_a = (jnp.minimum(gk, 0.0) - jnp.log1p(jnp.exp(-jnp.abs(gk)))) / GK_NORM
    la_ref[...] = log_a.reshape(nb, tt, W_BK)


def _inproj(x, mod, g_pre, wm, wg1, wg2, bgk, nb, tt):
    nbt, t, _ = x.shape
    grid = (nbt // nb, t // tt)
    tok = lambda w: pl.BlockSpec((nb, tt, w), lambda b, i: (b, i, 0))
    mod_spec = lambda col: pl.BlockSpec((nb, 1, D_MODEL), lambda b, i, col=col: (b, 0, col))
    widths = _IN_GROUPS + (W_BK,)
    dtypes = (BF16,) * len(_IN_GROUPS) + (F32,)
    return pl.pallas_call(
        _inproj_kernel,
        grid=grid,
        in_specs=[tok(D_MODEL), mod_spec(0), mod_spec(1), _full_spec((1, D_MODEL)),
                  _full_spec(wm.shape), _full_spec(wg1.shape), _full_spec(wg2.shape),
                  _full_spec((1, W_BK))],
        out_specs=[tok(w) for w in widths],
        out_shape=[jax.ShapeDtypeStruct((nbt, t, w), dt) for w, dt in zip(widths, dtypes)],
        compiler_params=_params(("parallel", "parallel")),
        name="inproj",
    )(x, mod, mod, g_pre, wm, wg1, wg2, bgk)


def _head_masks():
    lane = lax.broadcasted_iota(jnp.int32, (1, LANES), 1)
    first = lane < HD_A
    return first, jnp.logical_not(first)


def _toeplitz_bias(row_ref, h, rows):
    rb = jnp.broadcast_to(row_ref[h], (rows, BIAS_ROW))
    return pltpu.roll(rb, 0, 1, stride=1, stride_axis=0)


ATTN_RB = 32
ATTN_WIN = 640


def _attn_prompt_kernel(q_ref, k0_ref, k1_ref, k2_ref, v0_ref, v1_ref, v2_ref, row_ref, o_ref,
                        bias_ref, s_ref, p_ref):
    i = pl.program_id(1)

    @pl.when((pl.program_id(0) == 0) & (i == 0))
    def _():
        qc = lax.broadcasted_iota(jnp.int32, (ATTN_QB, ATTN_KB), 0) // CHUNK
        col = lax.broadcasted_iota(jnp.int32, (ATTN_QB, ATTN_KB), 1)
        kc = col // CHUNK - BAND_CHUNKS
        valid = (kc <= qc) & (kc >= qc - BAND_CHUNKS)
        for h in range(H_A):
            t = _toeplitz_bias(row_ref, h, ATTN_QB)
            band = jnp.where(valid, t[:, :ATTN_KB], NEG_INF)
            bias_ref[0, h] = jnp.where(col >= 2 * ATTN_QB, band, NEG_INF)
            bias_ref[1, h] = jnp.where(col >= ATTN_QB, band, NEG_INF)
            bias_ref[2, h] = band
        p_ref[...] = jnp.zeros_like(p_ref)

    var = jnp.minimum(i, 2)
    masks = _head_masks()

    def pair_rows(refs, p):
        sl = slice(p * LANES, (p + 1) * LANES)
        return jnp.concatenate([r[0, :, sl] for r in refs], axis=0)

    def scores(h):
        p, hh = divmod(h, 2)
        qp = q_ref[0, :, p * LANES:(p + 1) * LANES] * BF16(HD_A ** -0.5)
        qm = jnp.where(masks[hh], qp, jnp.zeros_like(qp))
        s_ref[h % 2] = _dot_nt(qm, pair_rows((k0_ref, k1_ref, k2_ref), p)) + bias_ref[var, h]

    scores(0)
    o_first = None
    for h in range(H_A):
        if h + 1 < H_A:
            scores(h + 1)
        slot = h % 2
        sums = []
        for r in range(ATTN_QB // ATTN_RB):
            rows = slice(r * ATTN_RB, (r + 1) * ATTN_RB)
            lo = 0 if r * ATTN_RB < ATTN_QB // 2 else ATTN_KB - ATTN_WIN
            sb = s_ref[slot, rows, lo:lo + ATTN_WIN]
            e = jnp.exp(sb - jnp.max(sb, axis=-1, keepdims=True))
            sums.append(jnp.sum(e, axis=-1, keepdims=True))
            p_ref[slot, rows, lo:lo + ATTN_WIN] = e.astype(BF16)
        p, hh = divmod(h, 2)
        o = _dot(p_ref[slot], pair_rows((v0_ref, v1_ref, v2_ref), p)) / jnp.concatenate(sums, axis=0)
        if hh == 0:
            o_first = o
        else:
            o_ref[0, :, p * LANES:(p + 1) * LANES] = jnp.where(masks[0], o_first, o).astype(o_ref.dtype)


def _attn_prompt(q, k, v, bias_rows):
    b, t, _ = q.shape
    blk = lambda off: pl.BlockSpec(
        (1, ATTN_QB, W_A), lambda bb, i, off=off: (bb, jnp.maximum(i - off, 0), 0))
    return pl.pallas_call(
        _attn_prompt_kernel,
        grid=(b, t // ATTN_QB),
        in_specs=[blk(0), blk(2), blk(1), blk(0), blk(2), blk(1), blk(0),
                  _full_spec(bias_rows.shape)],
        out_specs=blk(0),
        out_shape=jax.ShapeDtypeStruct((b, t, W_A), BF16),
        scratch_shapes=[pltpu.VMEM((3, H_A, ATTN_QB, ATTN_KB), F32),
                        pltpu.VMEM((2, ATTN_QB, ATTN_KB), F32),
                        pltpu.VMEM((2, ATTN_QB, ATTN_KB), BF16)],
        compiler_params=_params(("arbitrary", "arbitrary")),
        name="attn_prompt",
    )(q, k, k, k, v, v, v, bias_rows)


def _attn_sample_kernel(q_ref, kn_ref, vn_ref, kc_ref, vc_ref, row_ref, o_ref, bc_ref, bn_ref):
    nb, s_len, _ = q_ref.shape
    w = kc_ref.shape[1]

    @pl.when(pl.program_id(0) == 0)
    def _():
        for h in range(H_A):
            t = _toeplitz_bias(row_ref, h, s_len)
            bc_ref[h * s_len:(h + 1) * s_len, :] = t[:, :w]
            bn_ref[h * s_len:(h + 1) * s_len, :] = t[:, w:w + s_len]

    lane_head = lax.broadcasted_iota(jnp.int32, (1, W_A), 1) // HD_A
    row_head = lax.broadcasted_iota(jnp.int32, (H_A * s_len, 1), 0) // s_len
    own_head = row_head == lane_head
    for b in range(nb):
        q = q_ref[b] * BF16(HD_A ** -0.5)
        qs = jnp.concatenate([q] * H_A, axis=0)
        qs = jnp.where(own_head, qs, jnp.zeros_like(qs))
        sc = _dot_nt(qs, kc_ref[b]) + bc_ref[...]
        sn = _dot_nt(qs, kn_ref[b]) + bn_ref[...]
        m = jnp.maximum(jnp.max(sc, axis=-1, keepdims=True), jnp.max(sn, axis=-1, keepdims=True))
        ec = jnp.exp(sc - m)
        en = jnp.exp(sn - m)
        l = jnp.sum(ec, axis=-1, keepdims=True) + jnp.sum(en, axis=-1, keepdims=True)
        full = (_dot(ec.astype(BF16), vc_ref[b]) + _dot(en.astype(BF16), vn_ref[b])) / l
        o = full[:s_len]
        for h in range(1, H_A):
            o = jnp.where(lane_head == h, full[h * s_len:(h + 1) * s_len], o)
        o_ref[b] = o.astype(o_ref.dtype)


def _attn_sample(q, kn, vn, kc, vc, bias_rows, nb):
    b, s, _ = q.shape
    w = kc.shape[1]
    assert w == WINDOW_A
    new = pl.BlockSpec((nb, s, W_A), lambda i: (i, 0, 0))
    cache = pl.BlockSpec((nb, w, W_A), lambda i: (i, 0, 0))
    return pl.pallas_call(
        _attn_sample_kernel,
        grid=(b // nb,),
        in_specs=[new, new, new, cache, cache, _full_spec(bias_rows.shape)],
        out_specs=new,
        out_shape=jax.ShapeDtypeStruct((b, s, W_A), BF16),
        scratch_shapes=[pltpu.VMEM((H_A * s, w), F32), pltpu.VMEM((H_A * s, s), F32)],
        compiler_params=_params(("arbitrary",)),
        name="attn_sample",
    )(q, kn, vn, kc, vc, bias_rows)


def _prefix_sum_rows(x, row, period):
    d = 1
    while d < period:
        x = x + jnp.where((row & (period - 1)) >= d, pltpu.roll(x, d, 0), 0.0)
        d *= 2
    return x


def _gla_kernel(*refs, has_init):
    if has_init:
        q_ref, k_ref, v_ref, g_ref, la_ref, gg_ref, s0_ref, y_ref, so_ref, st_ref = refs
    else:
        q_ref, k_ref, v_ref, g_ref, la_ref, gg_ref, y_ref, so_ref, st_ref = refs
    nb, c, _ = q_ref.shape
    nsub = c // GLA_SUB
    j = pl.program_id(1)

    @pl.when(j == 0)
    def _():
        if has_init:
            for b in range(nb):
                for p in range(H_B // 2):
                    st_ref[b, p] = s0_ref[b, p].T
        else:
            st_ref[...] = jnp.zeros_like(st_ref)

    row = lax.broadcasted_iota(jnp.int32, (c, W_BK), 0)
    rowp = row[:, :LANES]
    colp = lax.broadcasted_iota(jnp.int32, (c, c), 1)
    tril = colp <= lax.broadcasted_iota(jnp.int32, (c, c), 0)
    masks = _head_masks()
    scale = DK_B ** -0.5

    for b in range(nb):
        la = la_ref[b]
        cum = _prefix_sum_rows(la, row, c)
        cum_sub = _prefix_sum_rows(la, row, GLA_SUB)
        cum_end = cum[c - 1:c, :]
        qf = q_ref[b].astype(F32)
        kf = k_ref[b].astype(F32)
        q_sub = qf * jnp.exp(cum_sub) * scale
        q_in = (qf * jnp.exp(cum) * scale).astype(BF16)
        k_end = (kf * jnp.exp(cum_end - cum)).astype(BF16)
        k_sub = []
        for i in range(nsub):
            ref_i = cum[i * GLA_SUB - 1:i * GLA_SUB, :] if i else jnp.zeros((1, W_BK), F32)
            k_i = jnp.where(row < (i + 1) * GLA_SUB, kf * jnp.exp(ref_i - cum), 0.0)
            k_sub.append(k_i.astype(BF16))
        for p in range(H_B // 2):
            sl = slice(p * LANES, (p + 1) * LANES)
            k_stack = jnp.concatenate([k_i[:, sl] for k_i in k_sub], axis=1)
            st_p = st_ref[b, p]
            st_pb = st_p.astype(BF16)
            upd = jnp.zeros((DV_B, LANES), F32)
            for hh in range(2):
                h = 2 * p + hh
                hs = slice(h * DV_B, (h + 1) * DV_B)
                q_m = jnp.where(masks[hh], q_sub[:, sl], 0.0)
                q_stack = jnp.concatenate(
                    [jnp.where((rowp >= i * GLA_SUB) & (rowp < (i + 1) * GLA_SUB), q_m, 0.0)
                     for i in range(nsub)], axis=1).astype(BF16)
                att = jnp.where(tril, _dot_nt(q_stack, k_stack), 0.0)
                v_h = v_ref[b, :, hs]
                q_im = jnp.where(masks[hh], q_in[:, sl], jnp.zeros((), BF16))
                o = _dot(att.astype(BF16), v_h) + _dot_nt(q_im, st_pb)
                gate = g_ref[b, :, hs].astype(F32)
                y = (o * _rms_scale(o)) * gg_ref[...] * (gate * _sigmoid(gate))
                y_ref[b, :, hs] = y.astype(y_ref.dtype)
                k_em = jnp.where(masks[hh], k_end[:, sl], jnp.zeros((), BF16))
                upd = upd + _dot_tn(v_h, k_em)
            st_ref[b, p] = st_p * jnp.exp(cum_end[:, sl]) + upd

    @pl.when(j == pl.num_programs(1) - 1)
    def _():
        for b in range(nb):
            for p in range(H_B // 2):
                so_ref[b, p] = st_ref[b, p].T


def _gla(q, k, v, g, la, g_gla, s0, nb, c):
    nbt, t, _ = q.shape
    tok = lambda w: pl.BlockSpec((nb, c, w), lambda b, j: (b, j, 0))
    st_spec = pl.BlockSpec((nb, H_B // 2, 2 * DK_B, DV_B), lambda b, j: (b, 0, 0, 0))
    in_specs = [tok(W_BK), tok(W_BK), tok(W_BV), tok(W_BV), tok(W_BK), _full_spec((1, DV_B))]
    args = [q, k, v, g, la, g_gla]
    if s0 is not None:
        in_specs.append(st_spec)
        args.append(s0)
    return pl.pallas_call(
        functools.partial(_gla_kernel, has_init=s0 is not None),
        grid=(nbt // nb, t // c),
        in_specs=in_specs,
        out_specs=[tok(W_BV), st_spec],
        out_shape=[jax.ShapeDtypeStruct((nbt, t, W_BV), BF16),
                   jax.ShapeDtypeStruct((nbt, H_B // 2, 2 * DK_B, DV_B), F32)],
        scratch_shapes=[pltpu.VMEM((nb, H_B // 2, DV_B, 2 * DK_B), F32)],
        compiler_params=_params(("parallel", "arbitrary")),
        name="gla",
    )(*args)


def _mixout_kernel(x_ref, ya_ref, yb_ref, ga_ref, gb_ref, gm_ref, gp_ref, wa_ref, wb_ref, wo_ref,
                   o_ref):
    nb, tt, _ = x_ref.shape
    m = nb * tt
    a = _dot(ya_ref[...].reshape(m, W_A), wa_ref[...])
    b = _dot(yb_ref[...].reshape(m, W_BV), wb_ref[...])
    ga = _sigmoid(ga_ref[...].reshape(m, D_MODEL).astype(F32))
    gb = _sigmoid(gb_ref[...].reshape(m, D_MODEL).astype(F32))
    merged = (ga * a + gb * b).astype(BF16)
    mo = _dot(merged, wo_ref[...])
    n = ((mo * _rms_scale(mo)) * gp_ref[...]).reshape(nb, tt, D_MODEL)
    o_ref[...] = x_ref[...] + gm_ref[...] * n


def _mixout(x, ya, yb, ga, gb, mod, g_post, wa, wb, wo, nb, tt):
    nbt, t, _ = x.shape
    tok = lambda w: pl.BlockSpec((nb, tt, w), lambda b, i: (b, i, 0))
    return pl.pallas_call(
        _mixout_kernel,
        grid=(nbt // nb, t // tt),
        in_specs=[tok(D_MODEL), tok(W_A), tok(W_BV), tok(D_MODEL), tok(D_MODEL),
                  pl.BlockSpec((nb, 1, D_MODEL), lambda b, i: (b, 0, 2)),
                  _full_spec((1, D_MODEL)), _full_spec(wa.shape), _full_spec(wb.shape),
                  _full_spec(wo.shape)],
        out_specs=tok(D_MODEL),
        out_shape=jax.ShapeDtypeStruct(x.shape, F32),
        compiler_params=_params(("parallel", "parallel")),
        name="mixout",
    )(x, ya, yb, ga, gb, mod, g_post, wa, wb, wo)


def _gelu_tanh(x):
    c = float(np.sqrt(2.0 / np.pi))
    half = 0.5 * x
    return half + half * jnp.tanh(x * (c + (0.044715 * c) * (x * x)))


def _ffn_kernel(*refs, has_state):
    if has_state:
        (x_ref, shift_ref, scale_ref, gate_ref, gpre_ref, gpost_ref, wu_ref, wd_ref,
         cwa_ref, cwg_ref, st_ref, o_ref, tail_ref, h_ref, act_ref) = refs
        prev_ref = st_ref
    else:
        (x_ref, shift_ref, scale_ref, gate_ref, gpre_ref, gpost_ref, wu_ref, wd_ref,
         cwa_ref, cwg_ref, o_ref, tail_ref, h_ref, act_ref, carry_ref) = refs
        prev_ref = carry_ref

        @pl.when(pl.program_id(1) == 0)
        def _():
            carry_ref[...] = jnp.zeros_like(carry_ref)

    nb, tt, _ = x_ref.shape
    m = nb * tt
    x = x_ref[...]
    h = (x * _rms_scale(x)) * gpre_ref[...] * (1.0 + scale_ref[...]) + shift_ref[...]
    h_ref[...] = h.reshape(m, D_MODEL).astype(BF16)
    ridx = lax.broadcasted_iota(jnp.int32, (nb, SUBLANES, FFN_FT), 1)

    def conv(u, prev, cw):
        u3 = u.reshape(nb, tt, FFN_FT)
        r1 = pltpu.roll(u, 1, 0).reshape(nb, tt, FFN_FT)
        r2 = pltpu.roll(u, 2, 0).reshape(nb, tt, FFN_FT)

        def taps(u_m2, u_m1, u_0):
            y = cw[3:4, :] + cw[0:1, :] * u_m2
            y = y + cw[1:2, :] * u_m1
            return y + cw[2:3, :] * u_0

        p1 = prev[:, SUBLANES - 1:SUBLANES, :]
        p2 = prev[:, SUBLANES - 2:SUBLANES - 1, :]
        h_m1 = jnp.where(ridx == 0, p1, r1[:, :SUBLANES])
        h_m2 = jnp.where(ridx == 0, p2, jnp.where(ridx == 1, p1, r2[:, :SUBLANES]))
        y = jnp.concatenate([taps(h_m2, h_m1, u3[:, :SUBLANES]),
                             taps(r2[:, SUBLANES:], r1[:, SUBLANES:], u3[:, SUBLANES:])], axis=1)
        return y.reshape(m, FFN_FT), u3[:, tt - SUBLANES:, :]

    def up(f):
        hb = h_ref[...]
        return _dot(hb, wu_ref[f]), _dot(hb, wu_ref[FFN_NF + f])

    ua, ug = up(0)
    for f in range(FFN_NF):
        if f + 1 < FFN_NF:
            ua_next, ug_next = up(f + 1)
        ya, ta = conv(ua, prev_ref[0, f], cwa_ref[f])
        yg, tg = conv(ug, prev_ref[1, f], cwg_ref[f])
        act_ref[:, f * FFN_FT:(f + 1) * FFN_FT] = (_gelu_tanh(ya) * yg).astype(BF16)
        tail_ref[0, f] = ta
        tail_ref[1, f] = tg
        if not has_state:
            carry_ref[0, f] = ta
            carry_ref[1, f] = tg
        if f + 1 < FFN_NF:
            ua, ug = ua_next, ug_next
    yf = _dot(act_ref[...], wd_ref[...])
    n = ((yf * _rms_scale(yf)) * gpost_ref[...]).reshape(nb, tt, D_MODEL)
    o_ref[...] = x_ref[...] + gate_ref[...] * n


def _ffn(x, mod, g_pre, g_post, wu, wd, cwa, cwg, state, nb, tt):
    nbt, t, _ = x.shape
    m = nb * tt
    tok = pl.BlockSpec((nb, tt, D_MODEL), lambda b, i: (b, i, 0))
    mod_spec = lambda col: pl.BlockSpec((nb, 1, D_MODEL), lambda b, i, col=col: (b, 0, col))
    tail_spec = pl.BlockSpec((2, FFN_NF, nb, SUBLANES, FFN_FT), lambda b, i: (0, 0, b, 0, 0))
    in_specs = [tok, mod_spec(3), mod_spec(4), mod_spec(5), _full_spec((1, D_MODEL)),
                _full_spec((1, D_MODEL)), _full_spec(wu.shape),
                _full_spec(wd.shape), _full_spec(cwa.shape), _full_spec(cwg.shape)]
    args = [x, mod, mod, mod, g_pre, g_post, wu, wd, cwa, cwg]
    scratch = [pltpu.VMEM((m, D_MODEL), BF16), pltpu.VMEM((m, D_FF), BF16)]
    if state is not None:
        in_specs.append(tail_spec)
        args.append(state)
    else:
        scratch.append(pltpu.VMEM((2, FFN_NF, nb, SUBLANES, FFN_FT), F32))
    return pl.pallas_call(
        functools.partial(_ffn_kernel, has_state=state is not None),
        grid=(nbt // nb, t // tt),
        in_specs=in_specs,
        out_specs=[tok, tail_spec],
        out_shape=[jax.ShapeDtypeStruct(x.shape, F32),
                   jax.ShapeDtypeStruct((2, FFN_NF, nbt, SUBLANES, FFN_FT), F32)],
        scratch_shapes=scratch,
        compiler_params=_params(("parallel", "arbitrary")),
        name="ffn",
    )(*args)


def _tail_to_state(tail):
    nbt = tail.shape[2]
    t = jnp.transpose(tail, (2, 3, 0, 1, 4)).reshape(nbt, SUBLANES, 2 * D_FF)
    return t[:, SUBLANES - (CONV_W - 1):, :]


def _state_to_prev(state):
    nbt = state.shape[0]
    s = jnp.pad(state, ((0, 0), (SUBLANES - (CONV_W - 1), 0), (0, 0)))
    s = s.reshape(nbt, SUBLANES, 2, FFN_NF, FFN_FT)
    return jnp.transpose(s, (2, 3, 0, 1, 4))


def _bias_rows(rel_bias):
    assert BIAS_ROW >= ATTN_KB + ATTN_QB - 1 and WINDOW_A == ATTN_KB - ATTN_QB
    far_pos = jnp.broadcast_to(rel_bias[:, -1:], (H_A, BIAS_ROW))
    far_neg = jnp.broadcast_to(rel_bias[:, :1], (H_A, BIAS_ROW))
    n_mid = 2 * MAX_REL + 1
    n_lo = ATTN_KB + 1 - (WINDOW_A - MAX_REL) - n_mid
    rows = jnp.concatenate([far_pos[:, :WINDOW_A - MAX_REL], rel_bias[:, ::-1], far_neg[:, :n_lo],
                            far_pos[:, :BIAS_ROW - ATTN_KB - 1]], axis=1)
    return rows.astype(F32).reshape(H_A, 1, BIAS_ROW)


def _layer(x, mod, cache, s_gla, s_conv, w, first_chunk):
    nbt, t, _ = x.shape
    if first_chunk:
        nb, tt = 1, 512
    else:
        nb, tt = 512 // t, t
    qa, ka, va, qb, kb, vb, gb, gate_a, gate_b, log_a = _inproj(
        x, mod, w['g_pre_mix'], w['w_main'], w['w_gk1'], w['w_gk2'], w['b_gk'], nb, tt)
    if first_chunk:
        ya = _attn_prompt(qa, ka, va, w['bias_rows'])
        rows = min(WINDOW_A, t)
        k_keep, v_keep = ka[:, t - rows:], va[:, t - rows:]
        yb, s_new = _gla(qb, kb, vb, gb, log_a, w['g_gla'], None, nbt, CHUNK)
    else:
        k_cache, v_cache = cache
        ya = _attn_sample(qa, ka, va, k_cache, v_cache, w['bias_rows'], 4)
        k_keep, v_keep = ka, va
        yb, s_new = _gla(qb, kb, vb, gb, log_a, w['g_gla'], s_gla, 4, t)
    x1 = _mixout(x, ya, yb, gate_a, gate_b, mod, w['g_post_mix'], w['w_br_a'], w['w_br_b'],
                 w['w_out'], nb, tt)
    prev = None if first_chunk else _state_to_prev(s_conv)
    y, tail = _ffn(x1, mod, w['g_pre_ffn'], w['g_post_ffn'], w['w_up'], w['w_down'],
                   w['cw_a'], w['cw_g'], prev, nb, tt)
    heads = lambda a: a.astype(F32).reshape(nbt, a.shape[1], H_A, HD_A)
    return (y, heads(k_keep), heads(v_keep), s_new.reshape(nbt, H_B, DK_B, DV_B),
            _tail_to_state(tail))


def _prep_weights(w_in, w_gk2, b_gk, rel_bias, g_gla, w_br_a, w_br_b, w_out, w_up, w_dw, b_dw,
                  w_down, g_pre_mix, g_post_mix, g_pre_ffn, g_post_ffn):
    w_main, w_gk1 = _split_w_in(w_in, 128)
    w_gk2p = jnp.pad(w_gk2, ((0, LANES - GK_RANK), (0, 0))).astype(BF16)
    w_br_a16, w_br_b16, w_out16, w_down16 = _cast_rows((w_br_a, w_br_b, w_out, w_down), 4)
    split_cols = lambda a: jnp.transpose(a.reshape(a.shape[0], FFN_NF, FFN_FT), (1, 0, 2))
    conv_w = lambda wd, bd: jnp.pad(split_cols(jnp.concatenate([wd, bd[None]], axis=0)),
                                    ((0, 0), (0, SUBLANES - CONV_W - 1), (0, 0)))
    row = lambda a: a.reshape(1, -1)
    return {
        'w_main': w_main, 'w_gk1': w_gk1, 'w_gk2': w_gk2p, 'b_gk': row(b_gk),
        'bias_rows': _bias_rows(rel_bias),
        'g_gla': row(g_gla), 'w_br_a': w_br_a16, 'w_br_b': w_br_b16, 'w_out': w_out16,
        'w_up': _cast_col_blocks(w_up, FFN_FT), 'w_down': w_down16,
        'cw_a': conv_w(w_dw[:, :D_FF], b_dw[:D_FF]), 'cw_g': conv_w(w_dw[:, D_FF:], b_dw[D_FF:]),
        'g_pre_mix': row(g_pre_mix), 'g_post_mix': row(g_post_mix),
        'g_pre_ffn': row(g_pre_ffn), 'g_post_ffn': row(g_post_ffn),
    }


def kernel(x_prompt, x_sample, cache_k_a, cache_v_a, state_gla, state_conv, c_prompt, c_sample, w_ada, b_ada, g_pre_mix, g_post_mix, g_pre_ffn, g_post_ffn, w_in, w_gk2, b_gk, rel_bias, g_gla, w_br_a, w_br_b, w_out, w_up, w_dw, b_dw, w_down):
    depth = w_ada.shape[0]
    assert depth == 1
    bp, bs = x_prompt.shape[0], x_sample.shape[0]
    s_len = x_sample.shape[1]
    cache_rows = cache_k_a.shape[2]
    yp, ys = x_prompt, x_sample
    outs = [[] for _ in range(8)]
    for l in range(depth):
        w = _prep_weights(w_in[l], w_gk2[l], b_gk[l], rel_bias[l], g_gla[l], w_br_a[l], w_br_b[l],
                          w_out[l], w_up[l], w_dw[l], b_dw[l], w_down[l], g_pre_mix[l],
                          g_post_mix[l], g_pre_ffn[l], g_post_ffn[l])
        c_all = jnp.concatenate([c_prompt, c_sample], axis=0)
        pad = (-c_all.shape[0]) % SUBLANES
        mod = _adaln(jnp.pad(c_all, ((0, pad), (0, 0))), w_ada[l], b_ada[l])
        mod_p = mod[:bp].reshape(bp, 1, 6 * D_MODEL)
        mod_s = mod[bp:bp + bs].reshape(bs, 1, 6 * D_MODEL)
        yp, kp, vp, gp, cp = _layer(yp, mod_p, None, None, None, w, True)
        cache = (cache_k_a[l].reshape(bs, cache_rows, W_A).astype(BF16),
                 cache_v_a[l].reshape(bs, cache_rows, W_A).astype(BF16))
        s0 = state_gla[l].reshape(bs, H_B // 2, 2 * DK_B, DV_B)
        ys, kn, vn, gn, cn = _layer(ys, mod_s, cache, s0, state_conv[l], w, False)
        for lst, a in zip(outs, (kp, vp, gp, cp, kn, vn, gn, cn)):
            lst.append(a)
    return (yp, ys) + tuple(jnp.stack(lst) for lst in outs)
```

```python
import functools

import jax
import jax.numpy as jnp
import numpy as np
from jax import lax
from jax.experimental import pallas as pl
from jax.experimental.pallas import tpu as pltpu

D_MODEL = 1024
CHUNK = 64
BAND_CHUNKS = 8
WINDOW_A = BAND_CHUNKS * CHUNK
H_A = 8
HD_A = 64
MAX_REL = 128
H_B = 4
DK_B = 64
DV_B = 128
GK_RANK = 16
GK_NORM = 16.0
GLA_SUB = 16
D_FF = 2816
CONV_W = 3
EPS = 1e-6
NEG_INF = -1e30
PAST_LEN = 2048

W_A = H_A * HD_A
W_BK = H_B * DK_B
W_BV = H_B * DV_B

LANES = 128
SUBLANES = 8
VMEM_LIMIT = 56 * 1024 * 1024

ATTN_QB = 256
ATTN_KB = 3 * ATTN_QB
BIAS_ROW = 1024
FFN_FT = 256
FFN_NF = D_FF // FFN_FT

BF16 = jnp.bfloat16
F32 = jnp.float32


def _params(sem):
    return pltpu.CompilerParams(dimension_semantics=sem, vmem_limit_bytes=VMEM_LIMIT)


def _full_spec(shape):
    nd = len(shape)
    return pl.BlockSpec(shape, lambda *_: (0,) * nd, pipeline_mode=pl.Buffered(1))


def _dot(a, b):
    return jnp.dot(a, b, preferred_element_type=F32)


def _dot_nt(a, b):
    return lax.dot_general(a, b, (((1,), (1,)), ((), ())), preferred_element_type=F32)


def _dot_tn(a, b):
    return lax.dot_general(a, b, (((0,), (0,)), ((), ())), preferred_element_type=F32)


def _sigmoid(x):
    return 1.0 / (1.0 + jnp.exp(-x))


def _rms_scale(x):
    return lax.rsqrt(jnp.mean(x * x, axis=-1, keepdims=True) + EPS)


def _cast_kernel(*refs):
    n = len(refs) // 2
    for w_ref, o_ref in zip(refs[:n], refs[n:]):
        o_ref[...] = w_ref[...].astype(o_ref.dtype).reshape(o_ref.shape)


def _cast_rows(ws, steps):
    specs = [pl.BlockSpec((w.shape[0] // steps, w.shape[1]), lambda i: (i, 0)) for w in ws]
    return pl.pallas_call(
        _cast_kernel,
        grid=(steps,),
        in_specs=specs,
        out_specs=specs,
        out_shape=[jax.ShapeDtypeStruct(w.shape, BF16) for w in ws],
        compiler_params=_params(("parallel",)),
        name="cast_rows",
    )(*ws)


def _cast_col_blocks(w, tn):
    k, n = w.shape
    return pl.pallas_call(
        _cast_kernel,
        grid=(n // tn,),
        in_specs=[pl.BlockSpec((k, tn), lambda j: (0, j))],
        out_specs=pl.BlockSpec((1, k, tn), lambda j: (j, 0, 0)),
        out_shape=jax.ShapeDtypeStruct((n // tn, k, tn), BF16),
        compiler_params=_params(("parallel",)),
        name="cast_col_blocks",
    )(w)


_GK_LO = 3 * W_A + 2 * W_BK + 2 * W_BV


def _split_w_in_kernel(w_ref, main_ref, gk_ref):
    main_ref[:_GK_LO, :] = w_ref[:_GK_LO, :].astype(BF16)
    main_ref[_GK_LO:, :] = w_ref[_GK_LO + GK_RANK:, :].astype(BF16)
    gk_ref[:GK_RANK, :] = w_ref[_GK_LO:_GK_LO + GK_RANK, :].astype(BF16)
    gk_ref[GK_RANK:, :] = jnp.zeros((LANES - GK_RANK, gk_ref.shape[1]), BF16)


def _split_w_in(w_in_t, cols):
    d_in, d = w_in_t.shape
    return pl.pallas_call(
        _split_w_in_kernel,
        grid=(d // cols,),
        in_specs=[pl.BlockSpec((d_in, cols), lambda i: (0, i))],
        out_specs=[pl.BlockSpec((d_in - GK_RANK, cols), lambda i: (0, i)),
                   pl.BlockSpec((LANES, cols), lambda i: (0, i))],
        out_shape=[jax.ShapeDtypeStruct((d_in - GK_RANK, d), BF16),
                   jax.ShapeDtypeStruct((LANES, d), BF16)],
        compiler_params=_params(("parallel",)),
        name="split_w_in",
    )(w_in_t)


def _adaln_kernel(c_ref, w_ref, b_ref, o_ref):
    c = c_ref[...]
    s = (c * _sigmoid(c)).astype(BF16)
    o_ref[...] = _dot(s, w_ref[...].astype(BF16)) + b_ref[...]


def _adaln(c_all, w_ada, b_ada):
    rows = c_all.shape[0]
    n = w_ada.shape[1]
    tn = D_MODEL
    return pl.pallas_call(
        _adaln_kernel,
        grid=(n // tn,),
        in_specs=[
            pl.BlockSpec((rows, D_MODEL), lambda j: (0, 0)),
            pl.BlockSpec((D_MODEL, tn), lambda j: (0, j)),
            pl.BlockSpec((1, tn), lambda j: (0, j)),
        ],
        out_specs=pl.BlockSpec((rows, tn), lambda j: (0, j)),
        out_shape=jax.ShapeDtypeStruct((rows, n), F32),
        compiler_params=_params(("parallel",)),
        name="adaln",
    )(c_all, w_ada, b_ada.reshape(1, n))


_IN_GROUPS = (W_A, W_A, W_A, W_BK, W_BK, W_BV, W_BV, D_MODEL, D_MODEL)
_IN_MAIN = sum(_IN_GROUPS)


def _inproj_kernel(x_ref, shift_ref, scale_ref, g_ref, wm_ref, wg1_ref, wg2_ref, bgk_ref,
                   qa_ref, ka_ref, va_ref, qb_ref, kb_ref, vb_ref, gb_ref, ga_ref, gtb_ref, la_ref):
    nb, tt, _ = x_ref.shape
    x = x_ref[...]
    h = (x * _rms_scale(x)) * g_ref[...] * (1.0 + scale_ref[...]) + shift_ref[...]
    hb = h.reshape(nb * tt, D_MODEL).astype(BF16)
    outs = (qa_ref, ka_ref, va_ref, qb_ref, kb_ref, vb_ref, gb_ref, ga_ref, gtb_ref)
    lo = 0
    for o_ref, w in zip(outs, _IN_GROUPS):
        z = _dot_nt(hb, wm_ref[lo:lo + w, :])
        o_ref[...] = z.reshape(nb, tt, w).astype(o_ref.dtype)
        lo += w
    gk_low = _dot_nt(hb, wg1_ref[...]).astype(BF16)
    gk = _dot(gk_low, wg2_ref[...]) + bgk_ref[...]
    log_a = (jnp.minimum(gk, 0.0) - jnp.log1p(jnp.exp(-jnp.abs(gk)))) / GK_NORM
    la_ref[...] = log_a.reshape(nb, tt, W_BK)


def _inproj(x, mod, g_pre, wm, wg1, wg2, bgk, nb, tt):
    nbt, t, _ = x.shape
    grid = (nbt // nb, t // tt)
    tok = lambda w: pl.BlockSpec((nb, tt, w), lambda b, i: (b, i, 0))
    mod_spec = lambda col: pl.BlockSpec((nb, 1, D_MODEL), lambda b, i, col=col: (b, 0, col))
    widths = _IN_GROUPS + (W_BK,)
    dtypes = (BF16,) * len(_IN_GROUPS) + (F32,)
    return pl.pallas_call(
        _inproj_kernel,
        grid=grid,
        in_specs=[tok(D_MODEL), mod_spec(0), mod_spec(1), _full_spec((1, D_MODEL)),
                  _full_spec(wm.shape), _full_spec(wg1.shape), _full_spec(wg2.shape),
                  _full_spec((1, W_BK))],
        out_specs=[tok(w) for w in widths],
        out_shape=[jax.ShapeDtypeStruct((nbt, t, w), dt) for w, dt in zip(widths, dtypes)],
        compiler_params=_params(("parallel", "parallel")),
        name="inproj",
    )(x, mod, mod, g_pre, wm, wg1, wg2, bgk)


def _head_masks():
    lane = lax.broadcasted_iota(jnp.int32, (1, LANES), 1)
    first = lane < HD_A
    return first, jnp.logical_not(first)


def _toeplitz_bias(row_ref, h, rows):
    rb = jnp.broadcast_to(row_ref[h], (rows, BIAS_ROW))
    return pltpu.roll(rb, 0, 1, stride=1, stride_axis=0)


ATTN_RB = 32
ATTN_WIN = 640


def _attn_prompt_kernel(q_ref, k0_ref, k1_ref, k2_ref, v0_ref, v1_ref, v2_ref, row_ref, o_ref,
                        bias_ref, s_ref, p_ref):
    i = pl.program_id(1)

    @pl.when((pl.program_id(0) == 0) & (i == 0))
    def _():
        qc = lax.broadcasted_iota(jnp.int32, (ATTN_QB, ATTN_KB), 0) // CHUNK
        col = lax.broadcasted_iota(jnp.int32, (ATTN_QB, ATTN_KB), 1)
        kc = col // CHUNK - BAND_CHUNKS
        valid = (kc <= qc) & (kc >= qc - BAND_CHUNKS)
        for h in range(H_A):
            t = _toeplitz_bias(row_ref, h, ATTN_QB)
            band = jnp.where(valid, t[:, :ATTN_KB], NEG_INF)
            bias_ref[0, h] = jnp.where(col >= 2 * ATTN_QB, band, NEG_INF)
            bias_ref[1, h] = jnp.where(col >= ATTN_QB, band, NEG_INF)
            bias_ref[2, h] = band
        p_ref[...] = jnp.zeros_like(p_ref)

    var = jnp.minimum(i, 2)
    masks = _head_masks()

    def pair_rows(refs, p):
        sl = slice(p * LANES, (p + 1) * LANES)
        return jnp.concatenate([r[0, :, sl] for r in refs], axis=0)

    def scores(h):
        p, hh = divmod(h, 2)
        qp = q_ref[0, :, p * LANES:(p + 1) * LANES] * BF16(HD_A ** -0.5)
        qm = jnp.where(masks[hh], qp, jnp.zeros_like(qp))
        s_ref[h % 2] = _dot_nt(qm, pair_rows((k0_ref, k1_ref, k2_ref), p)) + bias_ref[var, h]

    scores(0)
    o_first = None
    for h in range(H_A):
        if h + 1 < H_A:
            scores(h + 1)
        slot = h % 2
        sums = []
        for r in range(ATTN_QB // ATTN_RB):
            rows = slice(r * ATTN_RB, (r + 1) * ATTN_RB)
            lo = 0 if r * ATTN_RB < ATTN_QB // 2 else ATTN_KB - ATTN_WIN
            sb = s_ref[slot, rows, lo:lo + ATTN_WIN]
            e = jnp.exp(sb - jnp.max(sb, axis=-1, keepdims=True))
            sums.append(jnp.sum(e, axis=-1, keepdims=True))
            p_ref[slot, rows, lo:lo + ATTN_WIN] = e.astype(BF16)
        p, hh = divmod(h, 2)
        o = _dot(p_ref[slot], pair_rows((v0_ref, v1_ref, v2_ref), p)) / jnp.concatenate(sums, axis=0)
        if hh == 0:
            o_first = o
        else:
            o_ref[0, :, p * LANES:(p + 1) * LANES] = jnp.where(masks[0], o_first, o).astype(o_ref.dtype)


def _attn_prompt(q, k, v, bias_rows):
    b, t, _ = q.shape
    blk = lambda off: pl.BlockSpec(
        (1, ATTN_QB, W_A), lambda bb, i, off=off: (bb, jnp.maximum(i - off, 0), 0))
    return pl.pallas_call(
        _attn_prompt_kernel,
        grid=(b, t // ATTN_QB),
        in_specs=[blk(0), blk(2), blk(1), blk(0), blk(2), blk(1), blk(0),
                  _full_spec(bias_rows.shape)],
        out_specs=blk(0),
        out_shape=jax.ShapeDtypeStruct((b, t, W_A), BF16),
        scratch_shapes=[pltpu.VMEM((3, H_A, ATTN_QB, ATTN_KB), F32),
                        pltpu.VMEM((2, ATTN_QB, ATTN_KB), F32),
                        pltpu.VMEM((2, ATTN_QB, ATTN_KB), BF16)],
        compiler_params=_params(("arbitrary", "arbitrary")),
        name="attn_prompt",
    )(q, k, k, k, v, v, v, bias_rows)


def _attn_sample_kernel(q_ref, kn_ref, vn_ref, kc_ref, vc_ref, row_ref, o_ref, bc_ref, bn_ref):
    nb, s_len, _ = q_ref.shape
    w = kc_ref.shape[2]

    @pl.when(pl.program_id(0) == 0)
    def _():
        for h in range(H_A):
            t = _toeplitz_bias(row_ref, h, s_len)
            bc_ref[h * s_len:(h + 1) * s_len, :] = t[:, :w]
            bn_ref[h * s_len:(h + 1) * s_len, :] = t[:, w:w + s_len]

    lane_head = lax.broadcasted_iota(jnp.int32, (1, W_A), 1) // HD_A
    row_head = lax.broadcasted_iota(jnp.int32, (H_A * s_len, 1), 0) // s_len
    own_head = row_head == lane_head
    for b in range(nb):
        q = q_ref[b] * BF16(HD_A ** -0.5)
        qs = jnp.concatenate([q] * H_A, axis=0)
        qs = jnp.where(own_head, qs, jnp.zeros_like(qs))
        sc = _dot(qs, kc_ref[b].astype(BF16)) + bc_ref[...]
        sn = _dot_nt(qs, kn_ref[b]) + bn_ref[...]
        m = jnp.maximum(jnp.max(sc, axis=-1, keepdims=True), jnp.max(sn, axis=-1, keepdims=True))
        ec = jnp.exp(sc - m)
        en = jnp.exp(sn - m)
        l = jnp.sum(ec, axis=-1, keepdims=True) + jnp.sum(en, axis=-1, keepdims=True)
        full = (_dot_nt(ec.astype(BF16), vc_ref[b].astype(BF16))
                + _dot(en.astype(BF16), vn_ref[b])) / l
        o = full[:s_len]
        for h in range(1, H_A):
            o = jnp.where(lane_head == h, full[h * s_len:(h + 1) * s_len], o)
        o_ref[b] = o.astype(o_ref.dtype)


def _attn_sample(q, kn, vn, kc, vc, bias_rows, nb):
    b, s, _ = q.shape
    w = kc.shape[2]
    assert w == WINDOW_A
    new = pl.BlockSpec((nb, s, W_A), lambda i: (i, 0, 0))
    cache = pl.BlockSpec((nb, W_A, w), lambda i: (i, 0, 0))
    return pl.pallas_call(
        _attn_sample_kernel,
        grid=(b // nb,),
        in_specs=[new, new, new, cache, cache, _full_spec(bias_rows.shape)],
        out_specs=new,
        out_shape=jax.ShapeDtypeStruct((b, s, W_A), BF16),
        scratch_shapes=[pltpu.VMEM((H_A * s, w), F32), pltpu.VMEM((H_A * s, s), F32)],
        compiler_params=_params(("arbitrary",)),
        name="attn_sample",
    )(q, kn, vn, kc, vc, bias_rows)


def _prefix_sum_rows(x, row, period):
    d = 1
    while d < period:
        x = x + jnp.where((row & (period - 1)) >= d, pltpu.roll(x, d, 0), 0.0)
        d *= 2
    return x


def _gla_kernel(*refs, has_init):
    if has_init:
        q_ref, k_ref, v_ref, g_ref, la_ref, gg_ref, s0_ref, y_ref, so_ref, st_ref = refs
    else:
        q_ref, k_ref, v_ref, g_ref, la_ref, gg_ref, y_ref, so_ref, st_ref = refs
    nb, c, _ = q_ref.shape
    nsub = c // GLA_SUB
    j = pl.program_id(1)

    @pl.when(j == 0)
    def _():
        if has_init:
            for b in range(nb):
                for p in range(H_B // 2):
                    st_ref[b, p] = s0_ref[b, p].T
        else:
            st_ref[...] = jnp.zeros_like(st_ref)

    row = lax.broadcasted_iota(jnp.int32, (c, W_BK), 0)
    rowp = row[:, :LANES]
    colp = lax.broadcasted_iota(jnp.int32, (c, c), 1)
    tril = colp <= lax.broadcasted_iota(jnp.int32, (c, c), 0)
    masks = _head_masks()
    scale = DK_B ** -0.5

    for b in range(nb):
        la = la_ref[b]
        cum = _prefix_sum_rows(la, row, c)
        cum_sub = _prefix_sum_rows(la, row, GLA_SUB)
        cum_end = cum[c - 1:c, :]
        qf = q_ref[b].astype(F32)
        kf = k_ref[b].astype(F32)
        q_sub = qf * jnp.exp(cum_sub) * scale
        q_in = (qf * jnp.exp(cum) * scale).astype(BF16)
        k_end = (kf * jnp.exp(cum_end - cum)).astype(BF16)
        k_sub = []
        for i in range(nsub):
            ref_i = cum[i * GLA_SUB - 1:i * GLA_SUB, :] if i else jnp.zeros((1, W_BK), F32)
            k_i = jnp.where(row < (i + 1) * GLA_SUB, kf * jnp.exp(ref_i - cum), 0.0)
            k_sub.append(k_i.astype(BF16))
        for p in range(H_B // 2):
            sl = slice(p * LANES, (p + 1) * LANES)
            k_stack = jnp.concatenate([k_i[:, sl] for k_i in k_sub], axis=1)
            st_p = st_ref[b, p]
            st_pb = st_p.astype(BF16)
            upd = jnp.zeros((DV_B, LANES), F32)
            for hh in range(2):
                h = 2 * p + hh
                hs = slice(h * DV_B, (h + 1) * DV_B)
                q_m = jnp.where(masks[hh], q_sub[:, sl], 0.0)
                q_stack = jnp.concatenate(
                    [jnp.where((rowp >= i * GLA_SUB) & (rowp < (i + 1) * GLA_SUB), q_m, 0.0)
                     for i in range(nsub)], axis=1).astype(BF16)
                att = jnp.where(tril, _dot_nt(q_stack, k_stack), 0.0)
                v_h = v_ref[b, :, hs]
                q_im = jnp.where(masks[hh], q_in[:, sl], jnp.zeros((), BF16))
                o = _dot(att.astype(BF16), v_h) + _dot_nt(q_im, st_pb)
                gate = g_ref[b, :, hs].astype(F32)
                y = (o * _rms_scale(o)) * gg_ref[...] * (gate * _sigmoid(gate))
                y_ref[b, :, hs] = y.astype(y_ref.dtype)
                k_em = jnp.where(masks[hh], k_end[:, sl], jnp.zeros((), BF16))
                upd = upd + _dot_tn(v_h, k_em)
            st_ref[b, p] = st_p * jnp.exp(cum_end[:, sl]) + upd

    @pl.when(j == pl.num_programs(1) - 1)
    def _():
        for b in range(nb):
            for p in range(H_B // 2):
                so_ref[b, p] = st_ref[b, p].T


def _gla(q, k, v, g, la, g_gla, s0, nb, c):
    nbt, t, _ = q.shape
    tok = lambda w: pl.BlockSpec((nb, c, w), lambda b, j: (b, j, 0))
    st_spec = pl.BlockSpec((nb, H_B // 2, 2 * DK_B, DV_B), lambda b, j: (b, 0, 0, 0))
    in_specs = [tok(W_BK), tok(W_BK), tok(W_BV), tok(W_BV), tok(W_BK), _full_spec((1, DV_B))]
    args = [q, k, v, g, la, g_gla]
    if s0 is not None:
        in_specs.append(st_spec)
        args.append(s0)
    return pl.pallas_call(
        functools.partial(_gla_kernel, has_init=s0 is not None),
        grid=(nbt // nb, t // c),
        in_specs=in_specs,
        out_specs=[tok(W_BV), st_spec],
        out_shape=[jax.ShapeDtypeStruct((nbt, t, W_BV), BF16),
                   jax.ShapeDtypeStruct((nbt, H_B // 2, 2 * DK_B, DV_B), F32)],
        scratch_shapes=[pltpu.VMEM((nb, H_B // 2, DV_B, 2 * DK_B), F32)],
        compiler_params=_params(("parallel", "arbitrary")),
        name="gla",
    )(*args)


def _mixout_kernel(x_ref, ya_ref, yb_ref, ga_ref, gb_ref, gm_ref, gp_ref, wa_ref, wb_ref, wo_ref,
                   o_ref):
    nb, tt, _ = x_ref.shape
    m = nb * tt
    a = _dot(ya_ref[...].reshape(m, W_A), wa_ref[...])
    b = _dot(yb_ref[...].reshape(m, W_BV), wb_ref[...])
    ga = _sigmoid(ga_ref[...].reshape(m, D_MODEL).astype(F32))
    gb = _sigmoid(gb_ref[...].reshape(m, D_MODEL).astype(F32))
    merged = (ga * a + gb * b).astype(BF16)
    mo = _dot(merged, wo_ref[...])
    n = ((mo * _rms_scale(mo)) * gp_ref[...]).reshape(nb, tt, D_MODEL)
    o_ref[...] = x_ref[...] + gm_ref[...] * n


def _mixout(x, ya, yb, ga, gb, mod, g_post, wa, wb, wo, nb, tt):
    nbt, t, _ = x.shape
    tok = lambda w: pl.BlockSpec((nb, tt, w), lambda b, i: (b, i, 0))
    return pl.pallas_call(
        _mixout_kernel,
        grid=(nbt // nb, t // tt),
        in_specs=[tok(D_MODEL), tok(W_A), tok(W_BV), tok(D_MODEL), tok(D_MODEL),
                  pl.BlockSpec((nb, 1, D_MODEL), lambda b, i: (b, 0, 2)),
                  _full_spec((1, D_MODEL)), _full_spec(wa.shape), _full_spec(wb.shape),
                  _full_spec(wo.shape)],
        out_specs=tok(D_MODEL),
        out_shape=jax.ShapeDtypeStruct(x.shape, F32),
        compiler_params=_params(("parallel", "parallel")),
        name="mixout",
    )(x, ya, yb, ga, gb, mod, g_post, wa, wb, wo)


def _gelu_tanh(x):
    c = float(np.sqrt(2.0 / np.pi))
    half = 0.5 * x
    return half + half * jnp.tanh(x * (c + (0.044715 * c) * (x * x)))


def _ffn_kernel(*refs, has_state):
    if has_state:
        (x_ref, shift_ref, scale_ref, gate_ref, gpre_ref, gpost_ref, wu_ref, wd_ref,
         cwa_ref, cwg_ref, st_ref, o_ref, tail_ref, h_ref, act_ref) = refs
        prev_ref = st_ref
    else:
        (x_ref, shift_ref, scale_ref, gate_ref, gpre_ref, gpost_ref, wu_ref, wd_ref,
         cwa_ref, cwg_ref, o_ref, tail_ref, h_ref, act_ref, carry_ref) = refs
        prev_ref = carry_ref

        @pl.when(pl.program_id(1) == 0)
        def _():
            carry_ref[...] = jnp.zeros_like(carry_ref)

    nb, tt, _ = x_ref.shape
    m = nb * tt
    x = x_ref[...]
    h = (x * _rms_scale(x)) * gpre_ref[...] * (1.0 + scale_ref[...]) + shift_ref[...]
    h_ref[...] = h.reshape(m, D_MODEL).astype(BF16)
    ridx = lax.broadcasted_iota(jnp.int32, (nb, SUBLANES, FFN_FT), 1)

    def conv(u, prev, cw):
        u3 = u.reshape(nb, tt, FFN_FT)
        r1 = pltpu.roll(u, 1, 0).reshape(nb, tt, FFN_FT)
        r2 = pltpu.roll(u, 2, 0).reshape(nb, tt, FFN_FT)

        def taps(u_m2, u_m1, u_0):
            y = cw[3:4, :] + cw[0:1, :] * u_m2
            y = y + cw[1:2, :] * u_m1
            return y + cw[2:3, :] * u_0

        p1 = prev[:, SUBLANES - 1:SUBLANES, :]
        p2 = prev[:, SUBLANES - 2:SUBLANES - 1, :]
        h_m1 = jnp.where(ridx == 0, p1, r1[:, :SUBLANES])
        h_m2 = jnp.where(ridx == 0, p2, jnp.where(ridx == 1, p1, r2[:, :SUBLANES]))
        y = jnp.concatenate([taps(h_m2, h_m1, u3[:, :SUBLANES]),
                             taps(r2[:, SUBLANES:], r1[:, SUBLANES:], u3[:, SUBLANES:])], axis=1)
        return y.reshape(m, FFN_FT), u3[:, tt - SUBLANES:, :]

    def up(f):
        hb = h_ref[...]
        return _dot(hb, wu_ref[f]), _dot(hb, wu_ref[FFN_NF + f])

    ua, ug = up(0)
    for f in range(FFN_NF):
        if f + 1 < FFN_NF:
            ua_next, ug_next = up(f + 1)
        ya, ta = conv(ua, prev_ref[0, f], cwa_ref[f])
        yg, tg = conv(ug, prev_ref[1, f], cwg_ref[f])
        act_ref[:, f * FFN_FT:(f + 1) * FFN_FT] = (_gelu_tanh(ya) * yg).astype(BF16)
        tail_ref[0, f] = ta
        tail_ref[1, f] = tg
        if not has_state:
            carry_ref[0, f] = ta
            carry_ref[1, f] = tg
        if f + 1 < FFN_NF:
            ua, ug = ua_next, ug_next
    yf = _dot(act_ref[...], wd_ref[...])
    n = ((yf * _rms_scale(yf)) * gpost_ref[...]).reshape(nb, tt, D_MODEL)
    o_ref[...] = x_ref[...] + gate_ref[...] * n


def _ffn(x, mod, g_pre, g_post, wu, wd, cwa, cwg, state, nb, tt):
    nbt, t, _ = x.shape
    m = nb * tt
    tok = pl.BlockSpec((nb, tt, D_MODEL), lambda b, i: (b, i, 0))
    mod_spec = lambda col: pl.BlockSpec((nb, 1, D_MODEL), lambda b, i, col=col: (b, 0, col))
    tail_spec = pl.BlockSpec((2, FFN_NF, nb, SUBLANES, FFN_FT), lambda b, i: (0, 0, b, 0, 0))
    in_specs = [tok, mod_spec(3), mod_spec(4), mod_spec(5), _full_spec((1, D_MODEL)),
                _full_spec((1, D_MODEL)), _full_spec(wu.shape),
                _full_spec(wd.shape), _full_spec(cwa.shape), _full_spec(cwg.shape)]
    args = [x, mod, mod, mod, g_pre, g_post, wu, wd, cwa, cwg]
    scratch = [pltpu.VMEM((m, D_MODEL), BF16), pltpu.VMEM((m, D_FF), BF16)]
    if state is not None:
        in_specs.append(tail_spec)
        args.append(state)
    else:
        scratch.append(pltpu.VMEM((2, FFN_NF, nb, SUBLANES, FFN_FT), F32))
    return pl.pallas_call(
        functools.partial(_ffn_kernel, has_state=state is not None),
        grid=(nbt // nb, t // tt),
        in_specs=in_specs,
        out_specs=[tok, tail_spec],
        out_shape=[jax.ShapeDtypeStruct(x.shape, F32),
                   jax.ShapeDtypeStruct((2, FFN_NF, nbt, SUBLANES, FFN_FT), F32)],
        scratch_shapes=scratch,
        compiler_params=_params(("parallel", "arbitrary")),
        name="ffn",
    )(*args)


def _tail_to_state(tail):
    nbt = tail.shape[2]
    t = jnp.transpose(tail, (2, 3, 0, 1, 4)).reshape(nbt, SUBLANES, 2 * D_FF)
    return t[:, SUBLANES - (CONV_W - 1):, :]


def _state_to_prev(state):
    nbt = state.shape[0]
    s = jnp.pad(state, ((0, 0), (SUBLANES - (CONV_W - 1), 0), (0, 0)))
    s = s.reshape(nbt, SUBLANES, 2, FFN_NF, FFN_FT)
    return jnp.transpose(s, (2, 3, 0, 1, 4))


def _bias_rows(rel_bias):
    assert BIAS_ROW >= ATTN_KB + ATTN_QB - 1 and WINDOW_A == ATTN_KB - ATTN_QB
    far_pos = jnp.broadcast_to(rel_bias[:, -1:], (H_A, BIAS_ROW))
    far_neg = jnp.broadcast_to(rel_bias[:, :1], (H_A, BIAS_ROW))
    n_mid = 2 * MAX_REL + 1
    n_lo = ATTN_KB + 1 - (WINDOW_A - MAX_REL) - n_mid
    rows = jnp.concatenate([far_pos[:, :WINDOW_A - MAX_REL], rel_bias[:, ::-1], far_neg[:, :n_lo],
                            far_pos[:, :BIAS_ROW - ATTN_KB - 1]], axis=1)
    return rows.astype(F32).reshape(H_A, 1, BIAS_ROW)


def _layer(x, mod, cache, s_gla, s_conv, w, first_chunk):
    nbt, t, _ = x.shape
    if first_chunk:
        nb, tt = 1, 512
    else:
        nb, tt = 512 // t, t
    qa, ka, va, qb, kb, vb, gb, gate_a, gate_b, log_a = _inproj(
        x, mod, w['g_pre_mix'], w['w_main'], w['w_gk1'], w['w_gk2'], w['b_gk'], nb, tt)
    if first_chunk:
        ya = _attn_prompt(qa, ka, va, w['bias_rows'])
        rows = min(WINDOW_A, t)
        k_keep, v_keep = ka[:, t - rows:], va[:, t - rows:]
        yb, s_new = _gla(qb, kb, vb, gb, log_a, w['g_gla'], None, nbt, CHUNK)
    else:
        k_cache, v_cache = cache
        ya = _attn_sample(qa, ka, va, k_cache, v_cache, w['bias_rows'], 4)
        k_keep, v_keep = ka, va
        yb, s_new = _gla(qb, kb, vb, gb, log_a, w['g_gla'], s_gla, 4, t)
    x1 = _mixout(x, ya, yb, gate_a, gate_b, mod, w['g_post_mix'], w['w_br_a'], w['w_br_b'],
                 w['w_out'], nb, tt)
    prev = None if first_chunk else _state_to_prev(s_conv)
    y, tail = _ffn(x1, mod, w['g_pre_ffn'], w['g_post_ffn'], w['w_up'], w['w_down'],
                   w['cw_a'], w['cw_g'], prev, nb, tt)
    heads = lambda a: a.astype(F32).reshape(nbt, a.shape[1], H_A, HD_A)
    return (y, heads(k_keep), heads(v_keep), s_new.reshape(nbt, H_B, DK_B, DV_B),
            _tail_to_state(tail))


def _prep_weights(w_in, w_gk2, b_gk, rel_bias, g_gla, w_br_a, w_br_b, w_out, w_up, w_dw, b_dw,
                  w_down, g_pre_mix, g_post_mix, g_pre_ffn, g_post_ffn):
    w_main, w_gk1 = _split_w_in(w_in.T, 256)
    w_gk2p = jnp.pad(w_gk2, ((0, LANES - GK_RANK), (0, 0))).astype(BF16)
    w_br_a16, w_br_b16, w_out16, w_down16 = _cast_rows((w_br_a, w_br_b, w_out, w_down), 4)
    split_cols = lambda a: jnp.transpose(a.reshape(a.shape[0], FFN_NF, FFN_FT), (1, 0, 2))
    conv_w = lambda wd, bd: jnp.pad(split_cols(jnp.concatenate([wd, bd[None]], axis=0)),
                                    ((0, 0), (0, SUBLANES - CONV_W - 1), (0, 0)))
    row = lambda a: a.reshape(1, -1)
    return {
        'w_main': w_main, 'w_gk1': w_gk1, 'w_gk2': w_gk2p, 'b_gk': row(b_gk),
        'bias_rows': _bias_rows(rel_bias),
        'g_gla': row(g_gla), 'w_br_a': w_br_a16, 'w_br_b': w_br_b16, 'w_out': w_out16,
        'w_up': _cast_col_blocks(w_up, FFN_FT), 'w_down': w_down16,
        'cw_a': conv_w(w_dw[:, :D_FF], b_dw[:D_FF]), 'cw_g': conv_w(w_dw[:, D_FF:], b_dw[D_FF:]),
        'g_pre_mix': row(g_pre_mix), 'g_post_mix': row(g_post_mix),
        'g_pre_ffn': row(g_pre_ffn), 'g_post_ffn': row(g_post_ffn),
    }


def kernel(x_prompt, x_sample, cache_k_a, cache_v_a, state_gla, state_conv, c_prompt, c_sample, w_ada, b_ada, g_pre_mix, g_post_mix, g_pre_ffn, g_post_ffn, w_in, w_gk2, b_gk, rel_bias, g_gla, w_br_a, w_br_b, w_out, w_up, w_dw, b_dw, w_down):
    depth = w_ada.shape[0]
    assert depth == 1
    bp, bs = x_prompt.shape[0], x_sample.shape[0]
    s_len = x_sample.shape[1]
    cache_rows = cache_k_a.shape[2]
    yp, ys = x_prompt, x_sample
    outs = [[] for _ in range(8)]
    for l in range(depth):
        w = _prep_weights(w_in[l], w_gk2[l], b_gk[l], rel_bias[l], g_gla[l], w_br_a[l], w_br_b[l],
                          w_out[l], w_up[l], w_dw[l], b_dw[l], w_down[l], g_pre_mix[l],
                          g_post_mix[l], g_pre_ffn[l], g_post_ffn[l])
        c_all = jnp.concatenate([c_prompt, c_sample], axis=0)
        pad = (-c_all.shape[0]) % SUBLANES
        mod = _adaln(jnp.pad(c_all, ((0, pad), (0, 0))), w_ada[l], b_ada[l])
        mod_p = mod[:bp].reshape(bp, 1, 6 * D_MODEL)
        mod_s = mod[bp:bp + bs].reshape(bs, 1, 6 * D_MODEL)
        yp, kp, vp, gp, cp = _layer(yp, mod_p, None, None, None, w, True)
        to_t = lambda c: jnp.transpose(c, (0, 2, 3, 1)).reshape(bs, W_A, cache_rows)
        cache = (to_t(cache_k_a[l]), to_t(cache_v_a[l]))
        s0 = state_gla[l].reshape(bs, H_B // 2, 2 * DK_B, DV_B)
        ys, kn, vn, gn, cn = _layer(ys, mod_s, cache, s0, state_conv[l], w, False)
        for lst, a in zip(outs, (kp, vp, gp, cp, kn, vn, gn, cn)):
            lst.append(a)
    return (yp, ys) + tuple(jnp.stack(lst) for lst in outs)
```

```python
import functools

import jax
import jax.numpy as jnp
import numpy as np
from jax import lax
from jax.experimental import pallas as pl
from jax.experimental.pallas import tpu as pltpu

D_MODEL = 1024
CHUNK = 64
BAND_CHUNKS = 8
WINDOW_A = BAND_CHUNKS * CHUNK
H_A = 8
HD_A = 64
MAX_REL = 128
H_B = 4
DK_B = 64
DV_B = 128
GK_RANK = 16
GK_NORM = 16.0
GLA_SUB = 16
D_FF = 2816
CONV_W = 3
EPS = 1e-6
NEG_INF = -1e30
PAST_LEN = 2048

W_A = H_A * HD_A
W_BK = H_B * DK_B
W_BV = H_B * DV_B

LANES = 128
SUBLANES = 8
VMEM_LIMIT = 56 * 1024 * 1024

ATTN_QB = 256
ATTN_KB = 3 * ATTN_QB
BIAS_ROW = 1024
FFN_FT = 256
FFN_NF = D_FF // FFN_FT

BF16 = jnp.bfloat16
F32 = jnp.float32


def _params(sem):
    return pltpu.CompilerParams(dimension_semantics=sem, vmem_limit_bytes=VMEM_LIMIT)


def _full_spec(shape):
    nd = len(shape)
    return pl.BlockSpec(shape, lambda *_: (0,) * nd, pipeline_mode=pl.Buffered(1))


def _dot(a, b):
    return jnp.dot(a, b, preferred_element_type=F32)


def _dot_nt(a, b):
    return lax.dot_general(a, b, (((1,), (1,)), ((), ())), preferred_element_type=F32)


def _dot_tn(a, b):
    return lax.dot_general(a, b, (((0,), (0,)), ((), ())), preferred_element_type=F32)


def _sigmoid(x):
    return 1.0 / (1.0 + jnp.exp(-x))


def _rms_scale(x):
    return lax.rsqrt(jnp.mean(x * x, axis=-1, keepdims=True) + EPS)


def _cast_kernel(*refs):
    n = len(refs) // 2
    for w_ref, o_ref in zip(refs[:n], refs[n:]):
        o_ref[...] = w_ref[...].astype(o_ref.dtype).reshape(o_ref.shape)


def _cast_rows(ws, steps):
    specs = [pl.BlockSpec((w.shape[0] // steps, w.shape[1]), lambda i: (i, 0)) for w in ws]
    return pl.pallas_call(
        _cast_kernel,
        grid=(steps,),
        in_specs=specs,
        out_specs=specs,
        out_shape=[jax.ShapeDtypeStruct(w.shape, BF16) for w in ws],
        compiler_params=_params(("parallel",)),
        name="cast_rows",
    )(*ws)


def _cast_col_blocks_kernel(w_ref, o_ref):
    nblk, _, tn = o_ref.shape
    for t in range(nblk):
        o_ref[t] = w_ref[:, t * tn:(t + 1) * tn].astype(o_ref.dtype)


def _cast_col_blocks(w, tn, per_step):
    k, n = w.shape
    return pl.pallas_call(
        _cast_col_blocks_kernel,
        grid=(n // (tn * per_step),),
        in_specs=[pl.BlockSpec((k, tn * per_step), lambda j: (0, j))],
        out_specs=pl.BlockSpec((per_step, k, tn), lambda j: (j, 0, 0)),
        out_shape=jax.ShapeDtypeStruct((n // tn, k, tn), BF16),
        compiler_params=_params(("parallel",)),
        name="cast_col_blocks",
    )(w)


_GK_LO = 3 * W_A + 2 * W_BK + 2 * W_BV


def _split_w_in_kernel(w_ref, main_ref, gk_ref):
    main_ref[:_GK_LO, :] = w_ref[:_GK_LO, :].astype(BF16)
    main_ref[_GK_LO:, :] = w_ref[_GK_LO + GK_RANK:, :].astype(BF16)
    gk_ref[:GK_RANK, :] = w_ref[_GK_LO:_GK_LO + GK_RANK, :].astype(BF16)
    gk_ref[GK_RANK:, :] = jnp.zeros((LANES - GK_RANK, gk_ref.shape[1]), BF16)


def _split_w_in(w_in_t, cols):
    d_in, d = w_in_t.shape
    return pl.pallas_call(
        _split_w_in_kernel,
        grid=(d // cols,),
        in_specs=[pl.BlockSpec((d_in, cols), lambda i: (0, i))],
        out_specs=[pl.BlockSpec((d_in - GK_RANK, cols), lambda i: (0, i)),
                   pl.BlockSpec((LANES, cols), lambda i: (0, i))],
        out_shape=[jax.ShapeDtypeStruct((d_in - GK_RANK, d), BF16),
                   jax.ShapeDtypeStruct((LANES, d), BF16)],
        compiler_params=_params(("parallel",)),
        name="split_w_in",
    )(w_in_t)


def _adaln_kernel(c_ref, w_ref, b_ref, o_ref):
    c = c_ref[...]
    s = (c * _sigmoid(c)).astype(BF16)
    o_ref[...] = _dot(s, w_ref[...].astype(BF16)) + b_ref[...]


def _adaln(c_all, w_ada, b_ada):
    rows = c_all.shape[0]
    n = w_ada.shape[1]
    tn = D_MODEL
    return pl.pallas_call(
        _adaln_kernel,
        grid=(n // tn,),
        in_specs=[
            pl.BlockSpec((rows, D_MODEL), lambda j: (0, 0)),
            pl.BlockSpec((D_MODEL, tn), lambda j: (0, j)),
            pl.BlockSpec((1, tn), lambda j: (0, j)),
        ],
        out_specs=pl.BlockSpec((rows, tn), lambda j: (0, j)),
        out_shape=jax.ShapeDtypeStruct((rows, n), F32),
        compiler_params=_params(("parallel",)),
        name="adaln",
    )(c_all, w_ada, b_ada.reshape(1, n))


_IN_GROUPS = (W_A, W_A, W_A, W_BK, W_BK, W_BV, W_BV, D_MODEL, D_MODEL)
_IN_MAIN = sum(_IN_GROUPS)


def _inproj_kernel(x_ref, shift_ref, scale_ref, g_ref, wm_ref, wg1_ref, wg2_ref, bgk_ref,
                   qa_ref, ka_ref, va_ref, qb_ref, kb_ref, vb_ref, gb_ref, ga_ref, gtb_ref, la_ref):
    nb, tt, _ = x_ref.shape
    x = x_ref[...]
    h = (x * _rms_scale(x)) * g_ref[...] * (1.0 + scale_ref[...]) + shift_ref[...]
    hb = h.reshape(nb * tt, D_MODEL).astype(BF16)
    outs = (qa_ref, ka_ref, va_ref, qb_ref, kb_ref, vb_ref, gb_ref, ga_ref, gtb_ref)
    lo = 0
    for o_ref, w in zip(outs, _IN_GROUPS):
        z = _dot_nt(hb, wm_ref[lo:lo + w, :])
        o_ref[...] = z.reshape(nb, tt, w).astype(o_ref.dtype)
        lo += w
    gk_low = _dot_nt(hb, wg1_ref[...]).astype(BF16)
    gk = _dot(gk_low, wg2_ref[...]) + bgk_ref[...]
    log_a = (jnp.minimum(gk, 0.0) - jnp.log1p(jnp.exp(-jnp.abs(gk)))) / GK_NORM
    la_ref[...] = log_a.reshape(nb, tt, W_BK)


def _inproj(x, mod, g_pre, wm, wg1, wg2, bgk, nb, tt):
    nbt, t, _ = x.shape
    grid = (nbt // nb, t // tt)
    tok = lambda w: pl.BlockSpec((nb, tt, w), lambda b, i: (b, i, 0))
    mod_spec = lambda col: pl.BlockSpec((nb, 1, D_MODEL), lambda b, i, col=col: (b, 0, col))
    widths = _IN_GROUPS + (W_BK,)
    dtypes = (BF16,) * len(_IN_GROUPS) + (F32,)
    return pl.pallas_call(
        _inproj_kernel,
        grid=grid,
        in_specs=[tok(D_MODEL), mod_spec(0), mod_spec(1), _full_spec((1, D_MODEL)),
                  _full_spec(wm.shape), _full_spec(wg1.shape), _full_spec(wg2.shape),
                  _full_spec((1, W_BK))],
        out_specs=[tok(w) for w in widths],
        out_shape=[jax.ShapeDtypeStruct((nbt, t, w), dt) for w, dt in zip(widths, dtypes)],
        compiler_params=_params(("parallel", "parallel")),
        name="inproj",
    )(x, mod, mod, g_pre, wm, wg1, wg2, bgk)


def _head_masks():
    lane = lax.broadcasted_iota(jnp.int32, (1, LANES), 1)
    first = lane < HD_A
    return first, jnp.logical_not(first)


def _toeplitz_bias(row_ref, h, rows):
    rb = jnp.broadcast_to(row_ref[h], (rows, BIAS_ROW))
    return pltpu.roll(rb, 0, 1, stride=1, stride_axis=0)


ATTN_RB = 32
ATTN_WIN = 640


def _attn_prompt_kernel(q_ref, k0_ref, k1_ref, k2_ref, v0_ref, v1_ref, v2_ref, row_ref, o_ref,
                        bias_ref, s_ref, p_ref):
    i = pl.program_id(1)

    @pl.when((pl.program_id(0) == 0) & (i == 0))
    def _():
        qc = lax.broadcasted_iota(jnp.int32, (ATTN_QB, ATTN_KB), 0) // CHUNK
        col = lax.broadcasted_iota(jnp.int32, (ATTN_QB, ATTN_KB), 1)
        kc = col // CHUNK - BAND_CHUNKS
        valid = (kc <= qc) & (kc >= qc - BAND_CHUNKS)
        for h in range(H_A):
            t = _toeplitz_bias(row_ref, h, ATTN_QB)
            band = jnp.where(valid, t[:, :ATTN_KB], NEG_INF)
            bias_ref[0, h] = jnp.where(col >= 2 * ATTN_QB, band, NEG_INF)
            bias_ref[1, h] = jnp.where(col >= ATTN_QB, band, NEG_INF)
            bias_ref[2, h] = band
        p_ref[...] = jnp.zeros_like(p_ref)

    var = jnp.minimum(i, 2)
    masks = _head_masks()

    def pair_rows(refs, p):
        sl = slice(p * LANES, (p + 1) * LANES)
        return jnp.concatenate([r[0, :, sl] for r in refs], axis=0)

    def scores(h):
        p, hh = divmod(h, 2)
        qp = q_ref[0, :, p * LANES:(p + 1) * LANES] * BF16(HD_A ** -0.5)
        qm = jnp.where(masks[hh], qp, jnp.zeros_like(qp))
        s_ref[h % 2] = _dot_nt(qm, pair_rows((k0_ref, k1_ref, k2_ref), p)) + bias_ref[var, h]

    scores(0)
    o_first = None
    for h in range(H_A):
        if h + 1 < H_A:
            scores(h + 1)
        slot = h % 2
        sums = []
        for r in range(ATTN_QB // ATTN_RB):
            rows = slice(r * ATTN_RB, (r + 1) * ATTN_RB)
            lo = 0 if r * ATTN_RB < ATTN_QB // 2 else ATTN_KB - ATTN_WIN
            sb = s_ref[slot, rows, lo:lo + ATTN_WIN]
            e = jnp.exp(sb - jnp.max(sb, axis=-1, keepdims=True))
            sums.append(jnp.sum(e, axis=-1, keepdims=True))
            p_ref[slot, rows, lo:lo + ATTN_WIN] = e.astype(BF16)
        p, hh = divmod(h, 2)
        o = _dot(p_ref[slot], pair_rows((v0_ref, v1_ref, v2_ref), p)) / jnp.concatenate(sums, axis=0)
        if hh == 0:
            o_first = o
        else:
            o_ref[0, :, p * LANES:(p + 1) * LANES] = jnp.where(masks[0], o_first, o).astype(o_ref.dtype)


def _attn_prompt(q, k, v, bias_rows):
    b, t, _ = q.shape
    blk = lambda off: pl.BlockSpec(
        (1, ATTN_QB, W_A), lambda bb, i, off=off: (bb, jnp.maximum(i - off, 0), 0))
    return pl.pallas_call(
        _attn_prompt_kernel,
        grid=(b, t // ATTN_QB),
        in_specs=[blk(0), blk(2), blk(1), blk(0), blk(2), blk(1), blk(0),
                  _full_spec(bias_rows.shape)],
        out_specs=blk(0),
        out_shape=jax.ShapeDtypeStruct((b, t, W_A), BF16),
        scratch_shapes=[pltpu.VMEM((3, H_A, ATTN_QB, ATTN_KB), F32),
                        pltpu.VMEM((2, ATTN_QB, ATTN_KB), F32),
                        pltpu.VMEM((2, ATTN_QB, ATTN_KB), BF16)],
        compiler_params=_params(("arbitrary", "arbitrary")),
        name="attn_prompt",
    )(q, k, k, k, v, v, v, bias_rows)


def _attn_sample_kernel(q_ref, kn_ref, vn_ref, kc_ref, vc_ref, row_ref, o_ref, bc_ref, bn_ref):
    nb, s_len, _ = q_ref.shape
    w = kc_ref.shape[2]

    @pl.when(pl.program_id(0) == 0)
    def _():
        for h in range(H_A):
            t = _toeplitz_bias(row_ref, h, s_len)
            bc_ref[h * s_len:(h + 1) * s_len, :] = t[:, :w]
            bn_ref[h * s_len:(h + 1) * s_len, :] = t[:, w:w + s_len]

    lane_head = lax.broadcasted_iota(jnp.int32, (1, W_A), 1) // HD_A
    row_head = lax.broadcasted_iota(jnp.int32, (H_A * s_len, 1), 0) // s_len
    own_head = row_head == lane_head
    for b in range(nb):
        q = q_ref[b] * BF16(HD_A ** -0.5)
        qs = jnp.concatenate([q] * H_A, axis=0)
        qs = jnp.where(own_head, qs, jnp.zeros_like(qs))
        sc = _dot(qs, kc_ref[b].astype(BF16)) + bc_ref[...]
        sn = _dot_nt(qs, kn_ref[b]) + bn_ref[...]
        m = jnp.maximum(jnp.max(sc, axis=-1, keepdims=True), jnp.max(sn, axis=-1, keepdims=True))
        ec = jnp.exp(sc - m)
        en = jnp.exp(sn - m)
        l = jnp.sum(ec, axis=-1, keepdims=True) + jnp.sum(en, axis=-1, keepdims=True)
        full = (_dot_nt(ec.astype(BF16), vc_ref[b].astype(BF16))
                + _dot(en.astype(BF16), vn_ref[b])) / l
        o = full[:s_len]
        for h in range(1, H_A):
            o = jnp.where(lane_head == h, full[h * s_len:(h + 1) * s_len], o)
        o_ref[b] = o.astype(o_ref.dtype)


def _attn_sample(q, kn, vn, kc, vc, bias_rows, nb):
    b, s, _ = q.shape
    w = kc.shape[2]
    assert w == WINDOW_A
    new = pl.BlockSpec((nb, s, W_A), lambda i: (i, 0, 0))
    cache = pl.BlockSpec((nb, W_A, w), lambda i: (i, 0, 0))
    return pl.pallas_call(
        _attn_sample_kernel,
        grid=(b // nb,),
        in_specs=[new, new, new, cache, cache, _full_spec(bias_rows.shape)],
        out_specs=new,
        out_shape=jax.ShapeDtypeStruct((b, s, W_A), BF16),
        scratch_shapes=[pltpu.VMEM((H_A * s, w), F32), pltpu.VMEM((H_A * s, s), F32)],
        compiler_params=_params(("arbitrary",)),
        name="attn_sample",
    )(q, kn, vn, kc, vc, bias_rows)


def _gla_kernel(*refs, has_init):
    if has_init:
        q_ref, k_ref, v_ref, g_ref, la_ref, gg_ref, s0_ref, y_ref, so_ref, st_ref = refs
    else:
        q_ref, k_ref, v_ref, g_ref, la_ref, gg_ref, y_ref, so_ref, st_ref = refs
    nb, c, _ = q_ref.shape
    nsub = c // GLA_SUB
    npair = H_B // 2
    j = pl.program_id(1)

    @pl.when(j == 0)
    def _():
        if has_init:
            st_ref[...] = s0_ref[...]
        else:
            st_ref[...] = jnp.zeros_like(st_ref)

    r2 = lax.broadcasted_iota(jnp.int32, (2 * c, c), 0)
    s2 = lax.broadcasted_iota(jnp.int32, (2 * c, c), 1)
    t2 = r2 & (c - 1)
    same_sub = (s2 // GLA_SUB) == (t2 // GLA_SUB)
    sum_mat = jnp.where((s2 <= t2) & ((r2 < c) | same_sub), 1.0, 0.0).astype(F32)

    row = lax.broadcasted_iota(jnp.int32, (c, W_BK), 0)
    row2 = lax.broadcasted_iota(jnp.int32, (2 * c, LANES), 0)
    lane2 = lax.broadcasted_iota(jnp.int32, (2 * c, LANES), 1)
    own = (lane2 < DK_B) == (row2 < c)
    sub_of_row2 = (row2 & (c - 1)) // GLA_SUB
    tril2 = lane2 <= (row2 & (c - 1))
    eye = (lax.broadcasted_iota(jnp.int32, (LANES, LANES), 0)
           == lax.broadcasted_iota(jnp.int32, (LANES, LANES), 1))
    scale = DK_B ** -0.5
    zeros_k = jnp.zeros((LANES - c, nsub * LANES), BF16)
    zeros_v = jnp.zeros((LANES - c, DV_B), BF16)

    prep = []
    for b in range(nb):
        la = la_ref[b]
        sums = jnp.dot(sum_mat, la, precision=lax.Precision.HIGHEST, preferred_element_type=F32)
        cum, cum_sub = sums[:c], sums[c:]
        cum_end = cum[c - 1:c, :]
        qf = q_ref[b].astype(F32)
        kf = k_ref[b].astype(F32)
        q_sub = qf * jnp.exp(cum_sub) * scale
        q_in = (qf * jnp.exp(cum) * scale).astype(BF16)
        k_end = (kf * jnp.exp(cum_end - cum)).astype(BF16)
        k_sub = []
        for i in range(nsub):
            ref_i = cum[i * GLA_SUB - 1:i * GLA_SUB, :] if i else jnp.zeros((1, W_BK), F32)
            k_i = jnp.where(row < (i + 1) * GLA_SUB, kf * jnp.exp(ref_i - cum), 0.0)
            k_sub.append(k_i.astype(BF16))
        prep.append((q_sub, q_in, k_end, k_sub, cum_end))

    att = {}
    for b in range(nb):
        q_sub, _, _, k_sub, _ = prep[b]
        for p in range(npair):
            sl = slice(p * LANES, (p + 1) * LANES)
            k_stack = jnp.concatenate(
                [jnp.concatenate([k_i[:, sl] for k_i in k_sub], axis=1), zeros_k], axis=0)
            q2 = jnp.where(own, jnp.concatenate([q_sub[:, sl], q_sub[:, sl]], axis=0), 0.0)
            q_stack = jnp.concatenate(
                [jnp.where(sub_of_row2 == i, q2, 0.0) for i in range(nsub)], axis=1).astype(BF16)
            a = _dot_nt(q_stack, k_stack)
            att[b, p] = jnp.where(tril2, a, 0.0).astype(BF16)

    for b in range(nb):
        q_in = prep[b][1]
        for p in range(npair):
            sl = slice(p * LANES, (p + 1) * LANES)
            st_b = st_ref[b, p].astype(BF16)
            q_in2 = jnp.where(own, jnp.concatenate([q_in[:, sl], q_in[:, sl]], axis=0),
                              jnp.zeros((), BF16))
            for hh in range(2):
                h = 2 * p + hh
                hs = slice(h * DV_B, (h + 1) * DV_B)
                rows = slice(hh * c, (hh + 1) * c)
                lhs = jnp.concatenate([att[b, p][rows], q_in2[rows]], axis=1)
                rhs = jnp.concatenate([v_ref[b, :, hs], zeros_v, st_b], axis=0)
                o = _dot(lhs, rhs)
                gate = g_ref[b, :, hs].astype(F32)
                y = (o * _rms_scale(o)) * gg_ref[...] * (gate * _sigmoid(gate))
                y_ref[b, :, hs] = y.astype(y_ref.dtype)

    for b in range(nb):
        k_end, cum_end = prep[b][2], prep[b][4]
        for p in range(npair):
            sl = slice(p * LANES, (p + 1) * LANES)
            k2 = jnp.where(own, jnp.concatenate([k_end[:, sl], k_end[:, sl]], axis=0),
                           jnp.zeros((), BF16))
            v2 = jnp.concatenate([v_ref[b, :, 2 * p * DV_B:(2 * p + 1) * DV_B],
                                  v_ref[b, :, (2 * p + 1) * DV_B:(2 * p + 2) * DV_B]], axis=0)
            upd = _dot_tn(k2, v2)
            dec = jnp.exp(jnp.sum(jnp.where(eye, cum_end[:, sl], 0.0), axis=1, keepdims=True))
            st_ref[b, p] = st_ref[b, p] * dec + upd

    @pl.when(j == pl.num_programs(1) - 1)
    def _():
        so_ref[...] = st_ref[...]


def _gla(q, k, v, g, la, g_gla, s0, nb, c):
    nbt, t, _ = q.shape
    tok = lambda w: pl.BlockSpec((nb, c, w), lambda b, j: (b, j, 0))
    st_spec = pl.BlockSpec((nb, H_B // 2, 2 * DK_B, DV_B), lambda b, j: (b, 0, 0, 0))
    in_specs = [tok(W_BK), tok(W_BK), tok(W_BV), tok(W_BV), tok(W_BK), _full_spec((1, DV_B))]
    args = [q, k, v, g, la, g_gla]
    if s0 is not None:
        in_specs.append(st_spec)
        args.append(s0)
    return pl.pallas_call(
        functools.partial(_gla_kernel, has_init=s0 is not None),
        grid=(nbt // nb, t // c),
        in_specs=in_specs,
        out_specs=[tok(W_BV), st_spec],
        out_shape=[jax.ShapeDtypeStruct((nbt, t, W_BV), BF16),
                   jax.ShapeDtypeStruct((nbt, H_B // 2, 2 * DK_B, DV_B), F32)],
        scratch_shapes=[pltpu.VMEM((nb, H_B // 2, DV_B, 2 * DK_B), F32)],
        compiler_params=_params(("parallel", "arbitrary")),
        name="gla",
    )(*args)


def _mixout_kernel(x_ref, ya_ref, yb_ref, ga_ref, gb_ref, gm_ref, gp_ref, wa_ref, wb_ref, wo_ref,
                   o_ref):
    nb, tt, _ = x_ref.shape
    m = nb * tt
    a = _dot(ya_ref[...].reshape(m, W_A), wa_ref[...])
    b = _dot(yb_ref[...].reshape(m, W_BV), wb_ref[...])
    ga = _sigmoid(ga_ref[...].reshape(m, D_MODEL).astype(F32))
    gb = _sigmoid(gb_ref[...].reshape(m, D_MODEL).astype(F32))
    merged = (ga * a + gb * b).astype(BF16)
    mo = _dot(merged, wo_ref[...])
    n = ((mo * _rms_scale(mo)) * gp_ref[...]).reshape(nb, tt, D_MODEL)
    o_ref[...] = x_ref[...] + gm_ref[...] * n


def _mixout(x, ya, yb, ga, gb, mod, g_post, wa, wb, wo, nb, tt):
    nbt, t, _ = x.shape
    tok = lambda w: pl.BlockSpec((nb, tt, w), lambda b, i: (b, i, 0))
    return pl.pallas_call(
        _mixout_kernel,
        grid=(nbt // nb, t // tt),
        in_specs=[tok(D_MODEL), tok(W_A), tok(W_BV), tok(D_MODEL), tok(D_MODEL),
                  pl.BlockSpec((nb, 1, D_MODEL), lambda b, i: (b, 0, 2)),
                  _full_spec((1, D_MODEL)), _full_spec(wa.shape), _full_spec(wb.shape),
                  _full_spec(wo.shape)],
        out_specs=tok(D_MODEL),
        out_shape=jax.ShapeDtypeStruct(x.shape, F32),
        compiler_params=_params(("parallel", "parallel")),
        name="mixout",
    )(x, ya, yb, ga, gb, mod, g_post, wa, wb, wo)


def _gelu_tanh(x):
    c = float(np.sqrt(2.0 / np.pi))
    half = 0.5 * x
    return half + half * jnp.tanh(x * (c + (0.044715 * c) * (x * x)))


def _ffn_kernel(*refs, has_state):
    if has_state:
        (x_ref, shift_ref, scale_ref, gate_ref, gpre_ref, gpost_ref, wu_ref, wd_ref,
         cwa_ref, cwg_ref, st_ref, o_ref, tail_ref, h_ref, act_ref) = refs
        prev_ref = st_ref
    else:
        (x_ref, shift_ref, scale_ref, gate_ref, gpre_ref, gpost_ref, wu_ref, wd_ref,
         cwa_ref, cwg_ref, o_ref, tail_ref, h_ref, act_ref, carry_ref) = refs
        prev_ref = carry_ref

        @pl.when(pl.program_id(1) == 0)
        def _():
            carry_ref[...] = jnp.zeros_like(carry_ref)

    nb, tt, _ = x_ref.shape
    m = nb * tt
    x = x_ref[...]
    h = (x * _rms_scale(x)) * gpre_ref[...] * (1.0 + scale_ref[...]) + shift_ref[...]
    h_ref[...] = h.reshape(m, D_MODEL).astype(BF16)
    ridx = lax.broadcasted_iota(jnp.int32, (nb, SUBLANES, FFN_FT), 1)

    def conv(u, prev, cw):
        u3 = u.reshape(nb, tt, FFN_FT)
        r1 = pltpu.roll(u, 1, 0).reshape(nb, tt, FFN_FT)
        r2 = pltpu.roll(u, 2, 0).reshape(nb, tt, FFN_FT)

        def taps(u_m2, u_m1, u_0):
            y = cw[3:4, :] + cw[0:1, :] * u_m2
            y = y + cw[1:2, :] * u_m1
            return y + cw[2:3, :] * u_0

        p1 = prev[:, SUBLANES - 1:SUBLANES, :]
        p2 = prev[:, SUBLANES - 2:SUBLANES - 1, :]
        h_m1 = jnp.where(ridx == 0, p1, r1[:, :SUBLANES])
        h_m2 = jnp.where(ridx == 0, p2, jnp.where(ridx == 1, p1, r2[:, :SUBLANES]))
        y = jnp.concatenate([taps(h_m2, h_m1, u3[:, :SUBLANES]),
                             taps(r2[:, SUBLANES:], r1[:, SUBLANES:], u3[:, SUBLANES:])], axis=1)
        return y.reshape(m, FFN_FT), u3[:, tt - SUBLANES:, :]

    def up(f):
        hb = h_ref[...]
        return _dot(hb, wu_ref[f]), _dot(hb, wu_ref[FFN_NF + f])

    ua, ug = up(0)
    for f in range(FFN_NF):
        if f + 1 < FFN_NF:
            ua_next, ug_next = up(f + 1)
        ya, ta = conv(ua, prev_ref[0, f], cwa_ref[f])
        yg, tg = conv(ug, prev_ref[1, f], cwg_ref[f])
        act_ref[:, f * FFN_FT:(f + 1) * FFN_FT] = (_gelu_tanh(ya) * yg).astype(BF16)
        tail_ref[0, f] = ta
        tail_ref[1, f] = tg
        if not has_state:
            carry_ref[0, f] = ta
            carry_ref[1, f] = tg
        if f + 1 < FFN_NF:
            ua, ug = ua_next, ug_next
    yf = _dot(act_ref[...], wd_ref[...])
    n = ((yf * _rms_scale(yf)) * gpost_ref[...]).reshape(nb, tt, D_MODEL)
    o_ref[...] = x_ref[...] + gate_ref[...] * n


def _ffn(x, mod, g_pre, g_post, wu, wd, cwa, cwg, state, nb, tt):
    nbt, t, _ = x.shape
    m = nb * tt
    tok = pl.BlockSpec((nb, tt, D_MODEL), lambda b, i: (b, i, 0))
    mod_spec = lambda col: pl.BlockSpec((nb, 1, D_MODEL), lambda b, i, col=col: (b, 0, col))
    tail_spec = pl.BlockSpec((2, FFN_NF, nb, SUBLANES, FFN_FT), lambda b, i: (0, 0, b, 0, 0))
    in_specs = [tok, mod_spec(3), mod_spec(4), mod_spec(5), _full_spec((1, D_MODEL)),
                _full_spec((1, D_MODEL)), _full_spec(wu.shape),
                _full_spec(wd.shape), _full_spec(cwa.shape), _full_spec(cwg.shape)]
    args = [x, mod, mod, mod, g_pre, g_post, wu, wd, cwa, cwg]
    scratch = [pltpu.VMEM((m, D_MODEL), BF16), pltpu.VMEM((m, D_FF), BF16)]
    if state is not None:
        in_specs.append(tail_spec)
        args.append(state)
    else:
        scratch.append(pltpu.VMEM((2, FFN_NF, nb, SUBLANES, FFN_FT), F32))
    return pl.pallas_call(
        functools.partial(_ffn_kernel, has_state=state is not None),
        grid=(nbt // nb, t // tt),
        in_specs=in_specs,
        out_specs=[tok, tail_spec],
        out_shape=[jax.ShapeDtypeStruct(x.shape, F32),
                   jax.ShapeDtypeStruct((2, FFN_NF, nbt, SUBLANES, FFN_FT), F32)],
        scratch_shapes=scratch,
        compiler_params=_params(("parallel", "arbitrary")),
        name="ffn",
    )(*args)


def _tail_to_state(tail):
    nbt = tail.shape[2]
    t = jnp.transpose(tail, (2, 3, 0, 1, 4)).reshape(nbt, SUBLANES, 2 * D_FF)
    return t[:, SUBLANES - (CONV_W - 1):, :]


def _state_to_prev(state):
    nbt = state.shape[0]
    s = jnp.pad(state, ((0, 0), (SUBLANES - (CONV_W - 1), 0), (0, 0)))
    s = s.reshape(nbt, SUBLANES, 2, FFN_NF, FFN_FT)
    return jnp.transpose(s, (2, 3, 0, 1, 4))


def _bias_rows(rel_bias):
    assert BIAS_ROW >= ATTN_KB + ATTN_QB - 1 and WINDOW_A == ATTN_KB - ATTN_QB
    far_pos = jnp.broadcast_to(rel_bias[:, -1:], (H_A, BIAS_ROW))
    far_neg = jnp.broadcast_to(rel_bias[:, :1], (H_A, BIAS_ROW))
    n_mid = 2 * MAX_REL + 1
    n_lo = ATTN_KB + 1 - (WINDOW_A - MAX_REL) - n_mid
    rows = jnp.concatenate([far_pos[:, :WINDOW_A - MAX_REL], rel_bias[:, ::-1], far_neg[:, :n_lo],
                            far_pos[:, :BIAS_ROW - ATTN_KB - 1]], axis=1)
    return rows.astype(F32).reshape(H_A, 1, BIAS_ROW)


def _layer(x, mod, cache, s_gla, s_conv, w, first_chunk):
    nbt, t, _ = x.shape
    if first_chunk:
        nb, tt = 1, 512
    else:
        nb, tt = 512 // t, t
    qa, ka, va, qb, kb, vb, gb, gate_a, gate_b, log_a = _inproj(
        x, mod, w['g_pre_mix'], w['w_main'], w['w_gk1'], w['w_gk2'], w['b_gk'], nb, tt)
    if first_chunk:
        ya = _attn_prompt(qa, ka, va, w['bias_rows'])
        rows = min(WINDOW_A, t)
        k_keep, v_keep = ka[:, t - rows:], va[:, t - rows:]
        yb, s_new = _gla(qb, kb, vb, gb, log_a, w['g_gla'], None, nbt, CHUNK)
    else:
        k_cache, v_cache = cache
        ya = _attn_sample(qa, ka, va, k_cache, v_cache, w['bias_rows'], 4)
        k_keep, v_keep = ka, va
        yb, s_new = _gla(qb, kb, vb, gb, log_a, w['g_gla'], s_gla, 4, t)
    x1 = _mixout(x, ya, yb, gate_a, gate_b, mod, w['g_post_mix'], w['w_br_a'], w['w_br_b'],
                 w['w_out'], nb, tt)
    prev = None if first_chunk else _state_to_prev(s_conv)
    y, tail = _ffn(x1, mod, w['g_pre_ffn'], w['g_post_ffn'], w['w_up'], w['w_down'],
                   w['cw_a'], w['cw_g'], prev, nb, tt)
    heads = lambda a: a.astype(F32).reshape(nbt, a.shape[1], H_A, HD_A)
    return (y, heads(k_keep), heads(v_keep), s_new.reshape(nbt, H_B, DK_B, DV_B),
            _tail_to_state(tail))


def _prep_weights(w_in, w_gk2, b_gk, rel_bias, g_gla, w_br_a, w_br_b, w_out, w_up, w_dw, b_dw,
                  w_down, g_pre_mix, g_post_mix, g_pre_ffn, g_post_ffn):
    w_main, w_gk1 = _split_w_in(w_in.T, 256)
    w_gk2p = jnp.pad(w_gk2, ((0, LANES - GK_RANK), (0, 0))).astype(BF16)
    w_br_a16, w_br_b16, w_out16, w_down16 = _cast_rows((w_br_a, w_br_b, w_out, w_down), 4)
    split_cols = lambda a: jnp.transpose(a.reshape(a.shape[0], FFN_NF, FFN_FT), (1, 0, 2))
    conv_w = lambda wd, bd: jnp.pad(split_cols(jnp.concatenate([wd, bd[None]], axis=0)),
                                    ((0, 0), (0, SUBLANES - CONV_W - 1), (0, 0)))
    row = lambda a: a.reshape(1, -1)
    return {
        'w_main': w_main, 'w_gk1': w_gk1, 'w_gk2': w_gk2p, 'b_gk': row(b_gk),
        'bias_rows': _bias_rows(rel_bias),
        'g_gla': row(g_gla), 'w_br_a': w_br_a16, 'w_br_b': w_br_b16, 'w_out': w_out16,
        'w_up': _cast_col_blocks(w_up, FFN_FT, 2), 'w_down': w_down16,
        'cw_a': conv_w(w_dw[:, :D_FF], b_dw[:D_FF]), 'cw_g': conv_w(w_dw[:, D_FF:], b_dw[D_FF:]),
        'g_pre_mix': row(g_pre_mix), 'g_post_mix': row(g_post_mix),
        'g_pre_ffn': row(g_pre_ffn), 'g_post_ffn': row(g_post_ffn),
    }


def kernel(x_prompt, x_sample, cache_k_a, cache_v_a, state_gla, state_conv, c_prompt, c_sample, w_ada, b_ada, g_pre_mix, g_post_mix, g_pre_ffn, g_post_ffn, w_in, w_gk2, b_gk, rel_bias, g_gla, w_br_a, w_br_b, w_out, w_up, w_dw, b_dw, w_down):
    depth = w_ada.shape[0]
    assert depth == 1
    bp, bs = x_prompt.shape[0], x_sample.shape[0]
    s_len = x_sample.shape[1]
    cache_rows = cache_k_a.shape[2]
    yp, ys = x_prompt, x_sample
    outs = [[] for _ in range(8)]
    for l in range(depth):
        w = _prep_weights(w_in[l], w_gk2[l], b_gk[l], rel_bias[l], g_gla[l], w_br_a[l], w_br_b[l],
                          w_out[l], w_up[l], w_dw[l], b_dw[l], w_down[l], g_pre_mix[l],
                          g_post_mix[l], g_pre_ffn[l], g_post_ffn[l])
        c_all = jnp.concatenate([c_prompt, c_sample], axis=0)
        pad = (-c_all.shape[0]) % SUBLANES
        mod = _adaln(jnp.pad(c_all, ((0, pad), (0, 0))), w_ada[l], b_ada[l])
        mod_p = mod[:bp].reshape(bp, 1, 6 * D_MODEL)
        mod_s = mod[bp:bp + bs].reshape(bs, 1, 6 * D_MODEL)
        yp, kp, vp, gp, cp = _layer(yp, mod_p, None, None, None, w, True)
        to_t = lambda c: jnp.transpose(c, (0, 2, 3, 1)).reshape(bs, W_A, cache_rows)
        cache = (to_t(cache_k_a[l]), to_t(cache_v_a[l]))
        s0 = state_gla[l].reshape(bs, H_B // 2, 2 * DK_B, DV_B)
        ys, kn, vn, gn, cn = _layer(ys, mod_s, cache, s0, state_conv[l], w, False)
        for lst, a in zip(outs, (kp, vp, gp, cp, kn, vn, gn, cn)):
            lst.append(a)
    return (yp, ys) + tuple(jnp.stack(lst) for lst in outs)
```

```python
import functools

import jax
import jax.numpy as jnp
import numpy as np
from jax import lax
from jax.experimental import pallas as pl
from jax.experimental.pallas import tpu as pltpu

D_MODEL = 1024
CHUNK = 64
BAND_CHUNKS = 8
WINDOW_A = BAND_CHUNKS * CHUNK
H_A = 8
HD_A = 64
MAX_REL = 128
H_B = 4
DK_B = 64
DV_B = 128
GK_RANK = 16
GK_NORM = 16.0
GLA_SUB = 16
D_FF = 2816
CONV_W = 3
EPS = 1e-6
NEG_INF = -1e30
PAST_LEN = 2048

W_A = H_A * HD_A
W_BK = H_B * DK_B
W_BV = H_B * DV_B

LANES = 128
SUBLANES = 8
VMEM_LIMIT = 56 * 1024 * 1024

ATTN_QB = 256
ATTN_KB = 3 * ATTN_QB
BIAS_ROW = 1024
FFN_FT = 256
FFN_NF = D_FF // FFN_FT

BF16 = jnp.bfloat16
F32 = jnp.float32


def _params(sem):
    return pltpu.CompilerParams(dimension_semantics=sem, vmem_limit_bytes=VMEM_LIMIT)


def _full_spec(shape):
    nd = len(shape)
    return pl.BlockSpec(shape, lambda *_: (0,) * nd, pipeline_mode=pl.Buffered(1))


def _dot(a, b):
    return jnp.dot(a, b, preferred_element_type=F32)


def _dot_nt(a, b):
    return lax.dot_general(a, b, (((1,), (1,)), ((), ())), preferred_element_type=F32)


def _dot_tn(a, b):
    return lax.dot_general(a, b, (((0,), (0,)), ((), ())), preferred_element_type=F32)


def _sigmoid(x):
    return 1.0 / (1.0 + jnp.exp(-x))


def _rms_scale(x):
    return lax.rsqrt(jnp.mean(x * x, axis=-1, keepdims=True) + EPS)


def _cast_kernel(*refs):
    n = len(refs) // 2
    for w_ref, o_ref in zip(refs[:n], refs[n:]):
        o_ref[...] = w_ref[...].astype(o_ref.dtype).reshape(o_ref.shape)


def _cast_rows(ws, steps):
    specs = [pl.BlockSpec((w.shape[0] // steps, w.shape[1]), lambda i: (i, 0)) for w in ws]
    return pl.pallas_call(
        _cast_kernel,
        grid=(steps,),
        in_specs=specs,
        out_specs=specs,
        out_shape=[jax.ShapeDtypeStruct(w.shape, BF16) for w in ws],
        compiler_params=_params(("parallel",)),
        name="cast_rows",
    )(*ws)


def _cast_col_blocks_kernel(w_ref, o_ref):
    nblk, _, tn = o_ref.shape
    for t in range(nblk):
        o_ref[t] = w_ref[:, t * tn:(t + 1) * tn].astype(o_ref.dtype)


def _cast_col_blocks(w, tn, per_step):
    k, n = w.shape
    return pl.pallas_call(
        _cast_col_blocks_kernel,
        grid=(n // (tn * per_step),),
        in_specs=[pl.BlockSpec((k, tn * per_step), lambda j: (0, j))],
        out_specs=pl.BlockSpec((per_step, k, tn), lambda j: (j, 0, 0)),
        out_shape=jax.ShapeDtypeStruct((n // tn, k, tn), BF16),
        compiler_params=_params(("parallel",)),
        name="cast_col_blocks",
    )(w)


_GK_LO = 3 * W_A + 2 * W_BK + 2 * W_BV


def _split_w_in_kernel(w_ref, main_ref, gk_ref):
    main_ref[:_GK_LO, :] = w_ref[:_GK_LO, :].astype(BF16)
    main_ref[_GK_LO:, :] = w_ref[_GK_LO + GK_RANK:, :].astype(BF16)
    gk_ref[:GK_RANK, :] = w_ref[_GK_LO:_GK_LO + GK_RANK, :].astype(BF16)
    gk_ref[GK_RANK:, :] = jnp.zeros((LANES - GK_RANK, gk_ref.shape[1]), BF16)


def _split_w_in(w_in_t, cols):
    d_in, d = w_in_t.shape
    return pl.pallas_call(
        _split_w_in_kernel,
        grid=(d // cols,),
        in_specs=[pl.BlockSpec((d_in, cols), lambda i: (0, i))],
        out_specs=[pl.BlockSpec((d_in - GK_RANK, cols), lambda i: (0, i)),
                   pl.BlockSpec((LANES, cols), lambda i: (0, i))],
        out_shape=[jax.ShapeDtypeStruct((d_in - GK_RANK, d), BF16),
                   jax.ShapeDtypeStruct((LANES, d), BF16)],
        compiler_params=_params(("parallel",)),
        name="split_w_in",
    )(w_in_t)


def _adaln_kernel(c_ref, w_ref, b_ref, o_ref):
    c = c_ref[...]
    s = (c * _sigmoid(c)).astype(BF16)
    o_ref[...] = _dot(s, w_ref[...].astype(BF16)) + b_ref[...]


def _adaln(c_all, w_ada, b_ada):
    rows = c_all.shape[0]
    n = w_ada.shape[1]
    tn = D_MODEL
    return pl.pallas_call(
        _adaln_kernel,
        grid=(n // tn,),
        in_specs=[
            pl.BlockSpec((rows, D_MODEL), lambda j: (0, 0)),
            pl.BlockSpec((D_MODEL, tn), lambda j: (0, j)),
            pl.BlockSpec((1, tn), lambda j: (0, j)),
        ],
        out_specs=pl.BlockSpec((rows, tn), lambda j: (0, j)),
        out_shape=jax.ShapeDtypeStruct((rows, n), F32),
        compiler_params=_params(("parallel",)),
        name="adaln",
    )(c_all, w_ada, b_ada.reshape(1, n))


_IN_GROUPS = (W_A, W_A, W_A, W_BK, W_BK, W_BV, W_BV, D_MODEL, D_MODEL)
_IN_MAIN = sum(_IN_GROUPS)


def _inproj_kernel(x_ref, shift_ref, scale_ref, g_ref, wm_ref, wg1_ref, wg2_ref, bgk_ref,
                   qa_ref, ka_ref, va_ref, qb_ref, kb_ref, vb_ref, gb_ref, ga_ref, gtb_ref, la_ref,
                   *kv_t_refs):
    nb, tt, _ = x_ref.shape
    x = x_ref[...]
    h = (x * _rms_scale(x)) * g_ref[...] * (1.0 + scale_ref[...]) + shift_ref[...]
    hb = h.reshape(nb * tt, D_MODEL).astype(BF16)
    if kv_t_refs:
        @pl.when(pl.program_id(1) == pl.num_programs(1) - 1)
        def _():
            rows = kv_t_refs[0].shape[2]
            newest = hb[nb * tt - rows:, :]
            for o_ref, lo in zip(kv_t_refs, (W_A, 2 * W_A)):
                o_ref[0] = _dot_nt(wm_ref[lo:lo + W_A, :], newest)
    outs = (qa_ref, ka_ref, va_ref, qb_ref, kb_ref, vb_ref, gb_ref, ga_ref, gtb_ref)
    lo = 0
    for o_ref, w in zip(outs, _IN_GROUPS):
        z = _dot_nt(hb, wm_ref[lo:lo + w, :])
        o_ref[...] = z.reshape(nb, tt, w).astype(o_ref.dtype)
        lo += w
    gk_low = _dot_nt(hb, wg1_ref[...]).astype(BF16)
    gk = _dot(gk_low, wg2_ref[...]) + bgk_ref[...]
    log_a = (jnp.minimum(gk, 0.0) - jnp.log1p(jnp.exp(-jnp.abs(gk)))) / GK_NORM
    la_ref[...] = log_a.reshape(nb, tt, W_BK)


def _inproj(x, mod, g_pre, wm, wg1, wg2, bgk, nb, tt, keep_rows):
    nbt, t, _ = x.shape
    grid = (nbt // nb, t // tt)
    tok = lambda w: pl.BlockSpec((nb, tt, w), lambda b, i: (b, i, 0))
    mod_spec = lambda col: pl.BlockSpec((nb, 1, D_MODEL), lambda b, i, col=col: (b, 0, col))
    widths = _IN_GROUPS + (W_BK,)
    dtypes = (BF16,) * len(_IN_GROUPS) + (F32,)
    out_specs = [tok(w) for w in widths]
    out_shape = [jax.ShapeDtypeStruct((nbt, t, w), dt) for w, dt in zip(widths, dtypes)]
    if keep_rows:
        assert nb == 1 and keep_rows <= tt
        out_specs += [pl.BlockSpec((1, W_A, keep_rows), lambda b, i: (b, 0, 0))] * 2
        out_shape += [jax.ShapeDtypeStruct((nbt, W_A, keep_rows), F32)] * 2
    return pl.pallas_call(
        _inproj_kernel,
        grid=grid,
        in_specs=[tok(D_MODEL), mod_spec(0), mod_spec(1), _full_spec((1, D_MODEL)),
                  _full_spec(wm.shape), _full_spec(wg1.shape), _full_spec(wg2.shape),
                  _full_spec((1, W_BK))],
        out_specs=out_specs,
        out_shape=out_shape,
        compiler_params=_params(("parallel", "arbitrary")),
        name="inproj",
    )(x, mod, mod, g_pre, wm, wg1, wg2, bgk)


def _head_masks():
    lane = lax.broadcasted_iota(jnp.int32, (1, LANES), 1)
    first = lane < HD_A
    return first, jnp.logical_not(first)


def _toeplitz_bias(row_ref, h, rows):
    rb = jnp.broadcast_to(row_ref[h], (rows, BIAS_ROW))
    return pltpu.roll(rb, 0, 1, stride=1, stride_axis=0)


ATTN_RB = 32
ATTN_WIN = 640


def _attn_prompt_kernel(q_ref, k0_ref, k1_ref, k2_ref, v0_ref, v1_ref, v2_ref, row_ref, o_ref,
                        bias_ref, s_ref, p_ref):
    i = pl.program_id(1)

    @pl.when((pl.program_id(0) == 0) & (i == 0))
    def _():
        qc = lax.broadcasted_iota(jnp.int32, (ATTN_QB, ATTN_KB), 0) // CHUNK
        col = lax.broadcasted_iota(jnp.int32, (ATTN_QB, ATTN_KB), 1)
        kc = col // CHUNK - BAND_CHUNKS
        valid = (kc <= qc) & (kc >= qc - BAND_CHUNKS)
        for h in range(H_A):
            t = _toeplitz_bias(row_ref, h, ATTN_QB)
            band = jnp.where(valid, t[:, :ATTN_KB], NEG_INF)
            bias_ref[0, h] = jnp.where(col >= 2 * ATTN_QB, band, NEG_INF)
            bias_ref[1, h] = jnp.where(col >= ATTN_QB, band, NEG_INF)
            bias_ref[2, h] = band
        p_ref[...] = jnp.zeros_like(p_ref)

    var = jnp.minimum(i, 2)
    masks = _head_masks()

    def pair_rows(refs, p):
        sl = slice(p * LANES, (p + 1) * LANES)
        return jnp.concatenate([r[0, :, sl] for r in refs], axis=0)

    def scores(h):
        p, hh = divmod(h, 2)
        qp = q_ref[0, :, p * LANES:(p + 1) * LANES] * BF16(HD_A ** -0.5)
        qm = jnp.where(masks[hh], qp, jnp.zeros_like(qp))
        s_ref[h % 2] = _dot_nt(qm, pair_rows((k0_ref, k1_ref, k2_ref), p)) + bias_ref[var, h]

    scores(0)
    o_first = None
    for h in range(H_A):
        if h + 1 < H_A:
            scores(h + 1)
        slot = h % 2
        sums = []
        for r in range(ATTN_QB // ATTN_RB):
            rows = slice(r * ATTN_RB, (r + 1) * ATTN_RB)
            lo = 0 if r * ATTN_RB < ATTN_QB // 2 else ATTN_KB - ATTN_WIN
            sb = s_ref[slot, rows, lo:lo + ATTN_WIN]
            e = jnp.exp(sb - jnp.max(sb, axis=-1, keepdims=True))
            sums.append(jnp.sum(e, axis=-1, keepdims=True))
            p_ref[slot, rows, lo:lo + ATTN_WIN] = e.astype(BF16)
        p, hh = divmod(h, 2)
        o = _dot(p_ref[slot], pair_rows((v0_ref, v1_ref, v2_ref), p)) / jnp.concatenate(sums, axis=0)
        if hh == 0:
            o_first = o
        else:
            o_ref[0, :, p * LANES:(p + 1) * LANES] = jnp.where(masks[0], o_first, o).astype(o_ref.dtype)


def _attn_prompt(q, k, v, bias_rows):
    b, t, _ = q.shape
    blk = lambda off: pl.BlockSpec(
        (1, ATTN_QB, W_A), lambda bb, i, off=off: (bb, jnp.maximum(i - off, 0), 0))
    return pl.pallas_call(
        _attn_prompt_kernel,
        grid=(b, t // ATTN_QB),
        in_specs=[blk(0), blk(2), blk(1), blk(0), blk(2), blk(1), blk(0),
                  _full_spec(bias_rows.shape)],
        out_specs=blk(0),
        out_shape=jax.ShapeDtypeStruct((b, t, W_A), BF16),
        scratch_shapes=[pltpu.VMEM((3, H_A, ATTN_QB, ATTN_KB), F32),
                        pltpu.VMEM((2, ATTN_QB, ATTN_KB), F32),
                        pltpu.VMEM((2, ATTN_QB, ATTN_KB), BF16)],
        compiler_params=_params(("arbitrary", "arbitrary")),
        name="attn_prompt",
    )(q, k, k, k, v, v, v, bias_rows)


def _attn_sample_kernel(q_ref, kn_ref, vn_ref, kc_ref, vc_ref, row_ref, o_ref, bc_ref, bn_ref):
    nb, s_len, _ = q_ref.shape
    w = kc_ref.shape[2]

    @pl.when(pl.program_id(0) == 0)
    def _():
        for h in range(H_A):
            t = _toeplitz_bias(row_ref, h, s_len)
            bc_ref[h * s_len:(h + 1) * s_len, :] = t[:, :w]
            bn_ref[h * s_len:(h + 1) * s_len, :] = t[:, w:w + s_len]

    lane_head = lax.broadcasted_iota(jnp.int32, (1, W_A), 1) // HD_A
    row_head = lax.broadcasted_iota(jnp.int32, (H_A * s_len, 1), 0) // s_len
    own_head = row_head == lane_head
    for b in range(nb):
        q = q_ref[b] * BF16(HD_A ** -0.5)
        qs = jnp.concatenate([q] * H_A, axis=0)
        qs = jnp.where(own_head, qs, jnp.zeros_like(qs))
        sc = _dot(qs, kc_ref[b].astype(BF16)) + bc_ref[...]
        sn = _dot_nt(qs, kn_ref[b]) + bn_ref[...]
        m = jnp.maximum(jnp.max(sc, axis=-1, keepdims=True), jnp.max(sn, axis=-1, keepdims=True))
        ec = jnp.exp(sc - m)
        en = jnp.exp(sn - m)
        l = jnp.sum(ec, axis=-1, keepdims=True) + jnp.sum(en, axis=-1, keepdims=True)
        full = (_dot_nt(ec.astype(BF16), vc_ref[b].astype(BF16))
                + _dot(en.astype(BF16), vn_ref[b])) / l
        o = full[:s_len]
        for h in range(1, H_A):
            o = jnp.where(lane_head == h, full[h * s_len:(h + 1) * s_len], o)
        o_ref[b] = o.astype(o_ref.dtype)


def _attn_sample(q, kn, vn, kc, vc, bias_rows, nb):
    b, s, _ = q.shape
    w = kc.shape[2]
    assert w == WINDOW_A
    new = pl.BlockSpec((nb, s, W_A), lambda i: (i, 0, 0))
    cache = pl.BlockSpec((nb, W_A, w), lambda i: (i, 0, 0))
    return pl.pallas_call(
        _attn_sample_kernel,
        grid=(b // nb,),
        in_specs=[new, new, new, cache, cache, _full_spec(bias_rows.shape)],
        out_specs=new,
        out_shape=jax.ShapeDtypeStruct((b, s, W_A), BF16),
        scratch_shapes=[pltpu.VMEM((H_A * s, w), F32), pltpu.VMEM((H_A * s, s), F32)],
        compiler_params=_params(("arbitrary",)),
        name="attn_sample",
    )(q, kn, vn, kc, vc, bias_rows)


def _gla_kernel(*refs, has_init):
    if has_init:
        q_ref, k_ref, v_ref, g_ref, la_ref, gg_ref, s0_ref, y_ref, so_ref, st_ref = refs
    else:
        q_ref, k_ref, v_ref, g_ref, la_ref, gg_ref, y_ref, so_ref, st_ref = refs
    nb, c, _ = q_ref.shape
    nsub = c // GLA_SUB
    npair = H_B // 2
    j = pl.program_id(1)

    @pl.when(j == 0)
    def _():
        if has_init:
            st_ref[...] = s0_ref[...]
        else:
            st_ref[...] = jnp.zeros_like(st_ref)

    r2 = lax.broadcasted_iota(jnp.int32, (2 * c, c), 0)
    s2 = lax.broadcasted_iota(jnp.int32, (2 * c, c), 1)
    t2 = r2 & (c - 1)
    same_sub = (s2 // GLA_SUB) == (t2 // GLA_SUB)
    sum_mat = jnp.where((s2 <= t2) & ((r2 < c) | same_sub), 1.0, 0.0).astype(F32)

    row = lax.broadcasted_iota(jnp.int32, (c, W_BK), 0)
    row2 = lax.broadcasted_iota(jnp.int32, (2 * c, LANES), 0)
    lane2 = lax.broadcasted_iota(jnp.int32, (2 * c, LANES), 1)
    own = (lane2 < DK_B) == (row2 < c)
    sub_of_row2 = (row2 & (c - 1)) // GLA_SUB
    tril2 = lane2 <= (row2 & (c - 1))
    eye = (lax.broadcasted_iota(jnp.int32, (LANES, LANES), 0)
           == lax.broadcasted_iota(jnp.int32, (LANES, LANES), 1))
    scale = DK_B ** -0.5
    zeros_k = jnp.zeros((LANES - c, nsub * LANES), BF16)
    zeros_v = jnp.zeros((LANES - c, DV_B), BF16)

    prep = []
    for b in range(nb):
        la = la_ref[b]
        sums = jnp.dot(sum_mat, la, precision=lax.Precision.HIGHEST, preferred_element_type=F32)
        cum, cum_sub = sums[:c], sums[c:]
        cum_end = cum[c - 1:c, :]
        qf = q_ref[b].astype(F32)
        kf = k_ref[b].astype(F32)
        q_sub = qf * jnp.exp(cum_sub) * scale
        q_in = (qf * jnp.exp(cum) * scale).astype(BF16)
        k_end = (kf * jnp.exp(cum_end - cum)).astype(BF16)
        k_sub = []
        for i in range(nsub):
            ref_i = cum[i * GLA_SUB - 1:i * GLA_SUB, :] if i else jnp.zeros((1, W_BK), F32)
            k_i = jnp.where(row < (i + 1) * GLA_SUB, kf * jnp.exp(ref_i - cum), 0.0)
            k_sub.append(k_i.astype(BF16))
        prep.append((q_sub, q_in, k_end, k_sub, cum_end))

    att = {}
    for b in range(nb):
        q_sub, _, _, k_sub, _ = prep[b]
        for p in range(npair):
            sl = slice(p * LANES, (p + 1) * LANES)
            k_stack = jnp.concatenate(
                [jnp.concatenate([k_i[:, sl] for k_i in k_sub], axis=1), zeros_k], axis=0)
            q2 = jnp.where(own, jnp.concatenate([q_sub[:, sl], q_sub[:, sl]], axis=0), 0.0)
            q_stack = jnp.concatenate(
                [jnp.where(sub_of_row2 == i, q2, 0.0) for i in range(nsub)], axis=1).astype(BF16)
            a = _dot_nt(q_stack, k_stack)
            att[b, p] = jnp.where(tril2, a, 0.0).astype(BF16)

    for b in range(nb):
        q_in = prep[b][1]
        for p in range(npair):
            sl = slice(p * LANES, (p + 1) * LANES)
            st_b = st_ref[b, p].astype(BF16)
            q_in2 = jnp.where(own, jnp.concatenate([q_in[:, sl], q_in[:, sl]], axis=0),
                              jnp.zeros((), BF16))
            for hh in range(2):
                h = 2 * p + hh
                hs = slice(h * DV_B, (h + 1) * DV_B)
                rows = slice(hh * c, (hh + 1) * c)
                lhs = jnp.concatenate([att[b, p][rows], q_in2[rows]], axis=1)
                rhs = jnp.concatenate([v_ref[b, :, hs], zeros_v, st_b], axis=0)
                o = _dot(lhs, rhs)
                gate = g_ref[b, :, hs].astype(F32)
                y = (o * _rms_scale(o)) * gg_ref[...] * (gate * _sigmoid(gate))
                y_ref[b, :, hs] = y.astype(y_ref.dtype)

    for b in range(nb):
        k_end, cum_end = prep[b][2], prep[b][4]
        for p in range(npair):
            sl = slice(p * LANES, (p + 1) * LANES)
            k2 = jnp.where(own, jnp.concatenate([k_end[:, sl], k_end[:, sl]], axis=0),
                           jnp.zeros((), BF16))
            v2 = jnp.concatenate([v_ref[b, :, 2 * p * DV_B:(2 * p + 1) * DV_B],
                                  v_ref[b, :, (2 * p + 1) * DV_B:(2 * p + 2) * DV_B]], axis=0)
            upd = _dot_tn(k2, v2)
            dec = jnp.exp(jnp.sum(jnp.where(eye, cum_end[:, sl], 0.0), axis=1, keepdims=True))
            st_ref[b, p] = st_ref[b, p] * dec + upd

    @pl.when(j == pl.num_programs(1) - 1)
    def _():
        so_ref[...] = st_ref[...]


def _gla(q, k, v, g, la, g_gla, s0, nb, c):
    nbt, t, _ = q.shape
    tok = lambda w: pl.BlockSpec((nb, c, w), lambda b, j: (b, j, 0))
    st_spec = pl.BlockSpec((nb, H_B // 2, 2 * DK_B, DV_B), lambda b, j: (b, 0, 0, 0))
    in_specs = [tok(W_BK), tok(W_BK), tok(W_BV), tok(W_BV), tok(W_BK), _full_spec((1, DV_B))]
    args = [q, k, v, g, la, g_gla]
    if s0 is not None:
        in_specs.append(st_spec)
        args.append(s0)
    return pl.pallas_call(
        functools.partial(_gla_kernel, has_init=s0 is not None),
        grid=(nbt // nb, t // c),
        in_specs=in_specs,
        out_specs=[tok(W_BV), st_spec],
        out_shape=[jax.ShapeDtypeStruct((nbt, t, W_BV), BF16),
                   jax.ShapeDtypeStruct((nbt, H_B // 2, 2 * DK_B, DV_B), F32)],
        scratch_shapes=[pltpu.VMEM((nb, H_B // 2, DV_B, 2 * DK_B), F32)],
        compiler_params=_params(("parallel", "arbitrary")),
        name="gla",
    )(*args)


def _mixout_kernel(x_ref, ya_ref, yb_ref, ga_ref, gb_ref, gm_ref, gp_ref, wa_ref, wb_ref, wo_ref,
                   o_ref):
    nb, tt, _ = x_ref.shape
    if nb == 1:
        halves = [(slice(None), slice(s * tt // 2, (s + 1) * tt // 2)) for s in range(2)]
        nbh, tth = nb, tt // 2
    else:
        halves = [(slice(s * nb // 2, (s + 1) * nb // 2), slice(None)) for s in range(2)]
        nbh, tth = nb // 2, tt
    m = nbh * tth
    merged = []
    for bs, ts in halves:
        a = _dot(ya_ref[bs, ts, :].reshape(m, W_A), wa_ref[...])
        b = _dot(yb_ref[bs, ts, :].reshape(m, W_BV), wb_ref[...])
        ga = _sigmoid(ga_ref[bs, ts, :].reshape(m, D_MODEL).astype(F32))
        gb = _sigmoid(gb_ref[bs, ts, :].reshape(m, D_MODEL).astype(F32))
        merged.append((ga * a + gb * b).astype(BF16))
    for (bs, ts), mg in zip(halves, merged):
        mo = _dot(mg, wo_ref[...])
        n = ((mo * _rms_scale(mo)) * gp_ref[...]).reshape(nbh, tth, D_MODEL)
        o_ref[bs, ts, :] = x_ref[bs, ts, :] + gm_ref[bs] * n


def _mixout(x, ya, yb, ga, gb, mod, g_post, wa, wb, wo, nb, tt):
    nbt, t, _ = x.shape
    tok = lambda w: pl.BlockSpec((nb, tt, w), lambda b, i: (b, i, 0))
    return pl.pallas_call(
        _mixout_kernel,
        grid=(nbt // nb, t // tt),
        in_specs=[tok(D_MODEL), tok(W_A), tok(W_BV), tok(D_MODEL), tok(D_MODEL),
                  pl.BlockSpec((nb, 1, D_MODEL), lambda b, i: (b, 0, 2)),
                  _full_spec((1, D_MODEL)), _full_spec(wa.shape), _full_spec(wb.shape),
                  _full_spec(wo.shape)],
        out_specs=tok(D_MODEL),
        out_shape=jax.ShapeDtypeStruct(x.shape, F32),
        compiler_params=_params(("parallel", "parallel")),
        name="mixout",
    )(x, ya, yb, ga, gb, mod, g_post, wa, wb, wo)


def _gelu_tanh(x):
    c = float(np.sqrt(2.0 / np.pi))
    half = 0.5 * x
    return half + half * jnp.tanh(x * (c + (0.044715 * c) * (x * x)))


def _ffn_kernel(*refs, has_state):
    if has_state:
        (x_ref, shift_ref, scale_ref, gate_ref, gpre_ref, gpost_ref, wu_ref, wd_ref,
         wdw_ref, bdw_ref, st_ref, o_ref, tail_ref, h_ref, act_ref) = refs
    else:
        (x_ref, shift_ref, scale_ref, gate_ref, gpre_ref, gpost_ref, wu_ref, wd_ref,
         wdw_ref, bdw_ref, o_ref, tail_ref, h_ref, act_ref, carry_ref) = refs

        @pl.when(pl.program_id(1) == 0)
        def _():
            carry_ref[...] = jnp.zeros_like(carry_ref)

    nb, tt, _ = x_ref.shape
    m = nb * tt
    x = x_ref[...]
    h = (x * _rms_scale(x)) * gpre_ref[...] * (1.0 + scale_ref[...]) + shift_ref[...]
    h_ref[...] = h.reshape(m, D_MODEL).astype(BF16)
    ridx = lax.broadcasted_iota(jnp.int32, (nb, SUBLANES, FFN_FT), 1)

    def conv(u, lanes):
        u3 = u.reshape(nb, tt, FFN_FT)
        r1 = pltpu.roll(u, 1, 0).reshape(nb, tt, FFN_FT)
        r2 = pltpu.roll(u, 2, 0).reshape(nb, tt, FFN_FT)

        def taps(u_m2, u_m1, u_0):
            y = bdw_ref[:, lanes] + wdw_ref[0:1, lanes] * u_m2
            y = y + wdw_ref[1:2, lanes] * u_m1
            return y + wdw_ref[2:3, lanes] * u_0

        if has_state:
            p2, p1 = st_ref[:, 0:1, lanes], st_ref[:, 1:2, lanes]
        else:
            p2 = carry_ref[:, SUBLANES - 2:SUBLANES - 1, lanes]
            p1 = carry_ref[:, SUBLANES - 1:SUBLANES, lanes]
        h_m1 = jnp.where(ridx == 0, p1, r1[:, :SUBLANES])
        h_m2 = jnp.where(ridx == 0, p2, jnp.where(ridx == 1, p1, r2[:, :SUBLANES]))
        y = jnp.concatenate([taps(h_m2, h_m1, u3[:, :SUBLANES]),
                             taps(r2[:, SUBLANES:], r1[:, SUBLANES:], u3[:, SUBLANES:])], axis=1)
        tail = u3[:, tt - SUBLANES:, :]
        tail_ref[:, :, lanes] = tail
        if not has_state:
            carry_ref[:, :, lanes] = tail
        return y.reshape(m, FFN_FT)

    def up(f):
        hb = h_ref[...]
        return _dot(hb, wu_ref[f]), _dot(hb, wu_ref[FFN_NF + f])

    ua, ug = up(0)
    for f in range(FFN_NF):
        if f + 1 < FFN_NF:
            ua_next, ug_next = up(f + 1)
        cols = slice(f * FFN_FT, (f + 1) * FFN_FT)
        ya = conv(ua, cols)
        yg = conv(ug, slice(D_FF + f * FFN_FT, D_FF + (f + 1) * FFN_FT))
        act_ref[:, cols] = (_gelu_tanh(ya) * yg).astype(BF16)
        if f + 1 < FFN_NF:
            ua, ug = ua_next, ug_next
    yf = _dot(act_ref[...], wd_ref[...])
    n = ((yf * _rms_scale(yf)) * gpost_ref[...]).reshape(nb, tt, D_MODEL)
    o_ref[...] = x_ref[...] + gate_ref[...] * n


def _ffn(x, mod, g_pre, g_post, wu, wd, w_dw, b_dw, state, nb, tt):
    nbt, t, _ = x.shape
    m = nb * tt
    tok = pl.BlockSpec((nb, tt, D_MODEL), lambda b, i: (b, i, 0))
    mod_spec = lambda col: pl.BlockSpec((nb, 1, D_MODEL), lambda b, i, col=col: (b, 0, col))
    tail_spec = pl.BlockSpec((nb, SUBLANES, 2 * D_FF), lambda b, i: (b, 0, 0))
    in_specs = [tok, mod_spec(3), mod_spec(4), mod_spec(5), _full_spec((1, D_MODEL)),
                _full_spec((1, D_MODEL)), _full_spec(wu.shape),
                _full_spec(wd.shape), _full_spec(w_dw.shape), _full_spec(b_dw.shape)]
    args = [x, mod, mod, mod, g_pre, g_post, wu, wd, w_dw, b_dw]
    scratch = [pltpu.VMEM((m, D_MODEL), BF16), pltpu.VMEM((m, D_FF), BF16)]
    if state is not None:
        in_specs.append(pl.BlockSpec((nb, CONV_W - 1, 2 * D_FF), lambda b, i: (b, 0, 0)))
        args.append(state)
    else:
        scratch.append(pltpu.VMEM((nb, SUBLANES, 2 * D_FF), F32))
    return pl.pallas_call(
        functools.partial(_ffn_kernel, has_state=state is not None),
        grid=(nbt // nb, t // tt),
        in_specs=in_specs,
        out_specs=[tok, tail_spec],
        out_shape=[jax.ShapeDtypeStruct(x.shape, F32),
                   jax.ShapeDtypeStruct((nbt, SUBLANES, 2 * D_FF), F32)],
        scratch_shapes=scratch,
        compiler_params=_params(("parallel", "arbitrary")),
        name="ffn",
    )(*args)


def _bias_rows(rel_bias):
    assert BIAS_ROW >= ATTN_KB + ATTN_QB - 1 and WINDOW_A == ATTN_KB - ATTN_QB
    far_pos = jnp.broadcast_to(rel_bias[:, -1:], (H_A, BIAS_ROW))
    far_neg = jnp.broadcast_to(rel_bias[:, :1], (H_A, BIAS_ROW))
    n_mid = 2 * MAX_REL + 1
    n_lo = ATTN_KB + 1 - (WINDOW_A - MAX_REL) - n_mid
    rows = jnp.concatenate([far_pos[:, :WINDOW_A - MAX_REL], rel_bias[:, ::-1], far_neg[:, :n_lo],
                            far_pos[:, :BIAS_ROW - ATTN_KB - 1]], axis=1)
    return rows.astype(F32).reshape(H_A, 1, BIAS_ROW)


def _layer(x, mod, cache, s_gla, s_conv, w, first_chunk):
    nbt, t, _ = x.shape
    if first_chunk:
        nb, tt = 1, 1024
        nb_f, tt_f = 1, 512
    else:
        nb, tt = 1024 // t, t
        nb_f, tt_f = 512 // t, t
    keep_rows = min(WINDOW_A, t) if first_chunk else 0
    qa, ka, va, qb, kb, vb, gb, gate_a, gate_b, log_a, *kv_t = _inproj(
        x, mod, w['g_pre_mix'], w['w_main'], w['w_gk1'], w['w_gk2'], w['b_gk'], nb, tt, keep_rows)
    if first_chunk:
        ya = _attn_prompt(qa, ka, va, w['bias_rows'])
        k_keep, v_keep = (jnp.transpose(a.reshape(nbt, H_A, HD_A, keep_rows), (0, 3, 1, 2))
                          for a in kv_t)
        yb, s_new = _gla(qb, kb, vb, gb, log_a, w['g_gla'], None, nbt, CHUNK)
    else:
        k_cache, v_cache = cache
        ya = _attn_sample(qa, ka, va, k_cache, v_cache, w['bias_rows'], 4)
        k_keep, v_keep = (a.astype(F32).reshape(nbt, t, H_A, HD_A) for a in (ka, va))
        yb, s_new = _gla(qb, kb, vb, gb, log_a, w['g_gla'], s_gla, 4, t)
    x1 = _mixout(x, ya, yb, gate_a, gate_b, mod, w['g_post_mix'], w['w_br_a'], w['w_br_b'],
                 w['w_out'], nb, tt)
    y, tail = _ffn(x1, mod, w['g_pre_ffn'], w['g_post_ffn'], w['w_up'], w['w_down'],
                   w['w_dw'], w['b_dw'], None if first_chunk else s_conv, nb_f, tt_f)
    return (y, k_keep, v_keep, s_new.reshape(nbt, H_B, DK_B, DV_B),
            tail[:, SUBLANES - (CONV_W - 1):, :])


def _prep_weights(w_in, w_gk2, b_gk, rel_bias, g_gla, w_br_a, w_br_b, w_out, w_up, w_dw, b_dw,
                  w_down, g_pre_mix, g_post_mix, g_pre_ffn, g_post_ffn):
    w_main, w_gk1 = _split_w_in(w_in.T, 256)
    w_gk2p = jnp.pad(w_gk2, ((0, LANES - GK_RANK), (0, 0))).astype(BF16)
    w_br_a16, w_br_b16, w_out16, w_down16 = _cast_rows((w_br_a, w_br_b, w_out, w_down), 4)
    row = lambda a: a.reshape(1, -1)
    return {
        'w_main': w_main, 'w_gk1': w_gk1, 'w_gk2': w_gk2p, 'b_gk': row(b_gk),
        'bias_rows': _bias_rows(rel_bias),
        'g_gla': row(g_gla), 'w_br_a': w_br_a16, 'w_br_b': w_br_b16, 'w_out': w_out16,
        'w_up': _cast_col_blocks(w_up, FFN_FT, 2), 'w_down': w_down16,
        'w_dw': w_dw, 'b_dw': row(b_dw),
        'g_pre_mix': row(g_pre_mix), 'g_post_mix': row(g_post_mix),
        'g_pre_ffn': row(g_pre_ffn), 'g_post_ffn': row(g_post_ffn),
    }


def kernel(x_prompt, x_sample, cache_k_a, cache_v_a, state_gla, state_conv, c_prompt, c_sample, w_ada, b_ada, g_pre_mix, g_post_mix, g_pre_ffn, g_post_ffn, w_in, w_gk2, b_gk, rel_bias, g_gla, w_br_a, w_br_b, w_out, w_up, w_dw, b_dw, w_down):
    depth = w_ada.shape[0]
    assert depth == 1
    bp, bs = x_prompt.shape[0], x_sample.shape[0]
    s_len = x_sample.shape[1]
    cache_rows = cache_k_a.shape[2]
    yp, ys = x_prompt, x_sample
    outs = [[] for _ in range(8)]
    for l in range(depth):
        w = _prep_weights(w_in[l], w_gk2[l], b_gk[l], rel_bias[l], g_gla[l], w_br_a[l], w_br_b[l],
                          w_out[l], w_up[l], w_dw[l], b_dw[l], w_down[l], g_pre_mix[l],
                          g_post_mix[l], g_pre_ffn[l], g_post_ffn[l])
        c_all = jnp.concatenate([c_prompt, c_sample], axis=0)
        pad = (-c_all.shape[0]) % SUBLANES
        mod = _adaln(jnp.pad(c_all, ((0, pad), (0, 0))), w_ada[l], b_ada[l])
        mod_p = mod[:bp].reshape(bp, 1, 6 * D_MODEL)
        mod_s = mod[bp:bp + bs].reshape(bs, 1, 6 * D_MODEL)
        yp, kp, vp, gp, cp = _layer(yp, mod_p, None, None, None, w, True)
        to_t = lambda c: jnp.transpose(c, (0, 2, 3, 1)).reshape(bs, W_A, cache_rows)
        cache = (to_t(cache_k_a[l]), to_t(cache_v_a[l]))
        s0 = state_gla[l].reshape(bs, H_B // 2, 2 * DK_B, DV_B)
        ys, kn, vn, gn, cn = _layer(ys, mod_s, cache, s0, state_conv[l], w, False)
        for lst, a in zip(outs, (kp, vp, gp, cp, kn, vn, gn, cn)):
            lst.append(a)
    return (yp, ys) + tuple(jnp.stack(lst) for lst in outs)
```

```python
import functools

import jax
import jax.numpy as jnp
import numpy as np
from jax import lax
from jax.experimental import pallas as pl
from jax.experimental.pallas import tpu as pltpu

D_MODEL = 1024
CHUNK = 64
BAND_CHUNKS = 8
WINDOW_A = BAND_CHUNKS * CHUNK
H_A = 8
HD_A = 64
MAX_REL = 128
H_B = 4
DK_B = 64
DV_B = 128
GK_RANK = 16
GK_NORM = 16.0
GLA_SUB = 16
D_FF = 2816
CONV_W = 3
EPS = 1e-6
NEG_INF = -1e30
PAST_LEN = 2048

W_A = H_A * HD_A
W_BK = H_B * DK_B
W_BV = H_B * DV_B

LANES = 128
SUBLANES = 8
VMEM_LIMIT = 56 * 1024 * 1024

ATTN_QB = 256
ATTN_KB = 3 * ATTN_QB
BIAS_ROW = 1024
FFN_FT = 256
FFN_NF = D_FF // FFN_FT

BF16 = jnp.bfloat16
F32 = jnp.float32


def _params(sem):
    return pltpu.CompilerParams(dimension_semantics=sem, vmem_limit_bytes=VMEM_LIMIT)


def _full_spec(shape):
    nd = len(shape)
    return pl.BlockSpec(shape, lambda *_: (0,) * nd, pipeline_mode=pl.Buffered(1))


def _dot(a, b):
    return jnp.dot(a, b, preferred_element_type=F32)


def _dot_nt(a, b):
    return lax.dot_general(a, b, (((1,), (1,)), ((), ())), preferred_element_type=F32)


def _dot_tn(a, b):
    return lax.dot_general(a, b, (((0,), (0,)), ((), ())), preferred_element_type=F32)


def _sigmoid(x):
    return 1.0 / (1.0 + jnp.exp(-x))


def _rms_scale(x):
    return lax.rsqrt(jnp.mean(x * x, axis=-1, keepdims=True) + EPS)


_GK_LO = 3 * W_A + 2 * W_BK + 2 * W_BV


def _split_w_in_kernel(w_ref, main_ref, gk_ref):
    main_ref[:_GK_LO, :] = w_ref[:_GK_LO, :].astype(BF16)
    main_ref[_GK_LO:, :] = w_ref[_GK_LO + GK_RANK:, :].astype(BF16)
    gk_ref[:GK_RANK, :] = w_ref[_GK_LO:_GK_LO + GK_RANK, :].astype(BF16)
    gk_ref[GK_RANK:, :] = jnp.zeros((LANES - GK_RANK, gk_ref.shape[1]), BF16)


def _split_w_in(w_in_t, cols):
    d_in, d = w_in_t.shape
    return pl.pallas_call(
        _split_w_in_kernel,
        grid=(d // cols,),
        in_specs=[pl.BlockSpec((d_in, cols), lambda i: (0, i))],
        out_specs=[pl.BlockSpec((d_in - GK_RANK, cols), lambda i: (0, i)),
                   pl.BlockSpec((LANES, cols), lambda i: (0, i))],
        out_shape=[jax.ShapeDtypeStruct((d_in - GK_RANK, d), BF16),
                   jax.ShapeDtypeStruct((LANES, d), BF16)],
        compiler_params=_params(("parallel",)),
        name="split_w_in",
    )(w_in_t)


def _adaln_kernel(c_ref, w_ref, b_ref, o_ref):
    c = c_ref[...]
    s = (c * _sigmoid(c)).astype(BF16)
    o_ref[...] = _dot(s, w_ref[...].astype(BF16)) + b_ref[...]


def _adaln(c_all, w_ada, b_ada):
    rows = c_all.shape[0]
    n = w_ada.shape[1]
    tn = D_MODEL
    return pl.pallas_call(
        _adaln_kernel,
        grid=(n // tn,),
        in_specs=[
            pl.BlockSpec((rows, D_MODEL), lambda j: (0, 0)),
            pl.BlockSpec((D_MODEL, tn), lambda j: (0, j)),
            pl.BlockSpec((1, tn), lambda j: (0, j)),
        ],
        out_specs=pl.BlockSpec((rows, tn), lambda j: (0, j)),
        out_shape=jax.ShapeDtypeStruct((rows, n), F32),
        compiler_params=_params(("parallel",)),
        name="adaln",
    )(c_all, w_ada, b_ada.reshape(1, n))


_IN_GROUPS = (W_A, W_A, W_A, W_BK, W_BK, W_BV, W_BV, D_MODEL, D_MODEL)
_IN_MAIN = sum(_IN_GROUPS)


def _inproj_kernel(*refs, n_cast, n_kv_t):
    (x_ref, shift_ref, scale_ref, g_ref, wm_ref, wg1_ref, wg2_ref, bgk_ref), refs = refs[:8], refs[8:]
    cast_in, refs = refs[:n_cast], refs[n_cast:]
    (qa_ref, ka_ref, va_ref, qb_ref, kb_ref, vb_ref, gb_ref, ga_ref, gtb_ref, la_ref) = refs[:10]
    kv_t_refs, cast_out = refs[10:10 + n_kv_t], refs[10 + n_kv_t:]
    for w_ref, o_ref in zip(cast_in, cast_out):
        o_ref[...] = w_ref[...].astype(o_ref.dtype)
    nb, tt, _ = x_ref.shape
    x = x_ref[...]
    h = (x * _rms_scale(x)) * g_ref[...] * (1.0 + scale_ref[...]) + shift_ref[...]
    hb = h.reshape(nb * tt, D_MODEL).astype(BF16)
    if kv_t_refs:
        @pl.when(pl.program_id(1) == pl.num_programs(1) - 1)
        def _():
            rows = kv_t_refs[0].shape[2]
            newest = hb[nb * tt - rows:, :]
            for o_ref, lo in zip(kv_t_refs, (W_A, 2 * W_A)):
                o_ref[0] = _dot_nt(wm_ref[lo:lo + W_A, :], newest)
    outs = (qa_ref, ka_ref, va_ref, qb_ref, kb_ref, vb_ref, gb_ref, ga_ref, gtb_ref)
    lo = 0
    for o_ref, w in zip(outs, _IN_GROUPS):
        z = _dot_nt(hb, wm_ref[lo:lo + w, :])
        o_ref[...] = z.reshape(nb, tt, w).astype(o_ref.dtype)
        lo += w
    gk_low = _dot_nt(hb, wg1_ref[...]).astype(BF16)
    gk = _dot(gk_low, wg2_ref[...]) + bgk_ref[...]
    log_a = (jnp.minimum(gk, 0.0) - jnp.log1p(jnp.exp(-jnp.abs(gk)))) / GK_NORM
    la_ref[...] = log_a.reshape(nb, tt, W_BK)


def _inproj(x, mod, g_pre, wm, wg1, wg2, bgk, nb, tt, keep_rows, cast_ws=()):
    nbt, t, _ = x.shape
    grid = (nbt // nb, t // tt)
    steps = grid[0] * grid[1]
    cast_specs = [pl.BlockSpec((w.shape[0] // steps, w.shape[1]),
                               lambda b, i: (b * grid[1] + i, 0)) for w in cast_ws]
    tok = lambda w: pl.BlockSpec((nb, tt, w), lambda b, i: (b, i, 0))
    mod_spec = lambda col: pl.BlockSpec((nb, 1, D_MODEL), lambda b, i, col=col: (b, 0, col))
    widths = _IN_GROUPS + (W_BK,)
    dtypes = (BF16,) * len(_IN_GROUPS) + (F32,)
    out_specs = [tok(w) for w in widths]
    out_shape = [jax.ShapeDtypeStruct((nbt, t, w), dt) for w, dt in zip(widths, dtypes)]
    if keep_rows:
        assert nb == 1 and keep_rows <= tt
        out_specs += [pl.BlockSpec((1, W_A, keep_rows), lambda b, i: (b, 0, 0))] * 2
        out_shape += [jax.ShapeDtypeStruct((nbt, W_A, keep_rows), F32)] * 2
    return pl.pallas_call(
        functools.partial(_inproj_kernel, n_cast=len(cast_ws), n_kv_t=2 if keep_rows else 0),
        grid=grid,
        in_specs=[tok(D_MODEL), mod_spec(0), mod_spec(1), _full_spec((1, D_MODEL)),
                  _full_spec(wm.shape), _full_spec(wg1.shape), _full_spec(wg2.shape),
                  _full_spec((1, W_BK))] + cast_specs,
        out_specs=out_specs + cast_specs,
        out_shape=out_shape + [jax.ShapeDtypeStruct(w.shape, BF16) for w in cast_ws],
        compiler_params=_params(("arbitrary", "arbitrary")),
        name="inproj",
    )(x, mod, mod, g_pre, wm, wg1, wg2, bgk, *cast_ws)


def _head_masks():
    lane = lax.broadcasted_iota(jnp.int32, (1, LANES), 1)
    first = lane < HD_A
    return first, jnp.logical_not(first)


def _toeplitz_bias(row_ref, h, rows):
    rb = jnp.broadcast_to(row_ref[h], (rows, BIAS_ROW))
    return pltpu.roll(rb, 0, 1, stride=1, stride_axis=0)


ATTN_RB = 32
ATTN_WIN = 640


def _attn_prompt_kernel(q_ref, k0_ref, k1_ref, k2_ref, v0_ref, v1_ref, v2_ref, row_ref, o_ref,
                        bias_ref, s_ref, p_ref):
    i = pl.program_id(1)

    @pl.when((pl.program_id(0) == 0) & (i == 0))
    def _():
        qc = lax.broadcasted_iota(jnp.int32, (ATTN_QB, ATTN_KB), 0) // CHUNK
        col = lax.broadcasted_iota(jnp.int32, (ATTN_QB, ATTN_KB), 1)
        kc = col // CHUNK - BAND_CHUNKS
        valid = (kc <= qc) & (kc >= qc - BAND_CHUNKS)
        for h in range(H_A):
            t = _toeplitz_bias(row_ref, h, ATTN_QB)
            band = jnp.where(valid, t[:, :ATTN_KB], NEG_INF)
            bias_ref[0, h] = jnp.where(col >= 2 * ATTN_QB, band, NEG_INF)
            bias_ref[1, h] = jnp.where(col >= ATTN_QB, band, NEG_INF)
            bias_ref[2, h] = band
        p_ref[...] = jnp.zeros_like(p_ref)

    var = jnp.minimum(i, 2)
    masks = _head_masks()

    def pair_rows(refs, p):
        sl = slice(p * LANES, (p + 1) * LANES)
        return jnp.concatenate([r[0, :, sl] for r in refs], axis=0)

    def scores(h):
        p, hh = divmod(h, 2)
        qp = q_ref[0, :, p * LANES:(p + 1) * LANES] * BF16(HD_A ** -0.5)
        qm = jnp.where(masks[hh], qp, jnp.zeros_like(qp))
        s_ref[h % 2] = _dot_nt(qm, pair_rows((k0_ref, k1_ref, k2_ref), p)) + bias_ref[var, h]

    scores(0)
    o_first = None
    for h in range(H_A):
        if h + 1 < H_A:
            scores(h + 1)
        slot = h % 2
        sums = []
        for r in range(ATTN_QB // ATTN_RB):
            rows = slice(r * ATTN_RB, (r + 1) * ATTN_RB)
            lo = 0 if r * ATTN_RB < ATTN_QB // 2 else ATTN_KB - ATTN_WIN
            sb = s_ref[slot, rows, lo:lo + ATTN_WIN]
            e = jnp.exp(sb - jnp.max(sb, axis=-1, keepdims=True))
            sums.append(jnp.sum(e, axis=-1, keepdims=True))
            p_ref[slot, rows, lo:lo + ATTN_WIN] = e.astype(BF16)
        p, hh = divmod(h, 2)
        o = _dot(p_ref[slot], pair_rows((v0_ref, v1_ref, v2_ref), p)) / jnp.concatenate(sums, axis=0)
        if hh == 0:
            o_first = o
        else:
            o_ref[0, :, p * LANES:(p + 1) * LANES] = jnp.where(masks[0], o_first, o).astype(o_ref.dtype)


def _attn_prompt(q, k, v, bias_rows):
    b, t, _ = q.shape
    blk = lambda off: pl.BlockSpec(
        (1, ATTN_QB, W_A), lambda bb, i, off=off: (bb, jnp.maximum(i - off, 0), 0))
    return pl.pallas_call(
        _attn_prompt_kernel,
        grid=(b, t // ATTN_QB),
        in_specs=[blk(0), blk(2), blk(1), blk(0), blk(2), blk(1), blk(0),
                  _full_spec(bias_rows.shape)],
        out_specs=blk(0),
        out_shape=jax.ShapeDtypeStruct((b, t, W_A), BF16),
        scratch_shapes=[pltpu.VMEM((3, H_A, ATTN_QB, ATTN_KB), F32),
                        pltpu.VMEM((2, ATTN_QB, ATTN_KB), F32),
                        pltpu.VMEM((2, ATTN_QB, ATTN_KB), BF16)],
        compiler_params=_params(("arbitrary", "arbitrary")),
        name="attn_prompt",
    )(q, k, k, k, v, v, v, bias_rows)


def _attn_sample_kernel(q_ref, kn_ref, vn_ref, kc_ref, vc_ref, row_ref, o_ref, bc_ref, bn_ref):
    nb, s_len, _ = q_ref.shape
    w = kc_ref.shape[2]

    @pl.when(pl.program_id(0) == 0)
    def _():
        for h in range(H_A):
            t = _toeplitz_bias(row_ref, h, s_len)
            bc_ref[h * s_len:(h + 1) * s_len, :] = t[:, :w]
            bn_ref[h * s_len:(h + 1) * s_len, :] = t[:, w:w + s_len]

    lane_head = lax.broadcasted_iota(jnp.int32, (1, W_A), 1) // HD_A
    row_head = lax.broadcasted_iota(jnp.int32, (H_A * s_len, 1), 0) // s_len
    own_head = row_head == lane_head
    for b in range(nb):
        q = q_ref[b] * BF16(HD_A ** -0.5)
        qs = jnp.concatenate([q] * H_A, axis=0)
        qs = jnp.where(own_head, qs, jnp.zeros_like(qs))
        sc = _dot(qs, kc_ref[b].astype(BF16)) + bc_ref[...]
        sn = _dot_nt(qs, kn_ref[b]) + bn_ref[...]
        m = jnp.maximum(jnp.max(sc, axis=-1, keepdims=True), jnp.max(sn, axis=-1, keepdims=True))
        ec = jnp.exp(sc - m)
        en = jnp.exp(sn - m)
        l = jnp.sum(ec, axis=-1, keepdims=True) + jnp.sum(en, axis=-1, keepdims=True)
        full = (_dot_nt(ec.astype(BF16), vc_ref[b].astype(BF16))
                + _dot(en.astype(BF16), vn_ref[b])) / l
        o = full[:s_len]
        for h in range(1, H_A):
            o = jnp.where(lane_head == h, full[h * s_len:(h + 1) * s_len], o)
        o_ref[b] = o.astype(o_ref.dtype)


def _attn_sample(q, kn, vn, kc, vc, bias_rows, nb):
    b, s, _ = q.shape
    w = kc.shape[2]
    assert w == WINDOW_A
    new = pl.BlockSpec((nb, s, W_A), lambda i: (i, 0, 0))
    cache = pl.BlockSpec((nb, W_A, w), lambda i: (i, 0, 0))
    return pl.pallas_call(
        _attn_sample_kernel,
        grid=(b // nb,),
        in_specs=[new, new, new, cache, cache, _full_spec(bias_rows.shape)],
        out_specs=new,
        out_shape=jax.ShapeDtypeStruct((b, s, W_A), BF16),
        scratch_shapes=[pltpu.VMEM((H_A * s, w), F32), pltpu.VMEM((H_A * s, s), F32)],
        compiler_params=_params(("arbitrary",)),
        name="attn_sample",
    )(q, kn, vn, kc, vc, bias_rows)


def _gla_kernel(*refs, has_init):
    if has_init:
        q_ref, k_ref, v_ref, g_ref, la_ref, gg_ref, s0_ref, y_ref, so_ref, st_ref = refs
    else:
        q_ref, k_ref, v_ref, g_ref, la_ref, gg_ref, y_ref, so_ref, st_ref = refs
    nb, c, _ = q_ref.shape
    nsub = c // GLA_SUB
    npair = H_B // 2
    j = pl.program_id(1)

    @pl.when(j == 0)
    def _():
        if has_init:
            st_ref[...] = s0_ref[...]
        else:
            st_ref[...] = jnp.zeros_like(st_ref)

    r2 = lax.broadcasted_iota(jnp.int32, (2 * c, c), 0)
    s2 = lax.broadcasted_iota(jnp.int32, (2 * c, c), 1)
    t2 = r2 & (c - 1)
    same_sub = (s2 // GLA_SUB) == (t2 // GLA_SUB)
    sum_mat = jnp.where((s2 <= t2) & ((r2 < c) | same_sub), 1.0, 0.0).astype(F32)

    row = lax.broadcasted_iota(jnp.int32, (c, W_BK), 0)
    row2 = lax.broadcasted_iota(jnp.int32, (2 * c, LANES), 0)
    lane2 = lax.broadcasted_iota(jnp.int32, (2 * c, LANES), 1)
    own = (lane2 < DK_B) == (row2 < c)
    sub_of_row2 = (row2 & (c - 1)) // GLA_SUB
    tril2 = lane2 <= (row2 & (c - 1))
    eye = (lax.broadcasted_iota(jnp.int32, (LANES, LANES), 0)
           == lax.broadcasted_iota(jnp.int32, (LANES, LANES), 1))
    scale = DK_B ** -0.5
    zeros_k = jnp.zeros((LANES - c, nsub * LANES), BF16)
    zeros_v = jnp.zeros((LANES - c, DV_B), BF16)

    prep = []
    for b in range(nb):
        la = la_ref[b]
        sums = jnp.dot(sum_mat, la, precision=lax.Precision.HIGHEST, preferred_element_type=F32)
        cum, cum_sub = sums[:c], sums[c:]
        cum_end = cum[c - 1:c, :]
        qf = q_ref[b].astype(F32)
        kf = k_ref[b].astype(F32)
        q_sub = qf * jnp.exp(cum_sub) * scale
        q_in = (qf * jnp.exp(cum) * scale).astype(BF16)
        k_end = (kf * jnp.exp(cum_end - cum)).astype(BF16)
        k_sub = []
        for i in range(nsub):
            ref_i = cum[i * GLA_SUB - 1:i * GLA_SUB, :] if i else jnp.zeros((1, W_BK), F32)
            k_i = jnp.where(row < (i + 1) * GLA_SUB, kf * jnp.exp(ref_i - cum), 0.0)
            k_sub.append(k_i.astype(BF16))
        prep.append((q_sub, q_in, k_end, k_sub, cum_end))

    att = {}
    for b in range(nb):
        q_sub, _, _, k_sub, _ = prep[b]
        for p in range(npair):
            sl = slice(p * LANES, (p + 1) * LANES)
            k_stack = jnp.concatenate(
                [jnp.concatenate([k_i[:, sl] for k_i in k_sub], axis=1), zeros_k], axis=0)
            q2 = jnp.where(own, jnp.concatenate([q_sub[:, sl], q_sub[:, sl]], axis=0), 0.0)
            q_stack = jnp.concatenate(
                [jnp.where(sub_of_row2 == i, q2, 0.0) for i in range(nsub)], axis=1).astype(BF16)
            a = _dot_nt(q_stack, k_stack)
            att[b, p] = jnp.where(tril2, a, 0.0).astype(BF16)

    for b in range(nb):
        q_in = prep[b][1]
        for p in range(npair):
            sl = slice(p * LANES, (p + 1) * LANES)
            st_b = st_ref[b, p].astype(BF16)
            q_in2 = jnp.where(own, jnp.concatenate([q_in[:, sl], q_in[:, sl]], axis=0),
                              jnp.zeros((), BF16))
            for hh in range(2):
                h = 2 * p + hh
                hs = slice(h * DV_B, (h + 1) * DV_B)
                rows = slice(hh * c, (hh + 1) * c)
                lhs = jnp.concatenate([att[b, p][rows], q_in2[rows]], axis=1)
                rhs = jnp.concatenate([v_ref[b, :, hs], zeros_v, st_b], axis=0)
                o = _dot(lhs, rhs)
                gate = g_ref[b, :, hs].astype(F32)
                y = (o * _rms_scale(o)) * gg_ref[...] * (gate * _sigmoid(gate))
                y_ref[b, :, hs] = y.astype(y_ref.dtype)

    for b in range(nb):
        k_end, cum_end = prep[b][2], prep[b][4]
        for p in range(npair):
            sl = slice(p * LANES, (p + 1) * LANES)
            k2 = jnp.where(own, jnp.concatenate([k_end[:, sl], k_end[:, sl]], axis=0),
                           jnp.zeros((), BF16))
            v2 = jnp.concatenate([v_ref[b, :, 2 * p * DV_B:(2 * p + 1) * DV_B],
                                  v_ref[b, :, (2 * p + 1) * DV_B:(2 * p + 2) * DV_B]], axis=0)
            upd = _dot_tn(k2, v2)
            dec = jnp.exp(jnp.sum(jnp.where(eye, cum_end[:, sl], 0.0), axis=1, keepdims=True))
            st_ref[b, p] = st_ref[b, p] * dec + upd

    @pl.when(j == pl.num_programs(1) - 1)
    def _():
        so_ref[...] = st_ref[...]


def _gla(q, k, v, g, la, g_gla, s0, nb, c):
    nbt, t, _ = q.shape
    tok = lambda w: pl.BlockSpec((nb, c, w), lambda b, j: (b, j, 0))
    st_spec = pl.BlockSpec((nb, H_B // 2, 2 * DK_B, DV_B), lambda b, j: (b, 0, 0, 0))
    in_specs = [tok(W_BK), tok(W_BK), tok(W_BV), tok(W_BV), tok(W_BK), _full_spec((1, DV_B))]
    args = [q, k, v, g, la, g_gla]
    if s0 is not None:
        in_specs.append(st_spec)
        args.append(s0)
    return pl.pallas_call(
        functools.partial(_gla_kernel, has_init=s0 is not None),
        grid=(nbt // nb, t // c),
        in_specs=in_specs,
        out_specs=[tok(W_BV), st_spec],
        out_shape=[jax.ShapeDtypeStruct((nbt, t, W_BV), BF16),
                   jax.ShapeDtypeStruct((nbt, H_B // 2, 2 * DK_B, DV_B), F32)],
        scratch_shapes=[pltpu.VMEM((nb, H_B // 2, DV_B, 2 * DK_B), F32)],
        compiler_params=_params(("parallel", "arbitrary")),
        name="gla",
    )(*args)


def _mixout_kernel(x_ref, ya_ref, yb_ref, ga_ref, gb_ref, gm_ref, gp_ref, wa_ref, wb_ref, wo_ref,
                   o_ref):
    nb, tt, _ = x_ref.shape
    if nb == 1:
        halves = [(slice(None), slice(s * tt // 2, (s + 1) * tt // 2)) for s in range(2)]
        nbh, tth = nb, tt // 2
    else:
        halves = [(slice(s * nb // 2, (s + 1) * nb // 2), slice(None)) for s in range(2)]
        nbh, tth = nb // 2, tt
    m = nbh * tth
    merged = []
    for bs, ts in halves:
        a = _dot(ya_ref[bs, ts, :].reshape(m, W_A), wa_ref[...])
        b = _dot(yb_ref[bs, ts, :].reshape(m, W_BV), wb_ref[...])
        ga = _sigmoid(ga_ref[bs, ts, :].reshape(m, D_MODEL).astype(F32))
        gb = _sigmoid(gb_ref[bs, ts, :].reshape(m, D_MODEL).astype(F32))
        merged.append((ga * a + gb * b).astype(BF16))
    for (bs, ts), mg in zip(halves, merged):
        mo = _dot(mg, wo_ref[...])
        n = ((mo * _rms_scale(mo)) * gp_ref[...]).reshape(nbh, tth, D_MODEL)
        o_ref[bs, ts, :] = x_ref[bs, ts, :] + gm_ref[bs] * n


def _mixout(x, ya, yb, ga, gb, mod, g_post, wa, wb, wo, nb, tt):
    nbt, t, _ = x.shape
    tok = lambda w: pl.BlockSpec((nb, tt, w), lambda b, i: (b, i, 0))
    return pl.pallas_call(
        _mixout_kernel,
        grid=(nbt // nb, t // tt),
        in_specs=[tok(D_MODEL), tok(W_A), tok(W_BV), tok(D_MODEL), tok(D_MODEL),
                  pl.BlockSpec((nb, 1, D_MODEL), lambda b, i: (b, 0, 2)),
                  _full_spec((1, D_MODEL)), _full_spec(wa.shape), _full_spec(wb.shape),
                  _full_spec(wo.shape)],
        out_specs=tok(D_MODEL),
        out_shape=jax.ShapeDtypeStruct(x.shape, F32),
        compiler_params=_params(("parallel", "parallel")),
        name="mixout",
    )(x, ya, yb, ga, gb, mod, g_post, wa, wb, wo)


def _gelu_tanh(x):
    c = float(np.sqrt(2.0 / np.pi))
    half = 0.5 * x
    return half + half * jnp.tanh(x * (c + (0.044715 * c) * (x * x)))


def _ffn_kernel(*refs, has_state):
    if has_state:
        (x_ref, shift_ref, scale_ref, gate_ref, gpre_ref, gpost_ref, wu_ref, wd_ref,
         wdw_ref, bdw_ref, st_ref, o_ref, tail_ref, h_ref, act_ref) = refs
    else:
        (x_ref, shift_ref, scale_ref, gate_ref, gpre_ref, gpost_ref, wu_ref, wd_ref,
         wdw_ref, bdw_ref, o_ref, tail_ref, h_ref, act_ref, carry_ref) = refs

        @pl.when(pl.program_id(1) == 0)
        def _():
            carry_ref[...] = jnp.zeros_like(carry_ref)

    nb, tt, _ = x_ref.shape
    m = nb * tt
    x = x_ref[...]
    h = (x * _rms_scale(x)) * gpre_ref[...] * (1.0 + scale_ref[...]) + shift_ref[...]
    h_ref[...] = h.reshape(m, D_MODEL).astype(BF16)
    ridx = lax.broadcasted_iota(jnp.int32, (nb, SUBLANES, FFN_FT), 1)

    def conv(u, lanes):
        u3 = u.reshape(nb, tt, FFN_FT)
        r1 = pltpu.roll(u, 1, 0).reshape(nb, tt, FFN_FT)
        r2 = pltpu.roll(u, 2, 0).reshape(nb, tt, FFN_FT)

        def taps(u_m2, u_m1, u_0):
            y = bdw_ref[:, lanes] + wdw_ref[0:1, lanes] * u_m2
            y = y + wdw_ref[1:2, lanes] * u_m1
            return y + wdw_ref[2:3, lanes] * u_0

        if has_state:
            p2, p1 = st_ref[:, 0:1, lanes], st_ref[:, 1:2, lanes]
        else:
            p2 = carry_ref[:, SUBLANES - 2:SUBLANES - 1, lanes]
            p1 = carry_ref[:, SUBLANES - 1:SUBLANES, lanes]
        h_m1 = jnp.where(ridx == 0, p1, r1[:, :SUBLANES])
        h_m2 = jnp.where(ridx == 0, p2, jnp.where(ridx == 1, p1, r2[:, :SUBLANES]))
        y = jnp.concatenate([taps(h_m2, h_m1, u3[:, :SUBLANES]),
                             taps(r2[:, SUBLANES:], r1[:, SUBLANES:], u3[:, SUBLANES:])], axis=1)
        tail = u3[:, tt - SUBLANES:, :]
        tail_ref[:, :, lanes] = tail
        if not has_state:
            carry_ref[:, :, lanes] = tail
        return y.reshape(m, FFN_FT)

    def up(f):
        hb = h_ref[...]
        lo = f * FFN_FT
        return (_dot(hb, wu_ref[:, lo:lo + FFN_FT]),
                _dot(hb, wu_ref[:, D_FF + lo:D_FF + lo + FFN_FT]))

    ua, ug = up(0)
    for f in range(FFN_NF):
        if f + 1 < FFN_NF:
            ua_next, ug_next = up(f + 1)
        cols = slice(f * FFN_FT, (f + 1) * FFN_FT)
        ya = conv(ua, cols)
        yg = conv(ug, slice(D_FF + f * FFN_FT, D_FF + (f + 1) * FFN_FT))
        act_ref[:, cols] = (_gelu_tanh(ya) * yg).astype(BF16)
        if f + 1 < FFN_NF:
            ua, ug = ua_next, ug_next
    yf = _dot(act_ref[...], wd_ref[...])
    n = ((yf * _rms_scale(yf)) * gpost_ref[...]).reshape(nb, tt, D_MODEL)
    o_ref[...] = x_ref[...] + gate_ref[...] * n


def _ffn(x, mod, g_pre, g_post, wu, wd, w_dw, b_dw, state, nb, tt):
    nbt, t, _ = x.shape
    m = nb * tt
    tok = pl.BlockSpec((nb, tt, D_MODEL), lambda b, i: (b, i, 0))
    mod_spec = lambda col: pl.BlockSpec((nb, 1, D_MODEL), lambda b, i, col=col: (b, 0, col))
    tail_spec = pl.BlockSpec((nb, SUBLANES, 2 * D_FF), lambda b, i: (b, 0, 0))
    in_specs = [tok, mod_spec(3), mod_spec(4), mod_spec(5), _full_spec((1, D_MODEL)),
                _full_spec((1, D_MODEL)), _full_spec(wu.shape),
                _full_spec(wd.shape), _full_spec(w_dw.shape), _full_spec(b_dw.shape)]
    args = [x, mod, mod, mod, g_pre, g_post, wu, wd, w_dw, b_dw]
    scratch = [pltpu.VMEM((m, D_MODEL), BF16), pltpu.VMEM((m, D_FF), BF16)]
    if state is not None:
        in_specs.append(pl.BlockSpec((nb, CONV_W - 1, 2 * D_FF), lambda b, i: (b, 0, 0)))
        args.append(state)
    else:
        scratch.append(pltpu.VMEM((nb, SUBLANES, 2 * D_FF), F32))
    return pl.pallas_call(
        functools.partial(_ffn_kernel, has_state=state is not None),
        grid=(nbt // nb, t // tt),
        in_specs=in_specs,
        out_specs=[tok, tail_spec],
        out_shape=[jax.ShapeDtypeStruct(x.shape, F32),
                   jax.ShapeDtypeStruct((nbt, SUBLANES, 2 * D_FF), F32)],
        scratch_shapes=scratch,
        compiler_params=_params(("parallel", "arbitrary")),
        name="ffn",
    )(*args)


def _bias_rows(rel_bias):
    assert BIAS_ROW >= ATTN_KB + ATTN_QB - 1 and WINDOW_A == ATTN_KB - ATTN_QB
    far_pos = jnp.broadcast_to(rel_bias[:, -1:], (H_A, BIAS_ROW))
    far_neg = jnp.broadcast_to(rel_bias[:, :1], (H_A, BIAS_ROW))
    n_mid = 2 * MAX_REL + 1
    n_lo = ATTN_KB + 1 - (WINDOW_A - MAX_REL) - n_mid
    rows = jnp.concatenate([far_pos[:, :WINDOW_A - MAX_REL], rel_bias[:, ::-1], far_neg[:, :n_lo],
                            far_pos[:, :BIAS_ROW - ATTN_KB - 1]], axis=1)
    return rows.astype(F32).reshape(H_A, 1, BIAS_ROW)


_LATE_WEIGHTS = ('w_br_a', 'w_br_b', 'w_out', 'w_up', 'w_down')


def _layer(x, mod, cache, s_gla, s_conv, w, first_chunk):
    nbt, t, _ = x.shape
    if first_chunk:
        nb, tt = 1, 1024
        nb_f, tt_f = 1, 512
    else:
        nb, tt = 1024 // t, t
        nb_f, tt_f = 512 // t, t
    keep_rows = min(WINDOW_A, t) if first_chunk else 0
    cast_ws = () if _LATE_WEIGHTS[0] in w else tuple(w[k + '_f32'] for k in _LATE_WEIGHTS)
    qa, ka, va, qb, kb, vb, gb, gate_a, gate_b, log_a, *extra = _inproj(
        x, mod, w['g_pre_mix'], w['w_main'], w['w_gk1'], w['w_gk2'], w['b_gk'], nb, tt, keep_rows,
        cast_ws)
    kv_t = extra[:2] if keep_rows else []
    if cast_ws:
        w = {**w, **dict(zip(_LATE_WEIGHTS, extra[len(kv_t):]))}
    if first_chunk:
        ya = _attn_prompt(qa, ka, va, w['bias_rows'])
        k_keep, v_keep = (jnp.transpose(a.reshape(nbt, H_A, HD_A, keep_rows), (0, 3, 1, 2))
                          for a in kv_t)
        yb, s_new = _gla(qb, kb, vb, gb, log_a, w['g_gla'], None, nbt, CHUNK)
    else:
        k_cache, v_cache = cache
        ya = _attn_sample(qa, ka, va, k_cache, v_cache, w['bias_rows'], 4)
        k_keep, v_keep = (a.astype(F32).reshape(nbt, t, H_A, HD_A) for a in (ka, va))
        yb, s_new = _gla(qb, kb, vb, gb, log_a, w['g_gla'], s_gla, 4, t)
    x1 = _mixout(x, ya, yb, gate_a, gate_b, mod, w['g_post_mix'], w['w_br_a'], w['w_br_b'],
                 w['w_out'], nb, tt)
    y, tail = _ffn(x1, mod, w['g_pre_ffn'], w['g_post_ffn'], w['w_up'], w['w_down'],
                   w['w_dw'], w['b_dw'], None if first_chunk else s_conv, nb_f, tt_f)
    return (y, k_keep, v_keep, s_new.reshape(nbt, H_B, DK_B, DV_B),
            tail[:, SUBLANES - (CONV_W - 1):, :]), w


def _prep_weights(w_in, w_gk2, b_gk, rel_bias, g_gla, w_br_a, w_br_b, w_out, w_up, w_dw, b_dw,
                  w_down, g_pre_mix, g_post_mix, g_pre_ffn, g_post_ffn):
    w_main, w_gk1 = _split_w_in(w_in.T, 256)
    w_gk2p = jnp.pad(w_gk2, ((0, LANES - GK_RANK), (0, 0))).astype(BF16)
    row = lambda a: a.reshape(1, -1)
    return {
        'w_main': w_main, 'w_gk1': w_gk1, 'w_gk2': w_gk2p, 'b_gk': row(b_gk),
        'bias_rows': _bias_rows(rel_bias), 'g_gla': row(g_gla),
        'w_br_a_f32': w_br_a, 'w_br_b_f32': w_br_b, 'w_out_f32': w_out, 'w_up_f32': w_up,
        'w_down_f32': w_down, 'w_dw': w_dw, 'b_dw': row(b_dw),
        'g_pre_mix': row(g_pre_mix), 'g_post_mix': row(g_post_mix),
        'g_pre_ffn': row(g_pre_ffn), 'g_post_ffn': row(g_post_ffn),
    }


def kernel(x_prompt, x_sample, cache_k_a, cache_v_a, state_gla, state_conv, c_prompt, c_sample, w_ada, b_ada, g_pre_mix, g_post_mix, g_pre_ffn, g_post_ffn, w_in, w_gk2, b_gk, rel_bias, g_gla, w_br_a, w_br_b, w_out, w_up, w_dw, b_dw, w_down):
    depth = w_ada.shape[0]
    assert depth == 1
    bp, bs = x_prompt.shape[0], x_sample.shape[0]
    s_len = x_sample.shape[1]
    cache_rows = cache_k_a.shape[2]
    yp, ys = x_prompt, x_sample
    outs = [[] for _ in range(8)]
    for l in range(depth):
        w = _prep_weights(w_in[l], w_gk2[l], b_gk[l], rel_bias[l], g_gla[l], w_br_a[l], w_br_b[l],
                          w_out[l], w_up[l], w_dw[l], b_dw[l], w_down[l], g_pre_mix[l],
                          g_post_mix[l], g_pre_ffn[l], g_post_ffn[l])
        c_all = jnp.concatenate([c_prompt, c_sample], axis=0)
        pad = (-c_all.shape[0]) % SUBLANES
        mod = _adaln(jnp.pad(c_all, ((0, pad), (0, 0))), w_ada[l], b_ada[l])
        mod_p = mod[:bp].reshape(bp, 1, 6 * D_MODEL)
        mod_s = mod[bp:bp + bs].reshape(bs, 1, 6 * D_MODEL)
        (yp, kp, vp, gp, cp), w = _layer(yp, mod_p, None, None, None, w, True)
        to_t = lambda c: jnp.transpose(c, (0, 2, 3, 1)).reshape(bs, W_A, cache_rows)
        cache = (to_t(cache_k_a[l]), to_t(cache_v_a[l]))
        s0 = state_gla[l].reshape(bs, H_B // 2, 2 * DK_B, DV_B)
        (ys, kn, vn, gn, cn), _ = _layer(ys, mod_s, cache, s0, state_conv[l], w, False)
        for lst, a in zip(outs, (kp, vp, gp, cp, kn, vn, gn, cn)):
            lst.append(a)
    return (yp, ys) + tuple(jnp.stack(lst) for lst in outs)
```

```python
import functools

import jax
import jax.numpy as jnp
import numpy as np
from jax import lax
from jax.experimental import pallas as pl
from jax.experimental.pallas import tpu as pltpu

D_MODEL = 1024
CHUNK = 64
BAND_CHUNKS = 8
WINDOW_A = BAND_CHUNKS * CHUNK
H_A = 8
HD_A = 64
MAX_REL = 128
H_B = 4
DK_B = 64
DV_B = 128
GK_RANK = 16
GK_NORM = 16.0
GLA_SUB = 16
D_FF = 2816
CONV_W = 3
EPS = 1e-6
NEG_INF = -1e30
PAST_LEN = 2048

W_A = H_A * HD_A
W_BK = H_B * DK_B
W_BV = H_B * DV_B

LANES = 128
SUBLANES = 8
VMEM_LIMIT = 56 * 1024 * 1024

ATTN_QB = 256
ATTN_KB = 3 * ATTN_QB
BIAS_ROW = 1024
FFN_FT = 256
FFN_NF = D_FF // FFN_FT

BF16 = jnp.bfloat16
F32 = jnp.float32


def _params(sem):
    return pltpu.CompilerParams(dimension_semantics=sem, vmem_limit_bytes=VMEM_LIMIT)


def _full_spec(shape):
    nd = len(shape)
    return pl.BlockSpec(shape, lambda *_: (0,) * nd, pipeline_mode=pl.Buffered(1))


def _dot(a, b):
    return jnp.dot(a, b, preferred_element_type=F32)


def _dot_nt(a, b):
    return lax.dot_general(a, b, (((1,), (1,)), ((), ())), preferred_element_type=F32)


def _dot_tn(a, b):
    return lax.dot_general(a, b, (((0,), (0,)), ((), ())), preferred_element_type=F32)


def _sigmoid(x):
    return 1.0 / (1.0 + jnp.exp(-x))


def _rms_scale(x):
    return lax.rsqrt(jnp.mean(x * x, axis=-1, keepdims=True) + EPS)


_GK_LO = 3 * W_A + 2 * W_BK + 2 * W_BV


PREP_STEPS = 4


def _prep_kernel(c_ref, wa_ref, ba_ref, wi_ref, mod_ref, main_ref, gk_ref):
    c = c_ref[...]
    s = (c * _sigmoid(c)).astype(BF16)
    mod_ref[...] = _dot(s, wa_ref[...].astype(BF16)) + ba_ref[...]
    main_ref[:_GK_LO, :] = wi_ref[:_GK_LO, :].astype(BF16)
    main_ref[_GK_LO:, :] = wi_ref[_GK_LO + GK_RANK:, :].astype(BF16)
    gk_ref[:GK_RANK, :] = wi_ref[_GK_LO:_GK_LO + GK_RANK, :].astype(BF16)
    gk_ref[GK_RANK:, :] = jnp.zeros((LANES - GK_RANK, gk_ref.shape[1]), BF16)


def _prep(c_all, w_ada, b_ada, w_in_t):
    rows = c_all.shape[0]
    n = w_ada.shape[1]
    d_in, d = w_in_t.shape
    tn, cols = n // PREP_STEPS, d // PREP_STEPS
    return pl.pallas_call(
        _prep_kernel,
        grid=(PREP_STEPS,),
        in_specs=[pl.BlockSpec((rows, D_MODEL), lambda j: (0, 0)),
                  pl.BlockSpec((D_MODEL, tn), lambda j: (0, j)),
                  pl.BlockSpec((1, tn), lambda j: (0, j)),
                  pl.BlockSpec((d_in, cols), lambda j: (0, j))],
        out_specs=[pl.BlockSpec((rows, tn), lambda j: (0, j)),
                   pl.BlockSpec((d_in - GK_RANK, cols), lambda j: (0, j)),
                   pl.BlockSpec((LANES, cols), lambda j: (0, j))],
        out_shape=[jax.ShapeDtypeStruct((rows, n), F32),
                   jax.ShapeDtypeStruct((d_in - GK_RANK, d), BF16),
                   jax.ShapeDtypeStruct((LANES, d), BF16)],
        compiler_params=_params(("parallel",)),
        name="prep",
    )(c_all, w_ada, b_ada.reshape(1, n), w_in_t)


_IN_GROUPS = (W_A, W_A, W_A, W_BK, W_BK, W_BV, W_BV, D_MODEL, D_MODEL)
_IN_MAIN = sum(_IN_GROUPS)


def _inproj_kernel(*refs, n_cast, n_kv_t):
    (x_ref, shift_ref, scale_ref, g_ref, wm_ref, wg1_ref, wg2_ref, bgk_ref), refs = refs[:8], refs[8:]
    cast_in, refs = refs[:n_cast], refs[n_cast:]
    (qa_ref, ka_ref, va_ref, qb_ref, kb_ref, vb_ref, gb_ref, ga_ref, gtb_ref, la_ref) = refs[:10]
    kv_t_refs, cast_out = refs[10:10 + n_kv_t], refs[10 + n_kv_t:]
    for w_ref, o_ref in zip(cast_in, cast_out):
        o_ref[...] = w_ref[...].astype(o_ref.dtype)
    nb, tt, _ = x_ref.shape
    x = x_ref[...]
    h = (x * _rms_scale(x)) * (g_ref[...] * (1.0 + scale_ref[...])) + shift_ref[...]
    hb = h.reshape(nb * tt, D_MODEL).astype(BF16)
    if kv_t_refs:
        @pl.when(pl.program_id(1) == pl.num_programs(1) - 1)
        def _():
            rows = kv_t_refs[0].shape[2]
            newest = hb[nb * tt - rows:, :]
            for o_ref, lo in zip(kv_t_refs, (W_A, 2 * W_A)):
                o_ref[0] = _dot_nt(wm_ref[lo:lo + W_A, :], newest)
    outs = (qa_ref, ka_ref, va_ref, qb_ref, kb_ref, vb_ref, gb_ref, ga_ref, gtb_ref)
    lo = 0
    for o_ref, w in zip(outs, _IN_GROUPS):
        z = _dot_nt(hb, wm_ref[lo:lo + w, :])
        o_ref[...] = z.reshape(nb, tt, w).astype(o_ref.dtype)
        lo += w
    gk_low = _dot_nt(hb, wg1_ref[...]).astype(BF16)
    gk = _dot(gk_low, wg2_ref[...]) + bgk_ref[...]
    log_a = (jnp.minimum(gk, 0.0) - jnp.log1p(jnp.exp(-jnp.abs(gk)))) / GK_NORM
    la_ref[...] = log_a.reshape(nb, tt, W_BK)


def _inproj(x, mod, g_pre, wm, wg1, wg2, bgk, nb, tt, keep_rows, cast_ws=()):
    nbt, t, _ = x.shape
    grid = (nbt // nb, t // tt)
    steps = grid[0] * grid[1]
    cast_specs = [pl.BlockSpec((w.shape[0] // steps, w.shape[1]),
                               lambda b, i: (b * grid[1] + i, 0)) for w in cast_ws]
    tok = lambda w: pl.BlockSpec((nb, tt, w), lambda b, i: (b, i, 0))
    mod_spec = lambda col: pl.BlockSpec((nb, 1, D_MODEL), lambda b, i, col=col: (b, 0, col))
    widths = _IN_GROUPS + (W_BK,)
    dtypes = (BF16,) * len(_IN_GROUPS) + (F32,)
    out_specs = [tok(w) for w in widths]
    out_shape = [jax.ShapeDtypeStruct((nbt, t, w), dt) for w, dt in zip(widths, dtypes)]
    if keep_rows:
        assert nb == 1 and keep_rows <= tt
        out_specs += [pl.BlockSpec((1, W_A, keep_rows), lambda b, i: (b, 0, 0))] * 2
        out_shape += [jax.ShapeDtypeStruct((nbt, W_A, keep_rows), F32)] * 2
    return pl.pallas_call(
        functools.partial(_inproj_kernel, n_cast=len(cast_ws), n_kv_t=2 if keep_rows else 0),
        grid=grid,
        in_specs=[tok(D_MODEL), mod_spec(0), mod_spec(1), _full_spec((1, D_MODEL)),
                  _full_spec(wm.shape), _full_spec(wg1.shape), _full_spec(wg2.shape),
                  _full_spec((1, W_BK))] + cast_specs,
        out_specs=out_specs + cast_specs,
        out_shape=out_shape + [jax.ShapeDtypeStruct(w.shape, BF16) for w in cast_ws],
        compiler_params=_params(("arbitrary", "arbitrary")),
        name="inproj",
    )(x, mod, mod, g_pre, wm, wg1, wg2, bgk, *cast_ws)


def _head_masks():
    lane = lax.broadcasted_iota(jnp.int32, (1, LANES), 1)
    first = lane < HD_A
    return first, jnp.logical_not(first)


def _toeplitz_bias(row_ref, h, rows):
    rb = jnp.broadcast_to(row_ref[h], (rows, BIAS_ROW))
    return pltpu.roll(rb, 0, 1, stride=1, stride_axis=0)


ATTN_RB = 32
ATTN_WIN = 640


def _attn_prompt_kernel(q_ref, k0_ref, k1_ref, k2_ref, v0_ref, v1_ref, v2_ref, row_ref, o_ref,
                        bias_ref, s_ref, p_ref):
    i = pl.program_id(1)

    @pl.when((pl.program_id(0) == 0) & (i == 0))
    def _():
        qc = lax.broadcasted_iota(jnp.int32, (ATTN_QB, ATTN_KB), 0) // CHUNK
        col = lax.broadcasted_iota(jnp.int32, (ATTN_QB, ATTN_KB), 1)
        kc = col // CHUNK - BAND_CHUNKS
        valid = (kc <= qc) & (kc >= qc - BAND_CHUNKS)
        for h in range(H_A):
            t = _toeplitz_bias(row_ref, h, ATTN_QB)
            band = jnp.where(valid, t[:, :ATTN_KB], NEG_INF)
            bias_ref[0, h] = jnp.where(col >= 2 * ATTN_QB, band, NEG_INF)
            bias_ref[1, h] = jnp.where(col >= ATTN_QB, band, NEG_INF)
            bias_ref[2, h] = band
        p_ref[...] = jnp.zeros_like(p_ref)

    var = jnp.minimum(i, 2)
    masks = _head_masks()

    def pair_rows(refs, p):
        sl = slice(p * LANES, (p + 1) * LANES)
        return jnp.concatenate([r[0, :, sl] for r in refs], axis=0)

    def scores(h):
        p, hh = divmod(h, 2)
        qp = q_ref[0, :, p * LANES:(p + 1) * LANES] * BF16(HD_A ** -0.5)
        qm = jnp.where(masks[hh], qp, jnp.zeros_like(qp))
        s_ref[h % 2] = _dot_nt(qm, pair_rows((k0_ref, k1_ref, k2_ref), p))

    scores(0)
    o_first = None
    for h in range(H_A):
        if h + 1 < H_A:
            scores(h + 1)
        slot = h % 2
        sums = []
        for r in range(ATTN_QB // ATTN_RB):
            rows = slice(r * ATTN_RB, (r + 1) * ATTN_RB)
            lo = 0 if r * ATTN_RB < ATTN_QB // 2 else ATTN_KB - ATTN_WIN
            sb = s_ref[slot, rows, lo:lo + ATTN_WIN] + bias_ref[var, h, rows, lo:lo + ATTN_WIN]
            e = jnp.exp(sb - jnp.max(sb, axis=-1, keepdims=True))
            sums.append(jnp.sum(e, axis=-1, keepdims=True))
            p_ref[slot, rows, lo:lo + ATTN_WIN] = e.astype(BF16)
        p, hh = divmod(h, 2)
        o = _dot(p_ref[slot], pair_rows((v0_ref, v1_ref, v2_ref), p)) / jnp.concatenate(sums, axis=0)
        if hh == 0:
            o_first = o
        else:
            o_ref[0, :, p * LANES:(p + 1) * LANES] = jnp.where(masks[0], o_first, o).astype(o_ref.dtype)


def _attn_prompt(q, k, v, bias_rows):
    b, t, _ = q.shape
    blk = lambda off: pl.BlockSpec(
        (1, ATTN_QB, W_A), lambda bb, i, off=off: (bb, jnp.maximum(i - off, 0), 0))
    return pl.pallas_call(
        _attn_prompt_kernel,
        grid=(b, t // ATTN_QB),
        in_specs=[blk(0), blk(2), blk(1), blk(0), blk(2), blk(1), blk(0),
                  _full_spec(bias_rows.shape)],
        out_specs=blk(0),
        out_shape=jax.ShapeDtypeStruct((b, t, W_A), BF16),
        scratch_shapes=[pltpu.VMEM((3, H_A, ATTN_QB, ATTN_KB), F32),
                        pltpu.VMEM((2, ATTN_QB, ATTN_KB), F32),
                        pltpu.VMEM((2, ATTN_QB, ATTN_KB), BF16)],
        compiler_params=_params(("arbitrary", "arbitrary")),
        name="attn_prompt",
    )(q, k, k, k, v, v, v, bias_rows)


def _attn_sample_kernel(q_ref, kn_ref, vn_ref, kc_ref, vc_ref, row_ref, o_ref, bc_ref, bn_ref):
    nb, s_len, _ = q_ref.shape
    w = kc_ref.shape[2]

    @pl.when(pl.program_id(0) == 0)
    def _():
        for h in range(H_A):
            t = _toeplitz_bias(row_ref, h, s_len)
            bc_ref[h * s_len:(h + 1) * s_len, :] = t[:, :w]
            bn_ref[h * s_len:(h + 1) * s_len, :] = t[:, w:w + s_len]

    lane_head = lax.broadcasted_iota(jnp.int32, (1, W_A), 1) // HD_A
    row_head = lax.broadcasted_iota(jnp.int32, (H_A * s_len, 1), 0) // s_len
    own_head = row_head == lane_head
    for b in range(nb):
        q = q_ref[b] * BF16(HD_A ** -0.5)
        qs = jnp.concatenate([q] * H_A, axis=0)
        qs = jnp.where(own_head, qs, jnp.zeros_like(qs))
        sc = _dot(qs, kc_ref[b].astype(BF16)) + bc_ref[...]
        sn = _dot_nt(qs, kn_ref[b]) + bn_ref[...]
        m = jnp.maximum(jnp.max(sc, axis=-1, keepdims=True), jnp.max(sn, axis=-1, keepdims=True))
        ec = jnp.exp(sc - m)
        en = jnp.exp(sn - m)
        l = jnp.sum(ec, axis=-1, keepdims=True) + jnp.sum(en, axis=-1, keepdims=True)
        full = (_dot_nt(ec.astype(BF16), vc_ref[b].astype(BF16))
                + _dot(en.astype(BF16), vn_ref[b])) / l
        o = full[:s_len]
        for h in range(1, H_A):
            o = jnp.where(lane_head == h, full[h * s_len:(h + 1) * s_len], o)
        o_ref[b] = o.astype(o_ref.dtype)


def _attn_sample(q, kn, vn, kc, vc, bias_rows, nb):
    b, s, _ = q.shape
    w = kc.shape[2]
    assert w == WINDOW_A
    new = pl.BlockSpec((nb, s, W_A), lambda i: (i, 0, 0))
    cache = pl.BlockSpec((nb, W_A, w), lambda i: (i, 0, 0))
    return pl.pallas_call(
        _attn_sample_kernel,
        grid=(b // nb,),
        in_specs=[new, new, new, cache, cache, _full_spec(bias_rows.shape)],
        out_specs=new,
        out_shape=jax.ShapeDtypeStruct((b, s, W_A), BF16),
        scratch_shapes=[pltpu.VMEM((H_A * s, w), F32), pltpu.VMEM((H_A * s, s), F32)],
        compiler_params=_params(("arbitrary",)),
        name="attn_sample",
    )(q, kn, vn, kc, vc, bias_rows)


def _gla_kernel(*refs, has_init):
    if has_init:
        q_ref, k_ref, v_ref, g_ref, la_ref, gg_ref, s0_ref, y_ref, so_ref, st_ref = refs
    else:
        q_ref, k_ref, v_ref, g_ref, la_ref, gg_ref, y_ref, so_ref, st_ref = refs
    nb, c, _ = q_ref.shape
    nsub = c // GLA_SUB
    npair = H_B // 2
    j = pl.program_id(1)

    @pl.when(j == 0)
    def _():
        if has_init:
            st_ref[...] = s0_ref[...]
        else:
            st_ref[...] = jnp.zeros_like(st_ref)

    r2 = lax.broadcasted_iota(jnp.int32, (2 * c, c), 0)
    s2 = lax.broadcasted_iota(jnp.int32, (2 * c, c), 1)
    t2 = r2 & (c - 1)
    same_sub = (s2 // GLA_SUB) == (t2 // GLA_SUB)
    sum_mat = jnp.where((s2 <= t2) & ((r2 < c) | same_sub), 1.0, 0.0).astype(BF16)

    row = lax.broadcasted_iota(jnp.int32, (c, W_BK), 0)
    row2 = lax.broadcasted_iota(jnp.int32, (2 * c, LANES), 0)
    lane2 = lax.broadcasted_iota(jnp.int32, (2 * c, LANES), 1)
    own = (lane2 < DK_B) == (row2 < c)
    sub_of_row2 = (row2 & (c - 1)) // GLA_SUB
    tril2 = lane2 <= (row2 & (c - 1))
    eye = (lax.broadcasted_iota(jnp.int32, (LANES, LANES), 0)
           == lax.broadcasted_iota(jnp.int32, (LANES, LANES), 1))
    scale = DK_B ** -0.5
    zeros_k = jnp.zeros((LANES - c, nsub * LANES), BF16)
    zeros_v = jnp.zeros((LANES - c, DV_B), BF16)

    prep = []
    for b in range(nb):
        la = la_ref[b]
        la_hi = la.astype(BF16)
        rest = la - la_hi.astype(F32)
        la_mid = rest.astype(BF16)
        la_lo = (rest - la_mid.astype(F32)).astype(BF16)
        sums = _dot(sum_mat, la_hi) + _dot(sum_mat, la_mid) + _dot(sum_mat, la_lo)
        cum, cum_sub = sums[:c], sums[c:]
        cum_end = cum[c - 1:c, :]
        qf = q_ref[b].astype(F32)
        kf = k_ref[b].astype(F32)
        q_sub = qf * jnp.exp(cum_sub) * scale
        q_in = (qf * jnp.exp(cum) * scale).astype(BF16)
        k_end = (kf * jnp.exp(cum_end - cum)).astype(BF16)
        k_sub = []
        for i in range(nsub):
            ref_i = cum[i * GLA_SUB - 1:i * GLA_SUB, :] if i else jnp.zeros((1, W_BK), F32)
            k_i = jnp.where(row < (i + 1) * GLA_SUB, kf * jnp.exp(ref_i - cum), 0.0)
            k_sub.append(k_i.astype(BF16))
        prep.append((q_sub, q_in, k_end, k_sub, cum_end))

    att = {}
    for b in range(nb):
        q_sub, _, _, k_sub, _ = prep[b]
        for p in range(npair):
            sl = slice(p * LANES, (p + 1) * LANES)
            k_stack = jnp.concatenate(
                [jnp.concatenate([k_i[:, sl] for k_i in k_sub], axis=1), zeros_k], axis=0)
            q2 = jnp.where(own, jnp.concatenate([q_sub[:, sl], q_sub[:, sl]], axis=0), 0.0)
            q_stack = jnp.concatenate(
                [jnp.where(sub_of_row2 == i, q2, 0.0) for i in range(nsub)], axis=1).astype(BF16)
            a = _dot_nt(q_stack, k_stack)
            att[b, p] = jnp.where(tril2, a, 0.0).astype(BF16)

    for b in range(nb):
        q_in = prep[b][1]
        for p in range(npair):
            sl = slice(p * LANES, (p + 1) * LANES)
            st_b = st_ref[b, p].astype(BF16)
            q_in2 = jnp.where(own, jnp.concatenate([q_in[:, sl], q_in[:, sl]], axis=0),
                              jnp.zeros((), BF16))
            for hh in range(2):
                h = 2 * p + hh
                hs = slice(h * DV_B, (h + 1) * DV_B)
                rows = slice(hh * c, (hh + 1) * c)
                lhs = jnp.concatenate([att[b, p][rows], q_in2[rows]], axis=1)
                rhs = jnp.concatenate([v_ref[b, :, hs], zeros_v, st_b], axis=0)
                o = _dot(lhs, rhs)
                gate = g_ref[b, :, hs].astype(F32)
                y = (o * _rms_scale(o)) * gg_ref[...] * (gate * _sigmoid(gate))
                y_ref[b, :, hs] = y.astype(y_ref.dtype)

    for b in range(nb):
        k_end, cum_end = prep[b][2], prep[b][4]
        for p in range(npair):
            sl = slice(p * LANES, (p + 1) * LANES)
            k2 = jnp.where(own, jnp.concatenate([k_end[:, sl], k_end[:, sl]], axis=0),
                           jnp.zeros((), BF16))
            v2 = jnp.concatenate([v_ref[b, :, 2 * p * DV_B:(2 * p + 1) * DV_B],
                                  v_ref[b, :, (2 * p + 1) * DV_B:(2 * p + 2) * DV_B]], axis=0)
            upd = _dot_tn(k2, v2)
            dec = jnp.exp(jnp.sum(jnp.where(eye, cum_end[:, sl], 0.0), axis=1, keepdims=True))
            st_ref[b, p] = st_ref[b, p] * dec + upd

    @pl.when(j == pl.num_programs(1) - 1)
    def _():
        so_ref[...] = st_ref[...]


def _gla(q, k, v, g, la, g_gla, s0, nb, c):
    nbt, t, _ = q.shape
    tok = lambda w: pl.BlockSpec((nb, c, w), lambda b, j: (b, j, 0))
    st_spec = pl.BlockSpec((nb, H_B // 2, 2 * DK_B, DV_B), lambda b, j: (b, 0, 0, 0))
    in_specs = [tok(W_BK), tok(W_BK), tok(W_BV), tok(W_BV), tok(W_BK), _full_spec((1, DV_B))]
    args = [q, k, v, g, la, g_gla]
    if s0 is not None:
        in_specs.append(st_spec)
        args.append(s0)
    return pl.pallas_call(
        functools.partial(_gla_kernel, has_init=s0 is not None),
        grid=(nbt // nb, t // c),
        in_specs=in_specs,
        out_specs=[tok(W_BV), st_spec],
        out_shape=[jax.ShapeDtypeStruct((nbt, t, W_BV), BF16),
                   jax.ShapeDtypeStruct((nbt, H_B // 2, 2 * DK_B, DV_B), F32)],
        scratch_shapes=[pltpu.VMEM((nb, H_B // 2, DV_B, 2 * DK_B), F32)],
        compiler_params=_params(("parallel", "arbitrary")),
        name="gla",
    )(*args)


def _mixout_kernel(x_ref, ya_ref, yb_ref, ga_ref, gb_ref, gm_ref, gp_ref, wa_ref, wb_ref, wo_ref,
                   o_ref):
    nb, tt, _ = x_ref.shape
    if nb == 1:
        halves = [(slice(None), slice(s * tt // 2, (s + 1) * tt // 2)) for s in range(2)]
        nbh, tth = nb, tt // 2
    else:
        halves = [(slice(s * nb // 2, (s + 1) * nb // 2), slice(None)) for s in range(2)]
        nbh, tth = nb // 2, tt
    m = nbh * tth
    merged = []
    for bs, ts in halves:
        a = _dot(ya_ref[bs, ts, :].reshape(m, W_A), wa_ref[...])
        b = _dot(yb_ref[bs, ts, :].reshape(m, W_BV), wb_ref[...])
        ga = _sigmoid(ga_ref[bs, ts, :].reshape(m, D_MODEL).astype(F32))
        gb = _sigmoid(gb_ref[bs, ts, :].reshape(m, D_MODEL).astype(F32))
        merged.append((ga * a + gb * b).astype(BF16))
    for (bs, ts), mg in zip(halves, merged):
        mo = _dot(mg, wo_ref[...])
        n = ((mo * _rms_scale(mo)) * gp_ref[...]).reshape(nbh, tth, D_MODEL)
        o_ref[bs, ts, :] = x_ref[bs, ts, :] + gm_ref[bs] * n


def _mixout(x, ya, yb, ga, gb, mod, g_post, wa, wb, wo, nb, tt):
    nbt, t, _ = x.shape
    tok = lambda w: pl.BlockSpec((nb, tt, w), lambda b, i: (b, i, 0))
    return pl.pallas_call(
        _mixout_kernel,
        grid=(nbt // nb, t // tt),
        in_specs=[tok(D_MODEL), tok(W_A), tok(W_BV), tok(D_MODEL), tok(D_MODEL),
                  pl.BlockSpec((nb, 1, D_MODEL), lambda b, i: (b, 0, 2)),
                  _full_spec((1, D_MODEL)), _full_spec(wa.shape), _full_spec(wb.shape),
                  _full_spec(wo.shape)],
        out_specs=tok(D_MODEL),
        out_shape=jax.ShapeDtypeStruct(x.shape, F32),
        compiler_params=_params(("parallel", "parallel")),
        name="mixout",
    )(x, ya, yb, ga, gb, mod, g_post, wa, wb, wo)


def _gelu_tanh(x):
    c = float(np.sqrt(2.0 / np.pi))
    half = 0.5 * x
    return half + half * jnp.tanh(x * (c + (0.044715 * c) * (x * x)))


def _ffn_kernel(*refs, has_state):
    if has_state:
        (x_ref, shift_ref, scale_ref, gate_ref, gpre_ref, gpost_ref, wu_ref, wd_ref,
         wdw_ref, bdw_ref, st_ref, o_ref, tail_ref, h_ref, act_ref) = refs
    else:
        (x_ref, shift_ref, scale_ref, gate_ref, gpre_ref, gpost_ref, wu_ref, wd_ref,
         wdw_ref, bdw_ref, o_ref, tail_ref, h_ref, act_ref, carry_ref) = refs

        @pl.when(pl.program_id(1) == 0)
        def _():
            carry_ref[...] = jnp.zeros_like(carry_ref)

    nb, tt, _ = x_ref.shape
    m = nb * tt
    x = x_ref[...]
    h = (x * _rms_scale(x)) * (gpre_ref[...] * (1.0 + scale_ref[...])) + shift_ref[...]
    h_ref[...] = h.reshape(m, D_MODEL).astype(BF16)
    ridx = lax.broadcasted_iota(jnp.int32, (nb, SUBLANES, FFN_FT), 1)

    def conv(u, lanes):
        u3 = u.reshape(nb, tt, FFN_FT)
        r1 = pltpu.roll(u, 1, 0).reshape(nb, tt, FFN_FT)
        r2 = pltpu.roll(u, 2, 0).reshape(nb, tt, FFN_FT)

        def taps(u_m2, u_m1, u_0):
            y = bdw_ref[:, lanes] + wdw_ref[0:1, lanes] * u_m2
            y = y + wdw_ref[1:2, lanes] * u_m1
            return y + wdw_ref[2:3, lanes] * u_0

        if has_state:
            p2, p1 = st_ref[:, 0:1, lanes], st_ref[:, 1:2, lanes]
        else:
            p2 = carry_ref[:, SUBLANES - 2:SUBLANES - 1, lanes]
            p1 = carry_ref[:, SUBLANES - 1:SUBLANES, lanes]
        h_m1 = jnp.where(ridx == 0, p1, r1[:, :SUBLANES])
        h_m2 = jnp.where(ridx == 0, p2, jnp.where(ridx == 1, p1, r2[:, :SUBLANES]))
        y = jnp.concatenate([taps(h_m2, h_m1, u3[:, :SUBLANES]),
                             taps(r2[:, SUBLANES:], r1[:, SUBLANES:], u3[:, SUBLANES:])], axis=1)
        tail = u3[:, tt - SUBLANES:, :]
        tail_ref[:, :, lanes] = tail
        if not has_state:
            carry_ref[:, :, lanes] = tail
        return y.reshape(m, FFN_FT)

    def up(f):
        hb = h_ref[...]
        lo = f * FFN_FT
        return (_dot(hb, wu_ref[:, lo:lo + FFN_FT]),
                _dot(hb, wu_ref[:, D_FF + lo:D_FF + lo + FFN_FT]))

    ua, ug = up(0)
    for f in range(FFN_NF):
        if f + 1 < FFN_NF:
            ua_next, ug_next = up(f + 1)
        cols = slice(f * FFN_FT, (f + 1) * FFN_FT)
        ya = conv(ua, cols)
        yg = conv(ug, slice(D_FF + f * FFN_FT, D_FF + (f + 1) * FFN_FT))
        act_ref[:, cols] = (_gelu_tanh(ya) * yg).astype(BF16)
        if f + 1 < FFN_NF:
            ua, ug = ua_next, ug_next
    yf = _dot(act_ref[...], wd_ref[...])
    n = ((yf * _rms_scale(yf)) * gpost_ref[...]).reshape(nb, tt, D_MODEL)
    o_ref[...] = x_ref[...] + gate_ref[...] * n


def _ffn(x, mod, g_pre, g_post, wu, wd, w_dw, b_dw, state, nb, tt):
    nbt, t, _ = x.shape
    m = nb * tt
    tok = pl.BlockSpec((nb, tt, D_MODEL), lambda b, i: (b, i, 0))
    mod_spec = lambda col: pl.BlockSpec((nb, 1, D_MODEL), lambda b, i, col=col: (b, 0, col))
    tail_spec = pl.BlockSpec((nb, SUBLANES, 2 * D_FF), lambda b, i: (b, 0, 0))
    in_specs = [tok, mod_spec(3), mod_spec(4), mod_spec(5), _full_spec((1, D_MODEL)),
                _full_spec((1, D_MODEL)), _full_spec(wu.shape),
                _full_spec(wd.shape), _full_spec(w_dw.shape), _full_spec(b_dw.shape)]
    args = [x, mod, mod, mod, g_pre, g_post, wu, wd, w_dw, b_dw]
    scratch = [pltpu.VMEM((m, D_MODEL), BF16), pltpu.VMEM((m, D_FF), BF16)]
    if state is not None:
        in_specs.append(pl.BlockSpec((nb, CONV_W - 1, 2 * D_FF), lambda b, i: (b, 0, 0)))
        args.append(state)
    else:
        scratch.append(pltpu.VMEM((nb, SUBLANES, 2 * D_FF), F32))
    return pl.pallas_call(
        functools.partial(_ffn_kernel, has_state=state is not None),
        grid=(nbt // nb, t // tt),
        in_specs=in_specs,
        out_specs=[tok, tail_spec],
        out_shape=[jax.ShapeDtypeStruct(x.shape, F32),
                   jax.ShapeDtypeStruct((nbt, SUBLANES, 2 * D_FF), F32)],
        scratch_shapes=scratch,
        compiler_params=_params(("parallel", "arbitrary")),
        name="ffn",
    )(*args)


def _bias_rows(rel_bias):
    assert BIAS_ROW >= ATTN_KB + ATTN_QB - 1 and WINDOW_A == ATTN_KB - ATTN_QB
    far_pos = jnp.broadcast_to(rel_bias[:, -1:], (H_A, BIAS_ROW))
    far_neg = jnp.broadcast_to(rel_bias[:, :1], (H_A, BIAS_ROW))
    n_mid = 2 * MAX_REL + 1
    n_lo = ATTN_KB + 1 - (WINDOW_A - MAX_REL) - n_mid
    rows = jnp.concatenate([far_pos[:, :WINDOW_A - MAX_REL], rel_bias[:, ::-1], far_neg[:, :n_lo],
                            far_pos[:, :BIAS_ROW - ATTN_KB - 1]], axis=1)
    return rows.astype(F32).reshape(H_A, 1, BIAS_ROW)


_LATE_WEIGHTS = ('w_br_a', 'w_br_b', 'w_out', 'w_up', 'w_down')


def _layer(x, mod, cache, s_gla, s_conv, w, first_chunk):
    nbt, t, _ = x.shape
    if first_chunk:
        nb, tt = 1, 1024
        nb_f, tt_f = 1, 512
    else:
        nb, tt = 1024 // t, t
        nb_f, tt_f = 512 // t, t
    keep_rows = min(WINDOW_A, t) if first_chunk else 0
    cast_ws = () if _LATE_WEIGHTS[0] in w else tuple(w[k + '_f32'] for k in _LATE_WEIGHTS)
    qa, ka, va, qb, kb, vb, gb, gate_a, gate_b, log_a, *extra = _inproj(
        x, mod, w['g_pre_mix'], w['w_main'], w['w_gk1'], w['w_gk2'], w['b_gk'], nb, tt, keep_rows,
        cast_ws)
    kv_t = extra[:2] if keep_rows else []
    if cast_ws:
        w = {**w, **dict(zip(_LATE_WEIGHTS, extra[len(kv_t):]))}
    if first_chunk:
        ya = _attn_prompt(qa, ka, va, w['bias_rows'])
        k_keep, v_keep = (jnp.transpose(a.reshape(nbt, H_A, HD_A, keep_rows), (0, 3, 1, 2))
                          for a in kv_t)
        yb, s_new = _gla(qb, kb, vb, gb, log_a, w['g_gla'], None, nbt, CHUNK)
    else:
        k_cache, v_cache = cache
        ya = _attn_sample(qa, ka, va, k_cache, v_cache, w['bias_rows'], 4)
        k_keep, v_keep = (a.astype(F32).reshape(nbt, t, H_A, HD_A) for a in (ka, va))
        yb, s_new = _gla(qb, kb, vb, gb, log_a, w['g_gla'], s_gla, 4, t)
    x1 = _mixout(x, ya, yb, gate_a, gate_b, mod, w['g_post_mix'], w['w_br_a'], w['w_br_b'],
                 w['w_out'], nb, tt)
    y, tail = _ffn(x1, mod, w['g_pre_ffn'], w['g_post_ffn'], w['w_up'], w['w_down'],
                   w['w_dw'], w['b_dw'], None if first_chunk else s_conv, nb_f, tt_f)
    return (y, k_keep, v_keep, s_new.reshape(nbt, H_B, DK_B, DV_B),
            tail[:, SUBLANES - (CONV_W - 1):, :]), w


def _prep_weights(w_main, w_gk1, w_gk2, b_gk, rel_bias, g_gla, w_br_a, w_br_b, w_out, w_up, w_dw,
                  b_dw, w_down, g_pre_mix, g_post_mix, g_pre_ffn, g_post_ffn):
    w_gk2p = jnp.pad(w_gk2, ((0, LANES - GK_RANK), (0, 0))).astype(BF16)
    row = lambda a: a.reshape(1, -1)
    return {
        'w_main': w_main, 'w_gk1': w_gk1, 'w_gk2': w_gk2p, 'b_gk': row(b_gk),
        'bias_rows': _bias_rows(rel_bias), 'g_gla': row(g_gla),
        'w_br_a_f32': w_br_a, 'w_br_b_f32': w_br_b, 'w_out_f32': w_out, 'w_up_f32': w_up,
        'w_down_f32': w_down, 'w_dw': w_dw, 'b_dw': row(b_dw),
        'g_pre_mix': row(g_pre_mix), 'g_post_mix': row(g_post_mix),
        'g_pre_ffn': row(g_pre_ffn), 'g_post_ffn': row(g_post_ffn),
    }


def kernel(x_prompt, x_sample, cache_k_a, cache_v_a, state_gla, state_conv, c_prompt, c_sample, w_ada, b_ada, g_pre_mix, g_post_mix, g_pre_ffn, g_post_ffn, w_in, w_gk2, b_gk, rel_bias, g_gla, w_br_a, w_br_b, w_out, w_up, w_dw, b_dw, w_down):
    depth = w_ada.shape[0]
    assert depth == 1
    bp, bs = x_prompt.shape[0], x_sample.shape[0]
    s_len = x_sample.shape[1]
    cache_rows = cache_k_a.shape[2]
    yp, ys = x_prompt, x_sample
    outs = [[] for _ in range(8)]
    for l in range(depth):
        c_all = jnp.concatenate([c_prompt, c_sample], axis=0)
        pad = (-c_all.shape[0]) % SUBLANES
        mod, w_main, w_gk1 = _prep(jnp.pad(c_all, ((0, pad), (0, 0))), w_ada[l], b_ada[l], w_in[l].T)
        w = _prep_weights(w_main, w_gk1, w_gk2[l], b_gk[l], rel_bias[l], g_gla[l], w_br_a[l],
                          w_br_b[l], w_out[l], w_up[l], w_dw[l], b_dw[l], w_down[l], g_pre_mix[l],
                          g_post_mix[l], g_pre_ffn[l], g_post_ffn[l])
        mod_p = mod[:bp].reshape(bp, 1, 6 * D_MODEL)
        mod_s = mod[bp:bp + bs].reshape(bs, 1, 6 * D_MODEL)
        (yp, kp, vp, gp, cp), w = _layer(yp, mod_p, None, None, None, w, True)
        to_t = lambda c: jnp.transpose(c, (0, 2, 3, 1)).reshape(bs, W_A, cache_rows)
        cache = (to_t(cache_k_a[l]), to_t(cache_v_a[l]))
        s0 = state_gla[l].reshape(bs, H_B // 2, 2 * DK_B, DV_B)
        (ys, kn, vn, gn, cn), _ = _layer(ys, mod_s, cache, s0, state_conv[l], w, False)
        for lst, a in zip(outs, (kp, vp, gp, cp, kn, vn, gn, cn)):
            lst.append(a)
    return (yp, ys) + tuple(jnp.stack(lst) for lst in outs)
```

```python
import functools

import jax
import jax.numpy as jnp
import numpy as np
from jax import lax
from jax.experimental import pallas as pl
from jax.experimental.pallas import tpu as pltpu

D_MODEL = 1024
CHUNK = 64
BAND_CHUNKS = 8
WINDOW_A = BAND_CHUNKS * CHUNK
H_A = 8
HD_A = 64
MAX_REL = 128
H_B = 4
DK_B = 64
DV_B = 128
GK_RANK = 16
GK_NORM = 16.0
GLA_SUB = 16
D_FF = 2816
CONV_W = 3
EPS = 1e-6
NEG_INF = -1e30
PAST_LEN = 2048

W_A = H_A * HD_A
W_BK = H_B * DK_B
W_BV = H_B * DV_B

LANES = 128
SUBLANES = 8
VMEM_LIMIT = 56 * 1024 * 1024

ATTN_QB = 256
ATTN_KB = 3 * ATTN_QB
BIAS_ROW = 1024
FFN_FT = 256
FFN_NF = D_FF // FFN_FT

BF16 = jnp.bfloat16
F32 = jnp.float32


def _params(sem):
    return pltpu.CompilerParams(dimension_semantics=sem, vmem_limit_bytes=VMEM_LIMIT)


def _full_spec(shape):
    nd = len(shape)
    return pl.BlockSpec(shape, lambda *_: (0,) * nd, pipeline_mode=pl.Buffered(1))


def _dot(a, b):
    return jnp.dot(a, b, preferred_element_type=F32)


def _dot_nt(a, b):
    return lax.dot_general(a, b, (((1,), (1,)), ((), ())), preferred_element_type=F32)


def _dot_tn(a, b):
    return lax.dot_general(a, b, (((0,), (0,)), ((), ())), preferred_element_type=F32)


def _sigmoid(x):
    return 1.0 / (1.0 + jnp.exp(-x))


def _rms_scale(x):
    return lax.rsqrt(jnp.mean(x * x, axis=-1, keepdims=True) + EPS)


_GK_LO = 3 * W_A + 2 * W_BK + 2 * W_BV


PREP_STEPS = 4


def _prep_kernel(c_ref, wa_ref, ba_ref, wi_ref, mod_ref, main_ref, gk_ref):
    c = c_ref[...]
    s = (c * _sigmoid(c)).astype(BF16)
    mod_ref[...] = _dot(s, wa_ref[...].astype(BF16)) + ba_ref[...]
    main_ref[:_GK_LO, :] = wi_ref[:_GK_LO, :].astype(BF16)
    main_ref[_GK_LO:, :] = wi_ref[_GK_LO + GK_RANK:, :].astype(BF16)
    gk_ref[:GK_RANK, :] = wi_ref[_GK_LO:_GK_LO + GK_RANK, :].astype(BF16)
    gk_ref[GK_RANK:, :] = jnp.zeros((LANES - GK_RANK, gk_ref.shape[1]), BF16)


def _prep(c_all, w_ada, b_ada, w_in_t):
    rows = c_all.shape[0]
    n = w_ada.shape[1]
    d_in, d = w_in_t.shape
    tn, cols = n // PREP_STEPS, d // PREP_STEPS
    return pl.pallas_call(
        _prep_kernel,
        grid=(PREP_STEPS,),
        in_specs=[pl.BlockSpec((rows, D_MODEL), lambda j: (0, 0)),
                  pl.BlockSpec((D_MODEL, tn), lambda j: (0, j)),
                  pl.BlockSpec((1, tn), lambda j: (0, j)),
                  pl.BlockSpec((d_in, cols), lambda j: (0, j))],
        out_specs=[pl.BlockSpec((rows, tn), lambda j: (0, j)),
                   pl.BlockSpec((d_in - GK_RANK, cols), lambda j: (0, j)),
                   pl.BlockSpec((LANES, cols), lambda j: (0, j))],
        out_shape=[jax.ShapeDtypeStruct((rows, n), F32),
                   jax.ShapeDtypeStruct((d_in - GK_RANK, d), BF16),
                   jax.ShapeDtypeStruct((LANES, d), BF16)],
        compiler_params=_params(("parallel",)),
        name="prep",
    )(c_all, w_ada, b_ada.reshape(1, n), w_in_t)


_IN_GROUPS = (W_A, W_A, W_A, W_BK, W_BK, W_BV, W_BV, D_MODEL, D_MODEL)
_IN_MAIN = sum(_IN_GROUPS)


def _inproj_kernel(*refs, n_cast, n_kv_t):
    (x_ref, shift_ref, scale_ref, g_ref, wm_ref, wg1_ref, wg2_ref, bgk_ref), refs = refs[:8], refs[8:]
    cast_in, refs = refs[:n_cast], refs[n_cast:]
    (qa_ref, ka_ref, va_ref, qb_ref, kb_ref, vb_ref, gb_ref, ga_ref, gtb_ref, la_ref) = refs[:10]
    kv_t_refs, cast_out = refs[10:10 + n_kv_t], refs[10 + n_kv_t:]
    for w_ref, o_ref in zip(cast_in, cast_out):
        o_ref[...] = w_ref[...].astype(o_ref.dtype)
    nb, tt, _ = x_ref.shape
    x = x_ref[...]
    h = (x * _rms_scale(x)) * (g_ref[...] * (1.0 + scale_ref[...])) + shift_ref[...]
    hb = h.reshape(nb * tt, D_MODEL).astype(BF16)
    if kv_t_refs:
        @pl.when(pl.program_id(1) == pl.num_programs(1) - 1)
        def _():
            rows = kv_t_refs[0].shape[2]
            newest = hb[nb * tt - rows:, :]
            for o_ref, lo in zip(kv_t_refs, (W_A, 2 * W_A)):
                o_ref[0] = _dot_nt(wm_ref[lo:lo + W_A, :], newest)
    outs = (qa_ref, ka_ref, va_ref, qb_ref, kb_ref, vb_ref, gb_ref, ga_ref, gtb_ref)
    lo = 0
    for o_ref, w in zip(outs, _IN_GROUPS):
        z = _dot_nt(hb, wm_ref[lo:lo + w, :])
        o_ref[...] = z.reshape(nb, tt, w).astype(o_ref.dtype)
        lo += w
    gk_low = _dot_nt(hb, wg1_ref[...]).astype(BF16)
    gk = _dot(gk_low, wg2_ref[...]) + bgk_ref[...]
    log_a = (jnp.minimum(gk, 0.0) - jnp.log1p(jnp.exp(-jnp.abs(gk)))) / GK_NORM
    la_ref[...] = log_a.reshape(nb, tt, W_BK)


def _inproj(x, mod, g_pre, wm, wg1, wg2, bgk, nb, tt, keep_rows, cast_ws=()):
    nbt, t, _ = x.shape
    grid = (nbt // nb, t // tt)
    steps = grid[0] * grid[1]
    cast_specs = [pl.BlockSpec((w.shape[0] // steps, w.shape[1]),
                               lambda b, i: (b * grid[1] + i, 0)) for w in cast_ws]
    tok = lambda w: pl.BlockSpec((nb, tt, w), lambda b, i: (b, i, 0))
    mod_spec = lambda col: pl.BlockSpec((nb, 1, D_MODEL), lambda b, i, col=col: (b, 0, col))
    widths = _IN_GROUPS + (W_BK,)
    dtypes = (BF16,) * len(_IN_GROUPS) + (F32,)
    out_specs = [tok(w) for w in widths]
    out_shape = [jax.ShapeDtypeStruct((nbt, t, w), dt) for w, dt in zip(widths, dtypes)]
    if keep_rows:
        assert nb == 1 and keep_rows <= tt
        out_specs += [pl.BlockSpec((1, W_A, keep_rows), lambda b, i: (b, 0, 0))] * 2
        out_shape += [jax.ShapeDtypeStruct((nbt, W_A, keep_rows), F32)] * 2
    return pl.pallas_call(
        functools.partial(_inproj_kernel, n_cast=len(cast_ws), n_kv_t=2 if keep_rows else 0),
        grid=grid,
        in_specs=[tok(D_MODEL), mod_spec(0), mod_spec(1), _full_spec((1, D_MODEL)),
                  _full_spec(wm.shape), _full_spec(wg1.shape), _full_spec(wg2.shape),
                  _full_spec((1, W_BK))] + cast_specs,
        out_specs=out_specs + cast_specs,
        out_shape=out_shape + [jax.ShapeDtypeStruct(w.shape, BF16) for w in cast_ws],
        compiler_params=_params(("arbitrary", "arbitrary")),
        name="inproj",
    )(x, mod, mod, g_pre, wm, wg1, wg2, bgk, *cast_ws)


def _head_masks():
    lane = lax.broadcasted_iota(jnp.int32, (1, LANES), 1)
    first = lane < HD_A
    return first, jnp.logical_not(first)


def _toeplitz_bias(row_ref, h, rows):
    rb = jnp.broadcast_to(row_ref[h], (rows, BIAS_ROW))
    return pltpu.roll(rb, 0, 1, stride=1, stride_axis=0)


ATTN_RB = 32
ATTN_WIN = 640


def _attn_prompt_kernel(q_ref, k0_ref, k1_ref, k2_ref, v0_ref, v1_ref, v2_ref, row_ref, o_ref,
                        bias_ref, s_ref, p_ref):
    i = pl.program_id(1)

    @pl.when((pl.program_id(0) == 0) & (i == 0))
    def _():
        qc = lax.broadcasted_iota(jnp.int32, (ATTN_QB, ATTN_KB), 0) // CHUNK
        col = lax.broadcasted_iota(jnp.int32, (ATTN_QB, ATTN_KB), 1)
        kc = col // CHUNK - BAND_CHUNKS
        valid = (kc <= qc) & (kc >= qc - BAND_CHUNKS)
        for h in range(H_A):
            t = _toeplitz_bias(row_ref, h, ATTN_QB)
            band = jnp.where(valid, t[:, :ATTN_KB], NEG_INF)
            bias_ref[0, h] = jnp.where(col >= 2 * ATTN_QB, band, NEG_INF)
            bias_ref[1, h] = jnp.where(col >= ATTN_QB, band, NEG_INF)
            bias_ref[2, h] = band
        p_ref[...] = jnp.zeros_like(p_ref)

    var = jnp.minimum(i, 2)
    masks = _head_masks()

    def pair_rows(refs, p):
        sl = slice(p * LANES, (p + 1) * LANES)
        return jnp.concatenate([r[0, :, sl] for r in refs], axis=0)

    def scores(h):
        p, hh = divmod(h, 2)
        qp = q_ref[0, :, p * LANES:(p + 1) * LANES] * BF16(HD_A ** -0.5)
        qm = jnp.where(masks[hh], qp, jnp.zeros_like(qp))
        s_ref[h % 2] = _dot_nt(qm, pair_rows((k0_ref, k1_ref, k2_ref), p))

    scores(0)
    o_first = None
    for h in range(H_A):
        if h + 1 < H_A:
            scores(h + 1)
        slot = h % 2
        sums = []
        for r in range(ATTN_QB // ATTN_RB):
            rows = slice(r * ATTN_RB, (r + 1) * ATTN_RB)
            lo = 0 if r * ATTN_RB < ATTN_QB // 2 else ATTN_KB - ATTN_WIN
            sb = s_ref[slot, rows, lo:lo + ATTN_WIN] + bias_ref[var, h, rows, lo:lo + ATTN_WIN]
            e = jnp.exp(sb - jnp.max(sb, axis=-1, keepdims=True))
            sums.append(jnp.sum(e, axis=-1, keepdims=True))
            p_ref[slot, rows, lo:lo + ATTN_WIN] = e.astype(BF16)
        p, hh = divmod(h, 2)
        o = _dot(p_ref[slot], pair_rows((v0_ref, v1_ref, v2_ref), p)) / jnp.concatenate(sums, axis=0)
        if hh == 0:
            o_first = o
        else:
            o_ref[0, :, p * LANES:(p + 1) * LANES] = jnp.where(masks[0], o_first, o).astype(o_ref.dtype)


def _attn_prompt(q, k, v, bias_rows):
    b, t, _ = q.shape
    blk = lambda off: pl.BlockSpec(
        (1, ATTN_QB, W_A), lambda bb, i, off=off: (bb, jnp.maximum(i - off, 0), 0))
    return pl.pallas_call(
        _attn_prompt_kernel,
        grid=(b, t // ATTN_QB),
        in_specs=[blk(0), blk(2), blk(1), blk(0), blk(2), blk(1), blk(0),
                  _full_spec(bias_rows.shape)],
        out_specs=blk(0),
        out_shape=jax.ShapeDtypeStruct((b, t, W_A), BF16),
        scratch_shapes=[pltpu.VMEM((3, H_A, ATTN_QB, ATTN_KB), F32),
                        pltpu.VMEM((2, ATTN_QB, ATTN_KB), F32),
                        pltpu.VMEM((2, ATTN_QB, ATTN_KB), BF16)],
        compiler_params=_params(("arbitrary", "arbitrary")),
        name="attn_prompt",
    )(q, k, k, k, v, v, v, bias_rows)


def _attn_sample_kernel(q_ref, kn_ref, vn_ref, kc_ref, vc_ref, row_ref, o_ref, bc_ref, bn_ref):
    nb, s_len, _ = q_ref.shape
    w = kc_ref.shape[2]

    @pl.when(pl.program_id(0) == 0)
    def _():
        for h in range(H_A):
            t = _toeplitz_bias(row_ref, h, s_len)
            bc_ref[h * s_len:(h + 1) * s_len, :] = t[:, :w]
            bn_ref[h * s_len:(h + 1) * s_len, :] = t[:, w:w + s_len]

    lane_head = lax.broadcasted_iota(jnp.int32, (1, W_A), 1) // HD_A
    row_head = lax.broadcasted_iota(jnp.int32, (H_A * s_len, 1), 0) // s_len
    own_head = row_head == lane_head
    for b in range(nb):
        q = q_ref[b] * BF16(HD_A ** -0.5)
        qs = jnp.concatenate([q] * H_A, axis=0)
        qs = jnp.where(own_head, qs, jnp.zeros_like(qs))
        sc = _dot(qs, kc_ref[b].astype(BF16)) + bc_ref[...]
        sn = _dot_nt(qs, kn_ref[b]) + bn_ref[...]
        m = jnp.maximum(jnp.max(sc, axis=-1, keepdims=True), jnp.max(sn, axis=-1, keepdims=True))
        ec = jnp.exp(sc - m)
        en = jnp.exp(sn - m)
        l = jnp.sum(ec, axis=-1, keepdims=True) + jnp.sum(en, axis=-1, keepdims=True)
        full = (_dot_nt(ec.astype(BF16), vc_ref[b].astype(BF16))
                + _dot(en.astype(BF16), vn_ref[b])) / l
        o = full[:s_len]
        for h in range(1, H_A):
            o = jnp.where(lane_head == h, full[h * s_len:(h + 1) * s_len], o)
        o_ref[b] = o.astype(o_ref.dtype)


def _attn_sample(q, kn, vn, kc, vc, bias_rows, nb):
    b, s, _ = q.shape
    w = kc.shape[2]
    assert w == WINDOW_A
    new = pl.BlockSpec((nb, s, W_A), lambda i: (i, 0, 0))
    cache = pl.BlockSpec((nb, W_A, w), lambda i: (i, 0, 0))
    return pl.pallas_call(
        _attn_sample_kernel,
        grid=(b // nb,),
        in_specs=[new, new, new, cache, cache, _full_spec(bias_rows.shape)],
        out_specs=new,
        out_shape=jax.ShapeDtypeStruct((b, s, W_A), BF16),
        scratch_shapes=[pltpu.VMEM((H_A * s, w), F32), pltpu.VMEM((H_A * s, s), F32)],
        compiler_params=_params(("arbitrary",)),
        name="attn_sample",
    )(q, kn, vn, kc, vc, bias_rows)


def _gla_kernel(*refs, has_init):
    if has_init:
        q_ref, k_ref, v_ref, g_ref, la_ref, gg_ref, s0_ref, y_ref, so_ref, st_ref = refs
    else:
        q_ref, k_ref, v_ref, g_ref, la_ref, gg_ref, y_ref, so_ref, st_ref = refs
    nb, c, _ = q_ref.shape
    nsub = c // GLA_SUB
    npair = H_B // 2
    j = pl.program_id(1)

    @pl.when(j == 0)
    def _():
        if has_init:
            st_ref[...] = s0_ref[...]
        else:
            st_ref[...] = jnp.zeros_like(st_ref)

    r2 = lax.broadcasted_iota(jnp.int32, (2 * c, c), 0)
    s2 = lax.broadcasted_iota(jnp.int32, (2 * c, c), 1)
    t2 = r2 & (c - 1)
    same_sub = (s2 // GLA_SUB) == (t2 // GLA_SUB)
    sum_mat = jnp.where((s2 <= t2) & ((r2 < c) | same_sub), 1.0, 0.0).astype(BF16)

    row = lax.broadcasted_iota(jnp.int32, (c, W_BK), 0)
    row2 = lax.broadcasted_iota(jnp.int32, (2 * c, LANES), 0)
    lane2 = lax.broadcasted_iota(jnp.int32, (2 * c, LANES), 1)
    own = (lane2 < DK_B) == (row2 < c)
    sub_of_row2 = (row2 & (c - 1)) // GLA_SUB
    tril2 = lane2 <= (row2 & (c - 1))
    eye = (lax.broadcasted_iota(jnp.int32, (LANES, LANES), 0)
           == lax.broadcasted_iota(jnp.int32, (LANES, LANES), 1))
    scale = DK_B ** -0.5
    zeros_k = jnp.zeros((LANES - c, nsub * LANES), BF16)
    zeros_v = jnp.zeros((LANES - c, DV_B), BF16)

    prep = []
    for b in range(nb):
        la = la_ref[b]
        la_hi = la.astype(BF16)
        rest = la - la_hi.astype(F32)
        la_mid = rest.astype(BF16)
        la_lo = (rest - la_mid.astype(F32)).astype(BF16)
        sums = _dot(sum_mat, la_hi) + _dot(sum_mat, la_mid) + _dot(sum_mat, la_lo)
        cum, cum_sub = sums[:c], sums[c:]
        cum_end = cum[c - 1:c, :]
        qf = q_ref[b].astype(F32)
        kf = k_ref[b].astype(F32)
        q_sub = qf * jnp.exp(cum_sub) * scale
        q_in = (qf * jnp.exp(cum) * scale).astype(BF16)
        k_end = (kf * jnp.exp(cum_end - cum)).astype(BF16)
        k_sub = []
        for i in range(nsub):
            ref_i = cum[i * GLA_SUB - 1:i * GLA_SUB, :] if i else jnp.zeros((1, W_BK), F32)
            k_i = jnp.where(row < (i + 1) * GLA_SUB, kf * jnp.exp(ref_i - cum), 0.0)
            k_sub.append(k_i.astype(BF16))
        prep.append((q_sub, q_in, k_end, k_sub, cum_end))

    att = {}
    for b in range(nb):
        q_sub, _, _, k_sub, _ = prep[b]
        for p in range(npair):
            sl = slice(p * LANES, (p + 1) * LANES)
            k_stack = jnp.concatenate(
                [jnp.concatenate([k_i[:, sl] for k_i in k_sub], axis=1), zeros_k], axis=0)
            q2 = jnp.where(own, jnp.concatenate([q_sub[:, sl], q_sub[:, sl]], axis=0), 0.0)
            q_stack = jnp.concatenate(
                [jnp.where(sub_of_row2 == i, q2, 0.0) for i in range(nsub)], axis=1).astype(BF16)
            a = _dot_nt(q_stack, k_stack)
            att[b, p] = jnp.where(tril2, a, 0.0).astype(BF16)

    for b in range(nb):
        q_in = prep[b][1]
        for p in range(npair):
            sl = slice(p * LANES, (p + 1) * LANES)
            st_b = st_ref[b, p].astype(BF16)
            q_in2 = jnp.where(own, jnp.concatenate([q_in[:, sl], q_in[:, sl]], axis=0),
                              jnp.zeros((), BF16))
            for hh in range(2):
                h = 2 * p + hh
                hs = slice(h * DV_B, (h + 1) * DV_B)
                rows = slice(hh * c, (hh + 1) * c)
                lhs = jnp.concatenate([att[b, p][rows], q_in2[rows]], axis=1)
                rhs = jnp.concatenate([v_ref[b, :, hs], zeros_v, st_b], axis=0)
                o = _dot(lhs, rhs)
                gate = g_ref[b, :, hs].astype(F32)
                y = (o * _rms_scale(o)) * gg_ref[...] * (gate * _sigmoid(gate))
                y_ref[b, :, hs] = y.astype(y_ref.dtype)

    for b in range(nb):
        k_end, cum_end = prep[b][2], prep[b][4]
        for p in range(npair):
            sl = slice(p * LANES, (p + 1) * LANES)
            k2 = jnp.where(own, jnp.concatenate([k_end[:, sl], k_end[:, sl]], axis=0),
                           jnp.zeros((), BF16))
            v2 = jnp.concatenate([v_ref[b, :, 2 * p * DV_B:(2 * p + 1) * DV_B],
                                  v_ref[b, :, (2 * p + 1) * DV_B:(2 * p + 2) * DV_B]], axis=0)
            upd = _dot_tn(k2, v2)
            dec = jnp.exp(jnp.sum(jnp.where(eye, cum_end[:, sl], 0.0), axis=1, keepdims=True))
            st_ref[b, p] = st_ref[b, p] * dec + upd

    @pl.when(j == pl.num_programs(1) - 1)
    def _():
        so_ref[...] = st_ref[...]


def _gla(q, k, v, g, la, g_gla, s0, nb, c):
    nbt, t, _ = q.shape
    tok = lambda w: pl.BlockSpec((nb, c, w), lambda b, j: (b, j, 0))
    st_spec = pl.BlockSpec((nb, H_B // 2, 2 * DK_B, DV_B), lambda b, j: (b, 0, 0, 0))
    in_specs = [tok(W_BK), tok(W_BK), tok(W_BV), tok(W_BV), tok(W_BK), _full_spec((1, DV_B))]
    args = [q, k, v, g, la, g_gla]
    if s0 is not None:
        in_specs.append(st_spec)
        args.append(s0)
    return pl.pallas_call(
        functools.partial(_gla_kernel, has_init=s0 is not None),
        grid=(nbt // nb, t // c),
        in_specs=in_specs,
        out_specs=[tok(W_BV), st_spec],
        out_shape=[jax.ShapeDtypeStruct((nbt, t, W_BV), BF16),
                   jax.ShapeDtypeStruct((nbt, H_B // 2, 2 * DK_B, DV_B), F32)],
        scratch_shapes=[pltpu.VMEM((nb, H_B // 2, DV_B, 2 * DK_B), F32)],
        compiler_params=_params(("parallel", "arbitrary")),
        name="gla",
    )(*args)


def _mixout_kernel(x_ref, ya_ref, yb_ref, ga_ref, gb_ref, gm_ref, gp_ref, wa_ref, wb_ref, wo_ref,
                   o_ref):
    nb, tt, _ = x_ref.shape
    if nb == 1:
        halves = [(slice(None), slice(s * tt // 2, (s + 1) * tt // 2)) for s in range(2)]
        nbh, tth = nb, tt // 2
    else:
        halves = [(slice(s * nb // 2, (s + 1) * nb // 2), slice(None)) for s in range(2)]
        nbh, tth = nb // 2, tt
    m = nbh * tth
    merged = []
    for bs, ts in halves:
        a = _dot(ya_ref[bs, ts, :].reshape(m, W_A), wa_ref[...])
        b = _dot(yb_ref[bs, ts, :].reshape(m, W_BV), wb_ref[...])
        ga = _sigmoid(ga_ref[bs, ts, :].reshape(m, D_MODEL).astype(F32))
        gb = _sigmoid(gb_ref[bs, ts, :].reshape(m, D_MODEL).astype(F32))
        merged.append((ga * a + gb * b).astype(BF16))
    for (bs, ts), mg in zip(halves, merged):
        mo = _dot(mg, wo_ref[...])
        n = ((mo * _rms_scale(mo)) * gp_ref[...]).reshape(nbh, tth, D_MODEL)
        o_ref[bs, ts, :] = x_ref[bs, ts, :] + gm_ref[bs] * n


def _mixout(x, ya, yb, ga, gb, mod, g_post, wa, wb, wo, nb, tt):
    nbt, t, _ = x.shape
    tok = lambda w: pl.BlockSpec((nb, tt, w), lambda b, i: (b, i, 0))
    return pl.pallas_call(
        _mixout_kernel,
        grid=(nbt // nb, t // tt),
        in_specs=[tok(D_MODEL), tok(W_A), tok(W_BV), tok(D_MODEL), tok(D_MODEL),
                  pl.BlockSpec((nb, 1, D_MODEL), lambda b, i: (b, 0, 2)),
                  _full_spec((1, D_MODEL)), _full_spec(wa.shape), _full_spec(wb.shape),
                  _full_spec(wo.shape)],
        out_specs=tok(D_MODEL),
        out_shape=jax.ShapeDtypeStruct(x.shape, F32),
        compiler_params=_params(("parallel", "parallel")),
        name="mixout",
    )(x, ya, yb, ga, gb, mod, g_post, wa, wb, wo)


def _gelu_tanh(x):
    c = float(np.sqrt(2.0 / np.pi))
    half = 0.5 * x
    return half + half * jnp.tanh(x * (c + (0.044715 * c) * (x * x)))


def _ffn_kernel(*refs, has_state):
    if has_state:
        (x_ref, shift_ref, scale_ref, gate_ref, gpre_ref, gpost_ref, wu_ref, wd_ref,
         wdw_ref, bdw_ref, st_ref, o_ref, tail_ref, h_ref, act_ref) = refs
    else:
        (x_ref, shift_ref, scale_ref, gate_ref, gpre_ref, gpost_ref, wu_ref, wd_ref,
         wdw_ref, bdw_ref, o_ref, tail_ref, h_ref, act_ref, carry_ref) = refs

        @pl.when(pl.program_id(1) == 0)
        def _():
            carry_ref[...] = jnp.zeros_like(carry_ref)

    nb, tt, _ = x_ref.shape
    m = nb * tt
    x = x_ref[...]
    h = (x * _rms_scale(x)) * (gpre_ref[...] * (1.0 + scale_ref[...])) + shift_ref[...]
    h_ref[...] = h.reshape(m, D_MODEL).astype(BF16)
    ridx = lax.broadcasted_iota(jnp.int32, (nb, SUBLANES, FFN_FT), 1)

    def conv(u, lanes):
        u3 = u.reshape(nb, tt, FFN_FT)
        r1 = pltpu.roll(u, 1, 0).reshape(nb, tt, FFN_FT)
        r2 = pltpu.roll(u, 2, 0).reshape(nb, tt, FFN_FT)

        def taps(u_m2, u_m1, u_0):
            y = bdw_ref[:, lanes] + wdw_ref[0:1, lanes] * u_m2
            y = y + wdw_ref[1:2, lanes] * u_m1
            return y + wdw_ref[2:3, lanes] * u_0

        if has_state:
            p2, p1 = st_ref[:, 0:1, lanes], st_ref[:, 1:2, lanes]
        else:
            p2 = carry_ref[:, SUBLANES - 2:SUBLANES - 1, lanes]
            p1 = carry_ref[:, SUBLANES - 1:SUBLANES, lanes]
        h_m1 = jnp.where(ridx == 0, p1, r1[:, :SUBLANES])
        h_m2 = jnp.where(ridx == 0, p2, jnp.where(ridx == 1, p1, r2[:, :SUBLANES]))
        y = jnp.concatenate([taps(h_m2, h_m1, u3[:, :SUBLANES]),
                             taps(r2[:, SUBLANES:], r1[:, SUBLANES:], u3[:, SUBLANES:])], axis=1)
        tail = u3[:, tt - SUBLANES:, :]
        tail_ref[:, :, lanes] = tail
        if not has_state:
            carry_ref[:, :, lanes] = tail
        return y.reshape(m, FFN_FT)

    def up(f):
        hb = h_ref[...]
        lo = f * FFN_FT
        return (_dot(hb, wu_ref[:, lo:lo + FFN_FT]),
                _dot(hb, wu_ref[:, D_FF + lo:D_FF + lo + FFN_FT]))

    ua, ug = up(0)
    for f in range(FFN_NF):
        if f + 1 < FFN_NF:
            ua_next, ug_next = up(f + 1)
        cols = slice(f * FFN_FT, (f + 1) * FFN_FT)
        ya = conv(ua, cols)
        yg = conv(ug, slice(D_FF + f * FFN_FT, D_FF + (f + 1) * FFN_FT))
        act_ref[:, cols] = (_gelu_tanh(ya) * yg).astype(BF16)
        if f + 1 < FFN_NF:
            ua, ug = ua_next, ug_next
    yf = _dot(act_ref[...], wd_ref[...])
    n = ((yf * _rms_scale(yf)) * gpost_ref[...]).reshape(nb, tt, D_MODEL)
    o_ref[...] = x_ref[...] + gate_ref[...] * n


def _ffn(x, mod, g_pre, g_post, wu, wd, w_dw, b_dw, state, nb, tt):
    nbt, t, _ = x.shape
    m = nb * tt
    tok = pl.BlockSpec((nb, tt, D_MODEL), lambda b, i: (b, i, 0))
    mod_spec = lambda col: pl.BlockSpec((nb, 1, D_MODEL), lambda b, i, col=col: (b, 0, col))
    tail_spec = pl.BlockSpec((nb, SUBLANES, 2 * D_FF), lambda b, i: (b, 0, 0))
    in_specs = [tok, mod_spec(3), mod_spec(4), mod_spec(5), _full_spec((1, D_MODEL)),
                _full_spec((1, D_MODEL)), _full_spec(wu.shape),
                _full_spec(wd.shape), _full_spec(w_dw.shape), _full_spec(b_dw.shape)]
    args = [x, mod, mod, mod, g_pre, g_post, wu, wd, w_dw, b_dw]
    scratch = [pltpu.VMEM((m, D_MODEL), BF16), pltpu.VMEM((m, D_FF), BF16)]
    if state is not None:
        in_specs.append(pl.BlockSpec((nb, CONV_W - 1, 2 * D_FF), lambda b, i: (b, 0, 0)))
        args.append(state)
    else:
        scratch.append(pltpu.VMEM((nb, SUBLANES, 2 * D_FF), F32))
    return pl.pallas_call(
        functools.partial(_ffn_kernel, has_state=state is not None),
        grid=(nbt // nb, t // tt),
        in_specs=in_specs,
        out_specs=[tok, tail_spec],
        out_shape=[jax.ShapeDtypeStruct(x.shape, F32),
                   jax.ShapeDtypeStruct((nbt, SUBLANES, 2 * D_FF), F32)],
        scratch_shapes=scratch,
        compiler_params=_params(("parallel", "arbitrary")),
        name="ffn",
    )(*args)


def _bias_rows(rel_bias):
    assert BIAS_ROW >= ATTN_KB + ATTN_QB - 1 and WINDOW_A == ATTN_KB - ATTN_QB
    far_pos = jnp.broadcast_to(rel_bias[:, -1:], (H_A, BIAS_ROW))
    far_neg = jnp.broadcast_to(rel_bias[:, :1], (H_A, BIAS_ROW))
    n_mid = 2 * MAX_REL + 1
    n_lo = ATTN_KB + 1 - (WINDOW_A - MAX_REL) - n_mid
    rows = jnp.concatenate([far_pos[:, :WINDOW_A - MAX_REL], rel_bias[:, ::-1], far_neg[:, :n_lo],
                            far_pos[:, :BIAS_ROW - ATTN_KB - 1]], axis=1)
    return rows.astype(F32).reshape(H_A, 1, BIAS_ROW)


_LATE_WEIGHTS = ('w_br_a', 'w_br_b', 'w_out', 'w_up', 'w_down')


def _layer(x, mod, cache, s_gla, s_conv, w, first_chunk):
    nbt, t, _ = x.shape
    if first_chunk:
        nb, tt = 1, 1024
        nb_f, tt_f = 1, 1024
    else:
        nb, tt = 1024 // t, t
        nb_f, tt_f = 1024 // t, t
    keep_rows = min(WINDOW_A, t) if first_chunk else 0
    cast_ws = () if _LATE_WEIGHTS[0] in w else tuple(w[k + '_f32'] for k in _LATE_WEIGHTS)
    qa, ka, va, qb, kb, vb, gb, gate_a, gate_b, log_a, *extra = _inproj(
        x, mod, w['g_pre_mix'], w['w_main'], w['w_gk1'], w['w_gk2'], w['b_gk'], nb, tt, keep_rows,
        cast_ws)
    kv_t = extra[:2] if keep_rows else []
    if cast_ws:
        w = {**w, **dict(zip(_LATE_WEIGHTS, extra[len(kv_t):]))}
    if first_chunk:
        ya = _attn_prompt(qa, ka, va, w['bias_rows'])
        k_keep, v_keep = (jnp.transpose(a.reshape(nbt, H_A, HD_A, keep_rows), (0, 3, 1, 2))
                          for a in kv_t)
        yb, s_new = _gla(qb, kb, vb, gb, log_a, w['g_gla'], None, nbt, CHUNK)
    else:
        k_cache, v_cache = cache
        ya = _attn_sample(qa, ka, va, k_cache, v_cache, w['bias_rows'], 4)
        k_keep, v_keep = (a.astype(F32).reshape(nbt, t, H_A, HD_A) for a in (ka, va))
        yb, s_new = _gla(qb, kb, vb, gb, log_a, w['g_gla'], s_gla, 4, t)
    x1 = _mixout(x, ya, yb, gate_a, gate_b, mod, w['g_post_mix'], w['w_br_a'], w['w_br_b'],
                 w['w_out'], nb, tt)
    y, tail = _ffn(x1, mod, w['g_pre_ffn'], w['g_post_ffn'], w['w_up'], w['w_down'],
                   w['w_dw'], w['b_dw'], None if first_chunk else s_conv, nb_f, tt_f)
    return (y, k_keep, v_keep, s_new.reshape(nbt, H_B, DK_B, DV_B),
            tail[:, SUBLANES - (CONV_W - 1):, :]), w


def _prep_weights(w_main, w_gk1, w_gk2, b_gk, rel_bias, g_gla, w_br_a, w_br_b, w_out, w_up, w_dw,
                  b_dw, w_down, g_pre_mix, g_post_mix, g_pre_ffn, g_post_ffn):
    w_gk2p = jnp.pad(w_gk2, ((0, LANES - GK_RANK), (0, 0))).astype(BF16)
    row = lambda a: a.reshape(1, -1)
    return {
        'w_main': w_main, 'w_gk1': w_gk1, 'w_gk2': w_gk2p, 'b_gk': row(b_gk),
        'bias_rows': _bias_rows(rel_bias), 'g_gla': row(g_gla),
        'w_br_a_f32': w_br_a, 'w_br_b_f32': w_br_b, 'w_out_f32': w_out, 'w_up_f32': w_up,
        'w_down_f32': w_down, 'w_dw': w_dw, 'b_dw': row(b_dw),
        'g_pre_mix': row(g_pre_mix), 'g_post_mix': row(g_post_mix),
        'g_pre_ffn': row(g_pre_ffn), 'g_post_ffn': row(g_post_ffn),
    }


def kernel(x_prompt, x_sample, cache_k_a, cache_v_a, state_gla, state_conv, c_prompt, c_sample, w_ada, b_ada, g_pre_mix, g_post_mix, g_pre_ffn, g_post_ffn, w_in, w_gk2, b_gk, rel_bias, g_gla, w_br_a, w_br_b, w_out, w_up, w_dw, b_dw, w_down):
    depth = w_ada.shape[0]
    assert depth == 1
    bp, bs = x_prompt.shape[0], x_sample.shape[0]
    s_len = x_sample.shape[1]
    cache_rows = cache_k_a.shape[2]
    yp, ys = x_prompt, x_sample
    outs = [[] for _ in range(8)]
    for l in range(depth):
        c_all = jnp.concatenate([c_prompt, c_sample], axis=0)
        pad = (-c_all.shape[0]) % SUBLANES
        mod, w_main, w_gk1 = _prep(jnp.pad(c_all, ((0, pad), (0, 0))), w_ada[l], b_ada[l], w_in[l].T)
        w = _prep_weights(w_main, w_gk1, w_gk2[l], b_gk[l], rel_bias[l], g_gla[l], w_br_a[l],
                          w_br_b[l], w_out[l], w_up[l], w_dw[l], b_dw[l], w_down[l], g_pre_mix[l],
                          g_post_mix[l], g_pre_ffn[l], g_post_ffn[l])
        mod_p = mod[:bp].reshape(bp, 1, 6 * D_MODEL)
        mod_s = mod[bp:bp + bs].reshape(bs, 1, 6 * D_MODEL)
        (yp, kp, vp, gp, cp), w = _layer(yp, mod_p, None, None, None, w, True)
        to_t = lambda c: jnp.transpose(c, (0, 2, 3, 1)).reshape(bs, W_A, cache_rows)
        cache = (to_t(cache_k_a[l]), to_t(cache_v_a[l]))
        s0 = state_gla[l].reshape(bs, H_B // 2, 2 * DK_B, DV_B)
        (ys, kn, vn, gn, cn), _ = _layer(ys, mod_s, cache, s0, state_conv[l], w, False)
        for lst, a in zip(outs, (kp, vp, gp, cp, kn, vn, gn, cn)):
            lst.append(a)
    return (yp, ys) + tuple(jnp.stack(lst) for lst in outs)
```

```python
import functools

import jax
import jax.numpy as jnp
import numpy as np
from jax import lax
from jax.experimental import pallas as pl
from jax.experimental.pallas import tpu as pltpu

D_MODEL = 1024
CHUNK = 64
BAND_CHUNKS = 8
WINDOW_A = BAND_CHUNKS * CHUNK
H_A = 8
HD_A = 64
MAX_REL = 128
H_B = 4
DK_B = 64
DV_B = 128
GK_RANK = 16
GK_NORM = 16.0
GLA_SUB = 16
D_FF = 2816
CONV_W = 3
EPS = 1e-6
NEG_INF = -1e30
PAST_LEN = 2048

W_A = H_A * HD_A
W_BK = H_B * DK_B
W_BV = H_B * DV_B

LANES = 128
SUBLANES = 8
VMEM_LIMIT = 56 * 1024 * 1024

ATTN_QB = 256
ATTN_KB = 3 * ATTN_QB
BIAS_ROW = 1024
FFN_FT = 256
FFN_NF = D_FF // FFN_FT

BF16 = jnp.bfloat16
F32 = jnp.float32


def _params(sem):
    return pltpu.CompilerParams(dimension_semantics=sem, vmem_limit_bytes=VMEM_LIMIT)


def _full_spec(shape):
    nd = len(shape)
    return pl.BlockSpec(shape, lambda *_: (0,) * nd, pipeline_mode=pl.Buffered(1))


def _dot(a, b):
    return jnp.dot(a, b, preferred_element_type=F32)


def _dot_nt(a, b):
    return lax.dot_general(a, b, (((1,), (1,)), ((), ())), preferred_element_type=F32)


def _dot_tn(a, b):
    return lax.dot_general(a, b, (((0,), (0,)), ((), ())), preferred_element_type=F32)


def _sigmoid(x):
    return 1.0 / (1.0 + jnp.exp(-x))


def _rms_scale(x):
    return lax.rsqrt(jnp.mean(x * x, axis=-1, keepdims=True) + EPS)


_GK_LO = 3 * W_A + 2 * W_BK + 2 * W_BV


PREP_STEPS = 4


def _prep_kernel(c_ref, wa_ref, ba_ref, wi_ref, mod_ref, main_ref, gk_ref):
    c = c_ref[...]
    s = (c * _sigmoid(c)).astype(BF16)
    mod_ref[...] = _dot(s, wa_ref[...].astype(BF16)) + ba_ref[...]
    main_ref[:_GK_LO, :] = wi_ref[:_GK_LO, :].astype(BF16)
    main_ref[_GK_LO:, :] = wi_ref[_GK_LO + GK_RANK:, :].astype(BF16)
    gk_ref[:GK_RANK, :] = wi_ref[_GK_LO:_GK_LO + GK_RANK, :].astype(BF16)
    gk_ref[GK_RANK:, :] = jnp.zeros((LANES - GK_RANK, gk_ref.shape[1]), BF16)


def _prep(c_all, w_ada, b_ada, w_in_t):
    rows = c_all.shape[0]
    n = w_ada.shape[1]
    d_in, d = w_in_t.shape
    tn, cols = n // PREP_STEPS, d // PREP_STEPS
    return pl.pallas_call(
        _prep_kernel,
        grid=(PREP_STEPS,),
        in_specs=[pl.BlockSpec((rows, D_MODEL), lambda j: (0, 0)),
                  pl.BlockSpec((D_MODEL, tn), lambda j: (0, j)),
                  pl.BlockSpec((1, tn), lambda j: (0, j)),
                  pl.BlockSpec((d_in, cols), lambda j: (0, j))],
        out_specs=[pl.BlockSpec((rows, tn), lambda j: (0, j)),
                   pl.BlockSpec((d_in - GK_RANK, cols), lambda j: (0, j)),
                   pl.BlockSpec((LANES, cols), lambda j: (0, j))],
        out_shape=[jax.ShapeDtypeStruct((rows, n), F32),
                   jax.ShapeDtypeStruct((d_in - GK_RANK, d), BF16),
                   jax.ShapeDtypeStruct((LANES, d), BF16)],
        compiler_params=_params(("parallel",)),
        name="prep",
    )(c_all, w_ada, b_ada.reshape(1, n), w_in_t)


_IN_GROUPS = (W_A, W_A, W_A, W_BK, W_BK, W_BV, W_BV, D_MODEL, D_MODEL)
_IN_MAIN = sum(_IN_GROUPS)


def _inproj_kernel(*refs, n_cast, n_kv_t):
    (x_ref, shift_ref, scale_ref, g_ref, wm_ref, wg1_ref, wg2_ref, bgk_ref), refs = refs[:8], refs[8:]
    cast_in, refs = refs[:n_cast], refs[n_cast:]
    (qa_ref, ka_ref, va_ref, qb_ref, kb_ref, vb_ref, gb_ref, ga_ref, gtb_ref, la_ref) = refs[:10]
    kv_t_refs, cast_out = refs[10:10 + n_kv_t], refs[10 + n_kv_t:]
    for w_ref, o_ref in zip(cast_in, cast_out):
        o_ref[...] = w_ref[...].astype(o_ref.dtype)
    nb, tt, _ = x_ref.shape
    x = x_ref[...]
    h = (x * _rms_scale(x)) * (g_ref[...] * (1.0 + scale_ref[...])) + shift_ref[...]
    hb = h.reshape(nb * tt, D_MODEL).astype(BF16)
    if kv_t_refs:
        @pl.when(pl.program_id(1) == pl.num_programs(1) - 1)
        def _():
            rows = kv_t_refs[0].shape[2]
            newest = hb[nb * tt - rows:, :]
            for o_ref, lo in zip(kv_t_refs, (W_A, 2 * W_A)):
                o_ref[0] = _dot_nt(wm_ref[lo:lo + W_A, :], newest)
    outs = (qa_ref, ka_ref, va_ref, qb_ref, kb_ref, vb_ref, gb_ref, ga_ref, gtb_ref)
    lo = 0
    for o_ref, w in zip(outs, _IN_GROUPS):
        z = _dot_nt(hb, wm_ref[lo:lo + w, :])
        o_ref[...] = z.reshape(nb, tt, w).astype(o_ref.dtype)
        lo += w
    gk_low = _dot_nt(hb, wg1_ref[...]).astype(BF16)
    gk = _dot(gk_low, wg2_ref[...]) + bgk_ref[...]
    log_a = (jnp.minimum(gk, 0.0) - jnp.log1p(jnp.exp(-jnp.abs(gk)))) / GK_NORM
    la_ref[...] = log_a.reshape(nb, tt, W_BK)


def _inproj(x, mod, g_pre, wm, wg1, wg2, bgk, nb, tt, keep_rows, cast_ws=()):
    nbt, t, _ = x.shape
    grid = (nbt // nb, t // tt)
    steps = grid[0] * grid[1]
    cast_specs = [pl.BlockSpec((w.shape[0] // steps, w.shape[1]),
                               lambda b, i: (b * grid[1] + i, 0)) for w in cast_ws]
    tok = lambda w: pl.BlockSpec((nb, tt, w), lambda b, i: (b, i, 0))
    mod_spec = lambda col: pl.BlockSpec((nb, 1, D_MODEL), lambda b, i, col=col: (b, 0, col))
    widths = _IN_GROUPS + (W_BK,)
    dtypes = (BF16,) * len(_IN_GROUPS) + (F32,)
    out_specs = [tok(w) for w in widths]
    out_shape = [jax.ShapeDtypeStruct((nbt, t, w), dt) for w, dt in zip(widths, dtypes)]
    if keep_rows:
        assert nb == 1 and keep_rows <= tt
        out_specs += [pl.BlockSpec((1, W_A, keep_rows), lambda b, i: (b, 0, 0))] * 2
        out_shape += [jax.ShapeDtypeStruct((nbt, W_A, keep_rows), F32)] * 2
    return pl.pallas_call(
        functools.partial(_inproj_kernel, n_cast=len(cast_ws), n_kv_t=2 if keep_rows else 0),
        grid=grid,
        in_specs=[tok(D_MODEL), mod_spec(0), mod_spec(1), _full_spec((1, D_MODEL)),
                  _full_spec(wm.shape), _full_spec(wg1.shape), _full_spec(wg2.shape),
                  _full_spec((1, W_BK))] + cast_specs,
        out_specs=out_specs + cast_specs,
        out_shape=out_shape + [jax.ShapeDtypeStruct(w.shape, BF16) for w in cast_ws],
        compiler_params=_params(("arbitrary", "arbitrary")),
        name="inproj",
    )(x, mod, mod, g_pre, wm, wg1, wg2, bgk, *cast_ws)


def _head_masks():
    lane = lax.broadcasted_iota(jnp.int32, (1, LANES), 1)
    first = lane < HD_A
    return first, jnp.logical_not(first)


def _toeplitz_bias(row_ref, h, rows):
    rb = jnp.broadcast_to(row_ref[h], (rows, BIAS_ROW))
    return pltpu.roll(rb, 0, 1, stride=1, stride_axis=0)


ATTN_RB = 32
ATTN_WIN = 640


ATTN_SUB = 2


def _attn_prompt_kernel(*refs):
    q_ref, refs = refs[0], refs[1:]
    k_refs, v_refs = refs[:ATTN_SUB + 2], refs[ATTN_SUB + 2:2 * ATTN_SUB + 4]
    row_ref, o_ref, bias_ref, s_ref, p_ref = refs[2 * ATTN_SUB + 4:]
    i = pl.program_id(1)

    @pl.when((pl.program_id(0) == 0) & (i == 0))
    def _():
        qc = lax.broadcasted_iota(jnp.int32, (ATTN_QB, ATTN_KB), 0) // CHUNK
        col = lax.broadcasted_iota(jnp.int32, (ATTN_QB, ATTN_KB), 1)
        kc = col // CHUNK - BAND_CHUNKS
        valid = (kc <= qc) & (kc >= qc - BAND_CHUNKS)
        for h in range(H_A):
            t = _toeplitz_bias(row_ref, h, ATTN_QB)
            band = jnp.where(valid, t[:, :ATTN_KB], NEG_INF)
            bias_ref[0, h] = jnp.where(col >= 2 * ATTN_QB, band, NEG_INF)
            bias_ref[1, h] = jnp.where(col >= ATTN_QB, band, NEG_INF)
            bias_ref[2, h] = band
        p_ref[...] = jnp.zeros_like(p_ref)

    masks = _head_masks()

    def pair_rows(refs, p):
        sl = slice(p * LANES, (p + 1) * LANES)
        return jnp.concatenate([r[0, :, sl] for r in refs], axis=0)

    units = [(sub, h) for sub in range(ATTN_SUB) for h in range(H_A)]

    def scores(u):
        sub, h = units[u]
        p, hh = divmod(h, 2)
        qp = q_ref[0, sub * ATTN_QB:(sub + 1) * ATTN_QB, p * LANES:(p + 1) * LANES] * BF16(HD_A ** -0.5)
        qm = jnp.where(masks[hh], qp, jnp.zeros_like(qp))
        s_ref[u % 2] = _dot_nt(qm, pair_rows(k_refs[sub:sub + 3], p))

    scores(0)
    o_first = None
    for u, (sub, h) in enumerate(units):
        if u + 1 < len(units):
            scores(u + 1)
        slot = u % 2
        var = jnp.minimum(ATTN_SUB * i + sub, 2)
        sums = []
        for r in range(ATTN_QB // ATTN_RB):
            rows = slice(r * ATTN_RB, (r + 1) * ATTN_RB)
            lo = 0 if r * ATTN_RB < ATTN_QB // 2 else ATTN_KB - ATTN_WIN
            sb = s_ref[slot, rows, lo:lo + ATTN_WIN] + bias_ref[var, h, rows, lo:lo + ATTN_WIN]
            e = jnp.exp(sb - jnp.max(sb, axis=-1, keepdims=True))
            sums.append(jnp.sum(e, axis=-1, keepdims=True))
            p_ref[slot, rows, lo:lo + ATTN_WIN] = e.astype(BF16)
        p, hh = divmod(h, 2)
        o = _dot(p_ref[slot], pair_rows(v_refs[sub:sub + 3], p)) / jnp.concatenate(sums, axis=0)
        if hh == 0:
            o_first = o
        else:
            o_ref[0, sub * ATTN_QB:(sub + 1) * ATTN_QB, p * LANES:(p + 1) * LANES] = jnp.where(
                masks[0], o_first, o).astype(o_ref.dtype)


def _attn_prompt(q, k, v, bias_rows):
    b, t, _ = q.shape
    tok = pl.BlockSpec((1, ATTN_SUB * ATTN_QB, W_A), lambda bb, i: (bb, i, 0))
    blk = lambda j: pl.BlockSpec(
        (1, ATTN_QB, W_A), lambda bb, i, j=j: (bb, jnp.maximum(ATTN_SUB * i + j - 2, 0), 0))
    kv_specs = [blk(j) for j in range(ATTN_SUB + 2)]
    return pl.pallas_call(
        _attn_prompt_kernel,
        grid=(b, t // (ATTN_SUB * ATTN_QB)),
        in_specs=[tok] + kv_specs + kv_specs + [_full_spec(bias_rows.shape)],
        out_specs=tok,
        out_shape=jax.ShapeDtypeStruct((b, t, W_A), BF16),
        scratch_shapes=[pltpu.VMEM((3, H_A, ATTN_QB, ATTN_KB), F32),
                        pltpu.VMEM((2, ATTN_QB, ATTN_KB), F32),
                        pltpu.VMEM((2, ATTN_QB, ATTN_KB), BF16)],
        compiler_params=_params(("arbitrary", "arbitrary")),
        name="attn_prompt",
    )(q, *([k] * (ATTN_SUB + 2)), *([v] * (ATTN_SUB + 2)), bias_rows)


def _attn_sample_kernel(q_ref, kn_ref, vn_ref, kc_ref, vc_ref, row_ref, o_ref, bc_ref, bn_ref):
    nb, s_len, _ = q_ref.shape
    w = kc_ref.shape[2]

    @pl.when(pl.program_id(0) == 0)
    def _():
        for h in range(H_A):
            t = _toeplitz_bias(row_ref, h, s_len)
            bc_ref[h * s_len:(h + 1) * s_len, :] = t[:, :w]
            bn_ref[h * s_len:(h + 1) * s_len, :] = t[:, w:w + s_len]

    lane_head = lax.broadcasted_iota(jnp.int32, (1, W_A), 1) // HD_A
    row_head = lax.broadcasted_iota(jnp.int32, (H_A * s_len, 1), 0) // s_len
    own_head = row_head == lane_head
    for b in range(nb):
        q = q_ref[b] * BF16(HD_A ** -0.5)
        qs = jnp.concatenate([q] * H_A, axis=0)
        qs = jnp.where(own_head, qs, jnp.zeros_like(qs))
        sc = _dot(qs, kc_ref[b].astype(BF16)) + bc_ref[...]
        sn = _dot_nt(qs, kn_ref[b]) + bn_ref[...]
        m = jnp.maximum(jnp.max(sc, axis=-1, keepdims=True), jnp.max(sn, axis=-1, keepdims=True))
        ec = jnp.exp(sc - m)
        en = jnp.exp(sn - m)
        l = jnp.sum(ec, axis=-1, keepdims=True) + jnp.sum(en, axis=-1, keepdims=True)
        full = (_dot_nt(ec.astype(BF16), vc_ref[b].astype(BF16))
                + _dot(en.astype(BF16), vn_ref[b])) / l
        o = full[:s_len]
        for h in range(1, H_A):
            o = jnp.where(lane_head == h, full[h * s_len:(h + 1) * s_len], o)
        o_ref[b] = o.astype(o_ref.dtype)


def _attn_sample(q, kn, vn, kc, vc, bias_rows, nb):
    b, s, _ = q.shape
    w = kc.shape[2]
    assert w == WINDOW_A
    new = pl.BlockSpec((nb, s, W_A), lambda i: (i, 0, 0))
    cache = pl.BlockSpec((nb, W_A, w), lambda i: (i, 0, 0))
    return pl.pallas_call(
        _attn_sample_kernel,
        grid=(b // nb,),
        in_specs=[new, new, new, cache, cache, _full_spec(bias_rows.shape)],
        out_specs=new,
        out_shape=jax.ShapeDtypeStruct((b, s, W_A), BF16),
        scratch_shapes=[pltpu.VMEM((H_A * s, w), F32), pltpu.VMEM((H_A * s, s), F32)],
        compiler_params=_params(("arbitrary",)),
        name="attn_sample",
    )(q, kn, vn, kc, vc, bias_rows)


def _gla_kernel(*refs, has_init, c):
    if has_init:
        q_ref, k_ref, v_ref, g_ref, la_ref, gg_ref, s0_ref, y_ref, so_ref, st_ref = refs
    else:
        q_ref, k_ref, v_ref, g_ref, la_ref, gg_ref, y_ref, so_ref, st_ref = refs
    nb, blk, _ = q_ref.shape
    nsub = c // GLA_SUB
    npair = H_B // 2
    j = pl.program_id(1)

    @pl.when(j == 0)
    def _():
        if has_init:
            st_ref[...] = s0_ref[...]
        else:
            st_ref[...] = jnp.zeros_like(st_ref)

    r2 = lax.broadcasted_iota(jnp.int32, (2 * c, c), 0)
    s2 = lax.broadcasted_iota(jnp.int32, (2 * c, c), 1)
    t2 = r2 & (c - 1)
    same_sub = (s2 // GLA_SUB) == (t2 // GLA_SUB)
    sum_mat = jnp.where((s2 <= t2) & ((r2 < c) | same_sub), 1.0, 0.0).astype(BF16)

    row = lax.broadcasted_iota(jnp.int32, (c, W_BK), 0)
    row2 = lax.broadcasted_iota(jnp.int32, (2 * c, LANES), 0)
    lane2 = lax.broadcasted_iota(jnp.int32, (2 * c, LANES), 1)
    own = (lane2 < DK_B) == (row2 < c)
    sub_of_row2 = (row2 & (c - 1)) // GLA_SUB
    tril2 = lane2 <= (row2 & (c - 1))
    eye = (lax.broadcasted_iota(jnp.int32, (LANES, LANES), 0)
           == lax.broadcasted_iota(jnp.int32, (LANES, LANES), 1))
    scale = DK_B ** -0.5
    zeros_k = jnp.zeros((LANES - c, nsub * LANES), BF16)
    zeros_v = jnp.zeros((LANES - c, DV_B), BF16)

    for ci in range(blk // c):
        rows = slice(ci * c, (ci + 1) * c)
        prep = []
        for b in range(nb):
            la = la_ref[b, rows]
            la_hi = la.astype(BF16)
            rest = la - la_hi.astype(F32)
            la_mid = rest.astype(BF16)
            la_lo = (rest - la_mid.astype(F32)).astype(BF16)
            sums = _dot(sum_mat, la_hi) + _dot(sum_mat, la_mid) + _dot(sum_mat, la_lo)
            cum, cum_sub = sums[:c], sums[c:]
            cum_end = cum[c - 1:c, :]
            qf = q_ref[b, rows].astype(F32)
            kf = k_ref[b, rows].astype(F32)
            q_sub = qf * jnp.exp(cum_sub) * scale
            q_in = (qf * jnp.exp(cum) * scale).astype(BF16)
            k_end = (kf * jnp.exp(cum_end - cum)).astype(BF16)
            k_sub = []
            for i in range(nsub):
                ref_i = cum[i * GLA_SUB - 1:i * GLA_SUB, :] if i else jnp.zeros((1, W_BK), F32)
                k_i = jnp.where(row < (i + 1) * GLA_SUB, kf * jnp.exp(ref_i - cum), 0.0)
                k_sub.append(k_i.astype(BF16))
            prep.append((q_sub, q_in, k_end, k_sub, cum_end))

        att = {}
        for b in range(nb):
            q_sub, _, _, k_sub, _ = prep[b]
            for p in range(npair):
                sl = slice(p * LANES, (p + 1) * LANES)
                k_stack = jnp.concatenate(
                    [jnp.concatenate([k_i[:, sl] for k_i in k_sub], axis=1), zeros_k], axis=0)
                q2 = jnp.where(own, jnp.concatenate([q_sub[:, sl], q_sub[:, sl]], axis=0), 0.0)
                q_stack = jnp.concatenate(
                    [jnp.where(sub_of_row2 == i, q2, 0.0) for i in range(nsub)], axis=1).astype(BF16)
                a = _dot_nt(q_stack, k_stack)
                att[b, p] = jnp.where(tril2, a, 0.0).astype(BF16)

        for b in range(nb):
            q_in = prep[b][1]
            for p in range(npair):
                sl = slice(p * LANES, (p + 1) * LANES)
                st_b = st_ref[b, p].astype(BF16)
                q_in2 = jnp.where(own, jnp.concatenate([q_in[:, sl], q_in[:, sl]], axis=0),
                                  jnp.zeros((), BF16))
                for hh in range(2):
                    h = 2 * p + hh
                    hs = slice(h * DV_B, (h + 1) * DV_B)
                    hrows = slice(hh * c, (hh + 1) * c)
                    lhs = jnp.concatenate([att[b, p][hrows], q_in2[hrows]], axis=1)
                    rhs = jnp.concatenate([v_ref[b, rows, hs], zeros_v, st_b], axis=0)
                    o = _dot(lhs, rhs)
                    gate = g_ref[b, rows, hs].astype(F32)
                    y = (o * _rms_scale(o)) * gg_ref[...] * (gate * _sigmoid(gate))
                    y_ref[b, rows, hs] = y.astype(y_ref.dtype)

        for b in range(nb):
            k_end, cum_end = prep[b][2], prep[b][4]
            for p in range(npair):
                sl = slice(p * LANES, (p + 1) * LANES)
                k2 = jnp.where(own, jnp.concatenate([k_end[:, sl], k_end[:, sl]], axis=0),
                               jnp.zeros((), BF16))
                v2 = jnp.concatenate([v_ref[b, rows, 2 * p * DV_B:(2 * p + 1) * DV_B],
                                      v_ref[b, rows, (2 * p + 1) * DV_B:(2 * p + 2) * DV_B]], axis=0)
                upd = _dot_tn(k2, v2)
                dec = jnp.exp(jnp.sum(jnp.where(eye, cum_end[:, sl], 0.0), axis=1, keepdims=True))
                st_ref[b, p] = st_ref[b, p] * dec + upd

    @pl.when(j == pl.num_programs(1) - 1)
    def _():
        so_ref[...] = st_ref[...]


def _gla(q, k, v, g, la, g_gla, s0, nb, c, chunks_per_step):
    nbt, t, _ = q.shape
    blk = c * chunks_per_step
    tok = lambda w: pl.BlockSpec((nb, blk, w), lambda b, j: (b, j, 0))
    st_spec = pl.BlockSpec((nb, H_B // 2, 2 * DK_B, DV_B), lambda b, j: (b, 0, 0, 0))
    in_specs = [tok(W_BK), tok(W_BK), tok(W_BV), tok(W_BV), tok(W_BK), _full_spec((1, DV_B))]
    args = [q, k, v, g, la, g_gla]
    if s0 is not None:
        in_specs.append(st_spec)
        args.append(s0)
    return pl.pallas_call(
        functools.partial(_gla_kernel, has_init=s0 is not None, c=c),
        grid=(nbt // nb, t // blk),
        in_specs=in_specs,
        out_specs=[tok(W_BV), st_spec],
        out_shape=[jax.ShapeDtypeStruct((nbt, t, W_BV), BF16),
                   jax.ShapeDtypeStruct((nbt, H_B // 2, 2 * DK_B, DV_B), F32)],
        scratch_shapes=[pltpu.VMEM((nb, H_B // 2, DV_B, 2 * DK_B), F32)],
        compiler_params=_params(("parallel", "arbitrary")),
        name="gla",
    )(*args)


def _mixout_kernel(x_ref, ya_ref, yb_ref, ga_ref, gb_ref, gm_ref, gp_ref, wa_ref, wb_ref, wo_ref,
                   o_ref):
    nb, tt, _ = x_ref.shape
    if nb == 1:
        halves = [(slice(None), slice(s * tt // 2, (s + 1) * tt // 2)) for s in range(2)]
        nbh, tth = nb, tt // 2
    else:
        halves = [(slice(s * nb // 2, (s + 1) * nb // 2), slice(None)) for s in range(2)]
        nbh, tth = nb // 2, tt
    m = nbh * tth
    merged = []
    for bs, ts in halves:
        a = _dot(ya_ref[bs, ts, :].reshape(m, W_A), wa_ref[...])
        b = _dot(yb_ref[bs, ts, :].reshape(m, W_BV), wb_ref[...])
        ga = _sigmoid(ga_ref[bs, ts, :].reshape(m, D_MODEL).astype(F32))
        gb = _sigmoid(gb_ref[bs, ts, :].reshape(m, D_MODEL).astype(F32))
        merged.append((ga * a + gb * b).astype(BF16))
    for (bs, ts), mg in zip(halves, merged):
        mo = _dot(mg, wo_ref[...])
        n = ((mo * _rms_scale(mo)) * gp_ref[...]).reshape(nbh, tth, D_MODEL)
        o_ref[bs, ts, :] = x_ref[bs, ts, :] + gm_ref[bs] * n


def _mixout(x, ya, yb, ga, gb, mod, g_post, wa, wb, wo, nb, tt):
    nbt, t, _ = x.shape
    tok = lambda w: pl.BlockSpec((nb, tt, w), lambda b, i: (b, i, 0))
    return pl.pallas_call(
        _mixout_kernel,
        grid=(nbt // nb, t // tt),
        in_specs=[tok(D_MODEL), tok(W_A), tok(W_BV), tok(D_MODEL), tok(D_MODEL),
                  pl.BlockSpec((nb, 1, D_MODEL), lambda b, i: (b, 0, 2)),
                  _full_spec((1, D_MODEL)), _full_spec(wa.shape), _full_spec(wb.shape),
                  _full_spec(wo.shape)],
        out_specs=tok(D_MODEL),
        out_shape=jax.ShapeDtypeStruct(x.shape, F32),
        compiler_params=_params(("parallel", "parallel")),
        name="mixout",
    )(x, ya, yb, ga, gb, mod, g_post, wa, wb, wo)


def _gelu_tanh(x):
    c = float(np.sqrt(2.0 / np.pi))
    half = 0.5 * x
    return half + half * jnp.tanh(x * (c + (0.044715 * c) * (x * x)))


def _ffn_kernel(*refs, has_state):
    if has_state:
        (x_ref, shift_ref, scale_ref, gate_ref, gpre_ref, gpost_ref, wu_ref, wd_ref,
         wdw_ref, bdw_ref, st_ref, o_ref, tail_ref, h_ref, act_ref) = refs
    else:
        (x_ref, shift_ref, scale_ref, gate_ref, gpre_ref, gpost_ref, wu_ref, wd_ref,
         wdw_ref, bdw_ref, o_ref, tail_ref, h_ref, act_ref, carry_ref) = refs

        @pl.when(pl.program_id(1) == 0)
        def _():
            carry_ref[...] = jnp.zeros_like(carry_ref)

    nb, tt, _ = x_ref.shape
    m = nb * tt
    x = x_ref[...]
    h = (x * _rms_scale(x)) * (gpre_ref[...] * (1.0 + scale_ref[...])) + shift_ref[...]
    h_ref[...] = h.reshape(m, D_MODEL).astype(BF16)
    ridx = lax.broadcasted_iota(jnp.int32, (nb, SUBLANES, FFN_FT), 1)

    def conv(u, lanes):
        u3 = u.reshape(nb, tt, FFN_FT)
        r1 = pltpu.roll(u, 1, 0).reshape(nb, tt, FFN_FT)
        r2 = pltpu.roll(u, 2, 0).reshape(nb, tt, FFN_FT)

        def taps(u_m2, u_m1, u_0):
            y = bdw_ref[:, lanes] + wdw_ref[0:1, lanes] * u_m2
            y = y + wdw_ref[1:2, lanes] * u_m1
            return y + wdw_ref[2:3, lanes] * u_0

        if has_state:
            p2, p1 = st_ref[:, 0:1, lanes], st_ref[:, 1:2, lanes]
        else:
            p2 = carry_ref[:, SUBLANES - 2:SUBLANES - 1, lanes]
            p1 = carry_ref[:, SUBLANES - 1:SUBLANES, lanes]
        h_m1 = jnp.where(ridx == 0, p1, r1[:, :SUBLANES])
        h_m2 = jnp.where(ridx == 0, p2, jnp.where(ridx == 1, p1, r2[:, :SUBLANES]))
        y = jnp.concatenate([taps(h_m2, h_m1, u3[:, :SUBLANES]),
                             taps(r2[:, SUBLANES:], r1[:, SUBLANES:], u3[:, SUBLANES:])], axis=1)
        tail = u3[:, tt - SUBLANES:, :]
        tail_ref[:, :, lanes] = tail
        if not has_state:
            carry_ref[:, :, lanes] = tail
        return y.reshape(m, FFN_FT)

    def up(f):
        hb = h_ref[...]
        lo = f * FFN_FT
        return (_dot(hb, wu_ref[:, lo:lo + FFN_FT]),
                _dot(hb, wu_ref[:, D_FF + lo:D_FF + lo + FFN_FT]))

    ua, ug = up(0)
    for f in range(FFN_NF):
        if f + 1 < FFN_NF:
            ua_next, ug_next = up(f + 1)
        cols = slice(f * FFN_FT, (f + 1) * FFN_FT)
        ya = conv(ua, cols)
        yg = conv(ug, slice(D_FF + f * FFN_FT, D_FF + (f + 1) * FFN_FT))
        act_ref[:, cols] = (_gelu_tanh(ya) * yg).astype(BF16)
        if f + 1 < FFN_NF:
            ua, ug = ua_next, ug_next
    yf = _dot(act_ref[...], wd_ref[...])
    n = ((yf * _rms_scale(yf)) * gpost_ref[...]).reshape(nb, tt, D_MODEL)
    o_ref[...] = x_ref[...] + gate_ref[...] * n


def _ffn(x, mod, g_pre, g_post, wu, wd, w_dw, b_dw, state, nb, tt):
    nbt, t, _ = x.shape
    m = nb * tt
    tok = pl.BlockSpec((nb, tt, D_MODEL), lambda b, i: (b, i, 0))
    mod_spec = lambda col: pl.BlockSpec((nb, 1, D_MODEL), lambda b, i, col=col: (b, 0, col))
    tail_spec = pl.BlockSpec((nb, SUBLANES, 2 * D_FF), lambda b, i: (b, 0, 0))
    in_specs = [tok, mod_spec(3), mod_spec(4), mod_spec(5), _full_spec((1, D_MODEL)),
                _full_spec((1, D_MODEL)), _full_spec(wu.shape),
                _full_spec(wd.shape), _full_spec(w_dw.shape), _full_spec(b_dw.shape)]
    args = [x, mod, mod, mod, g_pre, g_post, wu, wd, w_dw, b_dw]
    scratch = [pltpu.VMEM((m, D_MODEL), BF16), pltpu.VMEM((m, D_FF), BF16)]
    if state is not None:
        in_specs.append(pl.BlockSpec((nb, CONV_W - 1, 2 * D_FF), lambda b, i: (b, 0, 0)))
        args.append(state)
    else:
        scratch.append(pltpu.VMEM((nb, SUBLANES, 2 * D_FF), F32))
    return pl.pallas_call(
        functools.partial(_ffn_kernel, has_state=state is not None),
        grid=(nbt // nb, t // tt),
        in_specs=in_specs,
        out_specs=[tok, tail_spec],
        out_shape=[jax.ShapeDtypeStruct(x.shape, F32),
                   jax.ShapeDtypeStruct((nbt, SUBLANES, 2 * D_FF), F32)],
        scratch_shapes=scratch,
        compiler_params=_params(("parallel", "arbitrary")),
        name="ffn",
    )(*args)


def _bias_rows(rel_bias):
    assert BIAS_ROW >= ATTN_KB + ATTN_QB - 1 and WINDOW_A == ATTN_KB - ATTN_QB
    far_pos = jnp.broadcast_to(rel_bias[:, -1:], (H_A, BIAS_ROW))
    far_neg = jnp.broadcast_to(rel_bias[:, :1], (H_A, BIAS_ROW))
    n_mid = 2 * MAX_REL + 1
    n_lo = ATTN_KB + 1 - (WINDOW_A - MAX_REL) - n_mid
    rows = jnp.concatenate([far_pos[:, :WINDOW_A - MAX_REL], rel_bias[:, ::-1], far_neg[:, :n_lo],
                            far_pos[:, :BIAS_ROW - ATTN_KB - 1]], axis=1)
    return rows.astype(F32).reshape(H_A, 1, BIAS_ROW)


_LATE_WEIGHTS = ('w_br_a', 'w_br_b', 'w_out', 'w_up', 'w_down')


def _layer(x, mod, cache, s_gla, s_conv, w, first_chunk):
    nbt, t, _ = x.shape
    if first_chunk:
        nb, tt = 1, 1024
        nb_f, tt_f = 1, 1024
    else:
        nb, tt = 1024 // t, t
        nb_f, tt_f = 1024 // t, t
    keep_rows = min(WINDOW_A, t) if first_chunk else 0
    cast_ws = () if _LATE_WEIGHTS[0] in w else tuple(w[k + '_f32'] for k in _LATE_WEIGHTS)
    qa, ka, va, qb, kb, vb, gb, gate_a, gate_b, log_a, *extra = _inproj(
        x, mod, w['g_pre_mix'], w['w_main'], w['w_gk1'], w['w_gk2'], w['b_gk'], nb, tt, keep_rows,
        cast_ws)
    kv_t = extra[:2] if keep_rows else []
    if cast_ws:
        w = {**w, **dict(zip(_LATE_WEIGHTS, extra[len(kv_t):]))}
    if first_chunk:
        ya = _attn_prompt(qa, ka, va, w['bias_rows'])
        k_keep, v_keep = (jnp.transpose(a.reshape(nbt, H_A, HD_A, keep_rows), (0, 3, 1, 2))
                          for a in kv_t)
        yb, s_new = _gla(qb, kb, vb, gb, log_a, w['g_gla'], None, nbt, CHUNK, 2)
    else:
        k_cache, v_cache = cache
        ya = _attn_sample(qa, ka, va, k_cache, v_cache, w['bias_rows'], 4)
        k_keep, v_keep = (a.astype(F32).reshape(nbt, t, H_A, HD_A) for a in (ka, va))
        yb, s_new = _gla(qb, kb, vb, gb, log_a, w['g_gla'], s_gla, 4, t, 1)
    x1 = _mixout(x, ya, yb, gate_a, gate_b, mod, w['g_post_mix'], w['w_br_a'], w['w_br_b'],
                 w['w_out'], nb, tt)
    y, tail = _ffn(x1, mod, w['g_pre_ffn'], w['g_post_ffn'], w['w_up'], w['w_down'],
                   w['w_dw'], w['b_dw'], None if first_chunk else s_conv, nb_f, tt_f)
    return (y, k_keep, v_keep, s_new.reshape(nbt, H_B, DK_B, DV_B),
            tail[:, SUBLANES - (CONV_W - 1):, :]), w


def _prep_weights(w_main, w_gk1, w_gk2, b_gk, rel_bias, g_gla, w_br_a, w_br_b, w_out, w_up, w_dw,
                  b_dw, w_down, g_pre_mix, g_post_mix, g_pre_ffn, g_post_ffn):
    w_gk2p = jnp.pad(w_gk2, ((0, LANES - GK_RANK), (0, 0))).astype(BF16)
    row = lambda a: a.reshape(1, -1)
    return {
        'w_main': w_main, 'w_gk1': w_gk1, 'w_gk2': w_gk2p, 'b_gk': row(b_gk),
        'bias_rows': _bias_rows(rel_bias), 'g_gla': row(g_gla),
        'w_br_a_f32': w_br_a, 'w_br_b_f32': w_br_b, 'w_out_f32': w_out, 'w_up_f32': w_up,
        'w_down_f32': w_down, 'w_dw': w_dw, 'b_dw': row(b_dw),
        'g_pre_mix': row(g_pre_mix), 'g_post_mix': row(g_post_mix),
        'g_pre_ffn': row(g_pre_ffn), 'g_post_ffn': row(g_post_ffn),
    }


def kernel(x_prompt, x_sample, cache_k_a, cache_v_a, state_gla, state_conv, c_prompt, c_sample, w_ada, b_ada, g_pre_mix, g_post_mix, g_pre_ffn, g_post_ffn, w_in, w_gk2, b_gk, rel_bias, g_gla, w_br_a, w_br_b, w_out, w_up, w_dw, b_dw, w_down):
    depth = w_ada.shape[0]
    assert depth == 1
    bp, bs = x_prompt.shape[0], x_sample.shape[0]
    s_len = x_sample.shape[1]
    cache_rows = cache_k_a.shape[2]
    yp, ys = x_prompt, x_sample
    outs = [[] for _ in range(8)]
    for l in range(depth):
        c_all = jnp.concatenate([c_prompt, c_sample], axis=0)
        pad = (-c_all.shape[0]) % SUBLANES
        mod, w_main, w_gk1 = _prep(jnp.pad(c_all, ((0, pad), (0, 0))), w_ada[l], b_ada[l], w_in[l].T)
        w = _prep_weights(w_main, w_gk1, w_gk2[l], b_gk[l], rel_bias[l], g_gla[l], w_br_a[l],
                          w_br_b[l], w_out[l], w_up[l], w_dw[l], b_dw[l], w_down[l], g_pre_mix[l],
                          g_post_mix[l], g_pre_ffn[l], g_post_ffn[l])
        mod_p = mod[:bp].reshape(bp, 1, 6 * D_MODEL)
        mod_s = mod[bp:bp + bs].reshape(bs, 1, 6 * D_MODEL)
        (yp, kp, vp, gp, cp), w = _layer(yp, mod_p, None, None, None, w, True)
        to_t = lambda c: jnp.transpose(c, (0, 2, 3, 1)).reshape(bs, W_A, cache_rows)
        cache = (to_t(cache_k_a[l]), to_t(cache_v_a[l]))
        s0 = state_gla[l].reshape(bs, H_B // 2, 2 * DK_B, DV_B)
        (ys, kn, vn, gn, cn), _ = _layer(ys, mod_s, cache, s0, state_conv[l], w, False)
        for lst, a in zip(outs, (kp, vp, gp, cp, kn, vn, gn, cn)):
            lst.append(a)
    return (yp, ys) + tuple(jnp.stack(lst) for lst in outs)
```

```python
import functools

import jax
import jax.numpy as jnp
import numpy as np
from jax import lax
from jax.experimental import pallas as pl
from jax.experimental.pallas import tpu as pltpu

D_MODEL = 1024
CHUNK = 64
BAND_CHUNKS = 8
WINDOW_A = BAND_CHUNKS * CHUNK
H_A = 8
HD_A = 64
MAX_REL = 128
H_B = 4
DK_B = 64
DV_B = 128
GK_RANK = 16
GK_NORM = 16.0
GLA_SUB = 16
D_FF = 2816
CONV_W = 3
EPS = 1e-6
NEG_INF = -1e30
PAST_LEN = 2048

W_A = H_A * HD_A
W_BK = H_B * DK_B
W_BV = H_B * DV_B

LANES = 128
SUBLANES = 8
VMEM_LIMIT = 56 * 1024 * 1024

ATTN_QB = 256
ATTN_KB = 3 * ATTN_QB
BIAS_ROW = 1024
FFN_FT = 256
FFN_NF = D_FF // FFN_FT

BF16 = jnp.bfloat16
F32 = jnp.float32


def _params(sem):
    return pltpu.CompilerParams(dimension_semantics=sem, vmem_limit_bytes=VMEM_LIMIT)


def _full_spec(shape):
    nd = len(shape)
    return pl.BlockSpec(shape, lambda *_: (0,) * nd, pipeline_mode=pl.Buffered(1))


def _dot(a, b):
    return jnp.dot(a, b, preferred_element_type=F32)


def _dot_nt(a, b):
    return lax.dot_general(a, b, (((1,), (1,)), ((), ())), preferred_element_type=F32)


def _dot_tn(a, b):
    return lax.dot_general(a, b, (((0,), (0,)), ((), ())), preferred_element_type=F32)


def _sigmoid(x):
    return 1.0 / (1.0 + jnp.exp(-x))


def _rms_scale(x):
    return lax.rsqrt(jnp.mean(x * x, axis=-1, keepdims=True) + EPS)


_GK_LO = 3 * W_A + 2 * W_BK + 2 * W_BV


PREP_STEPS = 4


def _prep_kernel(c_ref, wa_ref, ba_ref, wi_ref, mod_ref, main_ref, gk_ref):
    c = c_ref[...]
    s = (c * _sigmoid(c)).astype(BF16)
    mod_ref[...] = _dot(s, wa_ref[...].astype(BF16)) + ba_ref[...]
    main_ref[:_GK_LO, :] = wi_ref[:_GK_LO, :].astype(BF16)
    main_ref[_GK_LO:, :] = wi_ref[_GK_LO + GK_RANK:, :].astype(BF16)
    gk_ref[:GK_RANK, :] = wi_ref[_GK_LO:_GK_LO + GK_RANK, :].astype(BF16)
    gk_ref[GK_RANK:, :] = jnp.zeros((LANES - GK_RANK, gk_ref.shape[1]), BF16)


def _prep(c_all, w_ada, b_ada, w_in_t):
    rows = c_all.shape[0]
    n = w_ada.shape[1]
    d_in, d = w_in_t.shape
    tn, cols = n // PREP_STEPS, d // PREP_STEPS
    return pl.pallas_call(
        _prep_kernel,
        grid=(PREP_STEPS,),
        in_specs=[pl.BlockSpec((rows, D_MODEL), lambda j: (0, 0)),
                  pl.BlockSpec((D_MODEL, tn), lambda j: (0, j)),
                  pl.BlockSpec((1, tn), lambda j: (0, j)),
                  pl.BlockSpec((d_in, cols), lambda j: (0, j))],
        out_specs=[pl.BlockSpec((rows, tn), lambda j: (0, j)),
                   pl.BlockSpec((d_in - GK_RANK, cols), lambda j: (0, j)),
                   pl.BlockSpec((LANES, cols), lambda j: (0, j))],
        out_shape=[jax.ShapeDtypeStruct((rows, n), F32),
                   jax.ShapeDtypeStruct((d_in - GK_RANK, d), BF16),
                   jax.ShapeDtypeStruct((LANES, d), BF16)],
        compiler_params=_params(("parallel",)),
        name="prep",
    )(c_all, w_ada, b_ada.reshape(1, n), w_in_t)


_IN_GROUPS = (W_A, W_A, W_A, W_BK, W_BK, W_BV, W_BV, D_MODEL, D_MODEL)
_IN_MAIN = sum(_IN_GROUPS)


def _inproj_kernel(*refs, n_cast, n_kv_t):
    (x_ref, shift_ref, scale_ref, g_ref, wm_ref, wg1_ref, wg2_ref, bgk_ref), refs = refs[:8], refs[8:]
    cast_in, refs = refs[:n_cast], refs[n_cast:]
    (qa_ref, ka_ref, va_ref, qb_ref, kb_ref, vb_ref, gb_ref, ga_ref, gtb_ref, la_ref) = refs[:10]
    kv_t_refs, cast_out = refs[10:10 + n_kv_t], refs[10 + n_kv_t:]
    for w_ref, o_ref in zip(cast_in, cast_out):
        o_ref[...] = w_ref[...].astype(o_ref.dtype)
    nb, tt, _ = x_ref.shape
    x = x_ref[...]
    h = (x * _rms_scale(x)) * (g_ref[...] * (1.0 + scale_ref[...])) + shift_ref[...]
    hb = h.reshape(nb * tt, D_MODEL).astype(BF16)
    if kv_t_refs:
        @pl.when(pl.program_id(1) == pl.num_programs(1) - 1)
        def _():
            rows = kv_t_refs[0].shape[2]
            newest = hb[nb * tt - rows:, :]
            for o_ref, lo in zip(kv_t_refs, (W_A, 2 * W_A)):
                o_ref[0] = _dot_nt(wm_ref[lo:lo + W_A, :], newest)
    outs = (qa_ref, ka_ref, va_ref, qb_ref, kb_ref, vb_ref, gb_ref, ga_ref, gtb_ref)
    lo = 0
    for o_ref, w in zip(outs, _IN_GROUPS):
        z = _dot_nt(hb, wm_ref[lo:lo + w, :])
        o_ref[...] = z.reshape(nb, tt, w).astype(o_ref.dtype)
        lo += w
    gk_low = _dot_nt(hb, wg1_ref[...]).astype(BF16)
    gk = _dot(gk_low, wg2_ref[...]) + bgk_ref[...]
    log_a = (jnp.minimum(gk, 0.0) - jnp.log1p(jnp.exp(-jnp.abs(gk)))) / GK_NORM
    la_ref[...] = log_a.reshape(nb, tt, W_BK)


def _inproj(x, mod, g_pre, wm, wg1, wg2, bgk, nb, tt, keep_rows, cast_ws=()):
    nbt, t, _ = x.shape
    grid = (nbt // nb, t // tt)
    steps = grid[0] * grid[1]
    cast_specs = [pl.BlockSpec((w.shape[0] // steps, w.shape[1]),
                               lambda b, i: (b * grid[1] + i, 0)) for w in cast_ws]
    tok = lambda w: pl.BlockSpec((nb, tt, w), lambda b, i: (b, i, 0))
    mod_spec = lambda col: pl.BlockSpec((nb, 1, D_MODEL), lambda b, i, col=col: (b, 0, col))
    widths = _IN_GROUPS + (W_BK,)
    dtypes = (BF16,) * len(_IN_GROUPS) + (F32,)
    out_specs = [tok(w) for w in widths]
    out_shape = [jax.ShapeDtypeStruct((nbt, t, w), dt) for w, dt in zip(widths, dtypes)]
    if keep_rows:
        assert nb == 1 and keep_rows <= tt
        out_specs += [pl.BlockSpec((1, W_A, keep_rows), lambda b, i: (b, 0, 0))] * 2
        out_shape += [jax.ShapeDtypeStruct((nbt, W_A, keep_rows), F32)] * 2
    return pl.pallas_call(
        functools.partial(_inproj_kernel, n_cast=len(cast_ws), n_kv_t=2 if keep_rows else 0),
        grid=grid,
        in_specs=[tok(D_MODEL), mod_spec(0), mod_spec(1), _full_spec((1, D_MODEL)),
                  _full_spec(wm.shape), _full_spec(wg1.shape), _full_spec(wg2.shape),
                  _full_spec((1, W_BK))] + cast_specs,
        out_specs=out_specs + cast_specs,
        out_shape=out_shape + [jax.ShapeDtypeStruct(w.shape, BF16) for w in cast_ws],
        compiler_params=_params(("arbitrary", "arbitrary")),
        name="inproj",
    )(x, mod, mod, g_pre, wm, wg1, wg2, bgk, *cast_ws)


def _head_masks():
    lane = lax.broadcasted_iota(jnp.int32, (1, LANES), 1)
    first = lane < HD_A
    return first, jnp.logical_not(first)


def _toeplitz_bias(row_ref, h, rows):
    rb = jnp.broadcast_to(row_ref[h], (rows, BIAS_ROW))
    return pltpu.roll(rb, 0, 1, stride=1, stride_axis=0)


ATTN_RB = 32
ATTN_WIN = 640


ATTN_SUB = 4


def _attn_prompt_kernel(*refs):
    q_ref, refs = refs[0], refs[1:]
    k_refs, v_refs = refs[:ATTN_SUB + 2], refs[ATTN_SUB + 2:2 * ATTN_SUB + 4]
    row_ref, o_ref, bias_ref, s_ref, p_ref = refs[2 * ATTN_SUB + 4:]
    i = pl.program_id(1)

    @pl.when((pl.program_id(0) == 0) & (i == 0))
    def _():
        qc = lax.broadcasted_iota(jnp.int32, (ATTN_QB, ATTN_KB), 0) // CHUNK
        col = lax.broadcasted_iota(jnp.int32, (ATTN_QB, ATTN_KB), 1)
        kc = col // CHUNK - BAND_CHUNKS
        valid = (kc <= qc) & (kc >= qc - BAND_CHUNKS)
        for h in range(H_A):
            t = _toeplitz_bias(row_ref, h, ATTN_QB)
            band = jnp.where(valid, t[:, :ATTN_KB], NEG_INF)
            bias_ref[0, h] = jnp.where(col >= 2 * ATTN_QB, band, NEG_INF)
            bias_ref[1, h] = jnp.where(col >= ATTN_QB, band, NEG_INF)
            bias_ref[2, h] = band
        p_ref[...] = jnp.zeros_like(p_ref)

    masks = _head_masks()

    def pair_rows(refs, p):
        sl = slice(p * LANES, (p + 1) * LANES)
        return jnp.concatenate([r[0, :, sl] for r in refs], axis=0)

    units = [(sub, h) for sub in range(ATTN_SUB) for h in range(H_A)]

    def scores(u):
        sub, h = units[u]
        p, hh = divmod(h, 2)
        qp = q_ref[0, sub * ATTN_QB:(sub + 1) * ATTN_QB, p * LANES:(p + 1) * LANES] * BF16(HD_A ** -0.5)
        qm = jnp.where(masks[hh], qp, jnp.zeros_like(qp))
        s_ref[u % 2] = _dot_nt(qm, pair_rows(k_refs[sub:sub + 3], p))

    scores(0)
    o_first = None
    for u, (sub, h) in enumerate(units):
        if u + 1 < len(units):
            scores(u + 1)
        slot = u % 2
        var = jnp.minimum(ATTN_SUB * i + sub, 2)
        sums = []
        for r in range(ATTN_QB // ATTN_RB):
            rows = slice(r * ATTN_RB, (r + 1) * ATTN_RB)
            lo = 0 if r * ATTN_RB < ATTN_QB // 2 else ATTN_KB - ATTN_WIN
            sb = s_ref[slot, rows, lo:lo + ATTN_WIN] + bias_ref[var, h, rows, lo:lo + ATTN_WIN]
            e = jnp.exp(sb - jnp.max(sb, axis=-1, keepdims=True))
            sums.append(jnp.sum(e, axis=-1, keepdims=True))
            p_ref[slot, rows, lo:lo + ATTN_WIN] = e.astype(BF16)
        p, hh = divmod(h, 2)
        o = _dot(p_ref[slot], pair_rows(v_refs[sub:sub + 3], p)) / jnp.concatenate(sums, axis=0)
        if hh == 0:
            o_first = o
        else:
            o_ref[0, sub * ATTN_QB:(sub + 1) * ATTN_QB, p * LANES:(p + 1) * LANES] = jnp.where(
                masks[0], o_first, o).astype(o_ref.dtype)


def _attn_prompt(q, k, v, bias_rows):
    b, t, _ = q.shape
    tok = pl.BlockSpec((1, ATTN_SUB * ATTN_QB, W_A), lambda bb, i: (bb, i, 0))
    blk = lambda j: pl.BlockSpec(
        (1, ATTN_QB, W_A), lambda bb, i, j=j: (bb, jnp.maximum(ATTN_SUB * i + j - 2, 0), 0))
    kv_specs = [blk(j) for j in range(ATTN_SUB + 2)]
    return pl.pallas_call(
        _attn_prompt_kernel,
        grid=(b, t // (ATTN_SUB * ATTN_QB)),
        in_specs=[tok] + kv_specs + kv_specs + [_full_spec(bias_rows.shape)],
        out_specs=tok,
        out_shape=jax.ShapeDtypeStruct((b, t, W_A), BF16),
        scratch_shapes=[pltpu.VMEM((3, H_A, ATTN_QB, ATTN_KB), F32),
                        pltpu.VMEM((2, ATTN_QB, ATTN_KB), F32),
                        pltpu.VMEM((2, ATTN_QB, ATTN_KB), BF16)],
        compiler_params=_params(("arbitrary", "arbitrary")),
        name="attn_prompt",
    )(q, *([k] * (ATTN_SUB + 2)), *([v] * (ATTN_SUB + 2)), bias_rows)


def _attn_sample_kernel(q_ref, kn_ref, vn_ref, kc_ref, vc_ref, row_ref, o_ref, bc_ref, bn_ref):
    nb, s_len, _ = q_ref.shape
    w = kc_ref.shape[2]

    @pl.when(pl.program_id(0) == 0)
    def _():
        for h in range(H_A):
            t = _toeplitz_bias(row_ref, h, s_len)
            bc_ref[h * s_len:(h + 1) * s_len, :] = t[:, :w]
            bn_ref[h * s_len:(h + 1) * s_len, :] = t[:, w:w + s_len]

    lane_head = lax.broadcasted_iota(jnp.int32, (1, W_A), 1) // HD_A
    row_head = lax.broadcasted_iota(jnp.int32, (H_A * s_len, 1), 0) // s_len
    own_head = row_head == lane_head
    for b in range(nb):
        q = q_ref[b] * BF16(HD_A ** -0.5)
        qs = jnp.concatenate([q] * H_A, axis=0)
        qs = jnp.where(own_head, qs, jnp.zeros_like(qs))
        sc = _dot(qs, kc_ref[b].astype(BF16)) + bc_ref[...]
        sn = _dot_nt(qs, kn_ref[b]) + bn_ref[...]
        m = jnp.maximum(jnp.max(sc, axis=-1, keepdims=True), jnp.max(sn, axis=-1, keepdims=True))
        ec = jnp.exp(sc - m)
        en = jnp.exp(sn - m)
        l = jnp.sum(ec, axis=-1, keepdims=True) + jnp.sum(en, axis=-1, keepdims=True)
        full = (_dot_nt(ec.astype(BF16), vc_ref[b].astype(BF16))
                + _dot(en.astype(BF16), vn_ref[b])) / l
        o = full[:s_len]
        for h in range(1, H_A):
            o = jnp.where(lane_head == h, full[h * s_len:(h + 1) * s_len], o)
        o_ref[b] = o.astype(o_ref.dtype)


def _attn_sample(q, kn, vn, kc, vc, bias_rows, nb):
    b, s, _ = q.shape
    w = kc.shape[2]
    assert w == WINDOW_A
    new = pl.BlockSpec((nb, s, W_A), lambda i: (i, 0, 0))
    cache = pl.BlockSpec((nb, W_A, w), lambda i: (i, 0, 0))
    return pl.pallas_call(
        _attn_sample_kernel,
        grid=(b // nb,),
        in_specs=[new, new, new, cache, cache, _full_spec(bias_rows.shape)],
        out_specs=new,
        out_shape=jax.ShapeDtypeStruct((b, s, W_A), BF16),
        scratch_shapes=[pltpu.VMEM((H_A * s, w), F32), pltpu.VMEM((H_A * s, s), F32)],
        compiler_params=_params(("arbitrary",)),
        name="attn_sample",
    )(q, kn, vn, kc, vc, bias_rows)


def _gla_kernel(*refs, has_init, c):
    if has_init:
        q_ref, k_ref, v_ref, g_ref, la_ref, gg_ref, s0_ref, y_ref, so_ref, st_ref = refs
    else:
        q_ref, k_ref, v_ref, g_ref, la_ref, gg_ref, y_ref, so_ref, st_ref = refs
    nb, blk, _ = q_ref.shape
    nsub = c // GLA_SUB
    npair = H_B // 2
    j = pl.program_id(1)

    @pl.when(j == 0)
    def _():
        if has_init:
            st_ref[...] = s0_ref[...]
        else:
            st_ref[...] = jnp.zeros_like(st_ref)

    r2 = lax.broadcasted_iota(jnp.int32, (2 * c, c), 0)
    s2 = lax.broadcasted_iota(jnp.int32, (2 * c, c), 1)
    t2 = r2 & (c - 1)
    same_sub = (s2 // GLA_SUB) == (t2 // GLA_SUB)
    sum_mat = jnp.where((s2 <= t2) & ((r2 < c) | same_sub), 1.0, 0.0).astype(BF16)

    row = lax.broadcasted_iota(jnp.int32, (c, W_BK), 0)
    row2 = lax.broadcasted_iota(jnp.int32, (2 * c, LANES), 0)
    lane2 = lax.broadcasted_iota(jnp.int32, (2 * c, LANES), 1)
    own = (lane2 < DK_B) == (row2 < c)
    sub_of_row2 = (row2 & (c - 1)) // GLA_SUB
    tril2 = lane2 <= (row2 & (c - 1))
    eye = (lax.broadcasted_iota(jnp.int32, (LANES, LANES), 0)
           == lax.broadcasted_iota(jnp.int32, (LANES, LANES), 1))
    scale = DK_B ** -0.5
    zeros_k = jnp.zeros((LANES - c, nsub * LANES), BF16)
    zeros_v = jnp.zeros((LANES - c, DV_B), BF16)

    for ci in range(blk // c):
        rows = slice(ci * c, (ci + 1) * c)
        prep = []
        for b in range(nb):
            la = la_ref[b, rows]
            la_hi = la.astype(BF16)
            rest = la - la_hi.astype(F32)
            la_mid = rest.astype(BF16)
            la_lo = (rest - la_mid.astype(F32)).astype(BF16)
            sums = _dot(sum_mat, la_hi) + _dot(sum_mat, la_mid) + _dot(sum_mat, la_lo)
            cum, cum_sub = sums[:c], sums[c:]
            cum_end = cum[c - 1:c, :]
            qf = q_ref[b, rows].astype(F32)
            kf = k_ref[b, rows].astype(F32)
            q_sub = qf * jnp.exp(cum_sub) * scale
            q_in = (qf * jnp.exp(cum) * scale).astype(BF16)
            k_end = (kf * jnp.exp(cum_end - cum)).astype(BF16)
            k_sub = []
            for i in range(nsub):
                ref_i = cum[i * GLA_SUB - 1:i * GLA_SUB, :] if i else jnp.zeros((1, W_BK), F32)
                k_i = jnp.where(row < (i + 1) * GLA_SUB, kf * jnp.exp(ref_i - cum), 0.0)
                k_sub.append(k_i.astype(BF16))
            prep.append((q_sub, q_in, k_end, k_sub, cum_end))

        att = {}
        for b in range(nb):
            q_sub, _, _, k_sub, _ = prep[b]
            for p in range(npair):
                sl = slice(p * LANES, (p + 1) * LANES)
                k_stack = jnp.concatenate(
                    [jnp.concatenate([k_i[:, sl] for k_i in k_sub], axis=1), zeros_k], axis=0)
                q2 = jnp.where(own, jnp.concatenate([q_sub[:, sl], q_sub[:, sl]], axis=0), 0.0)
                q_stack = jnp.concatenate(
                    [jnp.where(sub_of_row2 == i, q2, 0.0) for i in range(nsub)], axis=1).astype(BF16)
                a = _dot_nt(q_stack, k_stack)
                att[b, p] = jnp.where(tril2, a, 0.0).astype(BF16)

        for b in range(nb):
            q_in = prep[b][1]
            for p in range(npair):
                sl = slice(p * LANES, (p + 1) * LANES)
                st_b = st_ref[b, p].astype(BF16)
                q_in2 = jnp.where(own, jnp.concatenate([q_in[:, sl], q_in[:, sl]], axis=0),
                                  jnp.zeros((), BF16))
                for hh in range(2):
                    h = 2 * p + hh
                    hs = slice(h * DV_B, (h + 1) * DV_B)
                    hrows = slice(hh * c, (hh + 1) * c)
                    lhs = jnp.concatenate([att[b, p][hrows], q_in2[hrows]], axis=1)
                    rhs = jnp.concatenate([v_ref[b, rows, hs], zeros_v, st_b], axis=0)
                    o = _dot(lhs, rhs)
                    gate = g_ref[b, rows, hs].astype(F32)
                    y = (o * _rms_scale(o)) * gg_ref[...] * (gate * _sigmoid(gate))
                    y_ref[b, rows, hs] = y.astype(y_ref.dtype)

        for b in range(nb):
            k_end, cum_end = prep[b][2], prep[b][4]
            for p in range(npair):
                sl = slice(p * LANES, (p + 1) * LANES)
                k2 = jnp.where(own, jnp.concatenate([k_end[:, sl], k_end[:, sl]], axis=0),
                               jnp.zeros((), BF16))
                v2 = jnp.concatenate([v_ref[b, rows, 2 * p * DV_B:(2 * p + 1) * DV_B],
                                      v_ref[b, rows, (2 * p + 1) * DV_B:(2 * p + 2) * DV_B]], axis=0)
                upd = _dot_tn(k2, v2)
                dec = jnp.exp(jnp.sum(jnp.where(eye, cum_end[:, sl], 0.0), axis=1, keepdims=True))
                st_ref[b, p] = st_ref[b, p] * dec + upd

    @pl.when(j == pl.num_programs(1) - 1)
    def _():
        so_ref[...] = st_ref[...]


def _gla(q, k, v, g, la, g_gla, s0, nb, c, chunks_per_step):
    nbt, t, _ = q.shape
    blk = c * chunks_per_step
    tok = lambda w: pl.BlockSpec((nb, blk, w), lambda b, j: (b, j, 0))
    st_spec = pl.BlockSpec((nb, H_B // 2, 2 * DK_B, DV_B), lambda b, j: (b, 0, 0, 0))
    in_specs = [tok(W_BK), tok(W_BK), tok(W_BV), tok(W_BV), tok(W_BK), _full_spec((1, DV_B))]
    args = [q, k, v, g, la, g_gla]
    if s0 is not None:
        in_specs.append(st_spec)
        args.append(s0)
    return pl.pallas_call(
        functools.partial(_gla_kernel, has_init=s0 is not None, c=c),
        grid=(nbt // nb, t // blk),
        in_specs=in_specs,
        out_specs=[tok(W_BV), st_spec],
        out_shape=[jax.ShapeDtypeStruct((nbt, t, W_BV), BF16),
                   jax.ShapeDtypeStruct((nbt, H_B // 2, 2 * DK_B, DV_B), F32)],
        scratch_shapes=[pltpu.VMEM((nb, H_B // 2, DV_B, 2 * DK_B), F32)],
        compiler_params=_params(("parallel", "arbitrary")),
        name="gla",
    )(*args)


def _mixout_kernel(x_ref, ya_ref, yb_ref, ga_ref, gb_ref, gm_ref, gp_ref, wa_ref, wb_ref, wo_ref,
                   o_ref):
    nb, tt, _ = x_ref.shape
    if nb == 1:
        halves = [(slice(None), slice(s * tt // 2, (s + 1) * tt // 2)) for s in range(2)]
        nbh, tth = nb, tt // 2
    else:
        halves = [(slice(s * nb // 2, (s + 1) * nb // 2), slice(None)) for s in range(2)]
        nbh, tth = nb // 2, tt
    m = nbh * tth
    merged = []
    for bs, ts in halves:
        a = _dot(ya_ref[bs, ts, :].reshape(m, W_A), wa_ref[...])
        b = _dot(yb_ref[bs, ts, :].reshape(m, W_BV), wb_ref[...])
        ga = _sigmoid(ga_ref[bs, ts, :].reshape(m, D_MODEL).astype(F32))
        gb = _sigmoid(gb_ref[bs, ts, :].reshape(m, D_MODEL).astype(F32))
        merged.append((ga * a + gb * b).astype(BF16))
    for (bs, ts), mg in zip(halves, merged):
        mo = _dot(mg, wo_ref[...])
        n = ((mo * _rms_scale(mo)) * gp_ref[...]).reshape(nbh, tth, D_MODEL)
        o_ref[bs, ts, :] = x_ref[bs, ts, :] + gm_ref[bs] * n


def _mixout(x, ya, yb, ga, gb, mod, g_post, wa, wb, wo, nb, tt):
    nbt, t, _ = x.shape
    tok = lambda w: pl.BlockSpec((nb, tt, w), lambda b, i: (b, i, 0))
    return pl.pallas_call(
        _mixout_kernel,
        grid=(nbt // nb, t // tt),
        in_specs=[tok(D_MODEL), tok(W_A), tok(W_BV), tok(D_MODEL), tok(D_MODEL),
                  pl.BlockSpec((nb, 1, D_MODEL), lambda b, i: (b, 0, 2)),
                  _full_spec((1, D_MODEL)), _full_spec(wa.shape), _full_spec(wb.shape),
                  _full_spec(wo.shape)],
        out_specs=tok(D_MODEL),
        out_shape=jax.ShapeDtypeStruct(x.shape, F32),
        compiler_params=_params(("parallel", "parallel")),
        name="mixout",
    )(x, ya, yb, ga, gb, mod, g_post, wa, wb, wo)


def _gelu_tanh(x):
    c = float(np.sqrt(2.0 / np.pi))
    half = 0.5 * x
    return half + half * jnp.tanh(x * (c + (0.044715 * c) * (x * x)))


def _ffn_kernel(*refs, has_state):
    if has_state:
        (x_ref, shift_ref, scale_ref, gate_ref, gpre_ref, gpost_ref, wu_ref, wd_ref,
         wdw_ref, bdw_ref, st_ref, o_ref, tail_ref, h_ref, act_ref) = refs
    else:
        (x_ref, shift_ref, scale_ref, gate_ref, gpre_ref, gpost_ref, wu_ref, wd_ref,
         wdw_ref, bdw_ref, o_ref, tail_ref, h_ref, act_ref, carry_ref) = refs

        @pl.when(pl.program_id(1) == 0)
        def _():
            carry_ref[...] = jnp.zeros_like(carry_ref)

    nb, tt, _ = x_ref.shape
    m = nb * tt
    x = x_ref[...]
    h = (x * _rms_scale(x)) * (gpre_ref[...] * (1.0 + scale_ref[...])) + shift_ref[...]
    h_ref[...] = h.reshape(m, D_MODEL).astype(BF16)
    ridx = lax.broadcasted_iota(jnp.int32, (nb, SUBLANES, FFN_FT), 1)

    def conv(u, lanes):
        u3 = u.reshape(nb, tt, FFN_FT)
        r1 = pltpu.roll(u, 1, 0).reshape(nb, tt, FFN_FT)
        r2 = pltpu.roll(u, 2, 0).reshape(nb, tt, FFN_FT)

        def taps(u_m2, u_m1, u_0):
            y = bdw_ref[:, lanes] + wdw_ref[0:1, lanes] * u_m2
            y = y + wdw_ref[1:2, lanes] * u_m1
            return y + wdw_ref[2:3, lanes] * u_0

        if has_state:
            p2, p1 = st_ref[:, 0:1, lanes], st_ref[:, 1:2, lanes]
        else:
            p2 = carry_ref[:, SUBLANES - 2:SUBLANES - 1, lanes]
            p1 = carry_ref[:, SUBLANES - 1:SUBLANES, lanes]
        h_m1 = jnp.where(ridx == 0, p1, r1[:, :SUBLANES])
        h_m2 = jnp.where(ridx == 0, p2, jnp.where(ridx == 1, p1, r2[:, :SUBLANES]))
        y = jnp.concatenate([taps(h_m2, h_m1, u3[:, :SUBLANES]),
                             taps(r2[:, SUBLANES:], r1[:, SUBLANES:], u3[:, SUBLANES:])], axis=1)
        tail = u3[:, tt - SUBLANES:, :]
        tail_ref[:, :, lanes] = tail
        if not has_state:
            carry_ref[:, :, lanes] = tail
        return y.reshape(m, FFN_FT)

    def up(f):
        hb = h_ref[...]
        lo = f * FFN_FT
        return (_dot(hb, wu_ref[:, lo:lo + FFN_FT]),
                _dot(hb, wu_ref[:, D_FF + lo:D_FF + lo + FFN_FT]))

    ua, ug = up(0)
    for f in range(FFN_NF):
        if f + 1 < FFN_NF:
            ua_next, ug_next = up(f + 1)
        cols = slice(f * FFN_FT, (f + 1) * FFN_FT)
        ya = conv(ua, cols)
        yg = conv(ug, slice(D_FF + f * FFN_FT, D_FF + (f + 1) * FFN_FT))
        act_ref[:, cols] = (_gelu_tanh(ya) * yg).astype(BF16)
        if f + 1 < FFN_NF:
            ua, ug = ua_next, ug_next
    yf = _dot(act_ref[...], wd_ref[...])
    n = ((yf * _rms_scale(yf)) * gpost_ref[...]).reshape(nb, tt, D_MODEL)
    o_ref[...] = x_ref[...] + gate_ref[...] * n


def _ffn(x, mod, g_pre, g_post, wu, wd, w_dw, b_dw, state, nb, tt):
    nbt, t, _ = x.shape
    m = nb * tt
    tok = pl.BlockSpec((nb, tt, D_MODEL), lambda b, i: (b, i, 0))
    mod_spec = lambda col: pl.BlockSpec((nb, 1, D_MODEL), lambda b, i, col=col: (b, 0, col))
    tail_spec = pl.BlockSpec((nb, SUBLANES, 2 * D_FF), lambda b, i: (b, 0, 0))
    in_specs = [tok, mod_spec(3), mod_spec(4), mod_spec(5), _full_spec((1, D_MODEL)),
                _full_spec((1, D_MODEL)), _full_spec(wu.shape),
                _full_spec(wd.shape), _full_spec(w_dw.shape), _full_spec(b_dw.shape)]
    args = [x, mod, mod, mod, g_pre, g_post, wu, wd, w_dw, b_dw]
    scratch = [pltpu.VMEM((m, D_MODEL), BF16), pltpu.VMEM((m, D_FF), BF16)]
    if state is not None:
        in_specs.append(pl.BlockSpec((nb, CONV_W - 1, 2 * D_FF), lambda b, i: (b, 0, 0)))
        args.append(state)
    else:
        scratch.append(pltpu.VMEM((nb, SUBLANES, 2 * D_FF), F32))
    return pl.pallas_call(
        functools.partial(_ffn_kernel, has_state=state is not None),
        grid=(nbt // nb, t // tt),
        in_specs=in_specs,
        out_specs=[tok, tail_spec],
        out_shape=[jax.ShapeDtypeStruct(x.shape, F32),
                   jax.ShapeDtypeStruct((nbt, SUBLANES, 2 * D_FF), F32)],
        scratch_shapes=scratch,
        compiler_params=_params(("parallel", "arbitrary")),
        name="ffn",
    )(*args)


def _bias_rows(rel_bias):
    assert BIAS_ROW >= ATTN_KB + ATTN_QB - 1 and WINDOW_A == ATTN_KB - ATTN_QB
    far_pos = jnp.broadcast_to(rel_bias[:, -1:], (H_A, BIAS_ROW))
    far_neg = jnp.broadcast_to(rel_bias[:, :1], (H_A, BIAS_ROW))
    n_mid = 2 * MAX_REL + 1
    n_lo = ATTN_KB + 1 - (WINDOW_A - MAX_REL) - n_mid
    rows = jnp.concatenate([far_pos[:, :WINDOW_A - MAX_REL], rel_bias[:, ::-1], far_neg[:, :n_lo],
                            far_pos[:, :BIAS_ROW - ATTN_KB - 1]], axis=1)
    return rows.astype(F32).reshape(H_A, 1, BIAS_ROW)


_LATE_WEIGHTS = ('w_br_a', 'w_br_b', 'w_out', 'w_up', 'w_down')


def _layer(x, mod, cache, s_gla, s_conv, w, first_chunk):
    nbt, t, _ = x.shape
    if first_chunk:
        nb, tt = 1, 1024
        nb_f, tt_f = 1, 1024
    else:
        nb, tt = 1024 // t, t
        nb_f, tt_f = 1024 // t, t
    keep_rows = min(WINDOW_A, t) if first_chunk else 0
    cast_ws = () if _LATE_WEIGHTS[0] in w else tuple(w[k + '_f32'] for k in _LATE_WEIGHTS)
    qa, ka, va, qb, kb, vb, gb, gate_a, gate_b, log_a, *extra = _inproj(
        x, mod, w['g_pre_mix'], w['w_main'], w['w_gk1'], w['w_gk2'], w['b_gk'], nb, tt, keep_rows,
        cast_ws)
    kv_t = extra[:2] if keep_rows else []
    if cast_ws:
        w = {**w, **dict(zip(_LATE_WEIGHTS, extra[len(kv_t):]))}
    if first_chunk:
        ya = _attn_prompt(qa, ka, va, w['bias_rows'])
        k_keep, v_keep = (jnp.transpose(a.reshape(nbt, H_A, HD_A, keep_rows), (0, 3, 1, 2))
                          for a in kv_t)
        yb, s_new = _gla(qb, kb, vb, gb, log_a, w['g_gla'], None, nbt, CHUNK, 4)
    else:
        k_cache, v_cache = cache
        ya = _attn_sample(qa, ka, va, k_cache, v_cache, w['bias_rows'], 4)
        k_keep, v_keep = (a.astype(F32).reshape(nbt, t, H_A, HD_A) for a in (ka, va))
        yb, s_new = _gla(qb, kb, vb, gb, log_a, w['g_gla'], s_gla, 4, t, 1)
    x1 = _mixout(x, ya, yb, gate_a, gate_b, mod, w['g_post_mix'], w['w_br_a'], w['w_br_b'],
                 w['w_out'], nb, tt)
    y, tail = _ffn(x1, mod, w['g_pre_ffn'], w['g_post_ffn'], w['w_up'], w['w_down'],
                   w['w_dw'], w['b_dw'], None if first_chunk else s_conv, nb_f, tt_f)
    return (y, k_keep, v_keep, s_new.reshape(nbt, H_B, DK_B, DV_B),
            tail[:, SUBLANES - (CONV_W - 1):, :]), w


def _prep_weights(w_main, w_gk1, w_gk2, b_gk, rel_bias, g_gla, w_br_a, w_br_b, w_out, w_up, w_dw,
                  b_dw, w_down, g_pre_mix, g_post_mix, g_pre_ffn, g_post_ffn):
    w_gk2p = jnp.pad(w_gk2, ((0, LANES - GK_RANK), (0, 0))).astype(BF16)
    row = lambda a: a.reshape(1, -1)
    return {
        'w_main': w_main, 'w_gk1': w_gk1, 'w_gk2': w_gk2p, 'b_gk': row(b_gk),
        'bias_rows': _bias_rows(rel_bias), 'g_gla': row(g_gla),
        'w_br_a_f32': w_br_a, 'w_br_b_f32': w_br_b, 'w_out_f32': w_out, 'w_up_f32': w_up,
        'w_down_f32': w_down, 'w_dw': w_dw, 'b_dw': row(b_dw),
        'g_pre_mix': row(g_pre_mix), 'g_post_mix': row(g_post_mix),
        'g_pre_ffn': row(g_pre_ffn), 'g_post_ffn': row(g_post_ffn),
    }


def kernel(x_prompt, x_sample, cache_k_a, cache_v_a, state_gla, state_conv, c_prompt, c_sample, w_ada, b_ada, g_pre_mix, g_post_mix, g_pre_ffn, g_post_ffn, w_in, w_gk2, b_gk, rel_bias, g_gla, w_br_a, w_br_b, w_out, w_up, w_dw, b_dw, w_down):
    depth = w_ada.shape[0]
    assert depth == 1
    bp, bs = x_prompt.shape[0], x_sample.shape[0]
    s_len = x_sample.shape[1]
    cache_rows = cache_k_a.shape[2]
    yp, ys = x_prompt, x_sample
    outs = [[] for _ in range(8)]
    for l in range(depth):
        c_all = jnp.concatenate([c_prompt, c_sample], axis=0)
        pad = (-c_all.shape[0]) % SUBLANES
        mod, w_main, w_gk1 = _prep(jnp.pad(c_all, ((0, pad), (0, 0))), w_ada[l], b_ada[l], w_in[l].T)
        w = _prep_weights(w_main, w_gk1, w_gk2[l], b_gk[l], rel_bias[l], g_gla[l], w_br_a[l],
                          w_br_b[l], w_out[l], w_up[l], w_dw[l], b_dw[l], w_down[l], g_pre_mix[l],
                          g_post_mix[l], g_pre_ffn[l], g_post_ffn[l])
        mod_p = mod[:bp].reshape(bp, 1, 6 * D_MODEL)
        mod_s = mod[bp:bp + bs].reshape(bs, 1, 6 * D_MODEL)
        (yp, kp, vp, gp, cp), w = _layer(yp, mod_p, None, None, None, w, True)
        to_t = lambda c: jnp.transpose(c, (0, 2, 3, 1)).reshape(bs, W_A, cache_rows)
        cache = (to_t(cache_k_a[l]), to_t(cache_v_a[l]))
        s0 = state_gla[l].reshape(bs, H_B // 2, 2 * DK_B, DV_B)
        (ys, kn, vn, gn, cn), _ = _layer(ys, mod_s, cache, s0, state_conv[l], w, False)
        for lst, a in zip(outs, (kp, vp, gp, cp, kn, vn, gn, cn)):
            lst.append(a)
    return (yp, ys) + tuple(jnp.stack(lst) for lst in outs)
```

```python
import functools

import jax
import jax.numpy as jnp
import numpy as np
from jax import lax
from jax.experimental import pallas as pl
from jax.experimental.pallas import tpu as pltpu

D_MODEL = 1024
CHUNK = 64
BAND_CHUNKS = 8
WINDOW_A = BAND_CHUNKS * CHUNK
H_A = 8
HD_A = 64
MAX_REL = 128
H_B = 4
DK_B = 64
DV_B = 128
GK_RANK = 16
GK_NORM = 16.0
GLA_SUB = 16
D_FF = 2816
CONV_W = 3
EPS = 1e-6
NEG_INF = -1e30

W_A = H_A * HD_A
W_BK = H_B * DK_B
W_BV = H_B * DV_B

LANES = 128
SUBLANES = 8
VMEM_LIMIT = 56 * 1024 * 1024

ATTN_QB = 256
ATTN_KB = 3 * ATTN_QB
BIAS_ROW = 1024
FFN_FT = 256
FFN_NF = D_FF // FFN_FT

BF16 = jnp.bfloat16
F32 = jnp.float32


def _params(sem):
    return pltpu.CompilerParams(dimension_semantics=sem, vmem_limit_bytes=VMEM_LIMIT)


def _full_spec(shape):
    nd = len(shape)
    return pl.BlockSpec(shape, lambda *_: (0,) * nd, pipeline_mode=pl.Buffered(1))


def _dot(a, b):
    return jnp.dot(a, b, preferred_element_type=F32)


def _dot_nt(a, b):
    return lax.dot_general(a, b, (((1,), (1,)), ((), ())), preferred_element_type=F32)


def _dot_tn(a, b):
    return lax.dot_general(a, b, (((0,), (0,)), ((), ())), preferred_element_type=F32)


def _sigmoid(x):
    return 1.0 / (1.0 + jnp.exp(-x))


def _rms_scale(x):
    return lax.rsqrt(jnp.mean(x * x, axis=-1, keepdims=True) + EPS)


_GK_LO = 3 * W_A + 2 * W_BK + 2 * W_BV


PREP_STEPS = 4


def _prep_kernel(c_ref, wa_ref, ba_ref, wi_ref, mod_ref, main_ref, gk_ref):
    c = c_ref[...]
    s = (c * _sigmoid(c)).astype(BF16)
    mod_ref[...] = _dot(s, wa_ref[...].astype(BF16)) + ba_ref[...]
    main_ref[:_GK_LO, :] = wi_ref[:_GK_LO, :].astype(BF16)
    main_ref[_GK_LO:, :] = wi_ref[_GK_LO + GK_RANK:, :].astype(BF16)
    gk_ref[:GK_RANK, :] = wi_ref[_GK_LO:_GK_LO + GK_RANK, :].astype(BF16)
    gk_ref[GK_RANK:, :] = jnp.zeros((LANES - GK_RANK, gk_ref.shape[1]), BF16)


def _prep(c_all, w_ada, b_ada, w_in_t):
    rows = c_all.shape[0]
    n = w_ada.shape[1]
    d_in, d = w_in_t.shape
    tn, cols = n // PREP_STEPS, d // PREP_STEPS
    return pl.pallas_call(
        _prep_kernel,
        grid=(PREP_STEPS,),
        in_specs=[pl.BlockSpec((rows, D_MODEL), lambda j: (0, 0)),
                  pl.BlockSpec((D_MODEL, tn), lambda j: (0, j)),
                  pl.BlockSpec((1, tn), lambda j: (0, j)),
                  pl.BlockSpec((d_in, cols), lambda j: (0, j))],
        out_specs=[pl.BlockSpec((rows, tn), lambda j: (0, j)),
                   pl.BlockSpec((d_in - GK_RANK, cols), lambda j: (0, j)),
                   pl.BlockSpec((LANES, cols), lambda j: (0, j))],
        out_shape=[jax.ShapeDtypeStruct((rows, n), F32),
                   jax.ShapeDtypeStruct((d_in - GK_RANK, d), BF16),
                   jax.ShapeDtypeStruct((LANES, d), BF16)],
        compiler_params=_params(("parallel",)),
        name="prep",
    )(c_all, w_ada, b_ada.reshape(1, n), w_in_t)


_IN_GROUPS = (W_A, W_A, W_A, W_BK, W_BK, W_BV, W_BV, D_MODEL, D_MODEL)


def _inproj_kernel(*refs, n_cast, n_kv_t):
    (x_ref, shift_ref, scale_ref, g_ref, wm_ref, wg1_ref, wg2_ref, bgk_ref), refs = refs[:8], refs[8:]
    cast_in, refs = refs[:n_cast], refs[n_cast:]
    (qa_ref, ka_ref, va_ref, qb_ref, kb_ref, vb_ref, gb_ref, ga_ref, gtb_ref, la_ref) = refs[:10]
    kv_t_refs, cast_out = refs[10:10 + n_kv_t], refs[10 + n_kv_t:]
    for w_ref, o_ref in zip(cast_in, cast_out):
        o_ref[...] = w_ref[...].astype(o_ref.dtype)
    nb, tt, _ = x_ref.shape
    x = x_ref[...]
    h = (x * _rms_scale(x)) * (g_ref[...] * (1.0 + scale_ref[...])) + shift_ref[...]
    hb = h.reshape(nb * tt, D_MODEL).astype(BF16)
    if kv_t_refs:
        @pl.when(pl.program_id(1) == pl.num_programs(1) - 1)
        def _():
            rows = kv_t_refs[0].shape[2]
            newest = hb[nb * tt - rows:, :]
            for o_ref, lo in zip(kv_t_refs, (W_A, 2 * W_A)):
                o_ref[0] = _dot_nt(wm_ref[lo:lo + W_A, :], newest)
    outs = (qa_ref, ka_ref, va_ref, qb_ref, kb_ref, vb_ref, gb_ref, ga_ref, gtb_ref)
    lo = 0
    for o_ref, w in zip(outs, _IN_GROUPS):
        z = _dot_nt(hb, wm_ref[lo:lo + w, :])
        o_ref[...] = z.reshape(nb, tt, w).astype(o_ref.dtype)
        lo += w
    gk_low = _dot_nt(hb, wg1_ref[...]).astype(BF16)
    gk = _dot(gk_low, wg2_ref[...]) + bgk_ref[...]
    log_a = (jnp.minimum(gk, 0.0) - jnp.log1p(jnp.exp(-jnp.abs(gk)))) / GK_NORM
    la_ref[...] = log_a.reshape(nb, tt, W_BK)


def _inproj(x, mod, g_pre, wm, wg1, wg2, bgk, nb, tt, keep_rows, cast_ws=()):
    nbt, t, _ = x.shape
    grid = (nbt // nb, t // tt)
    steps = grid[0] * grid[1]
    cast_specs = [pl.BlockSpec((w.shape[0] // steps, w.shape[1]),
                               lambda b, i: (b * grid[1] + i, 0)) for w in cast_ws]
    tok = lambda w: pl.BlockSpec((nb, tt, w), lambda b, i: (b, i, 0))
    mod_spec = lambda col: pl.BlockSpec((nb, 1, D_MODEL), lambda b, i, col=col: (b, 0, col))
    widths = _IN_GROUPS + (W_BK,)
    dtypes = (BF16,) * len(_IN_GROUPS) + (F32,)
    out_specs = [tok(w) for w in widths]
    out_shape = [jax.ShapeDtypeStruct((nbt, t, w), dt) for w, dt in zip(widths, dtypes)]
    if keep_rows:
        assert nb == 1 and keep_rows <= tt
        out_specs += [pl.BlockSpec((1, W_A, keep_rows), lambda b, i: (b, 0, 0))] * 2
        out_shape += [jax.ShapeDtypeStruct((nbt, W_A, keep_rows), F32)] * 2
    return pl.pallas_call(
        functools.partial(_inproj_kernel, n_cast=len(cast_ws), n_kv_t=2 if keep_rows else 0),
        grid=grid,
        in_specs=[tok(D_MODEL), mod_spec(0), mod_spec(1), _full_spec((1, D_MODEL)),
                  _full_spec(wm.shape), _full_spec(wg1.shape), _full_spec(wg2.shape),
                  _full_spec((1, W_BK))] + cast_specs,
        out_specs=out_specs + cast_specs,
        out_shape=out_shape + [jax.ShapeDtypeStruct(w.shape, BF16) for w in cast_ws],
        compiler_params=_params(("arbitrary", "arbitrary")),
        name="inproj",
    )(x, mod, mod, g_pre, wm, wg1, wg2, bgk, *cast_ws)


def _head_masks():
    lane = lax.broadcasted_iota(jnp.int32, (1, LANES), 1)
    first = lane < HD_A
    return first, jnp.logical_not(first)


def _toeplitz_bias(row_ref, h, rows):
    rb = jnp.broadcast_to(row_ref[h], (rows, BIAS_ROW))
    return pltpu.roll(rb, 0, 1, stride=1, stride_axis=0)


ATTN_RB = 32
ATTN_WIN = 640


ATTN_SUB = 4


def _attn_prompt_kernel(*refs):
    q_ref, refs = refs[0], refs[1:]
    k_refs, v_refs = refs[:ATTN_SUB + 2], refs[ATTN_SUB + 2:2 * ATTN_SUB + 4]
    row_ref, o_ref, bias_ref, s_ref, p_ref = refs[2 * ATTN_SUB + 4:]
    i = pl.program_id(1)

    @pl.when((pl.program_id(0) == 0) & (i == 0))
    def _():
        qc = lax.broadcasted_iota(jnp.int32, (ATTN_QB, ATTN_KB), 0) // CHUNK
        col = lax.broadcasted_iota(jnp.int32, (ATTN_QB, ATTN_KB), 1)
        kc = col // CHUNK - BAND_CHUNKS
        valid = (kc <= qc) & (kc >= qc - BAND_CHUNKS)
        for h in range(H_A):
            t = _toeplitz_bias(row_ref, h, ATTN_QB)
            band = jnp.where(valid, t[:, :ATTN_KB], NEG_INF)
            bias_ref[0, h] = jnp.where(col >= 2 * ATTN_QB, band, NEG_INF)
            bias_ref[1, h] = jnp.where(col >= ATTN_QB, band, NEG_INF)
            bias_ref[2, h] = band
        p_ref[...] = jnp.zeros_like(p_ref)

    masks = _head_masks()

    def pair_rows(refs, p):
        sl = slice(p * LANES, (p + 1) * LANES)
        return jnp.concatenate([r[0, :, sl] for r in refs], axis=0)

    units = [(sub, h) for sub in range(ATTN_SUB) for h in range(H_A)]

    def scores(u):
        sub, h = units[u]
        p, hh = divmod(h, 2)
        qp = q_ref[0, sub * ATTN_QB:(sub + 1) * ATTN_QB, p * LANES:(p + 1) * LANES] * BF16(HD_A ** -0.5)
        qm = jnp.where(masks[hh], qp, jnp.zeros_like(qp))
        s_ref[u % 2] = _dot_nt(qm, pair_rows(k_refs[sub:sub + 3], p))

    scores(0)
    o_first = None
    for u, (sub, h) in enumerate(units):
        if u + 1 < len(units):
            scores(u + 1)
        slot = u % 2
        var = jnp.minimum(ATTN_SUB * i + sub, 2)
        sums = []
        for r in range(ATTN_QB // ATTN_RB):
            rows = slice(r * ATTN_RB, (r + 1) * ATTN_RB)
            lo = 0 if r * ATTN_RB < ATTN_QB // 2 else ATTN_KB - ATTN_WIN
            sb = s_ref[slot, rows, lo:lo + ATTN_WIN] + bias_ref[var, h, rows, lo:lo + ATTN_WIN]
            e = jnp.exp(sb - jnp.max(sb, axis=-1, keepdims=True))
            sums.append(jnp.sum(e, axis=-1, keepdims=True))
            p_ref[slot, rows, lo:lo + ATTN_WIN] = e.astype(BF16)
        p, hh = divmod(h, 2)
        o = _dot(p_ref[slot], pair_rows(v_refs[sub:sub + 3], p)) / jnp.concatenate(sums, axis=0)
        if hh == 0:
            o_first = o
        else:
            o_ref[0, sub * ATTN_QB:(sub + 1) * ATTN_QB, p * LANES:(p + 1) * LANES] = jnp.where(
                masks[0], o_first, o).astype(o_ref.dtype)


def _attn_prompt(q, k, v, bias_rows):
    b, t, _ = q.shape
    tok = pl.BlockSpec((1, ATTN_SUB * ATTN_QB, W_A), lambda bb, i: (bb, i, 0))
    blk = lambda j: pl.BlockSpec(
        (1, ATTN_QB, W_A), lambda bb, i, j=j: (bb, jnp.maximum(ATTN_SUB * i + j - 2, 0), 0))
    kv_specs = [blk(j) for j in range(ATTN_SUB + 2)]
    return pl.pallas_call(
        _attn_prompt_kernel,
        grid=(b, t // (ATTN_SUB * ATTN_QB)),
        in_specs=[tok] + kv_specs + kv_specs + [_full_spec(bias_rows.shape)],
        out_specs=tok,
        out_shape=jax.ShapeDtypeStruct((b, t, W_A), BF16),
        scratch_shapes=[pltpu.VMEM((3, H_A, ATTN_QB, ATTN_KB), F32),
                        pltpu.VMEM((2, ATTN_QB, ATTN_KB), F32),
                        pltpu.VMEM((2, ATTN_QB, ATTN_KB), BF16)],
        compiler_params=_params(("arbitrary", "arbitrary")),
        name="attn_prompt",
    )(q, *([k] * (ATTN_SUB + 2)), *([v] * (ATTN_SUB + 2)), bias_rows)


def _attn_sample_kernel(q_ref, kn_ref, vn_ref, kc_ref, vc_ref, row_ref, o_ref, bc_ref, bn_ref):
    nb, s_len, _ = q_ref.shape
    w = kc_ref.shape[2]

    @pl.when(pl.program_id(0) == 0)
    def _():
        for h in range(H_A):
            t = _toeplitz_bias(row_ref, h, s_len)
            bc_ref[h * s_len:(h + 1) * s_len, :] = t[:, :w]
            bn_ref[h * s_len:(h + 1) * s_len, :] = t[:, w:w + s_len]

    lane_head = lax.broadcasted_iota(jnp.int32, (1, W_A), 1) // HD_A
    row_head = lax.broadcasted_iota(jnp.int32, (H_A * s_len, 1), 0) // s_len
    own_head = row_head == lane_head
    for b in range(nb):
        q = q_ref[b] * BF16(HD_A ** -0.5)
        qs = jnp.concatenate([q] * H_A, axis=0)
        qs = jnp.where(own_head, qs, jnp.zeros_like(qs))
        sc = _dot(qs, kc_ref[b].astype(BF16)) + bc_ref[...]
        sn = _dot_nt(qs, kn_ref[b]) + bn_ref[...]
        m = jnp.maximum(jnp.max(sc, axis=-1, keepdims=True), jnp.max(sn, axis=-1, keepdims=True))
        ec = jnp.exp(sc - m)
        en = jnp.exp(sn - m)
        l = jnp.sum(ec, axis=-1, keepdims=True) + jnp.sum(en, axis=-1, keepdims=True)
        full = (_dot_nt(ec.astype(BF16), vc_ref[b].astype(BF16))
                + _dot(en.astype(BF16), vn_ref[b])) / l
        o = full[:s_len]
        for h in range(1, H_A):
            o = jnp.where(lane_head == h, full[h * s_len:(h + 1) * s_len], o)
        o_ref[b] = o.astype(o_ref.dtype)


def _attn_sample(q, kn, vn, kc, vc, bias_rows, nb):
    b, s, _ = q.shape
    w = kc.shape[2]
    assert w == WINDOW_A
    new = pl.BlockSpec((nb, s, W_A), lambda i: (i, 0, 0))
    cache = pl.BlockSpec((nb, W_A, w), lambda i: (i, 0, 0))
    return pl.pallas_call(
        _attn_sample_kernel,
        grid=(b // nb,),
        in_specs=[new, new, new, cache, cache, _full_spec(bias_rows.shape)],
        out_specs=new,
        out_shape=jax.ShapeDtypeStruct((b, s, W_A), BF16),
        scratch_shapes=[pltpu.VMEM((H_A * s, w), F32), pltpu.VMEM((H_A * s, s), F32)],
        compiler_params=_params(("arbitrary",)),
        name="attn_sample",
    )(q, kn, vn, kc, vc, bias_rows)


def _gla_kernel(*refs, has_init, c):
    if has_init:
        q_ref, k_ref, v_ref, g_ref, la_ref, gg_ref, s0_ref, y_ref, so_ref, st_ref = refs
    else:
        q_ref, k_ref, v_ref, g_ref, la_ref, gg_ref, y_ref, so_ref, st_ref = refs
    nb, blk, _ = q_ref.shape
    nsub = c // GLA_SUB
    npair = H_B // 2
    j = pl.program_id(1)

    @pl.when(j == 0)
    def _():
        if has_init:
            st_ref[...] = s0_ref[...]
        else:
            st_ref[...] = jnp.zeros_like(st_ref)

    r2 = lax.broadcasted_iota(jnp.int32, (2 * c, c), 0)
    s2 = lax.broadcasted_iota(jnp.int32, (2 * c, c), 1)
    t2 = r2 & (c - 1)
    same_sub = (s2 // GLA_SUB) == (t2 // GLA_SUB)
    sum_mat = jnp.where((s2 <= t2) & ((r2 < c) | same_sub), 1.0, 0.0).astype(BF16)

    row = lax.broadcasted_iota(jnp.int32, (c, W_BK), 0)
    row2 = lax.broadcasted_iota(jnp.int32, (2 * c, LANES), 0)
    lane2 = lax.broadcasted_iota(jnp.int32, (2 * c, LANES), 1)
    own = (lane2 < DK_B) == (row2 < c)
    sub_of_row2 = (row2 & (c - 1)) // GLA_SUB
    tril2 = lane2 <= (row2 & (c - 1))
    eye = (lax.broadcasted_iota(jnp.int32, (LANES, LANES), 0)
           == lax.broadcasted_iota(jnp.int32, (LANES, LANES), 1))
    scale = DK_B ** -0.5
    zeros_k = jnp.zeros((LANES - c, nsub * LANES), BF16)
    zeros_v = jnp.zeros((LANES - c, DV_B), BF16)

    for ci in range(blk // c):
        rows = slice(ci * c, (ci + 1) * c)
        prep = []
        for b in range(nb):
            la = la_ref[b, rows]
            la_hi = la.astype(BF16)
            rest = la - la_hi.astype(F32)
            la_mid = rest.astype(BF16)
            la_lo = (rest - la_mid.astype(F32)).astype(BF16)
            sums = _dot(sum_mat, la_hi) + _dot(sum_mat, la_mid) + _dot(sum_mat, la_lo)
            cum, cum_sub = sums[:c], sums[c:]
            cum_end = cum[c - 1:c, :]
            qf = q_ref[b, rows].astype(F32)
            kf = k_ref[b, rows].astype(F32)
            q_sub = qf * jnp.exp(cum_sub) * scale
            q_in = (qf * jnp.exp(cum) * scale).astype(BF16)
            k_end = (kf * jnp.exp(cum_end - cum)).astype(BF16)
            k_sub = []
            for i in range(nsub):
                ref_i = cum[i * GLA_SUB - 1:i * GLA_SUB, :] if i else jnp.zeros((1, W_BK), F32)
                k_i = jnp.where(row < (i + 1) * GLA_SUB, kf * jnp.exp(ref_i - cum), 0.0)
                k_sub.append(k_i.astype(BF16))
            prep.append((q_sub, q_in, k_end, k_sub, cum_end))

        att = {}
        for b in range(nb):
            q_sub, _, _, k_sub, _ = prep[b]
            for p in range(npair):
                sl = slice(p * LANES, (p + 1) * LANES)
                k_stack = jnp.concatenate(
                    [jnp.concatenate([k_i[:, sl] for k_i in k_sub], axis=1), zeros_k], axis=0)
                q2 = jnp.where(own, jnp.concatenate([q_sub[:, sl], q_sub[:, sl]], axis=0), 0.0)
                q_stack = jnp.concatenate(
                    [jnp.where(sub_of_row2 == i, q2, 0.0) for i in range(nsub)], axis=1).astype(BF16)
                a = _dot_nt(q_stack, k_stack)
                att[b, p] = jnp.where(tril2, a, 0.0).astype(BF16)

        for b in range(nb):
            q_in = prep[b][1]
            for p in range(npair):
                sl = slice(p * LANES, (p + 1) * LANES)
                st_b = st_ref[b, p].astype(BF16)
                q_in2 = jnp.where(own, jnp.concatenate([q_in[:, sl], q_in[:, sl]], axis=0),
                                  jnp.zeros((), BF16))
                for hh in range(2):
                    h = 2 * p + hh
                    hs = slice(h * DV_B, (h + 1) * DV_B)
                    hrows = slice(hh * c, (hh + 1) * c)
                    lhs = jnp.concatenate([att[b, p][hrows], q_in2[hrows]], axis=1)
                    rhs = jnp.concatenate([v_ref[b, rows, hs], zeros_v, st_b], axis=0)
                    o = _dot(lhs, rhs)
                    gate = g_ref[b, rows, hs].astype(F32)
                    y = (o * _rms_scale(o)) * gg_ref[...] * (gate * _sigmoid(gate))
                    y_ref[b, rows, hs] = y.astype(y_ref.dtype)

        for b in range(nb):
            k_end, cum_end = prep[b][2], prep[b][4]
            for p in range(npair):
                sl = slice(p * LANES, (p + 1) * LANES)
                k2 = jnp.where(own, jnp.concatenate([k_end[:, sl], k_end[:, sl]], axis=0),
                               jnp.zeros((), BF16))
                v2 = jnp.concatenate([v_ref[b, rows, 2 * p * DV_B:(2 * p + 1) * DV_B],
                                      v_ref[b, rows, (2 * p + 1) * DV_B:(2 * p + 2) * DV_B]], axis=0)
                upd = _dot_tn(k2, v2)
                dec = jnp.exp(jnp.sum(jnp.where(eye, cum_end[:, sl], 0.0), axis=1, keepdims=True))
                st_ref[b, p] = st_ref[b, p] * dec + upd

    @pl.when(j == pl.num_programs(1) - 1)
    def _():
        so_ref[...] = st_ref[...]


def _gla(q, k, v, g, la, g_gla, s0, nb, c, chunks_per_step):
    nbt, t, _ = q.shape
    blk = c * chunks_per_step
    tok = lambda w: pl.BlockSpec((nb, blk, w), lambda b, j: (b, j, 0))
    st_spec = pl.BlockSpec((nb, H_B // 2, 2 * DK_B, DV_B), lambda b, j: (b, 0, 0, 0))
    in_specs = [tok(W_BK), tok(W_BK), tok(W_BV), tok(W_BV), tok(W_BK), _full_spec((1, DV_B))]
    args = [q, k, v, g, la, g_gla]
    if s0 is not None:
        in_specs.append(st_spec)
        args.append(s0)
    return pl.pallas_call(
        functools.partial(_gla_kernel, has_init=s0 is not None, c=c),
        grid=(nbt // nb, t // blk),
        in_specs=in_specs,
        out_specs=[tok(W_BV), st_spec],
        out_shape=[jax.ShapeDtypeStruct((nbt, t, W_BV), BF16),
                   jax.ShapeDtypeStruct((nbt, H_B // 2, 2 * DK_B, DV_B), F32)],
        scratch_shapes=[pltpu.VMEM((nb, H_B // 2, DV_B, 2 * DK_B), F32)],
        compiler_params=_params(("parallel", "arbitrary")),
        name="gla",
    )(*args)


def _mixout_kernel(x_ref, ya_ref, yb_ref, ga_ref, gb_ref, gm_ref, gp_ref, wa_ref, wb_ref, wo_ref,
                   o_ref):
    nb, tt, _ = x_ref.shape
    if nb == 1:
        halves = [(slice(None), slice(s * tt // 2, (s + 1) * tt // 2)) for s in range(2)]
        nbh, tth = nb, tt // 2
    else:
        halves = [(slice(s * nb // 2, (s + 1) * nb // 2), slice(None)) for s in range(2)]
        nbh, tth = nb // 2, tt
    m = nbh * tth
    merged = []
    for bs, ts in halves:
        a = _dot(ya_ref[bs, ts, :].reshape(m, W_A), wa_ref[...])
        b = _dot(yb_ref[bs, ts, :].reshape(m, W_BV), wb_ref[...])
        ga = _sigmoid(ga_ref[bs, ts, :].reshape(m, D_MODEL).astype(F32))
        gb = _sigmoid(gb_ref[bs, ts, :].reshape(m, D_MODEL).astype(F32))
        merged.append((ga * a + gb * b).astype(BF16))
    for (bs, ts), mg in zip(halves, merged):
        mo = _dot(mg, wo_ref[...])
        n = ((mo * _rms_scale(mo)) * gp_ref[...]).reshape(nbh, tth, D_MODEL)
        o_ref[bs, ts, :] = x_ref[bs, ts, :] + gm_ref[bs] * n


def _mixout(x, ya, yb, ga, gb, mod, g_post, wa, wb, wo, nb, tt):
    nbt, t, _ = x.shape
    tok = lambda w: pl.BlockSpec((nb, tt, w), lambda b, i: (b, i, 0))
    return pl.pallas_call(
        _mixout_kernel,
        grid=(nbt // nb, t // tt),
        in_specs=[tok(D_MODEL), tok(W_A), tok(W_BV), tok(D_MODEL), tok(D_MODEL),
                  pl.BlockSpec((nb, 1, D_MODEL), lambda b, i: (b, 0, 2)),
                  _full_spec((1, D_MODEL)), _full_spec(wa.shape), _full_spec(wb.shape),
                  _full_spec(wo.shape)],
        out_specs=tok(D_MODEL),
        out_shape=jax.ShapeDtypeStruct(x.shape, F32),
        compiler_params=_params(("parallel", "parallel")),
        name="mixout",
    )(x, ya, yb, ga, gb, mod, g_post, wa, wb, wo)


def _gelu_tanh(x):
    c = float(np.sqrt(2.0 / np.pi))
    half = 0.5 * x
    return half + half * jnp.tanh(x * (c + (0.044715 * c) * (x * x)))


def _ffn_kernel(*refs, has_state):
    if has_state:
        (x_ref, shift_ref, scale_ref, gate_ref, gpre_ref, gpost_ref, wu_ref, wd_ref,
         wdw_ref, bdw_ref, prev_ref, o_ref, tail_ref, h_ref, act_ref) = refs
    else:
        (x_ref, shift_ref, scale_ref, gate_ref, gpre_ref, gpost_ref, wu_ref, wd_ref,
         wdw_ref, bdw_ref, o_ref, tail_ref, h_ref, act_ref, prev_ref) = refs

        @pl.when(pl.program_id(1) == 0)
        def _():
            prev_ref[...] = jnp.zeros_like(prev_ref)

    nb, tt, _ = x_ref.shape
    m = nb * tt
    x = x_ref[...]
    h = (x * _rms_scale(x)) * (gpre_ref[...] * (1.0 + scale_ref[...])) + shift_ref[...]
    h_ref[...] = h.reshape(m, D_MODEL).astype(BF16)
    ridx = lax.broadcasted_iota(jnp.int32, (nb, SUBLANES, FFN_FT), 1)

    def conv(u, lanes):
        u3 = u.reshape(nb, tt, FFN_FT)
        r1 = pltpu.roll(u, 1, 0).reshape(nb, tt, FFN_FT)
        r2 = pltpu.roll(u, 2, 0).reshape(nb, tt, FFN_FT)

        def taps(u_m2, u_m1, u_0):
            y = bdw_ref[:, lanes] + wdw_ref[0:1, lanes] * u_m2
            y = y + wdw_ref[1:2, lanes] * u_m1
            return y + wdw_ref[2:3, lanes] * u_0

        p2, p1 = prev_ref[:, 0:1, lanes], prev_ref[:, 1:2, lanes]
        h_m1 = jnp.where(ridx == 0, p1, r1[:, :SUBLANES])
        h_m2 = jnp.where(ridx == 0, p2, jnp.where(ridx == 1, p1, r2[:, :SUBLANES]))
        y = jnp.concatenate([taps(h_m2, h_m1, u3[:, :SUBLANES]),
                             taps(r2[:, SUBLANES:], r1[:, SUBLANES:], u3[:, SUBLANES:])], axis=1)
        tail = u3[:, tt - (CONV_W - 1):, :]
        tail_ref[:, :, lanes] = tail
        if not has_state:
            prev_ref[:, :, lanes] = tail
        return y.reshape(m, FFN_FT)

    def up(f):
        hb = h_ref[...]
        lo = f * FFN_FT
        return (_dot(hb, wu_ref[:, lo:lo + FFN_FT]),
                _dot(hb, wu_ref[:, D_FF + lo:D_FF + lo + FFN_FT]))

    ua, ug = up(0)
    for f in range(FFN_NF):
        if f + 1 < FFN_NF:
            ua_next, ug_next = up(f + 1)
        cols = slice(f * FFN_FT, (f + 1) * FFN_FT)
        ya = conv(ua, cols)
        yg = conv(ug, slice(D_FF + f * FFN_FT, D_FF + (f + 1) * FFN_FT))
        act_ref[:, cols] = (_gelu_tanh(ya) * yg).astype(BF16)
        if f + 1 < FFN_NF:
            ua, ug = ua_next, ug_next
    yf = _dot(act_ref[...], wd_ref[...])
    n = ((yf * _rms_scale(yf)) * gpost_ref[...]).reshape(nb, tt, D_MODEL)
    o_ref[...] = x_ref[...] + gate_ref[...] * n


def _ffn(x, mod, g_pre, g_post, wu, wd, w_dw, b_dw, state, nb, tt):
    nbt, t, _ = x.shape
    m = nb * tt
    tok = pl.BlockSpec((nb, tt, D_MODEL), lambda b, i: (b, i, 0))
    mod_spec = lambda col: pl.BlockSpec((nb, 1, D_MODEL), lambda b, i, col=col: (b, 0, col))
    tail_spec = pl.BlockSpec((nb, CONV_W - 1, 2 * D_FF), lambda b, i: (b, 0, 0))
    in_specs = [tok, mod_spec(3), mod_spec(4), mod_spec(5), _full_spec((1, D_MODEL)),
                _full_spec((1, D_MODEL)), _full_spec(wu.shape),
                _full_spec(wd.shape), _full_spec(w_dw.shape), _full_spec(b_dw.shape)]
    args = [x, mod, mod, mod, g_pre, g_post, wu, wd, w_dw, b_dw]
    scratch = [pltpu.VMEM((m, D_MODEL), BF16), pltpu.VMEM((m, D_FF), BF16)]
    if state is not None:
        in_specs.append(tail_spec)
        args.append(state)
    else:
        scratch.append(pltpu.VMEM((nb, CONV_W - 1, 2 * D_FF), F32))
    return pl.pallas_call(
        functools.partial(_ffn_kernel, has_state=state is not None),
        grid=(nbt // nb, t // tt),
        in_specs=in_specs,
        out_specs=[tok, tail_spec],
        out_shape=[jax.ShapeDtypeStruct(x.shape, F32),
                   jax.ShapeDtypeStruct((nbt, CONV_W - 1, 2 * D_FF), F32)],
        scratch_shapes=scratch,
        compiler_params=_params(("parallel", "arbitrary")),
        name="ffn",
    )(*args)


def _bias_rows(rel_bias):
    assert BIAS_ROW >= ATTN_KB + ATTN_QB - 1 and WINDOW_A == ATTN_KB - ATTN_QB
    far_pos = jnp.broadcast_to(rel_bias[:, -1:], (H_A, BIAS_ROW))
    far_neg = jnp.broadcast_to(rel_bias[:, :1], (H_A, BIAS_ROW))
    n_mid = 2 * MAX_REL + 1
    n_lo = ATTN_KB + 1 - (WINDOW_A - MAX_REL) - n_mid
    rows = jnp.concatenate([far_pos[:, :WINDOW_A - MAX_REL], rel_bias[:, ::-1], far_neg[:, :n_lo],
                            far_pos[:, :BIAS_ROW - ATTN_KB - 1]], axis=1)
    return rows.astype(F32).reshape(H_A, 1, BIAS_ROW)


_LATE_WEIGHTS = ('w_br_a', 'w_br_b', 'w_out', 'w_up', 'w_down')


def _layer(x, mod, cache, s_gla, s_conv, w, first_chunk):
    nbt, t, _ = x.shape
    if first_chunk:
        nb, tt = 1, 1024
        nb_f, tt_f = 1, 1024
    else:
        nb, tt = 1024 // t, t
        nb_f, tt_f = 1024 // t, t
    keep_rows = min(WINDOW_A, t) if first_chunk else 0
    cast_ws = () if _LATE_WEIGHTS[0] in w else tuple(w[k + '_f32'] for k in _LATE_WEIGHTS)
    qa, ka, va, qb, kb, vb, gb, gate_a, gate_b, log_a, *extra = _inproj(
        x, mod, w['g_pre_mix'], w['w_main'], w['w_gk1'], w['w_gk2'], w['b_gk'], nb, tt, keep_rows,
        cast_ws)
    kv_t = extra[:2] if keep_rows else []
    if cast_ws:
        w = {**w, **dict(zip(_LATE_WEIGHTS, extra[len(kv_t):]))}
    if first_chunk:
        ya = _attn_prompt(qa, ka, va, w['bias_rows'])
        k_keep, v_keep = (jnp.transpose(a.reshape(nbt, H_A, HD_A, keep_rows), (0, 3, 1, 2))
                          for a in kv_t)
        yb, s_new = _gla(qb, kb, vb, gb, log_a, w['g_gla'], None, nbt, CHUNK, 4)
    else:
        k_cache, v_cache = cache
        ya = _attn_sample(qa, ka, va, k_cache, v_cache, w['bias_rows'], 4)
        k_keep, v_keep = (a.astype(F32).reshape(nbt, t, H_A, HD_A) for a in (ka, va))
        yb, s_new = _gla(qb, kb, vb, gb, log_a, w['g_gla'], s_gla, 4, t, 1)
    x1 = _mixout(x, ya, yb, gate_a, gate_b, mod, w['g_post_mix'], w['w_br_a'], w['w_br_b'],
                 w['w_out'], nb, tt)
    y, tail = _ffn(x1, mod, w['g_pre_ffn'], w['g_post_ffn'], w['w_up'], w['w_down'],
                   w['w_dw'], w['b_dw'], None if first_chunk else s_conv, nb_f, tt_f)
    return (y, k_keep, v_keep, s_new.reshape(nbt, H_B, DK_B, DV_B), tail), w


def _prep_weights(w_main, w_gk1, w_gk2, b_gk, rel_bias, g_gla, w_br_a, w_br_b, w_out, w_up, w_dw,
                  b_dw, w_down, g_pre_mix, g_post_mix, g_pre_ffn, g_post_ffn):
    w_gk2p = jnp.pad(w_gk2, ((0, LANES - GK_RANK), (0, 0))).astype(BF16)
    row = lambda a: a.reshape(1, -1)
    return {
        'w_main': w_main, 'w_gk1': w_gk1, 'w_gk2': w_gk2p, 'b_gk': row(b_gk),
        'bias_rows': _bias_rows(rel_bias), 'g_gla': row(g_gla),
        'w_br_a_f32': w_br_a, 'w_br_b_f32': w_br_b, 'w_out_f32': w_out, 'w_up_f32': w_up,
        'w_down_f32': w_down, 'w_dw': w_dw, 'b_dw': row(b_dw),
        'g_pre_mix': row(g_pre_mix), 'g_post_mix': row(g_post_mix),
        'g_pre_ffn': row(g_pre_ffn), 'g_post_ffn': row(g_post_ffn),
    }


def kernel(x_prompt, x_sample, cache_k_a, cache_v_a, state_gla, state_conv, c_prompt, c_sample, w_ada, b_ada, g_pre_mix, g_post_mix, g_pre_ffn, g_post_ffn, w_in, w_gk2, b_gk, rel_bias, g_gla, w_br_a, w_br_b, w_out, w_up, w_dw, b_dw, w_down):
    depth = w_ada.shape[0]
    assert depth == 1
    bp, bs = x_prompt.shape[0], x_sample.shape[0]
    cache_rows = cache_k_a.shape[2]
    yp, ys = x_prompt, x_sample
    outs = [[] for _ in range(8)]
    for l in range(depth):
        c_all = jnp.concatenate([c_prompt, c_sample], axis=0)
        pad = (-c_all.shape[0]) % SUBLANES
        mod, w_main, w_gk1 = _prep(jnp.pad(c_all, ((0, pad), (0, 0))), w_ada[l], b_ada[l], w_in[l].T)
        w = _prep_weights(w_main, w_gk1, w_gk2[l], b_gk[l], rel_bias[l], g_gla[l], w_br_a[l],
                          w_br_b[l], w_out[l], w_up[l], w_dw[l], b_dw[l], w_down[l], g_pre_mix[l],
                          g_post_mix[l], g_pre_ffn[l], g_post_ffn[l])
        mod_p = mod[:bp].reshape(bp, 1, 6 * D_MODEL)
        mod_s = mod[bp:bp + bs].reshape(bs, 1, 6 * D_MODEL)
        (yp, kp, vp, gp, cp), w = _layer(yp, mod_p, None, None, None, w, True)
        to_t = lambda c: jnp.transpose(c, (0, 2, 3, 1)).reshape(bs, W_A, cache_rows)
        cache = (to_t(cache_k_a[l]), to_t(cache_v_a[l]))
        s0 = state_gla[l].reshape(bs, H_B // 2, 2 * DK_B, DV_B)
        (ys, kn, vn, gn, cn), _ = _layer(ys, mod_s, cache, s0, state_conv[l], w, False)
        for lst, a in zip(outs, (kp, vp, gp, cp, kn, vn, gn, cn)):
            lst.append(a)
    return (yp, ys) + tuple(jnp.stack(lst) for lst in outs)
```

```python
import functools

import jax
import jax.numpy as jnp
import numpy as np
from jax import lax
from jax.experimental import pallas as pl
from jax.experimental.pallas import tpu as pltpu

D_MODEL = 1024
CHUNK = 64
BAND_CHUNKS = 8
WINDOW_A = BAND_CHUNKS * CHUNK
H_A = 8
HD_A = 64
MAX_REL = 128
H_B = 4
DK_B = 64
DV_B = 128
GK_RANK = 16
GK_NORM = 16.0
GLA_SUB = 16
D_FF = 2816
CONV_W = 3
EPS = 1e-6
NEG_INF = -1e30

W_A = H_A * HD_A
W_BK = H_B * DK_B
W_BV = H_B * DV_B

LANES = 128
SUBLANES = 8
VMEM_LIMIT = 56 * 1024 * 1024

ATTN_QB = 256
ATTN_KB = 3 * ATTN_QB
BIAS_ROW = 1024
FFN_FT = 256
FFN_NF = D_FF // FFN_FT

BF16 = jnp.bfloat16
F32 = jnp.float32


def _params(sem):
    return pltpu.CompilerParams(dimension_semantics=sem, vmem_limit_bytes=VMEM_LIMIT)


def _full_spec(shape):
    nd = len(shape)
    return pl.BlockSpec(shape, lambda *_: (0,) * nd, pipeline_mode=pl.Buffered(1))


def _dot(a, b):
    return jnp.dot(a, b, preferred_element_type=F32)


def _dot_nt(a, b):
    return lax.dot_general(a, b, (((1,), (1,)), ((), ())), preferred_element_type=F32)


def _dot_tn(a, b):
    return lax.dot_general(a, b, (((0,), (0,)), ((), ())), preferred_element_type=F32)


def _sigmoid(x):
    return 1.0 / (1.0 + jnp.exp(-x))


def _rms_scale(x):
    return lax.rsqrt(jnp.mean(x * x, axis=-1, keepdims=True) + EPS)


_GK_LO = 3 * W_A + 2 * W_BK + 2 * W_BV


PREP_STEPS = 8


def _prep_kernel(c_ref, wa_ref, ba_ref, wi_ref, mod_ref, main_ref, gk_ref):
    c = c_ref[...]
    s = (c * _sigmoid(c)).astype(BF16)
    mod_ref[...] = _dot(s, wa_ref[...].astype(BF16)) + ba_ref[...]
    main_ref[:_GK_LO, :] = wi_ref[:_GK_LO, :].astype(BF16)
    main_ref[_GK_LO:, :] = wi_ref[_GK_LO + GK_RANK:, :].astype(BF16)
    gk_ref[:GK_RANK, :] = wi_ref[_GK_LO:_GK_LO + GK_RANK, :].astype(BF16)
    gk_ref[GK_RANK:, :] = jnp.zeros((LANES - GK_RANK, gk_ref.shape[1]), BF16)


def _prep(c_all, w_ada, b_ada, w_in_t):
    rows = c_all.shape[0]
    n = w_ada.shape[1]
    d_in, d = w_in_t.shape
    tn, cols = n // PREP_STEPS, d // PREP_STEPS
    return pl.pallas_call(
        _prep_kernel,
        grid=(PREP_STEPS,),
        in_specs=[pl.BlockSpec((rows, D_MODEL), lambda j: (0, 0)),
                  pl.BlockSpec((D_MODEL, tn), lambda j: (0, j)),
                  pl.BlockSpec((1, tn), lambda j: (0, j)),
                  pl.BlockSpec((d_in, cols), lambda j: (0, j))],
        out_specs=[pl.BlockSpec((rows, tn), lambda j: (0, j)),
                   pl.BlockSpec((d_in - GK_RANK, cols), lambda j: (0, j)),
                   pl.BlockSpec((LANES, cols), lambda j: (0, j))],
        out_shape=[jax.ShapeDtypeStruct((rows, n), F32),
                   jax.ShapeDtypeStruct((d_in - GK_RANK, d), BF16),
                   jax.ShapeDtypeStruct((LANES, d), BF16)],
        compiler_params=_params(("parallel",)),
        name="prep",
    )(c_all, w_ada, b_ada.reshape(1, n), w_in_t)


_IN_GROUPS = (W_A, W_A, W_A, W_BK, W_BK, W_BV, W_BV, D_MODEL, D_MODEL)


def _inproj_kernel(*refs, n_cast, n_kv_t):
    (x_ref, shift_ref, scale_ref, g_ref, wm_ref, wg1_ref, wg2_ref, bgk_ref), refs = refs[:8], refs[8:]
    cast_in, refs = refs[:n_cast], refs[n_cast:]
    (qa_ref, ka_ref, va_ref, qb_ref, kb_ref, vb_ref, gb_ref, ga_ref, gtb_ref, la_ref) = refs[:10]
    kv_t_refs, cast_out = refs[10:10 + n_kv_t], refs[10 + n_kv_t:]
    for w_ref, o_ref in zip(cast_in, cast_out):
        o_ref[...] = w_ref[...].astype(o_ref.dtype)
    nb, tt, _ = x_ref.shape
    x = x_ref[...]
    h = (x * _rms_scale(x)) * (g_ref[...] * (1.0 + scale_ref[...])) + shift_ref[...]
    hb = h.reshape(nb * tt, D_MODEL).astype(BF16)
    if kv_t_refs:
        @pl.when(pl.program_id(1) == pl.num_programs(1) - 1)
        def _():
            rows = kv_t_refs[0].shape[2]
            newest = hb[nb * tt - rows:, :]
            for o_ref, lo in zip(kv_t_refs, (W_A, 2 * W_A)):
                o_ref[0] = _dot_nt(wm_ref[lo:lo + W_A, :], newest)
    outs = (qa_ref, ka_ref, va_ref, qb_ref, kb_ref, vb_ref, gb_ref, ga_ref, gtb_ref)
    lo = 0
    for o_ref, w in zip(outs, _IN_GROUPS):
        z = _dot_nt(hb, wm_ref[lo:lo + w, :])
        o_ref[...] = z.reshape(nb, tt, w).astype(o_ref.dtype)
        lo += w
    gk_low = _dot_nt(hb, wg1_ref[...]).astype(BF16)
    gk = _dot(gk_low, wg2_ref[...]) + bgk_ref[...]
    log_a = (jnp.minimum(gk, 0.0) - jnp.log1p(jnp.exp(-jnp.abs(gk)))) / GK_NORM
    la_ref[...] = log_a.reshape(nb, tt, W_BK)


def _inproj(x, mod, g_pre, wm, wg1, wg2, bgk, nb, tt, keep_rows, cast_ws=()):
    nbt, t, _ = x.shape
    grid = (nbt // nb, t // tt)
    steps = grid[0] * grid[1]
    cast_specs = [pl.BlockSpec((w.shape[0] // steps, w.shape[1]),
                               lambda b, i: (b * grid[1] + i, 0)) for w in cast_ws]
    tok = lambda w: pl.BlockSpec((nb, tt, w), lambda b, i: (b, i, 0))
    mod_spec = lambda col: pl.BlockSpec((nb, 1, D_MODEL), lambda b, i, col=col: (b, 0, col))
    widths = _IN_GROUPS + (W_BK,)
    dtypes = (BF16,) * len(_IN_GROUPS) + (F32,)
    out_specs = [tok(w) for w in widths]
    out_shape = [jax.ShapeDtypeStruct((nbt, t, w), dt) for w, dt in zip(widths, dtypes)]
    if keep_rows:
        assert nb == 1 and keep_rows <= tt
        out_specs += [pl.BlockSpec((1, W_A, keep_rows), lambda b, i: (b, 0, 0))] * 2
        out_shape += [jax.ShapeDtypeStruct((nbt, W_A, keep_rows), F32)] * 2
    return pl.pallas_call(
        functools.partial(_inproj_kernel, n_cast=len(cast_ws), n_kv_t=2 if keep_rows else 0),
        grid=grid,
        in_specs=[tok(D_MODEL), mod_spec(0), mod_spec(1), _full_spec((1, D_MODEL)),
                  _full_spec(wm.shape), _full_spec(wg1.shape), _full_spec(wg2.shape),
                  _full_spec((1, W_BK))] + cast_specs,
        out_specs=out_specs + cast_specs,
        out_shape=out_shape + [jax.ShapeDtypeStruct(w.shape, BF16) for w in cast_ws],
        compiler_params=_params(("arbitrary", "arbitrary")),
        name="inproj",
    )(x, mod, mod, g_pre, wm, wg1, wg2, bgk, *cast_ws)


def _head_masks():
    lane = lax.broadcasted_iota(jnp.int32, (1, LANES), 1)
    first = lane < HD_A
    return first, jnp.logical_not(first)


def _toeplitz_bias(row_ref, h, rows):
    rb = jnp.broadcast_to(row_ref[h], (rows, BIAS_ROW))
    return pltpu.roll(rb, 0, 1, stride=1, stride_axis=0)


ATTN_RB = 32
ATTN_WIN = 640


ATTN_SUB = 4


def _attn_prompt_kernel(*refs):
    q_ref, refs = refs[0], refs[1:]
    k_refs, v_refs = refs[:ATTN_SUB + 2], refs[ATTN_SUB + 2:2 * ATTN_SUB + 4]
    row_ref, o_ref, bias_ref, s_ref, p_ref = refs[2 * ATTN_SUB + 4:]
    i = pl.program_id(1)

    @pl.when((pl.program_id(0) == 0) & (i == 0))
    def _():
        qc = lax.broadcasted_iota(jnp.int32, (ATTN_QB, ATTN_KB), 0) // CHUNK
        col = lax.broadcasted_iota(jnp.int32, (ATTN_QB, ATTN_KB), 1)
        kc = col // CHUNK - BAND_CHUNKS
        valid = (kc <= qc) & (kc >= qc - BAND_CHUNKS)
        for h in range(H_A):
            t = _toeplitz_bias(row_ref, h, ATTN_QB)
            band = jnp.where(valid, t[:, :ATTN_KB], NEG_INF)
            bias_ref[0, h] = jnp.where(col >= 2 * ATTN_QB, band, NEG_INF)
            bias_ref[1, h] = jnp.where(col >= ATTN_QB, band, NEG_INF)
            bias_ref[2, h] = band
        p_ref[...] = jnp.zeros_like(p_ref)

    masks = _head_masks()

    def pair_rows(refs, p):
        sl = slice(p * LANES, (p + 1) * LANES)
        return jnp.concatenate([r[0, :, sl] for r in refs], axis=0)

    units = [(sub, h) for sub in range(ATTN_SUB) for h in range(H_A)]

    def scores(u):
        sub, h = units[u]
        p, hh = divmod(h, 2)
        qp = q_ref[0, sub * ATTN_QB:(sub + 1) * ATTN_QB, p * LANES:(p + 1) * LANES] * BF16(HD_A ** -0.5)
        qm = jnp.where(masks[hh], qp, jnp.zeros_like(qp))
        s_ref[u % 2] = _dot_nt(qm, pair_rows(k_refs[sub:sub + 3], p))

    scores(0)
    o_first = None
    for u, (sub, h) in enumerate(units):
        if u + 1 < len(units):
            scores(u + 1)
        slot = u % 2
        var = jnp.minimum(ATTN_SUB * i + sub, 2)
        sums = []
        for r in range(ATTN_QB // ATTN_RB):
            rows = slice(r * ATTN_RB, (r + 1) * ATTN_RB)
            lo = 0 if r * ATTN_RB < ATTN_QB // 2 else ATTN_KB - ATTN_WIN
            sb = s_ref[slot, rows, lo:lo + ATTN_WIN] + bias_ref[var, h, rows, lo:lo + ATTN_WIN]
            e = jnp.exp(sb - jnp.max(sb, axis=-1, keepdims=True))
            sums.append(jnp.sum(e, axis=-1, keepdims=True))
            p_ref[slot, rows, lo:lo + ATTN_WIN] = e.astype(BF16)
        p, hh = divmod(h, 2)
        o = _dot(p_ref[slot], pair_rows(v_refs[sub:sub + 3], p)) / jnp.concatenate(sums, axis=0)
        if hh == 0:
            o_first = o
        else:
            o_ref[0, sub * ATTN_QB:(sub + 1) * ATTN_QB, p * LANES:(p + 1) * LANES] = jnp.where(
                masks[0], o_first, o).astype(o_ref.dtype)


def _attn_prompt(q, k, v, bias_rows):
    b, t, _ = q.shape
    tok = pl.BlockSpec((1, ATTN_SUB * ATTN_QB, W_A), lambda bb, i: (bb, i, 0))
    blk = lambda j: pl.BlockSpec(
        (1, ATTN_QB, W_A), lambda bb, i, j=j: (bb, jnp.maximum(ATTN_SUB * i + j - 2, 0), 0))
    kv_specs = [blk(j) for j in range(ATTN_SUB + 2)]
    return pl.pallas_call(
        _attn_prompt_kernel,
        grid=(b, t // (ATTN_SUB * ATTN_QB)),
        in_specs=[tok] + kv_specs + kv_specs + [_full_spec(bias_rows.shape)],
        out_specs=tok,
        out_shape=jax.ShapeDtypeStruct((b, t, W_A), BF16),
        scratch_shapes=[pltpu.VMEM((3, H_A, ATTN_QB, ATTN_KB), F32),
                        pltpu.VMEM((2, ATTN_QB, ATTN_KB), F32),
                        pltpu.VMEM((2, ATTN_QB, ATTN_KB), BF16)],
        compiler_params=_params(("arbitrary", "arbitrary")),
        name="attn_prompt",
    )(q, *([k] * (ATTN_SUB + 2)), *([v] * (ATTN_SUB + 2)), bias_rows)


def _attn_sample_kernel(q_ref, kn_ref, vn_ref, kc_ref, vc_ref, row_ref, o_ref, bc_ref, bn_ref):
    nb, s_len, _ = q_ref.shape
    w = kc_ref.shape[2]

    @pl.when(pl.program_id(0) == 0)
    def _():
        for h in range(H_A):
            t = _toeplitz_bias(row_ref, h, s_len)
            bc_ref[h * s_len:(h + 1) * s_len, :] = t[:, :w]
            bn_ref[h * s_len:(h + 1) * s_len, :] = t[:, w:w + s_len]

    lane_head = lax.broadcasted_iota(jnp.int32, (1, W_A), 1) // HD_A
    row_head = lax.broadcasted_iota(jnp.int32, (H_A * s_len, 1), 0) // s_len
    own_head = row_head == lane_head
    for b in range(nb):
        q = q_ref[b] * BF16(HD_A ** -0.5)
        qs = jnp.concatenate([q] * H_A, axis=0)
        qs = jnp.where(own_head, qs, jnp.zeros_like(qs))
        sc = _dot(qs, kc_ref[b].astype(BF16)) + bc_ref[...]
        sn = _dot_nt(qs, kn_ref[b]) + bn_ref[...]
        m = jnp.maximum(jnp.max(sc, axis=-1, keepdims=True), jnp.max(sn, axis=-1, keepdims=True))
        ec = jnp.exp(sc - m)
        en = jnp.exp(sn - m)
        l = jnp.sum(ec, axis=-1, keepdims=True) + jnp.sum(en, axis=-1, keepdims=True)
        full = (_dot_nt(ec.astype(BF16), vc_ref[b].astype(BF16))
                + _dot(en.astype(BF16), vn_ref[b])) / l
        o = full[:s_len]
        for h in range(1, H_A):
            o = jnp.where(lane_head == h, full[h * s_len:(h + 1) * s_len], o)
        o_ref[b] = o.astype(o_ref.dtype)


def _attn_sample(q, kn, vn, kc, vc, bias_rows, nb):
    b, s, _ = q.shape
    w = kc.shape[2]
    assert w == WINDOW_A
    new = pl.BlockSpec((nb, s, W_A), lambda i: (i, 0, 0))
    cache = pl.BlockSpec((nb, W_A, w), lambda i: (i, 0, 0))
    return pl.pallas_call(
        _attn_sample_kernel,
        grid=(b // nb,),
        in_specs=[new, new, new, cache, cache, _full_spec(bias_rows.shape)],
        out_specs=new,
        out_shape=jax.ShapeDtypeStruct((b, s, W_A), BF16),
        scratch_shapes=[pltpu.VMEM((H_A * s, w), F32), pltpu.VMEM((H_A * s, s), F32)],
        compiler_params=_params(("arbitrary",)),
        name="attn_sample",
    )(q, kn, vn, kc, vc, bias_rows)


def _gla_kernel(*refs, has_init, c):
    if has_init:
        q_ref, k_ref, v_ref, g_ref, la_ref, gg_ref, s0_ref, y_ref, so_ref, st_ref = refs
    else:
        q_ref, k_ref, v_ref, g_ref, la_ref, gg_ref, y_ref, so_ref, st_ref = refs
    nb, blk, _ = q_ref.shape
    nsub = c // GLA_SUB
    npair = H_B // 2
    j = pl.program_id(1)

    @pl.when(j == 0)
    def _():
        if has_init:
            st_ref[...] = s0_ref[...]
        else:
            st_ref[...] = jnp.zeros_like(st_ref)

    r2 = lax.broadcasted_iota(jnp.int32, (2 * c, c), 0)
    s2 = lax.broadcasted_iota(jnp.int32, (2 * c, c), 1)
    t2 = r2 & (c - 1)
    same_sub = (s2 // GLA_SUB) == (t2 // GLA_SUB)
    sum_mat = jnp.where((s2 <= t2) & ((r2 < c) | same_sub), 1.0, 0.0).astype(BF16)

    row = lax.broadcasted_iota(jnp.int32, (c, W_BK), 0)
    row2 = lax.broadcasted_iota(jnp.int32, (2 * c, LANES), 0)
    lane2 = lax.broadcasted_iota(jnp.int32, (2 * c, LANES), 1)
    own = (lane2 < DK_B) == (row2 < c)
    sub_of_row2 = (row2 & (c - 1)) // GLA_SUB
    tril2 = lane2 <= (row2 & (c - 1))
    eye = (lax.broadcasted_iota(jnp.int32, (LANES, LANES), 0)
           == lax.broadcasted_iota(jnp.int32, (LANES, LANES), 1))
    scale = DK_B ** -0.5
    zeros_k = jnp.zeros((LANES - c, nsub * LANES), BF16)
    zeros_v = jnp.zeros((LANES - c, DV_B), BF16)

    for ci in range(blk // c):
        rows = slice(ci * c, (ci + 1) * c)
        prep = []
        for b in range(nb):
            la = la_ref[b, rows]
            la_hi = la.astype(BF16)
            rest = la - la_hi.astype(F32)
            la_mid = rest.astype(BF16)
            la_lo = (rest - la_mid.astype(F32)).astype(BF16)
            sums = _dot(sum_mat, la_hi) + _dot(sum_mat, la_mid) + _dot(sum_mat, la_lo)
            cum, cum_sub = sums[:c], sums[c:]
            cum_end = cum[c - 1:c, :]
            qf = q_ref[b, rows].astype(F32)
            kf = k_ref[b, rows].astype(F32)
            q_sub = qf * jnp.exp(cum_sub) * scale
            q_in = (qf * jnp.exp(cum) * scale).astype(BF16)
            k_end = (kf * jnp.exp(cum_end - cum)).astype(BF16)
            k_sub = []
            for i in range(nsub):
                ref_i = cum[i * GLA_SUB - 1:i * GLA_SUB, :] if i else jnp.zeros((1, W_BK), F32)
                k_i = jnp.where(row < (i + 1) * GLA_SUB, kf * jnp.exp(ref_i - cum), 0.0)
                k_sub.append(k_i.astype(BF16))
            prep.append((q_sub, q_in, k_end, k_sub, cum_end))

        att = {}
        for b in range(nb):
            q_sub, _, _, k_sub, _ = prep[b]
            for p in range(npair):
                sl = slice(p * LANES, (p + 1) * LANES)
                k_stack = jnp.concatenate(
                    [jnp.concatenate([k_i[:, sl] for k_i in k_sub], axis=1), zeros_k], axis=0)
                q2 = jnp.where(own, jnp.concatenate([q_sub[:, sl], q_sub[:, sl]], axis=0), 0.0)
                q_stack = jnp.concatenate(
                    [jnp.where(sub_of_row2 == i, q2, 0.0) for i in range(nsub)], axis=1).astype(BF16)
                a = _dot_nt(q_stack, k_stack)
                att[b, p] = jnp.where(tril2, a, 0.0).astype(BF16)

        for b in range(nb):
            q_in = prep[b][1]
            for p in range(npair):
                sl = slice(p * LANES, (p + 1) * LANES)
                st_b = st_ref[b, p].astype(BF16)
                q_in2 = jnp.where(own, jnp.concatenate([q_in[:, sl], q_in[:, sl]], axis=0),
                                  jnp.zeros((), BF16))
                for hh in range(2):
                    h = 2 * p + hh
                    hs = slice(h * DV_B, (h + 1) * DV_B)
                    hrows = slice(hh * c, (hh + 1) * c)
                    lhs = jnp.concatenate([att[b, p][hrows], q_in2[hrows]], axis=1)
                    rhs = jnp.concatenate([v_ref[b, rows, hs], zeros_v, st_b], axis=0)
                    o = _dot(lhs, rhs)
                    gate = g_ref[b, rows, hs].astype(F32)
                    y = (o * _rms_scale(o)) * gg_ref[...] * (gate * _sigmoid(gate))
                    y_ref[b, rows, hs] = y.astype(y_ref.dtype)

        for b in range(nb):
            k_end, cum_end = prep[b][2], prep[b][4]
            for p in range(npair):
                sl = slice(p * LANES, (p + 1) * LANES)
                k2 = jnp.where(own, jnp.concatenate([k_end[:, sl], k_end[:, sl]], axis=0),
                               jnp.zeros((), BF16))
                v2 = jnp.concatenate([v_ref[b, rows, 2 * p * DV_B:(2 * p + 1) * DV_B],
                                      v_ref[b, rows, (2 * p + 1) * DV_B:(2 * p + 2) * DV_B]], axis=0)
                upd = _dot_tn(k2, v2)
                dec = jnp.exp(jnp.sum(jnp.where(eye, cum_end[:, sl], 0.0), axis=1, keepdims=True))
                st_ref[b, p] = st_ref[b, p] * dec + upd

    @pl.when(j == pl.num_programs(1) - 1)
    def _():
        so_ref[...] = st_ref[...]


def _gla(q, k, v, g, la, g_gla, s0, nb, c, chunks_per_step):
    nbt, t, _ = q.shape
    blk = c * chunks_per_step
    tok = lambda w: pl.BlockSpec((nb, blk, w), lambda b, j: (b, j, 0))
    st_spec = pl.BlockSpec((nb, H_B // 2, 2 * DK_B, DV_B), lambda b, j: (b, 0, 0, 0))
    in_specs = [tok(W_BK), tok(W_BK), tok(W_BV), tok(W_BV), tok(W_BK), _full_spec((1, DV_B))]
    args = [q, k, v, g, la, g_gla]
    if s0 is not None:
        in_specs.append(st_spec)
        args.append(s0)
    return pl.pallas_call(
        functools.partial(_gla_kernel, has_init=s0 is not None, c=c),
        grid=(nbt // nb, t // blk),
        in_specs=in_specs,
        out_specs=[tok(W_BV), st_spec],
        out_shape=[jax.ShapeDtypeStruct((nbt, t, W_BV), BF16),
                   jax.ShapeDtypeStruct((nbt, H_B // 2, 2 * DK_B, DV_B), F32)],
        scratch_shapes=[pltpu.VMEM((nb, H_B // 2, DV_B, 2 * DK_B), F32)],
        compiler_params=_params(("parallel", "arbitrary")),
        name="gla",
    )(*args)


def _mixout_kernel(x_ref, ya_ref, yb_ref, ga_ref, gb_ref, gm_ref, gp_ref, wa_ref, wb_ref, wo_ref,
                   o_ref):
    nb, tt, _ = x_ref.shape
    if nb == 1:
        halves = [(slice(None), slice(s * tt // 2, (s + 1) * tt // 2)) for s in range(2)]
        nbh, tth = nb, tt // 2
    else:
        halves = [(slice(s * nb // 2, (s + 1) * nb // 2), slice(None)) for s in range(2)]
        nbh, tth = nb // 2, tt
    m = nbh * tth
    merged = []
    for bs, ts in halves:
        a = _dot(ya_ref[bs, ts, :].reshape(m, W_A), wa_ref[...])
        b = _dot(yb_ref[bs, ts, :].reshape(m, W_BV), wb_ref[...])
        ga = _sigmoid(ga_ref[bs, ts, :].reshape(m, D_MODEL).astype(F32))
        gb = _sigmoid(gb_ref[bs, ts, :].reshape(m, D_MODEL).astype(F32))
        merged.append((ga * a + gb * b).astype(BF16))
    for (bs, ts), mg in zip(halves, merged):
        mo = _dot(mg, wo_ref[...])
        n = ((mo * _rms_scale(mo)) * gp_ref[...]).reshape(nbh, tth, D_MODEL)
        o_ref[bs, ts, :] = x_ref[bs, ts, :] + gm_ref[bs] * n


def _mixout(x, ya, yb, ga, gb, mod, g_post, wa, wb, wo, nb, tt):
    nbt, t, _ = x.shape
    tok = lambda w: pl.BlockSpec((nb, tt, w), lambda b, i: (b, i, 0))
    return pl.pallas_call(
        _mixout_kernel,
        grid=(nbt // nb, t // tt),
        in_specs=[tok(D_MODEL), tok(W_A), tok(W_BV), tok(D_MODEL), tok(D_MODEL),
                  pl.BlockSpec((nb, 1, D_MODEL), lambda b, i: (b, 0, 2)),
                  _full_spec((1, D_MODEL)), _full_spec(wa.shape), _full_spec(wb.shape),
                  _full_spec(wo.shape)],
        out_specs=tok(D_MODEL),
        out_shape=jax.ShapeDtypeStruct(x.shape, F32),
        compiler_params=_params(("parallel", "parallel")),
        name="mixout",
    )(x, ya, yb, ga, gb, mod, g_post, wa, wb, wo)


def _gelu_tanh(x):
    c = float(np.sqrt(2.0 / np.pi))
    half = 0.5 * x
    return half + half * jnp.tanh(x * (c + (0.044715 * c) * (x * x)))


def _ffn_kernel(*refs, has_state):
    if has_state:
        (x_ref, shift_ref, scale_ref, gate_ref, gpre_ref, gpost_ref, wu_ref, wd_ref,
         wdw_ref, bdw_ref, prev_ref, o_ref, tail_ref, h_ref, act_ref) = refs
    else:
        (x_ref, shift_ref, scale_ref, gate_ref, gpre_ref, gpost_ref, wu_ref, wd_ref,
         wdw_ref, bdw_ref, o_ref, tail_ref, h_ref, act_ref, prev_ref) = refs

        @pl.when(pl.program_id(1) == 0)
        def _():
            prev_ref[...] = jnp.zeros_like(prev_ref)

    nb, tt, _ = x_ref.shape
    m = nb * tt
    x = x_ref[...]
    h = (x * _rms_scale(x)) * (gpre_ref[...] * (1.0 + scale_ref[...])) + shift_ref[...]
    h_ref[...] = h.reshape(m, D_MODEL).astype(BF16)
    ridx = lax.broadcasted_iota(jnp.int32, (nb, SUBLANES, FFN_FT), 1)

    def conv(u, lanes):
        u3 = u.reshape(nb, tt, FFN_FT)
        r1 = pltpu.roll(u, 1, 0).reshape(nb, tt, FFN_FT)
        r2 = pltpu.roll(u, 2, 0).reshape(nb, tt, FFN_FT)

        def taps(u_m2, u_m1, u_0):
            y = bdw_ref[:, lanes] + wdw_ref[0:1, lanes] * u_m2
            y = y + wdw_ref[1:2, lanes] * u_m1
            return y + wdw_ref[2:3, lanes] * u_0

        p2, p1 = prev_ref[:, 0:1, lanes], prev_ref[:, 1:2, lanes]
        h_m1 = jnp.where(ridx == 0, p1, r1[:, :SUBLANES])
        h_m2 = jnp.where(ridx == 0, p2, jnp.where(ridx == 1, p1, r2[:, :SUBLANES]))
        y = jnp.concatenate([taps(h_m2, h_m1, u3[:, :SUBLANES]),
                             taps(r2[:, SUBLANES:], r1[:, SUBLANES:], u3[:, SUBLANES:])], axis=1)
        tail = u3[:, tt - (CONV_W - 1):, :]
        tail_ref[:, :, lanes] = tail
        if not has_state:
            prev_ref[:, :, lanes] = tail
        return y.reshape(m, FFN_FT)

    def up(f):
        hb = h_ref[...]
        lo = f * FFN_FT
        return (_dot(hb, wu_ref[:, lo:lo + FFN_FT]),
                _dot(hb, wu_ref[:, D_FF + lo:D_FF + lo + FFN_FT]))

    ua, ug = up(0)
    for f in range(FFN_NF):
        if f + 1 < FFN_NF:
            ua_next, ug_next = up(f + 1)
        cols = slice(f * FFN_FT, (f + 1) * FFN_FT)
        ya = conv(ua, cols)
        yg = conv(ug, slice(D_FF + f * FFN_FT, D_FF + (f + 1) * FFN_FT))
        act_ref[:, cols] = (_gelu_tanh(ya) * yg).astype(BF16)
        if f + 1 < FFN_NF:
            ua, ug = ua_next, ug_next
    yf = _dot(act_ref[...], wd_ref[...])
    n = ((yf * _rms_scale(yf)) * gpost_ref[...]).reshape(nb, tt, D_MODEL)
    o_ref[...] = x_ref[...] + gate_ref[...] * n


def _ffn(x, mod, g_pre, g_post, wu, wd, w_dw, b_dw, state, nb, tt):
    nbt, t, _ = x.shape
    m = nb * tt
    tok = pl.BlockSpec((nb, tt, D_MODEL), lambda b, i: (b, i, 0))
    mod_spec = lambda col: pl.BlockSpec((nb, 1, D_MODEL), lambda b, i, col=col: (b, 0, col))
    tail_spec = pl.BlockSpec((nb, CONV_W - 1, 2 * D_FF), lambda b, i: (b, 0, 0))
    in_specs = [tok, mod_spec(3), mod_spec(4), mod_spec(5), _full_spec((1, D_MODEL)),
                _full_spec((1, D_MODEL)), _full_spec(wu.shape),
                _full_spec(wd.shape), _full_spec(w_dw.shape), _full_spec(b_dw.shape)]
    args = [x, mod, mod, mod, g_pre, g_post, wu, wd, w_dw, b_dw]
    scratch = [pltpu.VMEM((m, D_MODEL), BF16), pltpu.VMEM((m, D_FF), BF16)]
    if state is not None:
        in_specs.append(tail_spec)
        args.append(state)
    else:
        scratch.append(pltpu.VMEM((nb, CONV_W - 1, 2 * D_FF), F32))
    return pl.pallas_call(
        functools.partial(_ffn_kernel, has_state=state is not None),
        grid=(nbt // nb, t // tt),
        in_specs=in_specs,
        out_specs=[tok, tail_spec],
        out_shape=[jax.ShapeDtypeStruct(x.shape, F32),
                   jax.ShapeDtypeStruct((nbt, CONV_W - 1, 2 * D_FF), F32)],
        scratch_shapes=scratch,
        compiler_params=_params(("parallel", "arbitrary")),
        name="ffn",
    )(*args)


def _bias_rows(rel_bias):
    assert BIAS_ROW >= ATTN_KB + ATTN_QB - 1 and WINDOW_A == ATTN_KB - ATTN_QB
    far_pos = jnp.broadcast_to(rel_bias[:, -1:], (H_A, BIAS_ROW))
    far_neg = jnp.broadcast_to(rel_bias[:, :1], (H_A, BIAS_ROW))
    n_mid = 2 * MAX_REL + 1
    n_lo = ATTN_KB + 1 - (WINDOW_A - MAX_REL) - n_mid
    rows = jnp.concatenate([far_pos[:, :WINDOW_A - MAX_REL], rel_bias[:, ::-1], far_neg[:, :n_lo],
                            far_pos[:, :BIAS_ROW - ATTN_KB - 1]], axis=1)
    return rows.astype(F32).reshape(H_A, 1, BIAS_ROW)


_LATE_WEIGHTS = ('w_br_a', 'w_br_b', 'w_out', 'w_up', 'w_down')


def _layer(x, mod, cache, s_gla, s_conv, w, first_chunk):
    nbt, t, _ = x.shape
    if first_chunk:
        nb, tt = 1, 1024
        nb_f, tt_f = 1, 1024
    else:
        nb, tt = 1024 // t, t
        nb_f, tt_f = 1024 // t, t
    keep_rows = min(WINDOW_A, t) if first_chunk else 0
    cast_ws = () if _LATE_WEIGHTS[0] in w else tuple(w[k + '_f32'] for k in _LATE_WEIGHTS)
    qa, ka, va, qb, kb, vb, gb, gate_a, gate_b, log_a, *extra = _inproj(
        x, mod, w['g_pre_mix'], w['w_main'], w['w_gk1'], w['w_gk2'], w['b_gk'], nb, tt, keep_rows,
        cast_ws)
    kv_t = extra[:2] if keep_rows else []
    if cast_ws:
        w = {**w, **dict(zip(_LATE_WEIGHTS, extra[len(kv_t):]))}
    if first_chunk:
        ya = _attn_prompt(qa, ka, va, w['bias_rows'])
        k_keep, v_keep = (jnp.transpose(a.reshape(nbt, H_A, HD_A, keep_rows), (0, 3, 1, 2))
                          for a in kv_t)
        yb, s_new = _gla(qb, kb, vb, gb, log_a, w['g_gla'], None, nbt, CHUNK, 8)
    else:
        k_cache, v_cache = cache
        ya = _attn_sample(qa, ka, va, k_cache, v_cache, w['bias_rows'], 4)
        k_keep, v_keep = (a.astype(F32).reshape(nbt, t, H_A, HD_A) for a in (ka, va))
        yb, s_new = _gla(qb, kb, vb, gb, log_a, w['g_gla'], s_gla, 4, t, 1)
    x1 = _mixout(x, ya, yb, gate_a, gate_b, mod, w['g_post_mix'], w['w_br_a'], w['w_br_b'],
                 w['w_out'], nb, tt)
    y, tail = _ffn(x1, mod, w['g_pre_ffn'], w['g_post_ffn'], w['w_up'], w['w_down'],
                   w['w_dw'], w['b_dw'], None if first_chunk else s_conv, nb_f, tt_f)
    return (y, k_keep, v_keep, s_new.reshape(nbt, H_B, DK_B, DV_B), tail), w


def _prep_weights(w_main, w_gk1, w_gk2, b_gk, rel_bias, g_gla, w_br_a, w_br_b, w_out, w_up, w_dw,
                  b_dw, w_down, g_pre_mix, g_post_mix, g_pre_ffn, g_post_ffn):
    w_gk2p = jnp.pad(w_gk2, ((0, LANES - GK_RANK), (0, 0))).astype(BF16)
    row = lambda a: a.reshape(1, -1)
    return {
        'w_main': w_main, 'w_gk1': w_gk1, 'w_gk2': w_gk2p, 'b_gk': row(b_gk),
        'bias_rows': _bias_rows(rel_bias), 'g_gla': row(g_gla),
        'w_br_a_f32': w_br_a, 'w_br_b_f32': w_br_b, 'w_out_f32': w_out, 'w_up_f32': w_up,
        'w_down_f32': w_down, 'w_dw': w_dw, 'b_dw': row(b_dw),
        'g_pre_mix': row(g_pre_mix), 'g_post_mix': row(g_post_mix),
        'g_pre_ffn': row(g_pre_ffn), 'g_post_ffn': row(g_post_ffn),
    }


def kernel(x_prompt, x_sample, cache_k_a, cache_v_a, state_gla, state_conv, c_prompt, c_sample, w_ada, b_ada, g_pre_mix, g_post_mix, g_pre_ffn, g_post_ffn, w_in, w_gk2, b_gk, rel_bias, g_gla, w_br_a, w_br_b, w_out, w_up, w_dw, b_dw, w_down):
    depth = w_ada.shape[0]
    assert depth == 1
    bp, bs = x_prompt.shape[0], x_sample.shape[0]
    cache_rows = cache_k_a.shape[2]
    yp, ys = x_prompt, x_sample
    outs = [[] for _ in range(8)]
    for l in range(depth):
        c_all = jnp.concatenate([c_prompt, c_sample], axis=0)
        pad = (-c_all.shape[0]) % SUBLANES
        mod, w_main, w_gk1 = _prep(jnp.pad(c_all, ((0, pad), (0, 0))), w_ada[l], b_ada[l], w_in[l].T)
        w = _prep_weights(w_main, w_gk1, w_gk2[l], b_gk[l], rel_bias[l], g_gla[l], w_br_a[l],
                          w_br_b[l], w_out[l], w_up[l], w_dw[l], b_dw[l], w_down[l], g_pre_mix[l],
                          g_post_mix[l], g_pre_ffn[l], g_post_ffn[l])
        mod_p = mod[:bp].reshape(bp, 1, 6 * D_MODEL)
        mod_s = mod[bp:bp + bs].reshape(bs, 1, 6 * D_MODEL)
        (yp, kp, vp, gp, cp), w = _layer(yp, mod_p, None, None, None, w, True)
        to_t = lambda c: jnp.transpose(c, (0, 2, 3, 1)).reshape(bs, W_A, cache_rows)
        cache = (to_t(cache_k_a[l]), to_t(cache_v_a[l]))
        s0 = state_gla[l].reshape(bs, H_B // 2, 2 * DK_B, DV_B)
        (ys, kn, vn, gn, cn), _ = _layer(ys, mod_s, cache, s0, state_conv[l], w, False)
        for lst, a in zip(outs, (kp, vp, gp, cp, kn, vn, gn, cn)):
            lst.append(a)
    return (yp, ys) + tuple(jnp.stack(lst) for lst in outs)
```

```python
import functools

import jax
import jax.numpy as jnp
import numpy as np
from jax import lax
from jax.experimental import pallas as pl
from jax.experimental.pallas import tpu as pltpu

D_MODEL = 1024
CHUNK = 64
BAND_CHUNKS = 8
WINDOW_A = BAND_CHUNKS * CHUNK
H_A = 8
HD_A = 64
MAX_REL = 128
H_B = 4
DK_B = 64
DV_B = 128
GK_RANK = 16
GK_NORM = 16.0
GLA_SUB = 16
D_FF = 2816
CONV_W = 3
EPS = 1e-6
NEG_INF = -1e30

W_A = H_A * HD_A
W_BK = H_B * DK_B
W_BV = H_B * DV_B

LANES = 128
SUBLANES = 8
VMEM_LIMIT = 60 * 1024 * 1024

ATTN_QB = 256
ATTN_KB = 3 * ATTN_QB
BIAS_ROW = 1024
FFN_FT = 256
FFN_NF = D_FF // FFN_FT

BF16 = jnp.bfloat16
F32 = jnp.float32


def _params(sem):
    return pltpu.CompilerParams(dimension_semantics=sem, vmem_limit_bytes=VMEM_LIMIT)


def _full_spec(shape):
    nd = len(shape)
    return pl.BlockSpec(shape, lambda *_: (0,) * nd, pipeline_mode=pl.Buffered(1))


def _dot(a, b):
    return jnp.dot(a, b, preferred_element_type=F32)


def _dot_nt(a, b):
    return lax.dot_general(a, b, (((1,), (1,)), ((), ())), preferred_element_type=F32)


def _dot_tn(a, b):
    return lax.dot_general(a, b, (((0,), (0,)), ((), ())), preferred_element_type=F32)


def _sigmoid(x):
    return 1.0 / (1.0 + jnp.exp(-x))


def _rms_scale(x):
    return lax.rsqrt(jnp.mean(x * x, axis=-1, keepdims=True) + EPS)


_GK_LO = 3 * W_A + 2 * W_BK + 2 * W_BV


PREP_STEPS = 4


def _prep_kernel(c_ref, wa_ref, ba_ref, wi_ref, mod_ref, main_ref, gk_ref):
    c = c_ref[...]
    s = (c * _sigmoid(c)).astype(BF16)
    mod_ref[...] = _dot(s, wa_ref[...].astype(BF16)) + ba_ref[...]
    main_ref[:_GK_LO, :] = wi_ref[:_GK_LO, :].astype(BF16)
    main_ref[_GK_LO:, :] = wi_ref[_GK_LO + GK_RANK:, :].astype(BF16)
    gk_ref[:GK_RANK, :] = wi_ref[_GK_LO:_GK_LO + GK_RANK, :].astype(BF16)
    gk_ref[GK_RANK:, :] = jnp.zeros((LANES - GK_RANK, gk_ref.shape[1]), BF16)


def _prep(c_all, w_ada, b_ada, w_in_t):
    rows = c_all.shape[0]
    n = w_ada.shape[1]
    d_in, d = w_in_t.shape
    tn, cols = n // PREP_STEPS, d // PREP_STEPS
    return pl.pallas_call(
        _prep_kernel,
        grid=(PREP_STEPS,),
        in_specs=[pl.BlockSpec((rows, D_MODEL), lambda j: (0, 0)),
                  pl.BlockSpec((D_MODEL, tn), lambda j: (0, j)),
                  pl.BlockSpec((1, tn), lambda j: (0, j)),
                  pl.BlockSpec((d_in, cols), lambda j: (0, j))],
        out_specs=[pl.BlockSpec((rows, tn), lambda j: (0, j)),
                   pl.BlockSpec((d_in - GK_RANK, cols), lambda j: (0, j)),
                   pl.BlockSpec((LANES, cols), lambda j: (0, j))],
        out_shape=[jax.ShapeDtypeStruct((rows, n), F32),
                   jax.ShapeDtypeStruct((d_in - GK_RANK, d), BF16),
                   jax.ShapeDtypeStruct((LANES, d), BF16)],
        compiler_params=_params(("parallel",)),
        name="prep",
    )(c_all, w_ada, b_ada.reshape(1, n), w_in_t)


_IN_GROUPS = (W_A, W_A, W_A, W_BK, W_BK, W_BV, W_BV, D_MODEL, D_MODEL)


def _inproj_kernel(*refs, n_cast, n_kv_t):
    (x_ref, shift_ref, scale_ref, g_ref, wm_ref, wg1_ref, wg2_ref, bgk_ref), refs = refs[:8], refs[8:]
    cast_in, refs = refs[:n_cast], refs[n_cast:]
    (qa_ref, ka_ref, va_ref, qb_ref, kb_ref, vb_ref, gb_ref, ga_ref, gtb_ref, la_ref) = refs[:10]
    kv_t_refs, cast_out = refs[10:10 + n_kv_t], refs[10 + n_kv_t:]
    for w_ref, o_ref in zip(cast_in, cast_out):
        o_ref[...] = w_ref[...].astype(o_ref.dtype)
    nb, tt, _ = x_ref.shape
    x = x_ref[...]
    h = (x * _rms_scale(x)) * (g_ref[...] * (1.0 + scale_ref[...])) + shift_ref[...]
    hb = h.reshape(nb * tt, D_MODEL).astype(BF16)
    if kv_t_refs:
        @pl.when(pl.program_id(1) == pl.num_programs(1) - 1)
        def _():
            rows = kv_t_refs[0].shape[2]
            newest = hb[nb * tt - rows:, :]
            for o_ref, lo in zip(kv_t_refs, (W_A, 2 * W_A)):
                o_ref[0] = _dot_nt(wm_ref[lo:lo + W_A, :], newest)
    outs = (qa_ref, ka_ref, va_ref, qb_ref, kb_ref, vb_ref, gb_ref, ga_ref, gtb_ref)
    lo = 0
    for o_ref, w in zip(outs, _IN_GROUPS):
        z = _dot_nt(hb, wm_ref[lo:lo + w, :])
        o_ref[...] = z.reshape(nb, tt, w).astype(o_ref.dtype)
        lo += w
    gk_low = _dot_nt(hb, wg1_ref[...]).astype(BF16)
    gk = _dot(gk_low, wg2_ref[...]) + bgk_ref[...]
    log_a = (jnp.minimum(gk, 0.0) - jnp.log1p(jnp.exp(-jnp.abs(gk)))) / GK_NORM
    la_ref[...] = log_a.reshape(nb, tt, W_BK)


def _inproj(x, mod, g_pre, wm, wg1, wg2, bgk, nb, tt, keep_rows, cast_ws=()):
    nbt, t, _ = x.shape
    grid = (nbt // nb, t // tt)
    steps = grid[0] * grid[1]
    cast_specs = [pl.BlockSpec((w.shape[0] // steps, w.shape[1]),
                               lambda b, i: (b * grid[1] + i, 0)) for w in cast_ws]
    tok = lambda w: pl.BlockSpec((nb, tt, w), lambda b, i: (b, i, 0))
    mod_spec = lambda col: pl.BlockSpec((nb, 1, D_MODEL), lambda b, i, col=col: (b, 0, col))
    widths = _IN_GROUPS + (W_BK,)
    dtypes = (BF16, BF16, BF16, F32, F32, BF16, BF16, BF16, BF16, F32)
    out_specs = [tok(w) for w in widths]
    out_shape = [jax.ShapeDtypeStruct((nbt, t, w), dt) for w, dt in zip(widths, dtypes)]
    if keep_rows:
        assert nb == 1 and keep_rows <= tt
        out_specs += [pl.BlockSpec((1, W_A, keep_rows), lambda b, i: (b, 0, 0))] * 2
        out_shape += [jax.ShapeDtypeStruct((nbt, W_A, keep_rows), F32)] * 2
    return pl.pallas_call(
        functools.partial(_inproj_kernel, n_cast=len(cast_ws), n_kv_t=2 if keep_rows else 0),
        grid=grid,
        in_specs=[tok(D_MODEL), mod_spec(0), mod_spec(1), _full_spec((1, D_MODEL)),
                  _full_spec(wm.shape), _full_spec(wg1.shape), _full_spec(wg2.shape),
                  _full_spec((1, W_BK))] + cast_specs,
        out_specs=out_specs + cast_specs,
        out_shape=out_shape + [jax.ShapeDtypeStruct(w.shape, BF16) for w in cast_ws],
        compiler_params=_params(("arbitrary", "arbitrary")),
        name="inproj",
    )(x, mod, mod, g_pre, wm, wg1, wg2, bgk, *cast_ws)


def _head_masks():
    lane = lax.broadcasted_iota(jnp.int32, (1, LANES), 1)
    first = lane < HD_A
    return first, jnp.logical_not(first)


def _toeplitz_bias(row_ref, h, rows):
    rb = jnp.broadcast_to(row_ref[h], (rows, BIAS_ROW))
    return pltpu.roll(rb, 0, 1, stride=1, stride_axis=0)


ATTN_RB = 32
ATTN_WIN = 640


ATTN_SUB = 4


def _attn_prompt_kernel(*refs):
    q_ref, refs = refs[0], refs[1:]
    k_refs, v_refs = refs[:ATTN_SUB + 2], refs[ATTN_SUB + 2:2 * ATTN_SUB + 4]
    row_ref, o_ref, bias_ref, s_ref, p_ref = refs[2 * ATTN_SUB + 4:]
    i = pl.program_id(1)

    @pl.when((pl.program_id(0) == 0) & (i == 0))
    def _():
        qc = lax.broadcasted_iota(jnp.int32, (ATTN_QB, ATTN_KB), 0) // CHUNK
        col = lax.broadcasted_iota(jnp.int32, (ATTN_QB, ATTN_KB), 1)
        kc = col // CHUNK - BAND_CHUNKS
        valid = (kc <= qc) & (kc >= qc - BAND_CHUNKS)
        for h in range(H_A):
            t = _toeplitz_bias(row_ref, h, ATTN_QB)
            band = jnp.where(valid, t[:, :ATTN_KB], NEG_INF)
            bias_ref[0, h] = jnp.where(col >= 2 * ATTN_QB, band, NEG_INF)
            bias_ref[1, h] = jnp.where(col >= ATTN_QB, band, NEG_INF)
            bias_ref[2, h] = band
        p_ref[...] = jnp.zeros_like(p_ref)

    masks = _head_masks()

    def pair_rows(refs, p):
        sl = slice(p * LANES, (p + 1) * LANES)
        return jnp.concatenate([r[0, :, sl] for r in refs], axis=0)

    units = [(sub, h) for sub in range(ATTN_SUB) for h in range(H_A)]

    def scores(u):
        sub, h = units[u]
        p, hh = divmod(h, 2)
        qp = q_ref[0, sub * ATTN_QB:(sub + 1) * ATTN_QB, p * LANES:(p + 1) * LANES] * BF16(HD_A ** -0.5)
        qm = jnp.where(masks[hh], qp, jnp.zeros_like(qp))
        s_ref[u % 2] = _dot_nt(qm, pair_rows(k_refs[sub:sub + 3], p))

    scores(0)
    o_first = None
    for u, (sub, h) in enumerate(units):
        if u + 1 < len(units):
            scores(u + 1)
        slot = u % 2
        var = jnp.minimum(ATTN_SUB * i + sub, 2)
        sums = []
        for r in range(ATTN_QB // ATTN_RB):
            rows = slice(r * ATTN_RB, (r + 1) * ATTN_RB)
            lo = 0 if r * ATTN_RB < ATTN_QB // 2 else ATTN_KB - ATTN_WIN
            sb = s_ref[slot, rows, lo:lo + ATTN_WIN] + bias_ref[var, h, rows, lo:lo + ATTN_WIN]
            e = jnp.exp(sb - jnp.max(sb, axis=-1, keepdims=True))
            sums.append(jnp.sum(e, axis=-1, keepdims=True))
            p_ref[slot, rows, lo:lo + ATTN_WIN] = e.astype(BF16)
        p, hh = divmod(h, 2)
        o = _dot(p_ref[slot], pair_rows(v_refs[sub:sub + 3], p)) / jnp.concatenate(sums, axis=0)
        if hh == 0:
            o_first = o
        else:
            o_ref[0, sub * ATTN_QB:(sub + 1) * ATTN_QB, p * LANES:(p + 1) * LANES] = jnp.where(
                masks[0], o_first, o).astype(o_ref.dtype)


def _attn_prompt(q, k, v, bias_rows):
    b, t, _ = q.shape
    tok = pl.BlockSpec((1, ATTN_SUB * ATTN_QB, W_A), lambda bb, i: (bb, i, 0))
    blk = lambda j: pl.BlockSpec(
        (1, ATTN_QB, W_A), lambda bb, i, j=j: (bb, jnp.maximum(ATTN_SUB * i + j - 2, 0), 0))
    kv_specs = [blk(j) for j in range(ATTN_SUB + 2)]
    return pl.pallas_call(
        _attn_prompt_kernel,
        grid=(b, t // (ATTN_SUB * ATTN_QB)),
        in_specs=[tok] + kv_specs + kv_specs + [_full_spec(bias_rows.shape)],
        out_specs=tok,
        out_shape=jax.ShapeDtypeStruct((b, t, W_A), BF16),
        scratch_shapes=[pltpu.VMEM((3, H_A, ATTN_QB, ATTN_KB), F32),
                        pltpu.VMEM((2, ATTN_QB, ATTN_KB), F32),
                        pltpu.VMEM((2, ATTN_QB, ATTN_KB), BF16)],
        compiler_params=_params(("arbitrary", "arbitrary")),
        name="attn_prompt",
    )(q, *([k] * (ATTN_SUB + 2)), *([v] * (ATTN_SUB + 2)), bias_rows)


def _attn_sample_kernel(q_ref, kn_ref, vn_ref, kc_ref, vc_ref, row_ref, o_ref, bc_ref, bn_ref):
    nb, s_len, _ = q_ref.shape
    w = kc_ref.shape[2]

    @pl.when(pl.program_id(0) == 0)
    def _():
        for h in range(H_A):
            t = _toeplitz_bias(row_ref, h, s_len)
            bc_ref[h * s_len:(h + 1) * s_len, :] = t[:, :w]
            bn_ref[h * s_len:(h + 1) * s_len, :] = t[:, w:w + s_len]

    lane_head = lax.broadcasted_iota(jnp.int32, (1, W_A), 1) // HD_A
    row_head = lax.broadcasted_iota(jnp.int32, (H_A * s_len, 1), 0) // s_len
    own_head = row_head == lane_head
    for b in range(nb):
        q = q_ref[b] * BF16(HD_A ** -0.5)
        qs = jnp.concatenate([q] * H_A, axis=0)
        qs = jnp.where(own_head, qs, jnp.zeros_like(qs))
        sc = _dot(qs, kc_ref[b].astype(BF16)) + bc_ref[...]
        sn = _dot_nt(qs, kn_ref[b]) + bn_ref[...]
        m = jnp.maximum(jnp.max(sc, axis=-1, keepdims=True), jnp.max(sn, axis=-1, keepdims=True))
        ec = jnp.exp(sc - m)
        en = jnp.exp(sn - m)
        l = jnp.sum(ec, axis=-1, keepdims=True) + jnp.sum(en, axis=-1, keepdims=True)
        full = (_dot_nt(ec.astype(BF16), vc_ref[b].astype(BF16))
                + _dot(en.astype(BF16), vn_ref[b])) / l
        o = full[:s_len]
        for h in range(1, H_A):
            o = jnp.where(lane_head == h, full[h * s_len:(h + 1) * s_len], o)
        o_ref[b] = o.astype(o_ref.dtype)


def _attn_sample(q, kn, vn, kc, vc, bias_rows, nb):
    b, s, _ = q.shape
    w = kc.shape[2]
    assert w == WINDOW_A
    new = pl.BlockSpec((nb, s, W_A), lambda i: (i, 0, 0))
    cache = pl.BlockSpec((nb, W_A, w), lambda i: (i, 0, 0))
    return pl.pallas_call(
        _attn_sample_kernel,
        grid=(b // nb,),
        in_specs=[new, new, new, cache, cache, _full_spec(bias_rows.shape)],
        out_specs=new,
        out_shape=jax.ShapeDtypeStruct((b, s, W_A), BF16),
        scratch_shapes=[pltpu.VMEM((H_A * s, w), F32), pltpu.VMEM((H_A * s, s), F32)],
        compiler_params=_params(("arbitrary",)),
        name="attn_sample",
    )(q, kn, vn, kc, vc, bias_rows)


def _gla_kernel(*refs, has_init, c):
    if has_init:
        q_ref, k_ref, v_ref, g_ref, la_ref, gg_ref, s0_ref, y_ref, so_ref, st_ref = refs
    else:
        q_ref, k_ref, v_ref, g_ref, la_ref, gg_ref, y_ref, so_ref, st_ref = refs
    nb, blk, _ = q_ref.shape
    nsub = c // GLA_SUB
    npair = H_B // 2
    j = pl.program_id(1)

    @pl.when(j == 0)
    def _():
        if has_init:
            st_ref[...] = s0_ref[...]
        else:
            st_ref[...] = jnp.zeros_like(st_ref)

    r2 = lax.broadcasted_iota(jnp.int32, (2 * c, c), 0)
    s2 = lax.broadcasted_iota(jnp.int32, (2 * c, c), 1)
    t2 = r2 & (c - 1)
    same_sub = (s2 // GLA_SUB) == (t2 // GLA_SUB)
    sum_mat = jnp.where((s2 <= t2) & ((r2 < c) | same_sub), 1.0, 0.0).astype(BF16)

    row = lax.broadcasted_iota(jnp.int32, (c, W_BK), 0)
    row2 = lax.broadcasted_iota(jnp.int32, (2 * c, LANES), 0)
    lane2 = lax.broadcasted_iota(jnp.int32, (2 * c, LANES), 1)
    own = (lane2 < DK_B) == (row2 < c)
    sub_of_row2 = (row2 & (c - 1)) // GLA_SUB
    tril2 = lane2 <= (row2 & (c - 1))
    eye = (lax.broadcasted_iota(jnp.int32, (LANES, LANES), 0)
           == lax.broadcasted_iota(jnp.int32, (LANES, LANES), 1))
    scale = DK_B ** -0.5
    zeros_k = jnp.zeros((LANES - c, nsub * LANES), BF16)
    zeros_v = jnp.zeros((LANES - c, DV_B), BF16)

    for ci in range(blk // c):
        rows = slice(ci * c, (ci + 1) * c)
        prep = []
        for b in range(nb):
            la = la_ref[b, rows]
            la_hi = la.astype(BF16)
            rest = la - la_hi.astype(F32)
            la_mid = rest.astype(BF16)
            la_lo = (rest - la_mid.astype(F32)).astype(BF16)
            sums = _dot(sum_mat, la_hi) + _dot(sum_mat, la_mid) + _dot(sum_mat, la_lo)
            cum, cum_sub = sums[:c], sums[c:]
            cum_end = cum[c - 1:c, :]
            qf = q_ref[b, rows]
            kf = k_ref[b, rows]
            q_sub = qf * jnp.exp(cum_sub) * scale
            q_in = (qf * jnp.exp(cum) * scale).astype(BF16)
            k_end = (kf * jnp.exp(cum_end - cum)).astype(BF16)
            k_sub = []
            for i in range(nsub):
                ref_i = cum[i * GLA_SUB - 1:i * GLA_SUB, :] if i else jnp.zeros((1, W_BK), F32)
                k_i = jnp.where(row < (i + 1) * GLA_SUB, kf * jnp.exp(ref_i - cum), 0.0)
                k_sub.append(k_i.astype(BF16))
            prep.append((q_sub, q_in, k_end, k_sub, cum_end))

        att = {}
        for b in range(nb):
            q_sub, _, _, k_sub, _ = prep[b]
            for p in range(npair):
                sl = slice(p * LANES, (p + 1) * LANES)
                k_stack = jnp.concatenate(
                    [jnp.concatenate([k_i[:, sl] for k_i in k_sub], axis=1), zeros_k], axis=0)
                q2 = jnp.where(own, jnp.concatenate([q_sub[:, sl], q_sub[:, sl]], axis=0), 0.0)
                q_stack = jnp.concatenate(
                    [jnp.where(sub_of_row2 == i, q2, 0.0) for i in range(nsub)], axis=1).astype(BF16)
                a = _dot_nt(q_stack, k_stack)
                att[b, p] = jnp.where(tril2, a, 0.0).astype(BF16)

        for b in range(nb):
            q_in = prep[b][1]
            for p in range(npair):
                sl = slice(p * LANES, (p + 1) * LANES)
                st_b = st_ref[b, p].astype(BF16)
                q_in2 = jnp.where(own, jnp.concatenate([q_in[:, sl], q_in[:, sl]], axis=0),
                                  jnp.zeros((), BF16))
                for hh in range(2):
                    h = 2 * p + hh
                    hs = slice(h * DV_B, (h + 1) * DV_B)
                    hrows = slice(hh * c, (hh + 1) * c)
                    lhs = jnp.concatenate([att[b, p][hrows], q_in2[hrows]], axis=1)
                    rhs = jnp.concatenate([v_ref[b, rows, hs], zeros_v, st_b], axis=0)
                    o = _dot(lhs, rhs)
                    gate = g_ref[b, rows, hs].astype(F32)
                    y = (o * _rms_scale(o)) * gg_ref[...] * (gate * _sigmoid(gate))
                    y_ref[b, rows, hs] = y.astype(y_ref.dtype)

        for b in range(nb):
            k_end, cum_end = prep[b][2], prep[b][4]
            for p in range(npair):
                sl = slice(p * LANES, (p + 1) * LANES)
                k2 = jnp.where(own, jnp.concatenate([k_end[:, sl], k_end[:, sl]], axis=0),
                               jnp.zeros((), BF16))
                v2 = jnp.concatenate([v_ref[b, rows, 2 * p * DV_B:(2 * p + 1) * DV_B],
                                      v_ref[b, rows, (2 * p + 1) * DV_B:(2 * p + 2) * DV_B]], axis=0)
                upd = _dot_tn(k2, v2)
                dec = jnp.exp(jnp.sum(jnp.where(eye, cum_end[:, sl], 0.0), axis=1, keepdims=True))
                st_ref[b, p] = st_ref[b, p] * dec + upd

    @pl.when(j == pl.num_programs(1) - 1)
    def _():
        so_ref[...] = st_ref[...]


def _gla(q, k, v, g, la, g_gla, s0, nb, c, chunks_per_step):
    nbt, t, _ = q.shape
    blk = c * chunks_per_step
    tok = lambda w: pl.BlockSpec((nb, blk, w), lambda b, j: (b, j, 0))
    st_spec = pl.BlockSpec((nb, H_B // 2, 2 * DK_B, DV_B), lambda b, j: (b, 0, 0, 0))
    in_specs = [tok(W_BK), tok(W_BK), tok(W_BV), tok(W_BV), tok(W_BK), _full_spec((1, DV_B))]
    args = [q, k, v, g, la, g_gla]
    if s0 is not None:
        in_specs.append(st_spec)
        args.append(s0)
    return pl.pallas_call(
        functools.partial(_gla_kernel, has_init=s0 is not None, c=c),
        grid=(nbt // nb, t // blk),
        in_specs=in_specs,
        out_specs=[tok(W_BV), st_spec],
        out_shape=[jax.ShapeDtypeStruct((nbt, t, W_BV), BF16),
                   jax.ShapeDtypeStruct((nbt, H_B // 2, 2 * DK_B, DV_B), F32)],
        scratch_shapes=[pltpu.VMEM((nb, H_B // 2, DV_B, 2 * DK_B), F32)],
        compiler_params=_params(("parallel", "arbitrary")),
        name="gla",
    )(*args)


def _mixout_kernel(x_ref, ya_ref, yb_ref, ga_ref, gb_ref, gm_ref, gp_ref, wa_ref, wb_ref, wo_ref,
                   o_ref):
    nb, tt, _ = x_ref.shape
    if nb == 1:
        halves = [(slice(None), slice(s * tt // 2, (s + 1) * tt // 2)) for s in range(2)]
        nbh, tth = nb, tt // 2
    else:
        halves = [(slice(s * nb // 2, (s + 1) * nb // 2), slice(None)) for s in range(2)]
        nbh, tth = nb // 2, tt
    m = nbh * tth
    merged = []
    for bs, ts in halves:
        a = _dot(ya_ref[bs, ts, :].reshape(m, W_A), wa_ref[...])
        b = _dot(yb_ref[bs, ts, :].reshape(m, W_BV), wb_ref[...])
        ga = _sigmoid(ga_ref[bs, ts, :].reshape(m, D_MODEL).astype(F32))
        gb = _sigmoid(gb_ref[bs, ts, :].reshape(m, D_MODEL).astype(F32))
        merged.append((ga * a + gb * b).astype(BF16))
    for (bs, ts), mg in zip(halves, merged):
        mo = _dot(mg, wo_ref[...])
        n = ((mo * _rms_scale(mo)) * gp_ref[...]).reshape(nbh, tth, D_MODEL)
        o_ref[bs, ts, :] = x_ref[bs, ts, :] + gm_ref[bs] * n


def _mixout(x, ya, yb, ga, gb, mod, g_post, wa, wb, wo, nb, tt):
    nbt, t, _ = x.shape
    tok = lambda w: pl.BlockSpec((nb, tt, w), lambda b, i: (b, i, 0))
    return pl.pallas_call(
        _mixout_kernel,
        grid=(nbt // nb, t // tt),
        in_specs=[tok(D_MODEL), tok(W_A), tok(W_BV), tok(D_MODEL), tok(D_MODEL),
                  pl.BlockSpec((nb, 1, D_MODEL), lambda b, i: (b, 0, 2)),
                  _full_spec((1, D_MODEL)), _full_spec(wa.shape), _full_spec(wb.shape),
                  _full_spec(wo.shape)],
        out_specs=tok(D_MODEL),
        out_shape=jax.ShapeDtypeStruct(x.shape, F32),
        compiler_params=_params(("parallel", "parallel")),
        name="mixout",
    )(x, ya, yb, ga, gb, mod, g_post, wa, wb, wo)


def _gelu_tanh(x):
    c = float(np.sqrt(2.0 / np.pi))
    half = 0.5 * x
    return half + half * jnp.tanh(x * (c + (0.044715 * c) * (x * x)))


def _ffn_kernel(*refs, has_state):
    if has_state:
        (x_ref, shift_ref, scale_ref, gate_ref, gpre_ref, gpost_ref, wu_ref, wd_ref,
         wdw_ref, bdw_ref, prev_ref, o_ref, tail_ref, h_ref, act_ref) = refs
    else:
        (x_ref, shift_ref, scale_ref, gate_ref, gpre_ref, gpost_ref, wu_ref, wd_ref,
         wdw_ref, bdw_ref, o_ref, tail_ref, h_ref, act_ref, prev_ref) = refs

        @pl.when(pl.program_id(1) == 0)
        def _():
            prev_ref[...] = jnp.zeros_like(prev_ref)

    nb, tt, _ = x_ref.shape
    m = nb * tt
    x = x_ref[...]
    h = (x * _rms_scale(x)) * (gpre_ref[...] * (1.0 + scale_ref[...])) + shift_ref[...]
    h_ref[...] = h.reshape(m, D_MODEL).astype(BF16)
    ridx = lax.broadcasted_iota(jnp.int32, (nb, SUBLANES, FFN_FT), 1)

    def conv(u, lanes):
        u3 = u.reshape(nb, tt, FFN_FT)
        r1 = pltpu.roll(u, 1, 0).reshape(nb, tt, FFN_FT)
        r2 = pltpu.roll(u, 2, 0).reshape(nb, tt, FFN_FT)

        def taps(u_m2, u_m1, u_0):
            y = bdw_ref[:, lanes] + wdw_ref[0:1, lanes] * u_m2
            y = y + wdw_ref[1:2, lanes] * u_m1
            return y + wdw_ref[2:3, lanes] * u_0

        p2, p1 = prev_ref[:, 0:1, lanes], prev_ref[:, 1:2, lanes]
        h_m1 = jnp.where(ridx == 0, p1, r1[:, :SUBLANES])
        h_m2 = jnp.where(ridx == 0, p2, jnp.where(ridx == 1, p1, r2[:, :SUBLANES]))
        y = jnp.concatenate([taps(h_m2, h_m1, u3[:, :SUBLANES]),
                             taps(r2[:, SUBLANES:], r1[:, SUBLANES:], u3[:, SUBLANES:])], axis=1)
        tail = u3[:, tt - (CONV_W - 1):, :]
        tail_ref[:, :, lanes] = tail
        if not has_state:
            prev_ref[:, :, lanes] = tail
        return y.reshape(m, FFN_FT)

    def up(f):
        hb = h_ref[...]
        lo = f * FFN_FT
        return (_dot(hb, wu_ref[:, lo:lo + FFN_FT]),
                _dot(hb, wu_ref[:, D_FF + lo:D_FF + lo + FFN_FT]))

    ua, ug = up(0)
    for f in range(FFN_NF):
        if f + 1 < FFN_NF:
            ua_next, ug_next = up(f + 1)
        cols = slice(f * FFN_FT, (f + 1) * FFN_FT)
        ya = conv(ua, cols)
        yg = conv(ug, slice(D_FF + f * FFN_FT, D_FF + (f + 1) * FFN_FT))
        act_ref[:, cols] = (_gelu_tanh(ya) * yg).astype(BF16)
        if f + 1 < FFN_NF:
            ua, ug = ua_next, ug_next
    yf = _dot(act_ref[...], wd_ref[...])
    n = ((yf * _rms_scale(yf)) * gpost_ref[...]).reshape(nb, tt, D_MODEL)
    o_ref[...] = x_ref[...] + gate_ref[...] * n


def _ffn(x, mod, g_pre, g_post, wu, wd, w_dw, b_dw, state, nb, tt):
    nbt, t, _ = x.shape
    m = nb * tt
    tok = pl.BlockSpec((nb, tt, D_MODEL), lambda b, i: (b, i, 0))
    mod_spec = lambda col: pl.BlockSpec((nb, 1, D_MODEL), lambda b, i, col=col: (b, 0, col))
    tail_spec = pl.BlockSpec((nb, CONV_W - 1, 2 * D_FF), lambda b, i: (b, 0, 0))
    in_specs = [tok, mod_spec(3), mod_spec(4), mod_spec(5), _full_spec((1, D_MODEL)),
                _full_spec((1, D_MODEL)), _full_spec(wu.shape),
                _full_spec(wd.shape), _full_spec(w_dw.shape), _full_spec(b_dw.shape)]
    args = [x, mod, mod, mod, g_pre, g_post, wu, wd, w_dw, b_dw]
    scratch = [pltpu.VMEM((m, D_MODEL), BF16), pltpu.VMEM((m, D_FF), BF16)]
    if state is not None:
        in_specs.append(tail_spec)
        args.append(state)
    else:
        scratch.append(pltpu.VMEM((nb, CONV_W - 1, 2 * D_FF), F32))
    return pl.pallas_call(
        functools.partial(_ffn_kernel, has_state=state is not None),
        grid=(nbt // nb, t // tt),
        in_specs=in_specs,
        out_specs=[tok, tail_spec],
        out_shape=[jax.ShapeDtypeStruct(x.shape, F32),
                   jax.ShapeDtypeStruct((nbt, CONV_W - 1, 2 * D_FF), F32)],
        scratch_shapes=scratch,
        compiler_params=_params(("parallel", "arbitrary")),
        name="ffn",
    )(*args)


def _bias_rows(rel_bias):
    assert BIAS_ROW >= ATTN_KB + ATTN_QB - 1 and WINDOW_A == ATTN_KB - ATTN_QB
    far_pos = jnp.broadcast_to(rel_bias[:, -1:], (H_A, BIAS_ROW))
    far_neg = jnp.broadcast_to(rel_bias[:, :1], (H_A, BIAS_ROW))
    n_mid = 2 * MAX_REL + 1
    n_lo = ATTN_KB + 1 - (WINDOW_A - MAX_REL) - n_mid
    rows = jnp.concatenate([far_pos[:, :WINDOW_A - MAX_REL], rel_bias[:, ::-1], far_neg[:, :n_lo],
                            far_pos[:, :BIAS_ROW - ATTN_KB - 1]], axis=1)
    return rows.astype(F32).reshape(H_A, 1, BIAS_ROW)


_LATE_WEIGHTS = ('w_br_a', 'w_br_b', 'w_out', 'w_up', 'w_down')


def _layer(x, mod, cache, s_gla, s_conv, w, first_chunk):
    nbt, t, _ = x.shape
    if first_chunk:
        nb, tt = 1, 1024
        nb_f, tt_f = 1, 1024
    else:
        nb, tt = 1024 // t, t
        nb_f, tt_f = 1024 // t, t
    keep_rows = min(WINDOW_A, t) if first_chunk else 0
    cast_ws = () if _LATE_WEIGHTS[0] in w else tuple(w[k + '_f32'] for k in _LATE_WEIGHTS)
    qa, ka, va, qb, kb, vb, gb, gate_a, gate_b, log_a, *extra = _inproj(
        x, mod, w['g_pre_mix'], w['w_main'], w['w_gk1'], w['w_gk2'], w['b_gk'], nb, tt, keep_rows,
        cast_ws)
    kv_t = extra[:2] if keep_rows else []
    if cast_ws:
        w = {**w, **dict(zip(_LATE_WEIGHTS, extra[len(kv_t):]))}
    if first_chunk:
        ya = _attn_prompt(qa, ka, va, w['bias_rows'])
        k_keep, v_keep = (jnp.transpose(a.reshape(nbt, H_A, HD_A, keep_rows), (0, 3, 1, 2))
                          for a in kv_t)
        yb, s_new = _gla(qb, kb, vb, gb, log_a, w['g_gla'], None, nbt, CHUNK, 4)
    else:
        k_cache, v_cache = cache
        ya = _attn_sample(qa, ka, va, k_cache, v_cache, w['bias_rows'], 4)
        k_keep, v_keep = (a.astype(F32).reshape(nbt, t, H_A, HD_A) for a in (ka, va))
        yb, s_new = _gla(qb, kb, vb, gb, log_a, w['g_gla'], s_gla, 4, t, 1)
    x1 = _mixout(x, ya, yb, gate_a, gate_b, mod, w['g_post_mix'], w['w_br_a'], w['w_br_b'],
                 w['w_out'], nb, tt)
    y, tail = _ffn(x1, mod, w['g_pre_ffn'], w['g_post_ffn'], w['w_up'], w['w_down'],
                   w['w_dw'], w['b_dw'], None if first_chunk else s_conv, nb_f, tt_f)
    return (y, k_keep, v_keep, s_new.reshape(nbt, H_B, DK_B, DV_B), tail), w


def _prep_weights(w_main, w_gk1, w_gk2, b_gk, rel_bias, g_gla, w_br_a, w_br_b, w_out, w_up, w_dw,
                  b_dw, w_down, g_pre_mix, g_post_mix, g_pre_ffn, g_post_ffn):
    w_gk2p = jnp.pad(w_gk2, ((0, LANES - GK_RANK), (0, 0))).astype(BF16)
    row = lambda a: a.reshape(1, -1)
    return {
        'w_main': w_main, 'w_gk1': w_gk1, 'w_gk2': w_gk2p, 'b_gk': row(b_gk),
        'bias_rows': _bias_rows(rel_bias), 'g_gla': row(g_gla),
        'w_br_a_f32': w_br_a, 'w_br_b_f32': w_br_b, 'w_out_f32': w_out, 'w_up_f32': w_up,
        'w_down_f32': w_down, 'w_dw': w_dw, 'b_dw': row(b_dw),
        'g_pre_mix': row(g_pre_mix), 'g_post_mix': row(g_post_mix),
        'g_pre_ffn': row(g_pre_ffn), 'g_post_ffn': row(g_post_ffn),
    }


def kernel(x_prompt, x_sample, cache_k_a, cache_v_a, state_gla, state_conv, c_prompt, c_sample, w_ada, b_ada, g_pre_mix, g_post_mix, g_pre_ffn, g_post_ffn, w_in, w_gk2, b_gk, rel_bias, g_gla, w_br_a, w_br_b, w_out, w_up, w_dw, b_dw, w_down):
    depth = w_ada.shape[0]
    assert depth == 1
    bp, bs = x_prompt.shape[0], x_sample.shape[0]
    cache_rows = cache_k_a.shape[2]
    yp, ys = x_prompt, x_sample
    outs = [[] for _ in range(8)]
    for l in range(depth):
        c_all = jnp.concatenate([c_prompt, c_sample], axis=0)
        pad = (-c_all.shape[0]) % SUBLANES
        mod, w_main, w_gk1 = _prep(jnp.pad(c_all, ((0, pad), (0, 0))), w_ada[l], b_ada[l], w_in[l].T)
        w = _prep_weights(w_main, w_gk1, w_gk2[l], b_gk[l], rel_bias[l], g_gla[l], w_br_a[l],
                          w_br_b[l], w_out[l], w_up[l], w_dw[l], b_dw[l], w_down[l], g_pre_mix[l],
                          g_post_mix[l], g_pre_ffn[l], g_post_ffn[l])
        mod_p = mod[:bp].reshape(bp, 1, 6 * D_MODEL)
        mod_s = mod[bp:bp + bs].reshape(bs, 1, 6 * D_MODEL)
        (yp, kp, vp, gp, cp), w = _layer(yp, mod_p, None, None, None, w, True)
        to_t = lambda c: jnp.transpose(c, (0, 2, 3, 1)).reshape(bs, W_A, cache_rows)
        cache = (to_t(cache_k_a[l]), to_t(cache_v_a[l]))
        s0 = state_gla[l].reshape(bs, H_B // 2, 2 * DK_B, DV_B)
        (ys, kn, vn, gn, cn), _ = _layer(ys, mod_s, cache, s0, state_conv[l], w, False)
        for lst, a in zip(outs, (kp, vp, gp, cp, kn, vn, gn, cn)):
            lst.append(a)
    return (yp, ys) + tuple(jnp.stack(lst) for lst in outs)
```

```python
import functools

import jax
import jax.numpy as jnp
import numpy as np
from jax import lax
from jax.experimental import pallas as pl
from jax.experimental.pallas import tpu as pltpu

D_MODEL = 1024
CHUNK = 64
BAND_CHUNKS = 8
WINDOW_A = BAND_CHUNKS * CHUNK
H_A = 8
HD_A = 64
MAX_REL = 128
H_B = 4
DK_B = 64
DV_B = 128
GK_RANK = 16
GK_NORM = 16.0
GLA_SUB = 16
D_FF = 2816
CONV_W = 3
EPS = 1e-6
NEG_INF = -1e30

W_A = H_A * HD_A
W_BK = H_B * DK_B
W_BV = H_B * DV_B

LANES = 128
SUBLANES = 8
VMEM_LIMIT = 60 * 1024 * 1024

ATTN_QB = 256
ATTN_KB = 3 * ATTN_QB
BIAS_ROW = 1024
FFN_FT = 256
FFN_NF = D_FF // FFN_FT

BF16 = jnp.bfloat16
F32 = jnp.float32


def _params(sem):
    return pltpu.CompilerParams(dimension_semantics=sem, vmem_limit_bytes=VMEM_LIMIT)


def _full_spec(shape):
    nd = len(shape)
    return pl.BlockSpec(shape, lambda *_: (0,) * nd, pipeline_mode=pl.Buffered(1))


def _dot(a, b):
    return jnp.dot(a, b, preferred_element_type=F32)


def _dot_nt(a, b):
    return lax.dot_general(a, b, (((1,), (1,)), ((), ())), preferred_element_type=F32)


def _dot_tn(a, b):
    return lax.dot_general(a, b, (((0,), (0,)), ((), ())), preferred_element_type=F32)


def _sigmoid(x):
    return 1.0 / (1.0 + jnp.exp(-x))


def _rms_scale(x):
    return lax.rsqrt(jnp.mean(x * x, axis=-1, keepdims=True) + EPS)


_GK_LO = 3 * W_A + 2 * W_BK + 2 * W_BV


PREP_STEPS = 4


def _prep_kernel(c_ref, wa_ref, ba_ref, wi_ref, mod_ref, main_ref, gk_ref):
    c = c_ref[...]
    s = (c * _sigmoid(c)).astype(BF16)
    mod_ref[...] = _dot(s, wa_ref[...].astype(BF16)) + ba_ref[...]
    main_ref[:_GK_LO, :] = wi_ref[:_GK_LO, :].astype(BF16)
    main_ref[_GK_LO:, :] = wi_ref[_GK_LO + GK_RANK:, :].astype(BF16)
    gk_ref[:GK_RANK, :] = wi_ref[_GK_LO:_GK_LO + GK_RANK, :].astype(BF16)
    gk_ref[GK_RANK:, :] = jnp.zeros((LANES - GK_RANK, gk_ref.shape[1]), BF16)


def _prep(c_all, w_ada, b_ada, w_in_t):
    rows = c_all.shape[0]
    n = w_ada.shape[1]
    d_in, d = w_in_t.shape
    tn, cols = n // PREP_STEPS, d // PREP_STEPS
    return pl.pallas_call(
        _prep_kernel,
        grid=(PREP_STEPS,),
        in_specs=[pl.BlockSpec((rows, D_MODEL), lambda j: (0, 0)),
                  pl.BlockSpec((D_MODEL, tn), lambda j: (0, j)),
                  pl.BlockSpec((1, tn), lambda j: (0, j)),
                  pl.BlockSpec((d_in, cols), lambda j: (0, j))],
        out_specs=[pl.BlockSpec((rows, tn), lambda j: (0, j)),
                   pl.BlockSpec((d_in - GK_RANK, cols), lambda j: (0, j)),
                   pl.BlockSpec((LANES, cols), lambda j: (0, j))],
        out_shape=[jax.ShapeDtypeStruct((rows, n), F32),
                   jax.ShapeDtypeStruct((d_in - GK_RANK, d), BF16),
                   jax.ShapeDtypeStruct((LANES, d), BF16)],
        compiler_params=_params(("parallel",)),
        name="prep",
    )(c_all, w_ada, b_ada.reshape(1, n), w_in_t)


_IN_GROUPS = (W_A, W_A, W_A, W_BK, W_BK, W_BV, W_BV, D_MODEL, D_MODEL)


def _inproj_kernel(*refs, n_cast, n_kv_t):
    (x_ref, shift_ref, scale_ref, g_ref, wm_ref, wg1_ref, wg2_ref, bgk_ref), refs = refs[:8], refs[8:]
    cast_in, refs = refs[:n_cast], refs[n_cast:]
    (qa_ref, ka_ref, va_ref, qb_ref, kb_ref, vb_ref, gb_ref, ga_ref, gtb_ref, la_ref) = refs[:10]
    kv_t_refs, cast_out = refs[10:10 + n_kv_t], refs[10 + n_kv_t:]
    for w_ref, o_ref in zip(cast_in, cast_out):
        o_ref[...] = w_ref[...].astype(o_ref.dtype)
    nb, tt, _ = x_ref.shape
    x = x_ref[...]
    h = (x * _rms_scale(x)) * (g_ref[...] * (1.0 + scale_ref[...])) + shift_ref[...]
    hb = h.reshape(nb * tt, D_MODEL).astype(BF16)
    if kv_t_refs:
        @pl.when(pl.program_id(1) == pl.num_programs(1) - 1)
        def _():
            rows = kv_t_refs[0].shape[2]
            newest = hb[nb * tt - rows:, :]
            for o_ref, lo in zip(kv_t_refs, (W_A, 2 * W_A)):
                o_ref[0] = _dot_nt(wm_ref[lo:lo + W_A, :], newest)
    outs = (qa_ref, ka_ref, va_ref, qb_ref, kb_ref, vb_ref, gb_ref, ga_ref, gtb_ref)
    lo = 0
    for o_ref, w in zip(outs, _IN_GROUPS):
        z = _dot_nt(hb, wm_ref[lo:lo + w, :])
        o_ref[...] = z.reshape(nb, tt, w).astype(o_ref.dtype)
        lo += w
    gk_low = _dot_nt(hb, wg1_ref[...]).astype(BF16)
    gk = _dot(gk_low, wg2_ref[...]) + bgk_ref[...]
    log_a = (jnp.minimum(gk, 0.0) - jnp.log1p(jnp.exp(-jnp.abs(gk)))) / GK_NORM
    la_ref[...] = log_a.reshape(nb, tt, W_BK)


def _inproj(x, mod, g_pre, wm, wg1, wg2, bgk, nb, tt, keep_rows, cast_ws=()):
    nbt, t, _ = x.shape
    grid = (nbt // nb, t // tt)
    steps = grid[0] * grid[1]
    cast_specs = [pl.BlockSpec((w.shape[0] // steps, w.shape[1]),
                               lambda b, i: (b * grid[1] + i, 0)) for w in cast_ws]
    tok = lambda w: pl.BlockSpec((nb, tt, w), lambda b, i: (b, i, 0))
    mod_spec = lambda col: pl.BlockSpec((nb, 1, D_MODEL), lambda b, i, col=col: (b, 0, col))
    widths = _IN_GROUPS + (W_BK,)
    dtypes = (BF16, BF16, BF16, F32, F32, BF16, BF16, BF16, BF16, F32)
    out_specs = [tok(w) for w in widths]
    out_shape = [jax.ShapeDtypeStruct((nbt, t, w), dt) for w, dt in zip(widths, dtypes)]
    if keep_rows:
        assert nb == 1 and keep_rows <= tt
        out_specs += [pl.BlockSpec((1, W_A, keep_rows), lambda b, i: (b, 0, 0))] * 2
        out_shape += [jax.ShapeDtypeStruct((nbt, W_A, keep_rows), F32)] * 2
    return pl.pallas_call(
        functools.partial(_inproj_kernel, n_cast=len(cast_ws), n_kv_t=2 if keep_rows else 0),
        grid=grid,
        in_specs=[tok(D_MODEL), mod_spec(0), mod_spec(1), _full_spec((1, D_MODEL)),
                  _full_spec(wm.shape), _full_spec(wg1.shape), _full_spec(wg2.shape),
                  _full_spec((1, W_BK))] + cast_specs,
        out_specs=out_specs + cast_specs,
        out_shape=out_shape + [jax.ShapeDtypeStruct(w.shape, BF16) for w in cast_ws],
        compiler_params=_params(("arbitrary", "arbitrary")),
        name="inproj",
    )(x, mod, mod, g_pre, wm, wg1, wg2, bgk, *cast_ws)


def _head_masks():
    lane = lax.broadcasted_iota(jnp.int32, (1, LANES), 1)
    first = lane < HD_A
    return first, jnp.logical_not(first)


def _toeplitz_bias(row_ref, h, rows):
    rb = jnp.broadcast_to(row_ref[h], (rows, BIAS_ROW))
    return pltpu.roll(rb, 0, 1, stride=1, stride_axis=0)


ATTN_RB = 32
ATTN_WIN = 640


ATTN_SUB = 4


def _attn_prompt_kernel(*refs):
    q_ref, refs = refs[0], refs[1:]
    k_refs, v_refs = refs[:ATTN_SUB + 2], refs[ATTN_SUB + 2:2 * ATTN_SUB + 4]
    row_ref, o_ref, bias_ref, s_ref, p_ref = refs[2 * ATTN_SUB + 4:]
    i = pl.program_id(1)

    @pl.when((pl.program_id(0) == 0) & (i == 0))
    def _():
        qc = lax.broadcasted_iota(jnp.int32, (ATTN_QB, ATTN_KB), 0) // CHUNK
        col = lax.broadcasted_iota(jnp.int32, (ATTN_QB, ATTN_KB), 1)
        kc = col // CHUNK - BAND_CHUNKS
        valid = (kc <= qc) & (kc >= qc - BAND_CHUNKS)
        for h in range(H_A):
            t = _toeplitz_bias(row_ref, h, ATTN_QB)
            band = jnp.where(valid, t[:, :ATTN_KB], NEG_INF)
            bias_ref[0, h] = jnp.where(col >= 2 * ATTN_QB, band, NEG_INF)
            bias_ref[1, h] = jnp.where(col >= ATTN_QB, band, NEG_INF)
            bias_ref[2, h] = band
        p_ref[...] = jnp.zeros_like(p_ref)

    masks = _head_masks()

    def pair_rows(refs, p):
        sl = slice(p * LANES, (p + 1) * LANES)
        return jnp.concatenate([r[0, :, sl] for r in refs], axis=0)

    units = [(sub, h) for sub in range(ATTN_SUB) for h in range(H_A)]

    def scores(u):
        sub, h = units[u]
        p, hh = divmod(h, 2)
        qp = q_ref[0, sub * ATTN_QB:(sub + 1) * ATTN_QB, p * LANES:(p + 1) * LANES] * BF16(HD_A ** -0.5)
        qm = jnp.where(masks[hh], qp, jnp.zeros_like(qp))
        s_ref[u % 2] = _dot_nt(qm, pair_rows(k_refs[sub:sub + 3], p))

    scores(0)
    o_first = None
    for u, (sub, h) in enumerate(units):
        if u + 1 < len(units):
            scores(u + 1)
        slot = u % 2
        var = jnp.minimum(ATTN_SUB * i + sub, 2)
        sums = []
        for r in range(ATTN_QB // ATTN_RB):
            rows = slice(r * ATTN_RB, (r + 1) * ATTN_RB)
            lo = 0 if r * ATTN_RB < ATTN_QB // 2 else ATTN_KB - ATTN_WIN
            sb = s_ref[slot, rows, lo:lo + ATTN_WIN] + bias_ref[var, h, rows, lo:lo + ATTN_WIN]
            e = jnp.exp(sb - jnp.max(sb, axis=-1, keepdims=True))
            sums.append(jnp.sum(e, axis=-1, keepdims=True))
            p_ref[slot, rows, lo:lo + ATTN_WIN] = e.astype(BF16)
        p, hh = divmod(h, 2)
        o = _dot(p_ref[slot], pair_rows(v_refs[sub:sub + 3], p)) / jnp.concatenate(sums, axis=0)
        if hh == 0:
            o_first = o
        else:
            o_ref[0, sub * ATTN_QB:(sub + 1) * ATTN_QB, p * LANES:(p + 1) * LANES] = jnp.where(
                masks[0], o_first, o).astype(o_ref.dtype)


def _attn_prompt(q, k, v, bias_rows):
    b, t, _ = q.shape
    tok = pl.BlockSpec((1, ATTN_SUB * ATTN_QB, W_A), lambda bb, i: (bb, i, 0))
    blk = lambda j: pl.BlockSpec(
        (1, ATTN_QB, W_A), lambda bb, i, j=j: (bb, jnp.maximum(ATTN_SUB * i + j - 2, 0), 0))
    kv_specs = [blk(j) for j in range(ATTN_SUB + 2)]
    return pl.pallas_call(
        _attn_prompt_kernel,
        grid=(b, t // (ATTN_SUB * ATTN_QB)),
        in_specs=[tok] + kv_specs + kv_specs + [_full_spec(bias_rows.shape)],
        out_specs=tok,
        out_shape=jax.ShapeDtypeStruct((b, t, W_A), BF16),
        scratch_shapes=[pltpu.VMEM((3, H_A, ATTN_QB, ATTN_KB), F32),
                        pltpu.VMEM((2, ATTN_QB, ATTN_KB), F32),
                        pltpu.VMEM((2, ATTN_QB, ATTN_KB), BF16)],
        compiler_params=_params(("arbitrary", "arbitrary")),
        name="attn_prompt",
    )(q, *([k] * (ATTN_SUB + 2)), *([v] * (ATTN_SUB + 2)), bias_rows)


def _attn_sample_kernel(q_ref, kn_ref, vn_ref, kc_ref, vc_ref, row_ref, o_ref, bc_ref, bn_ref):
    nb, s_len, _ = q_ref.shape
    w = kc_ref.shape[2]

    @pl.when(pl.program_id(0) == 0)
    def _():
        for h in range(H_A):
            t = _toeplitz_bias(row_ref, h, s_len)
            bc_ref[h * s_len:(h + 1) * s_len, :] = t[:, :w]
            bn_ref[h * s_len:(h + 1) * s_len, :] = t[:, w:w + s_len]

    lane_head = lax.broadcasted_iota(jnp.int32, (1, W_A), 1) // HD_A
    row_head = lax.broadcasted_iota(jnp.int32, (H_A * s_len, 1), 0) // s_len
    own_head = row_head == lane_head
    for b in range(nb):
        q = q_ref[b] * BF16(HD_A ** -0.5)
        qs = jnp.concatenate([q] * H_A, axis=0)
        qs = jnp.where(own_head, qs, jnp.zeros_like(qs))
        sc = _dot(qs, kc_ref[b].astype(BF16)) + bc_ref[...]
        sn = _dot_nt(qs, kn_ref[b]) + bn_ref[...]
        m = jnp.maximum(jnp.max(sc, axis=-1, keepdims=True), jnp.max(sn, axis=-1, keepdims=True))
        ec = jnp.exp(sc - m)
        en = jnp.exp(sn - m)
        l = jnp.sum(ec, axis=-1, keepdims=True) + jnp.sum(en, axis=-1, keepdims=True)
        full = (_dot_nt(ec.astype(BF16), vc_ref[b].astype(BF16))
                + _dot(en.astype(BF16), vn_ref[b])) / l
        o = full[:s_len]
        for h in range(1, H_A):
            o = jnp.where(lane_head == h, full[h * s_len:(h + 1) * s_len], o)
        o_ref[b] = o.astype(o_ref.dtype)


def _attn_sample(q, kn, vn, kc, vc, bias_rows, nb):
    b, s, _ = q.shape
    w = kc.shape[2]
    assert w == WINDOW_A
    new = pl.BlockSpec((nb, s, W_A), lambda i: (i, 0, 0))
    cache = pl.BlockSpec((nb, W_A, w), lambda i: (i, 0, 0))
    return pl.pallas_call(
        _attn_sample_kernel,
        grid=(b // nb,),
        in_specs=[new, new, new, cache, cache, _full_spec(bias_rows.shape)],
        out_specs=new,
        out_shape=jax.ShapeDtypeStruct((b, s, W_A), BF16),
        scratch_shapes=[pltpu.VMEM((H_A * s, w), F32), pltpu.VMEM((H_A * s, s), F32)],
        compiler_params=_params(("arbitrary",)),
        name="attn_sample",
    )(q, kn, vn, kc, vc, bias_rows)


def _gla_kernel(*refs, has_init, c):
    if has_init:
        q_ref, k_ref, v_ref, g_ref, la_ref, gg_ref, s0_ref, y_ref, so_ref, st_ref = refs
    else:
        q_ref, k_ref, v_ref, g_ref, la_ref, gg_ref, y_ref, so_ref, st_ref = refs
    nb, blk, _ = q_ref.shape
    nsub = c // GLA_SUB
    npair = H_B // 2
    j = pl.program_id(1)

    @pl.when(j == 0)
    def _():
        if has_init:
            st_ref[...] = s0_ref[...]
        else:
            st_ref[...] = jnp.zeros_like(st_ref)

    r2 = lax.broadcasted_iota(jnp.int32, (2 * c, c), 0)
    s2 = lax.broadcasted_iota(jnp.int32, (2 * c, c), 1)
    t2 = r2 & (c - 1)
    same_sub = (s2 // GLA_SUB) == (t2 // GLA_SUB)
    sum_mat = jnp.where((s2 <= t2) & ((r2 < c) | same_sub), 1.0, 0.0).astype(BF16)

    row = lax.broadcasted_iota(jnp.int32, (c, W_BK), 0)
    row2 = lax.broadcasted_iota(jnp.int32, (2 * c, LANES), 0)
    lane2 = lax.broadcasted_iota(jnp.int32, (2 * c, LANES), 1)
    own = (lane2 < DK_B) == (row2 < c)
    sub_of_row2 = (row2 & (c - 1)) // GLA_SUB
    tril2 = lane2 <= (row2 & (c - 1))
    eye = (lax.broadcasted_iota(jnp.int32, (LANES, LANES), 0)
           == lax.broadcasted_iota(jnp.int32, (LANES, LANES), 1))
    scale = DK_B ** -0.5
    zeros_k = jnp.zeros((LANES - c, nsub * LANES), BF16)
    zeros_v = jnp.zeros((LANES - c, DV_B), BF16)

    for ci in range(blk // c):
        rows = slice(ci * c, (ci + 1) * c)
        prep = []
        for b in range(nb):
            la = la_ref[b, rows]
            la_hi = la.astype(BF16)
            rest = la - la_hi.astype(F32)
            la_mid = rest.astype(BF16)
            la_lo = (rest - la_mid.astype(F32)).astype(BF16)
            sums = _dot(sum_mat, la_hi) + _dot(sum_mat, la_mid) + _dot(sum_mat, la_lo)
            cum, cum_sub = sums[:c], sums[c:]
            cum_end = cum[c - 1:c, :]
            qf = q_ref[b, rows]
            kf = k_ref[b, rows]
            q_sub = qf * jnp.exp(cum_sub) * scale
            q_in = (qf * jnp.exp(cum) * scale).astype(BF16)
            k_end = (kf * jnp.exp(cum_end - cum)).astype(BF16)
            k_sub = []
            for i in range(nsub):
                ref_i = cum[i * GLA_SUB - 1:i * GLA_SUB, :] if i else jnp.zeros((1, W_BK), F32)
                k_i = jnp.where(row < (i + 1) * GLA_SUB, kf * jnp.exp(ref_i - cum), 0.0)
                k_sub.append(k_i.astype(BF16))
            prep.append((q_sub, q_in, k_end, k_sub, cum_end))

        att = {}
        for b in range(nb):
            q_sub, _, _, k_sub, _ = prep[b]
            for p in range(npair):
                sl = slice(p * LANES, (p + 1) * LANES)
                k_stack = jnp.concatenate(
                    [jnp.concatenate([k_i[:, sl] for k_i in k_sub], axis=1), zeros_k], axis=0)
                q2 = jnp.where(own, jnp.concatenate([q_sub[:, sl], q_sub[:, sl]], axis=0), 0.0)
                q_stack = jnp.concatenate(
                    [jnp.where(sub_of_row2 == i, q2, 0.0) for i in range(nsub)], axis=1).astype(BF16)
                a = _dot_nt(q_stack, k_stack)
                att[b, p] = jnp.where(tril2, a, 0.0).astype(BF16)

        for b in range(nb):
            q_in = prep[b][1]
            for p in range(npair):
                sl = slice(p * LANES, (p + 1) * LANES)
                st_b = st_ref[b, p].astype(BF16)
                q_in2 = jnp.where(own, jnp.concatenate([q_in[:, sl], q_in[:, sl]], axis=0),
                                  jnp.zeros((), BF16))
                for hh in range(2):
                    h = 2 * p + hh
                    hs = slice(h * DV_B, (h + 1) * DV_B)
                    hrows = slice(hh * c, (hh + 1) * c)
                    lhs = jnp.concatenate([att[b, p][hrows], q_in2[hrows]], axis=1)
                    rhs = jnp.concatenate([v_ref[b, rows, hs], zeros_v, st_b], axis=0)
                    o = _dot(lhs, rhs)
                    gate = g_ref[b, rows, hs].astype(F32)
                    y = (o * _rms_scale(o)) * gg_ref[...] * (gate * _sigmoid(gate))
                    y_ref[b, rows, hs] = y.astype(y_ref.dtype)

        for b in range(nb):
            k_end, cum_end = prep[b][2], prep[b][4]
            for p in range(npair):
                sl = slice(p * LANES, (p + 1) * LANES)
                k2 = jnp.where(own, jnp.concatenate([k_end[:, sl], k_end[:, sl]], axis=0),
                               jnp.zeros((), BF16))
                v2 = jnp.concatenate([v_ref[b, rows, 2 * p * DV_B:(2 * p + 1) * DV_B],
                                      v_ref[b, rows, (2 * p + 1) * DV_B:(2 * p + 2) * DV_B]], axis=0)
                upd = _dot_tn(k2, v2)
                dec = jnp.exp(jnp.sum(jnp.where(eye, cum_end[:, sl], 0.0), axis=1, keepdims=True))
                st_ref[b, p] = st_ref[b, p] * dec + upd

    @pl.when(j == pl.num_programs(1) - 1)
    def _():
        so_ref[...] = st_ref[...]


def _gla(q, k, v, g, la, g_gla, s0, nb, c, chunks_per_step):
    nbt, t, _ = q.shape
    blk = c * chunks_per_step
    tok = lambda w: pl.BlockSpec((nb, blk, w), lambda b, j: (b, j, 0))
    st_spec = pl.BlockSpec((nb, H_B // 2, 2 * DK_B, DV_B), lambda b, j: (b, 0, 0, 0))
    in_specs = [tok(W_BK), tok(W_BK), tok(W_BV), tok(W_BV), tok(W_BK), _full_spec((1, DV_B))]
    args = [q, k, v, g, la, g_gla]
    if s0 is not None:
        in_specs.append(st_spec)
        args.append(s0)
    return pl.pallas_call(
        functools.partial(_gla_kernel, has_init=s0 is not None, c=c),
        grid=(nbt // nb, t // blk),
        in_specs=in_specs,
        out_specs=[tok(W_BV), st_spec],
        out_shape=[jax.ShapeDtypeStruct((nbt, t, W_BV), BF16),
                   jax.ShapeDtypeStruct((nbt, H_B // 2, 2 * DK_B, DV_B), F32)],
        scratch_shapes=[pltpu.VMEM((nb, H_B // 2, DV_B, 2 * DK_B), F32)],
        compiler_params=_params(("parallel", "arbitrary")),
        name="gla",
    )(*args)


def _mixout_kernel(x_ref, ya_ref, yb_ref, ga_ref, gb_ref, gm_ref, gp_ref, wa_ref, wb_ref, wo_ref,
                   o_ref):
    nb, tt, _ = x_ref.shape
    if nb == 1:
        halves = [(slice(None), slice(s * tt // 2, (s + 1) * tt // 2)) for s in range(2)]
        nbh, tth = nb, tt // 2
    else:
        halves = [(slice(s * nb // 2, (s + 1) * nb // 2), slice(None)) for s in range(2)]
        nbh, tth = nb // 2, tt
    m = nbh * tth
    merged = []
    for bs, ts in halves:
        a = _dot(ya_ref[bs, ts, :].reshape(m, W_A), wa_ref[...])
        b = _dot(yb_ref[bs, ts, :].reshape(m, W_BV), wb_ref[...])
        ga = _sigmoid(ga_ref[bs, ts, :].reshape(m, D_MODEL).astype(F32))
        gb = _sigmoid(gb_ref[bs, ts, :].reshape(m, D_MODEL).astype(F32))
        merged.append((ga * a + gb * b).astype(BF16))
    for (bs, ts), mg in zip(halves, merged):
        mo = _dot(mg, wo_ref[...])
        n = ((mo * _rms_scale(mo)) * gp_ref[...]).reshape(nbh, tth, D_MODEL)
        o_ref[bs, ts, :] = x_ref[bs, ts, :] + gm_ref[bs] * n


def _mixout(x, ya, yb, ga, gb, mod, g_post, wa, wb, wo, nb, tt):
    nbt, t, _ = x.shape
    tok = lambda w: pl.BlockSpec((nb, tt, w), lambda b, i: (b, i, 0))
    return pl.pallas_call(
        _mixout_kernel,
        grid=(nbt // nb, t // tt),
        in_specs=[tok(D_MODEL), tok(W_A), tok(W_BV), tok(D_MODEL), tok(D_MODEL),
                  pl.BlockSpec((nb, 1, D_MODEL), lambda b, i: (b, 0, 2)),
                  _full_spec((1, D_MODEL)), _full_spec(wa.shape), _full_spec(wb.shape),
                  _full_spec(wo.shape)],
        out_specs=tok(D_MODEL),
        out_shape=jax.ShapeDtypeStruct(x.shape, F32),
        compiler_params=_params(("parallel", "parallel")),
        name="mixout",
    )(x, ya, yb, ga, gb, mod, g_post, wa, wb, wo)


def _gelu_tanh(x):
    c = float(np.sqrt(2.0 / np.pi))
    half = 0.5 * x
    return half + half * jnp.tanh(x * (c + (0.044715 * c) * (x * x)))


def _ffn_kernel(*refs, has_state):
    if has_state:
        (x_ref, shift_ref, scale_ref, gate_ref, gpre_ref, gpost_ref, wu_ref, wd_ref,
         wdw_ref, bdw_ref, prev_ref, o_ref, tail_ref, h_ref, act_ref) = refs
    else:
        (x_ref, shift_ref, scale_ref, gate_ref, gpre_ref, gpost_ref, wu_ref, wd_ref,
         wdw_ref, bdw_ref, o_ref, tail_ref, h_ref, act_ref, prev_ref) = refs

        @pl.when(pl.program_id(1) == 0)
        def _():
            prev_ref[...] = jnp.zeros_like(prev_ref)

    nb, tt, _ = x_ref.shape
    m = nb * tt
    x = x_ref[...]
    h = (x * _rms_scale(x)) * (gpre_ref[...] * (1.0 + scale_ref[...])) + shift_ref[...]
    h_ref[...] = h.reshape(m, D_MODEL).astype(BF16)
    ridx = lax.broadcasted_iota(jnp.int32, (nb, SUBLANES, FFN_FT), 1)

    def conv(u, lanes):
        u3 = u.reshape(nb, tt, FFN_FT)
        r1 = pltpu.roll(u, 1, 0).reshape(nb, tt, FFN_FT)
        r2 = pltpu.roll(u, 2, 0).reshape(nb, tt, FFN_FT)

        def taps(u_m2, u_m1, u_0):
            y = bdw_ref[:, lanes] + wdw_ref[0:1, lanes] * u_m2
            y = y + wdw_ref[1:2, lanes] * u_m1
            return y + wdw_ref[2:3, lanes] * u_0

        p2, p1 = prev_ref[:, 0:1, lanes], prev_ref[:, 1:2, lanes]
        h_m1 = jnp.where(ridx == 0, p1, r1[:, :SUBLANES])
        h_m2 = jnp.where(ridx == 0, p2, jnp.where(ridx == 1, p1, r2[:, :SUBLANES]))
        y = jnp.concatenate([taps(h_m2, h_m1, u3[:, :SUBLANES]),
                             taps(r2[:, SUBLANES:], r1[:, SUBLANES:], u3[:, SUBLANES:])], axis=1)
        tail = u3[:, tt - (CONV_W - 1):, :]
        tail_ref[:, :, lanes] = tail
        if not has_state:
            prev_ref[:, :, lanes] = tail
        return y.reshape(m, FFN_FT)

    def up(f):
        hb = h_ref[...]
        lo = f * FFN_FT
        return (_dot(hb, wu_ref[:, lo:lo + FFN_FT]),
                _dot(hb, wu_ref[:, D_FF + lo:D_FF + lo + FFN_FT]))

    ua, ug = up(0)
    for f in range(FFN_NF):
        if f + 1 < FFN_NF:
            ua_next, ug_next = up(f + 1)
        cols = slice(f * FFN_FT, (f + 1) * FFN_FT)
        ya = conv(ua, cols)
        yg = conv(ug, slice(D_FF + f * FFN_FT, D_FF + (f + 1) * FFN_FT))
        act_ref[:, cols] = (_gelu_tanh(ya) * yg).astype(BF16)
        if f + 1 < FFN_NF:
            ua, ug = ua_next, ug_next
    yf = _dot(act_ref[...], wd_ref[...])
    n = ((yf * _rms_scale(yf)) * gpost_ref[...]).reshape(nb, tt, D_MODEL)
    o_ref[...] = x_ref[...] + gate_ref[...] * n


def _ffn(x, mod, g_pre, g_post, wu, wd, w_dw, b_dw, state, nb, tt):
    nbt, t, _ = x.shape
    m = nb * tt
    tok = pl.BlockSpec((nb, tt, D_MODEL), lambda b, i: (b, i, 0))
    mod_spec = lambda col: pl.BlockSpec((nb, 1, D_MODEL), lambda b, i, col=col: (b, 0, col))
    tail_spec = pl.BlockSpec((nb, CONV_W - 1, 2 * D_FF), lambda b, i: (b, 0, 0))
    in_specs = [tok, mod_spec(3), mod_spec(4), mod_spec(5), _full_spec((1, D_MODEL)),
                _full_spec((1, D_MODEL)), _full_spec(wu.shape),
                _full_spec(wd.shape), _full_spec(w_dw.shape), _full_spec(b_dw.shape)]
    args = [x, mod, mod, mod, g_pre, g_post, wu, wd, w_dw, b_dw]
    scratch = [pltpu.VMEM((m, D_MODEL), BF16), pltpu.VMEM((m, D_FF), BF16)]
    if state is not None:
        in_specs.append(tail_spec)
        args.append(state)
    else:
        scratch.append(pltpu.VMEM((nb, CONV_W - 1, 2 * D_FF), F32))
    return pl.pallas_call(
        functools.partial(_ffn_kernel, has_state=state is not None),
        grid=(nbt // nb, t // tt),
        in_specs=in_specs,
        out_specs=[tok, tail_spec],
        out_shape=[jax.ShapeDtypeStruct(x.shape, F32),
                   jax.ShapeDtypeStruct((nbt, CONV_W - 1, 2 * D_FF), F32)],
        scratch_shapes=scratch,
        compiler_params=_params(("parallel", "arbitrary")),
        name="ffn",
    )(*args)


def _bias_rows(rel_bias):
    assert BIAS_ROW >= ATTN_KB + ATTN_QB - 1 and WINDOW_A == ATTN_KB - ATTN_QB
    far_pos = jnp.broadcast_to(rel_bias[:, -1:], (H_A, BIAS_ROW))
    far_neg = jnp.broadcast_to(rel_bias[:, :1], (H_A, BIAS_ROW))
    n_mid = 2 * MAX_REL + 1
    n_lo = ATTN_KB + 1 - (WINDOW_A - MAX_REL) - n_mid
    rows = jnp.concatenate([far_pos[:, :WINDOW_A - MAX_REL], rel_bias[:, ::-1], far_neg[:, :n_lo],
                            far_pos[:, :BIAS_ROW - ATTN_KB - 1]], axis=1)
    return rows.astype(F32).reshape(H_A, 1, BIAS_ROW)


_LATE_WEIGHTS = ('w_br_a', 'w_br_b', 'w_out', 'w_up', 'w_down')


def _layer(x, mod, cache, s_gla, s_conv, w, first_chunk):
    nbt, t, _ = x.shape
    if first_chunk:
        nb, tt = 1, 1024
        nb_f, tt_f = 1, 1024
    else:
        nb, tt = 1024 // t, t
        nb_f, tt_f = 1024 // t, t
    keep_rows = min(WINDOW_A, t) if first_chunk else 0
    cast_ws = () if _LATE_WEIGHTS[0] in w else tuple(w[k + '_f32'] for k in _LATE_WEIGHTS)
    qa, ka, va, qb, kb, vb, gb, gate_a, gate_b, log_a, *extra = _inproj(
        x, mod, w['g_pre_mix'], w['w_main'], w['w_gk1'], w['w_gk2'], w['b_gk'], nb, tt, keep_rows,
        cast_ws)
    kv_t = extra[:2] if keep_rows else []
    if cast_ws:
        w = {**w, **dict(zip(_LATE_WEIGHTS, extra[len(kv_t):]))}
    if first_chunk:
        ya = _attn_prompt(qa, ka, va, w['bias_rows'])
        k_keep, v_keep = (jnp.transpose(a.reshape(nbt, H_A, HD_A, keep_rows), (0, 3, 1, 2))
                          for a in kv_t)
        yb, s_new = _gla(qb, kb, vb, gb, log_a, w['g_gla'], None, nbt, CHUNK, 4)
    else:
        k_cache, v_cache = cache
        ya = _attn_sample(qa, ka, va, k_cache, v_cache, w['bias_rows'], 8)
        k_keep, v_keep = (a.astype(F32).reshape(nbt, t, H_A, HD_A) for a in (ka, va))
        yb, s_new = _gla(qb, kb, vb, gb, log_a, w['g_gla'], s_gla, 8, t, 1)
    x1 = _mixout(x, ya, yb, gate_a, gate_b, mod, w['g_post_mix'], w['w_br_a'], w['w_br_b'],
                 w['w_out'], nb, tt)
    y, tail = _ffn(x1, mod, w['g_pre_ffn'], w['g_post_ffn'], w['w_up'], w['w_down'],
                   w['w_dw'], w['b_dw'], None if first_chunk else s_conv, nb_f, tt_f)
    return (y, k_keep, v_keep, s_new.reshape(nbt, H_B, DK_B, DV_B), tail), w


def _prep_weights(w_main, w_gk1, w_gk2, b_gk, rel_bias, g_gla, w_br_a, w_br_b, w_out, w_up, w_dw,
                  b_dw, w_down, g_pre_mix, g_post_mix, g_pre_ffn, g_post_ffn):
    w_gk2p = jnp.pad(w_gk2, ((0, LANES - GK_RANK), (0, 0))).astype(BF16)
    row = lambda a: a.reshape(1, -1)
    return {
        'w_main': w_main, 'w_gk1': w_gk1, 'w_gk2': w_gk2p, 'b_gk': row(b_gk),
        'bias_rows': _bias_rows(rel_bias), 'g_gla': row(g_gla),
        'w_br_a_f32': w_br_a, 'w_br_b_f32': w_br_b, 'w_out_f32': w_out, 'w_up_f32': w_up,
        'w_down_f32': w_down, 'w_dw': w_dw, 'b_dw': row(b_dw),
        'g_pre_mix': row(g_pre_mix), 'g_post_mix': row(g_post_mix),
        'g_pre_ffn': row(g_pre_ffn), 'g_post_ffn': row(g_post_ffn),
    }


def kernel(x_prompt, x_sample, cache_k_a, cache_v_a, state_gla, state_conv, c_prompt, c_sample, w_ada, b_ada, g_pre_mix, g_post_mix, g_pre_ffn, g_post_ffn, w_in, w_gk2, b_gk, rel_bias, g_gla, w_br_a, w_br_b, w_out, w_up, w_dw, b_dw, w_down):
    depth = w_ada.shape[0]
    assert depth == 1
    bp, bs = x_prompt.shape[0], x_sample.shape[0]
    cache_rows = cache_k_a.shape[2]
    yp, ys = x_prompt, x_sample
    outs = [[] for _ in range(8)]
    for l in range(depth):
        c_all = jnp.concatenate([c_prompt, c_sample], axis=0)
        pad = (-c_all.shape[0]) % SUBLANES
        mod, w_main, w_gk1 = _prep(jnp.pad(c_all, ((0, pad), (0, 0))), w_ada[l], b_ada[l], w_in[l].T)
        w = _prep_weights(w_main, w_gk1, w_gk2[l], b_gk[l], rel_bias[l], g_gla[l], w_br_a[l],
                          w_br_b[l], w_out[l], w_up[l], w_dw[l], b_dw[l], w_down[l], g_pre_mix[l],
                          g_post_mix[l], g_pre_ffn[l], g_post_ffn[l])
        mod_p = mod[:bp].reshape(bp, 1, 6 * D_MODEL)
        mod_s = mod[bp:bp + bs].reshape(bs, 1, 6 * D_MODEL)
        (yp, kp, vp, gp, cp), w = _layer(yp, mod_p, None, None, None, w, True)
        to_t = lambda c: jnp.transpose(c, (0, 2, 3, 1)).reshape(bs, W_A, cache_rows)
        cache = (to_t(cache_k_a[l]), to_t(cache_v_a[l]))
        s0 = state_gla[l].reshape(bs, H_B // 2, 2 * DK_B, DV_B)
        (ys, kn, vn, gn, cn), _ = _layer(ys, mod_s, cache, s0, state_conv[l], w, False)
        for lst, a in zip(outs, (kp, vp, gp, cp, kn, vn, gn, cn)):
            lst.append(a)
    return (yp, ys) + tuple(jnp.stack(lst) for lst in outs)
```

```python
import functools

import jax
import jax.numpy as jnp
import numpy as np
from jax import lax
from jax.experimental import pallas as pl
from jax.experimental.pallas import tpu as pltpu

D_MODEL = 1024
CHUNK = 64
BAND_CHUNKS = 8
WINDOW_A = BAND_CHUNKS * CHUNK
H_A = 8
HD_A = 64
MAX_REL = 128
H_B = 4
DK_B = 64
DV_B = 128
GK_RANK = 16
GK_NORM = 16.0
GLA_SUB = 16
D_FF = 2816
CONV_W = 3
EPS = 1e-6
NEG_INF = -1e30

W_A = H_A * HD_A
W_BK = H_B * DK_B
W_BV = H_B * DV_B

LANES = 128
SUBLANES = 8
VMEM_LIMIT = 60 * 1024 * 1024

ATTN_QB = 256
ATTN_KB = 3 * ATTN_QB
BIAS_ROW = 1024
FFN_FT = 256
FFN_NF = D_FF // FFN_FT

BF16 = jnp.bfloat16
F32 = jnp.float32


def _params(sem):
    return pltpu.CompilerParams(dimension_semantics=sem, vmem_limit_bytes=VMEM_LIMIT)


def _full_spec(shape):
    nd = len(shape)
    return pl.BlockSpec(shape, lambda *_: (0,) * nd, pipeline_mode=pl.Buffered(1))


def _dot(a, b):
    return jnp.dot(a, b, preferred_element_type=F32)


def _dot_nt(a, b):
    return lax.dot_general(a, b, (((1,), (1,)), ((), ())), preferred_element_type=F32)


def _dot_tn(a, b):
    return lax.dot_general(a, b, (((0,), (0,)), ((), ())), preferred_element_type=F32)


def _sigmoid(x):
    return 1.0 / (1.0 + jnp.exp(-x))


def _rms_scale(x):
    return lax.rsqrt(jnp.mean(x * x, axis=-1, keepdims=True) + EPS)


_GK_LO = 3 * W_A + 2 * W_BK + 2 * W_BV


PREP_STEPS = 4


def _prep_kernel(c_ref, wa_ref, ba_ref, wi_ref, mod_ref, main_ref, gk_ref):
    c = c_ref[...]
    s = (c * _sigmoid(c)).astype(BF16)
    mod_ref[...] = _dot(s, wa_ref[...].astype(BF16)) + ba_ref[...]
    main_ref[:_GK_LO, :] = wi_ref[:_GK_LO, :].astype(BF16)
    main_ref[_GK_LO:, :] = wi_ref[_GK_LO + GK_RANK:, :].astype(BF16)
    gk_ref[:GK_RANK, :] = wi_ref[_GK_LO:_GK_LO + GK_RANK, :].astype(BF16)
    gk_ref[GK_RANK:, :] = jnp.zeros((LANES - GK_RANK, gk_ref.shape[1]), BF16)


def _prep(c_all, w_ada, b_ada, w_in_t):
    rows = c_all.shape[0]
    n = w_ada.shape[1]
    d_in, d = w_in_t.shape
    tn, cols = n // PREP_STEPS, d // PREP_STEPS
    return pl.pallas_call(
        _prep_kernel,
        grid=(PREP_STEPS,),
        in_specs=[pl.BlockSpec((rows, D_MODEL), lambda j: (0, 0)),
                  pl.BlockSpec((D_MODEL, tn), lambda j: (0, j)),
                  pl.BlockSpec((1, tn), lambda j: (0, j)),
                  pl.BlockSpec((d_in, cols), lambda j: (0, j))],
        out_specs=[pl.BlockSpec((rows, tn), lambda j: (0, j)),
                   pl.BlockSpec((d_in - GK_RANK, cols), lambda j: (0, j)),
                   pl.BlockSpec((LANES, cols), lambda j: (0, j))],
        out_shape=[jax.ShapeDtypeStruct((rows, n), F32),
                   jax.ShapeDtypeStruct((d_in - GK_RANK, d), BF16),
                   jax.ShapeDtypeStruct((LANES, d), BF16)],
        compiler_params=_params(("parallel",)),
        name="prep",
    )(c_all, w_ada, b_ada.reshape(1, n), w_in_t)


_IN_GROUPS = (W_A, W_A, W_A, W_BK, W_BK, W_BV, W_BV, D_MODEL, D_MODEL)


def _inproj_kernel(*refs, n_cast, n_kv_t):
    (x_ref, shift_ref, scale_ref, g_ref, wm_ref, wg1_ref, wg2_ref, bgk_ref), refs = refs[:8], refs[8:]
    cast_in, refs = refs[:n_cast], refs[n_cast:]
    (qa_ref, ka_ref, va_ref, qb_ref, kb_ref, vb_ref, gb_ref, ga_ref, gtb_ref, la_ref) = refs[:10]
    kv_t_refs, cast_out = refs[10:10 + n_kv_t], refs[10 + n_kv_t:]
    for w_ref, o_ref in zip(cast_in, cast_out):
        o_ref[...] = w_ref[...].astype(o_ref.dtype)
    nb, tt, _ = x_ref.shape
    x = x_ref[...]
    h = (x * _rms_scale(x)) * (g_ref[...] * (1.0 + scale_ref[...])) + shift_ref[...]
    hb = h.reshape(nb * tt, D_MODEL).astype(BF16)
    if kv_t_refs:
        @pl.when(pl.program_id(1) == pl.num_programs(1) - 1)
        def _():
            rows = kv_t_refs[0].shape[2]
            newest = hb[nb * tt - rows:, :]
            for o_ref, lo in zip(kv_t_refs, (W_A, 2 * W_A)):
                o_ref[0] = _dot_nt(wm_ref[lo:lo + W_A, :], newest)
    outs = (qa_ref, ka_ref, va_ref, qb_ref, kb_ref, vb_ref, gb_ref, ga_ref, gtb_ref)
    lo = 0
    for o_ref, w in zip(outs, _IN_GROUPS):
        z = _dot_nt(hb, wm_ref[lo:lo + w, :])
        o_ref[...] = z.reshape(nb, tt, w).astype(o_ref.dtype)
        lo += w
    gk_low = _dot_nt(hb, wg1_ref[...]).astype(BF16)
    gk = _dot(gk_low, wg2_ref[...]) + bgk_ref[...]
    log_a = (jnp.minimum(gk, 0.0) - jnp.log1p(jnp.exp(-jnp.abs(gk)))) / GK_NORM
    la_ref[...] = log_a.reshape(nb, tt, W_BK)


def _inproj(x, mod, g_pre, wm, wg1, wg2, bgk, nb, tt, keep_rows, cast_ws=()):
    nbt, t, _ = x.shape
    grid = (nbt // nb, t // tt)
    steps = grid[0] * grid[1]
    cast_specs = [pl.BlockSpec((w.shape[0] // steps, w.shape[1]),
                               lambda b, i: (b * grid[1] + i, 0)) for w in cast_ws]
    tok = lambda w: pl.BlockSpec((nb, tt, w), lambda b, i: (b, i, 0))
    mod_spec = lambda col: pl.BlockSpec((nb, 1, D_MODEL), lambda b, i, col=col: (b, 0, col))
    widths = _IN_GROUPS + (W_BK,)
    dtypes = (BF16, BF16, BF16, F32, F32, BF16, BF16, BF16, BF16, F32)
    out_specs = [tok(w) for w in widths]
    out_shape = [jax.ShapeDtypeStruct((nbt, t, w), dt) for w, dt in zip(widths, dtypes)]
    if keep_rows:
        assert nb == 1 and keep_rows <= tt
        out_specs += [pl.BlockSpec((1, W_A, keep_rows), lambda b, i: (b, 0, 0))] * 2
        out_shape += [jax.ShapeDtypeStruct((nbt, W_A, keep_rows), F32)] * 2
    return pl.pallas_call(
        functools.partial(_inproj_kernel, n_cast=len(cast_ws), n_kv_t=2 if keep_rows else 0),
        grid=grid,
        in_specs=[tok(D_MODEL), mod_spec(0), mod_spec(1), _full_spec((1, D_MODEL)),
                  _full_spec(wm.shape), _full_spec(wg1.shape), _full_spec(wg2.shape),
                  _full_spec((1, W_BK))] + cast_specs,
        out_specs=out_specs + cast_specs,
        out_shape=out_shape + [jax.ShapeDtypeStruct(w.shape, BF16) for w in cast_ws],
        compiler_params=_params(("arbitrary", "arbitrary")),
        name="inproj",
    )(x, mod, mod, g_pre, wm, wg1, wg2, bgk, *cast_ws)


def _head_masks():
    lane = lax.broadcasted_iota(jnp.int32, (1, LANES), 1)
    first = lane < HD_A
    return first, jnp.logical_not(first)


def _toeplitz_bias(row_ref, h, rows):
    rb = jnp.broadcast_to(row_ref[h], (rows, BIAS_ROW))
    return pltpu.roll(rb, 0, 1, stride=1, stride_axis=0)


ATTN_RB = 32
ATTN_WIN = 640


ATTN_SUB = 4


def _attn_prompt_kernel(*refs):
    q_ref, refs = refs[0], refs[1:]
    k_refs, v_refs = refs[:ATTN_SUB + 2], refs[ATTN_SUB + 2:2 * ATTN_SUB + 4]
    row_ref, o_ref, bias_ref, s_ref, p_ref = refs[2 * ATTN_SUB + 4:]
    i = pl.program_id(1)

    @pl.when((pl.program_id(0) == 0) & (i == 0))
    def _():
        qc = lax.broadcasted_iota(jnp.int32, (ATTN_QB, ATTN_KB), 0) // CHUNK
        col = lax.broadcasted_iota(jnp.int32, (ATTN_QB, ATTN_KB), 1)
        kc = col // CHUNK - BAND_CHUNKS
        valid = (kc <= qc) & (kc >= qc - BAND_CHUNKS)
        for h in range(H_A):
            t = _toeplitz_bias(row_ref, h, ATTN_QB)
            band = jnp.where(valid, t[:, :ATTN_KB], NEG_INF)
            bias_ref[0, h] = jnp.where(col >= 2 * ATTN_QB, band, NEG_INF)
            bias_ref[1, h] = jnp.where(col >= ATTN_QB, band, NEG_INF)
            bias_ref[2, h] = band
        p_ref[...] = jnp.zeros_like(p_ref)

    masks = _head_masks()

    def pair_rows(refs, p):
        sl = slice(p * LANES, (p + 1) * LANES)
        return jnp.concatenate([r[0, :, sl] for r in refs], axis=0)

    units = [(sub, h) for sub in range(ATTN_SUB) for h in range(H_A)]

    def scores(u):
        sub, h = units[u]
        p, hh = divmod(h, 2)
        qp = q_ref[0, sub * ATTN_QB:(sub + 1) * ATTN_QB, p * LANES:(p + 1) * LANES] * BF16(HD_A ** -0.5)
        qm = jnp.where(masks[hh], qp, jnp.zeros_like(qp))
        s_ref[u % 2] = _dot_nt(qm, pair_rows(k_refs[sub:sub + 3], p))

    scores(0)
    o_first = None
    for u, (sub, h) in enumerate(units):
        if u + 1 < len(units):
            scores(u + 1)
        slot = u % 2
        var = jnp.minimum(ATTN_SUB * i + sub, 2)
        sums = []
        for r in range(ATTN_QB // ATTN_RB):
            rows = slice(r * ATTN_RB, (r + 1) * ATTN_RB)
            lo = 0 if r * ATTN_RB < ATTN_QB // 2 else ATTN_KB - ATTN_WIN
            sb = s_ref[slot, rows, lo:lo + ATTN_WIN] + bias_ref[var, h, rows, lo:lo + ATTN_WIN]
            e = jnp.exp(sb - jnp.max(sb, axis=-1, keepdims=True))
            sums.append(jnp.sum(e, axis=-1, keepdims=True))
            p_ref[slot, rows, lo:lo + ATTN_WIN] = e.astype(BF16)
        p, hh = divmod(h, 2)
        o = _dot(p_ref[slot], pair_rows(v_refs[sub:sub + 3], p)) / jnp.concatenate(sums, axis=0)
        if hh == 0:
            o_first = o
        else:
            o_ref[0, sub * ATTN_QB:(sub + 1) * ATTN_QB, p * LANES:(p + 1) * LANES] = jnp.where(
                masks[0], o_first, o).astype(o_ref.dtype)


def _attn_prompt(q, k, v, bias_rows):
    b, t, _ = q.shape
    tok = pl.BlockSpec((1, ATTN_SUB * ATTN_QB, W_A), lambda bb, i: (bb, i, 0))
    blk = lambda j: pl.BlockSpec(
        (1, ATTN_QB, W_A), lambda bb, i, j=j: (bb, jnp.maximum(ATTN_SUB * i + j - 2, 0), 0))
    kv_specs = [blk(j) for j in range(ATTN_SUB + 2)]
    return pl.pallas_call(
        _attn_prompt_kernel,
        grid=(b, t // (ATTN_SUB * ATTN_QB)),
        in_specs=[tok] + kv_specs + kv_specs + [_full_spec(bias_rows.shape)],
        out_specs=tok,
        out_shape=jax.ShapeDtypeStruct((b, t, W_A), BF16),
        scratch_shapes=[pltpu.VMEM((3, H_A, ATTN_QB, ATTN_KB), F32),
                        pltpu.VMEM((2, ATTN_QB, ATTN_KB), F32),
                        pltpu.VMEM((2, ATTN_QB, ATTN_KB), BF16)],
        compiler_params=_params(("arbitrary", "arbitrary")),
        name="attn_prompt",
    )(q, *([k] * (ATTN_SUB + 2)), *([v] * (ATTN_SUB + 2)), bias_rows)


def _attn_sample_kernel(q_ref, kn_ref, vn_ref, kc_ref, vc_ref, row_ref, o_ref, bc_ref, bn_ref):
    nb, s_len, _ = q_ref.shape
    w = kc_ref.shape[2]

    @pl.when(pl.program_id(0) == 0)
    def _():
        for h in range(H_A):
            t = _toeplitz_bias(row_ref, h, s_len)
            bc_ref[h * s_len:(h + 1) * s_len, :] = t[:, :w]
            bn_ref[h * s_len:(h + 1) * s_len, :] = t[:, w:w + s_len]

    lane_head = lax.broadcasted_iota(jnp.int32, (1, W_A), 1) // HD_A
    row_head = lax.broadcasted_iota(jnp.int32, (H_A * s_len, 1), 0) // s_len
    own_head = row_head == lane_head
    for b in range(nb):
        q = q_ref[b] * BF16(HD_A ** -0.5)
        qs = jnp.concatenate([q] * H_A, axis=0)
        qs = jnp.where(own_head, qs, jnp.zeros_like(qs))
        sc = _dot(qs, kc_ref[b].astype(BF16)) + bc_ref[...]
        sn = _dot_nt(qs, kn_ref[b]) + bn_ref[...]
        m = jnp.maximum(jnp.max(sc, axis=-1, keepdims=True), jnp.max(sn, axis=-1, keepdims=True))
        ec = jnp.exp(sc - m)
        en = jnp.exp(sn - m)
        l = jnp.sum(ec, axis=-1, keepdims=True) + jnp.sum(en, axis=-1, keepdims=True)
        full = (_dot_nt(ec.astype(BF16), vc_ref[b].astype(BF16))
                + _dot(en.astype(BF16), vn_ref[b])) / l
        o = full[:s_len]
        for h in range(1, H_A):
            o = jnp.where(lane_head == h, full[h * s_len:(h + 1) * s_len], o)
        o_ref[b] = o.astype(o_ref.dtype)


def _attn_sample(q, kn, vn, kc, vc, bias_rows, nb):
    b, s, _ = q.shape
    w = kc.shape[2]
    assert w == WINDOW_A
    new = pl.BlockSpec((nb, s, W_A), lambda i: (i, 0, 0))
    cache = pl.BlockSpec((nb, W_A, w), lambda i: (i, 0, 0))
    return pl.pallas_call(
        _attn_sample_kernel,
        grid=(b // nb,),
        in_specs=[new, new, new, cache, cache, _full_spec(bias_rows.shape)],
        out_specs=new,
        out_shape=jax.ShapeDtypeStruct((b, s, W_A), BF16),
        scratch_shapes=[pltpu.VMEM((H_A * s, w), F32), pltpu.VMEM((H_A * s, s), F32)],
        compiler_params=_params(("arbitrary",)),
        name="attn_sample",
    )(q, kn, vn, kc, vc, bias_rows)


def _gla_kernel(*refs, has_init, c):
    if has_init:
        q_ref, k_ref, v_ref, g_ref, la_ref, gg_ref, s0_ref, y_ref, so_ref, st_ref = refs
    else:
        q_ref, k_ref, v_ref, g_ref, la_ref, gg_ref, y_ref, so_ref, st_ref = refs
    nb, blk, _ = q_ref.shape
    nsub = c // GLA_SUB
    npair = H_B // 2
    j = pl.program_id(1)

    @pl.when(j == 0)
    def _():
        if has_init:
            st_ref[...] = s0_ref[...]
        else:
            st_ref[...] = jnp.zeros_like(st_ref)

    r2 = lax.broadcasted_iota(jnp.int32, (2 * c, c), 0)
    s2 = lax.broadcasted_iota(jnp.int32, (2 * c, c), 1)
    t2 = r2 & (c - 1)
    same_sub = (s2 // GLA_SUB) == (t2 // GLA_SUB)
    sum_mat = jnp.where((s2 <= t2) & ((r2 < c) | same_sub), 1.0, 0.0).astype(BF16)

    row = lax.broadcasted_iota(jnp.int32, (c, W_BK), 0)
    row2 = lax.broadcasted_iota(jnp.int32, (2 * c, LANES), 0)
    lane2 = lax.broadcasted_iota(jnp.int32, (2 * c, LANES), 1)
    own = (lane2 < DK_B) == (row2 < c)
    sub_of_row2 = (row2 & (c - 1)) // GLA_SUB
    tril2 = lane2 <= (row2 & (c - 1))
    eye = (lax.broadcasted_iota(jnp.int32, (LANES, LANES), 0)
           == lax.broadcasted_iota(jnp.int32, (LANES, LANES), 1))
    scale = DK_B ** -0.5
    zeros_k = jnp.zeros((LANES - c, nsub * LANES), BF16)
    zeros_v = jnp.zeros((LANES - c, DV_B), BF16)

    for ci in range(blk // c):
        rows = slice(ci * c, (ci + 1) * c)
        prep = []
        for b in range(nb):
            la = la_ref[b, rows]
            la_hi = la.astype(BF16)
            rest = la - la_hi.astype(F32)
            la_mid = rest.astype(BF16)
            la_lo = (rest - la_mid.astype(F32)).astype(BF16)
            sums = _dot(sum_mat, la_hi) + _dot(sum_mat, la_mid) + _dot(sum_mat, la_lo)
            cum, cum_sub = sums[:c], sums[c:]
            cum_end = cum[c - 1:c, :]
            qf = q_ref[b, rows]
            kf = k_ref[b, rows]
            q_sub = qf * jnp.exp(cum_sub) * scale
            q_in = (qf * jnp.exp(cum) * scale).astype(BF16)
            k_end = (kf * jnp.exp(cum_end - cum)).astype(BF16)
            k_sub = []
            for i in range(nsub):
                ref_i = cum[i * GLA_SUB - 1:i * GLA_SUB, :] if i else jnp.zeros((1, W_BK), F32)
                k_i = jnp.where(row < (i + 1) * GLA_SUB, kf * jnp.exp(ref_i - cum), 0.0)
                k_sub.append(k_i.astype(BF16))
            prep.append((q_sub, q_in, k_end, k_sub, cum_end))

        att = {}
        for b in range(nb):
            q_sub, _, _, k_sub, _ = prep[b]
            for p in range(npair):
                sl = slice(p * LANES, (p + 1) * LANES)
                k_stack = jnp.concatenate(
                    [jnp.concatenate([k_i[:, sl] for k_i in k_sub], axis=1), zeros_k], axis=0)
                q2 = jnp.where(own, jnp.concatenate([q_sub[:, sl], q_sub[:, sl]], axis=0), 0.0)
                q_stack = jnp.concatenate(
                    [jnp.where(sub_of_row2 == i, q2, 0.0) for i in range(nsub)], axis=1).astype(BF16)
                a = _dot_nt(q_stack, k_stack)
                att[b, p] = jnp.where(tril2, a, 0.0).astype(BF16)

        for b in range(nb):
            q_in = prep[b][1]
            for p in range(npair):
                sl = slice(p * LANES, (p + 1) * LANES)
                st_b = st_ref[b, p].astype(BF16)
                q_in2 = jnp.where(own, jnp.concatenate([q_in[:, sl], q_in[:, sl]], axis=0),
                                  jnp.zeros((), BF16))
                for hh in range(2):
                    h = 2 * p + hh
                    hs = slice(h * DV_B, (h + 1) * DV_B)
                    hrows = slice(hh * c, (hh + 1) * c)
                    lhs = jnp.concatenate([att[b, p][hrows], q_in2[hrows]], axis=1)
                    rhs = jnp.concatenate([v_ref[b, rows, hs], zeros_v, st_b], axis=0)
                    o = _dot(lhs, rhs)
                    gate = g_ref[b, rows, hs].astype(F32)
                    y = (o * _rms_scale(o)) * gg_ref[...] * (gate * _sigmoid(gate))
                    y_ref[b, rows, hs] = y.astype(y_ref.dtype)

        for b in range(nb):
            k_end, cum_end = prep[b][2], prep[b][4]
            for p in range(npair):
                sl = slice(p * LANES, (p + 1) * LANES)
                k2 = jnp.where(own, jnp.concatenate([k_end[:, sl], k_end[:, sl]], axis=0),
                               jnp.zeros((), BF16))
                v2 = jnp.concatenate([v_ref[b, rows, 2 * p * DV_B:(2 * p + 1) * DV_B],
                                      v_ref[b, rows, (2 * p + 1) * DV_B:(2 * p + 2) * DV_B]], axis=0)
                upd = _dot_tn(k2, v2)
                dec = jnp.exp(jnp.sum(jnp.where(eye, cum_end[:, sl], 0.0), axis=1, keepdims=True))
                st_ref[b, p] = st_ref[b, p] * dec + upd

    @pl.when(j == pl.num_programs(1) - 1)
    def _():
        so_ref[...] = st_ref[...]


def _gla(q, k, v, g, la, g_gla, s0, nb, c, chunks_per_step):
    nbt, t, _ = q.shape
    blk = c * chunks_per_step
    tok = lambda w: pl.BlockSpec((nb, blk, w), lambda b, j: (b, j, 0))
    st_spec = pl.BlockSpec((nb, H_B // 2, 2 * DK_B, DV_B), lambda b, j: (b, 0, 0, 0))
    in_specs = [tok(W_BK), tok(W_BK), tok(W_BV), tok(W_BV), tok(W_BK), _full_spec((1, DV_B))]
    args = [q, k, v, g, la, g_gla]
    if s0 is not None:
        in_specs.append(st_spec)
        args.append(s0)
    return pl.pallas_call(
        functools.partial(_gla_kernel, has_init=s0 is not None, c=c),
        grid=(nbt // nb, t // blk),
        in_specs=in_specs,
        out_specs=[tok(W_BV), st_spec],
        out_shape=[jax.ShapeDtypeStruct((nbt, t, W_BV), BF16),
                   jax.ShapeDtypeStruct((nbt, H_B // 2, 2 * DK_B, DV_B), F32)],
        scratch_shapes=[pltpu.VMEM((nb, H_B // 2, DV_B, 2 * DK_B), F32)],
        compiler_params=_params(("parallel", "arbitrary")),
        name="gla",
    )(*args)


def _mixout_kernel(x_ref, ya_ref, yb_ref, ga_ref, gb_ref, gm_ref, gp_ref, wa_ref, wb_ref, wo_ref,
                   o_ref):
    nb, tt, _ = x_ref.shape
    if nb == 1:
        halves = [(slice(None), slice(s * tt // 2, (s + 1) * tt // 2)) for s in range(2)]
        nbh, tth = nb, tt // 2
    else:
        halves = [(slice(s * nb // 2, (s + 1) * nb // 2), slice(None)) for s in range(2)]
        nbh, tth = nb // 2, tt
    m = nbh * tth
    merged = []
    for bs, ts in halves:
        a = _dot(ya_ref[bs, ts, :].reshape(m, W_A), wa_ref[...])
        b = _dot(yb_ref[bs, ts, :].reshape(m, W_BV), wb_ref[...])
        ga = _sigmoid(ga_ref[bs, ts, :].reshape(m, D_MODEL).astype(F32))
        gb = _sigmoid(gb_ref[bs, ts, :].reshape(m, D_MODEL).astype(F32))
        merged.append((ga * a + gb * b).astype(BF16))
    for (bs, ts), mg in zip(halves, merged):
        mo = _dot(mg, wo_ref[...])
        n = ((mo * _rms_scale(mo)) * gp_ref[...]).reshape(nbh, tth, D_MODEL)
        o_ref[bs, ts, :] = x_ref[bs, ts, :] + gm_ref[bs] * n


def _mixout(x, ya, yb, ga, gb, mod, g_post, wa, wb, wo, nb, tt):
    nbt, t, _ = x.shape
    tok = lambda w: pl.BlockSpec((nb, tt, w), lambda b, i: (b, i, 0))
    return pl.pallas_call(
        _mixout_kernel,
        grid=(nbt // nb, t // tt),
        in_specs=[tok(D_MODEL), tok(W_A), tok(W_BV), tok(D_MODEL), tok(D_MODEL),
                  pl.BlockSpec((nb, 1, D_MODEL), lambda b, i: (b, 0, 2)),
                  _full_spec((1, D_MODEL)), _full_spec(wa.shape), _full_spec(wb.shape),
                  _full_spec(wo.shape)],
        out_specs=tok(D_MODEL),
        out_shape=jax.ShapeDtypeStruct(x.shape, F32),
        compiler_params=_params(("parallel", "parallel")),
        name="mixout",
    )(x, ya, yb, ga, gb, mod, g_post, wa, wb, wo)


def _gelu_tanh(x):
    c = float(np.sqrt(2.0 / np.pi))
    half = 0.5 * x
    return half + half * jnp.tanh(x * (c + (0.044715 * c) * (x * x)))


def _ffn_kernel(*refs, has_state):
    if has_state:
        (x_ref, shift_ref, scale_ref, gate_ref, gpre_ref, gpost_ref, wu_ref, wd_ref,
         wdw_ref, bdw_ref, prev_ref, o_ref, tail_ref, h_ref, act_ref) = refs
    else:
        (x_ref, shift_ref, scale_ref, gate_ref, gpre_ref, gpost_ref, wu_ref, wd_ref,
         wdw_ref, bdw_ref, o_ref, tail_ref, h_ref, act_ref, prev_ref) = refs

        @pl.when(pl.program_id(1) == 0)
        def _():
            prev_ref[...] = jnp.zeros_like(prev_ref)

    nb, tt, _ = x_ref.shape
    m = nb * tt
    x = x_ref[...]
    h = (x * _rms_scale(x)) * (gpre_ref[...] * (1.0 + scale_ref[...])) + shift_ref[...]
    h_ref[...] = h.reshape(m, D_MODEL).astype(BF16)
    ridx = lax.broadcasted_iota(jnp.int32, (nb, SUBLANES, FFN_FT), 1)

    def conv(u, lanes):
        u3 = u.reshape(nb, tt, FFN_FT)
        r1 = pltpu.roll(u, 1, 0).reshape(nb, tt, FFN_FT)
        r2 = pltpu.roll(u, 2, 0).reshape(nb, tt, FFN_FT)

        def taps(u_m2, u_m1, u_0):
            y = bdw_ref[:, lanes] + wdw_ref[0:1, lanes] * u_m2
            y = y + wdw_ref[1:2, lanes] * u_m1
            return y + wdw_ref[2:3, lanes] * u_0

        p2, p1 = prev_ref[:, 0:1, lanes], prev_ref[:, 1:2, lanes]
        h_m1 = jnp.where(ridx == 0, p1, r1[:, :SUBLANES])
        h_m2 = jnp.where(ridx == 0, p2, jnp.where(ridx == 1, p1, r2[:, :SUBLANES]))
        y = jnp.concatenate([taps(h_m2, h_m1, u3[:, :SUBLANES]),
                             taps(r2[:, SUBLANES:], r1[:, SUBLANES:], u3[:, SUBLANES:])], axis=1)
        tail = u3[:, tt - (CONV_W - 1):, :]
        tail_ref[:, :, lanes] = tail
        if not has_state:
            prev_ref[:, :, lanes] = tail
        return y.reshape(m, FFN_FT)

    def up(f):
        hb = h_ref[...]
        lo = f * FFN_FT
        return (_dot(hb, wu_ref[:, lo:lo + FFN_FT]),
                _dot(hb, wu_ref[:, D_FF + lo:D_FF + lo + FFN_FT]))

    ua, ug = up(0)
    for f in range(FFN_NF):
        if f + 1 < FFN_NF:
            ua_next, ug_next = up(f + 1)
        cols = slice(f * FFN_FT, (f + 1) * FFN_FT)
        ya = conv(ua, cols)
        yg = conv(ug, slice(D_FF + f * FFN_FT, D_FF + (f + 1) * FFN_FT))
        act_ref[:, cols] = (_gelu_tanh(ya) * yg).astype(BF16)
        if f + 1 < FFN_NF:
            ua, ug = ua_next, ug_next
    yf = _dot(act_ref[...], wd_ref[...])
    n = ((yf * _rms_scale(yf)) * gpost_ref[...]).reshape(nb, tt, D_MODEL)
    o_ref[...] = x_ref[...] + gate_ref[...] * n


def _ffn(x, mod, g_pre, g_post, wu, wd, w_dw, b_dw, state, nb, tt):
    nbt, t, _ = x.shape
    m = nb * tt
    tok = pl.BlockSpec((nb, tt, D_MODEL), lambda b, i: (b, i, 0))
    mod_spec = lambda col: pl.BlockSpec((nb, 1, D_MODEL), lambda b, i, col=col: (b, 0, col))
    tail_spec = pl.BlockSpec((nb, CONV_W - 1, 2 * D_FF), lambda b, i: (b, 0, 0))
    in_specs = [tok, mod_spec(3), mod_spec(4), mod_spec(5), _full_spec((1, D_MODEL)),
                _full_spec((1, D_MODEL)), _full_spec(wu.shape),
                _full_spec(wd.shape), _full_spec(w_dw.shape), _full_spec(b_dw.shape)]
    args = [x, mod, mod, mod, g_pre, g_post, wu, wd, w_dw, b_dw]
    scratch = [pltpu.VMEM((m, D_MODEL), BF16), pltpu.VMEM((m, D_FF), BF16)]
    if state is not None:
        in_specs.append(tail_spec)
        args.append(state)
    else:
        scratch.append(pltpu.VMEM((nb, CONV_W - 1, 2 * D_FF), F32))
    return pl.pallas_call(
        functools.partial(_ffn_kernel, has_state=state is not None),
        grid=(nbt // nb, t // tt),
        in_specs=in_specs,
        out_specs=[tok, tail_spec],
        out_shape=[jax.ShapeDtypeStruct(x.shape, F32),
                   jax.ShapeDtypeStruct((nbt, CONV_W - 1, 2 * D_FF), F32)],
        scratch_shapes=scratch,
        compiler_params=_params(("parallel", "arbitrary")),
        name="ffn",
    )(*args)


def _bias_rows(rel_bias):
    assert BIAS_ROW >= ATTN_KB + ATTN_QB - 1 and WINDOW_A == ATTN_KB - ATTN_QB
    far_pos = jnp.broadcast_to(rel_bias[:, -1:], (H_A, BIAS_ROW))
    far_neg = jnp.broadcast_to(rel_bias[:, :1], (H_A, BIAS_ROW))
    n_mid = 2 * MAX_REL + 1
    n_lo = ATTN_KB + 1 - (WINDOW_A - MAX_REL) - n_mid
    rows = jnp.concatenate([far_pos[:, :WINDOW_A - MAX_REL], rel_bias[:, ::-1], far_neg[:, :n_lo],
                            far_pos[:, :BIAS_ROW - ATTN_KB - 1]], axis=1)
    return rows.astype(F32).reshape(H_A, 1, BIAS_ROW)


_LATE_WEIGHTS = ('w_br_a', 'w_br_b', 'w_out', 'w_up', 'w_down')


def _layer(x, mod, cache, s_gla, s_conv, w, first_chunk):
    nbt, t, _ = x.shape
    if first_chunk:
        nb, tt = 1, 1024
        nb_f, tt_f = 1, 1024
    else:
        nb, tt = 512 // t, t
        nb_f, tt_f = 1024 // t, t
    keep_rows = min(WINDOW_A, t) if first_chunk else 0
    cast_ws = () if _LATE_WEIGHTS[0] in w else tuple(w[k + '_f32'] for k in _LATE_WEIGHTS)
    qa, ka, va, qb, kb, vb, gb, gate_a, gate_b, log_a, *extra = _inproj(
        x, mod, w['g_pre_mix'], w['w_main'], w['w_gk1'], w['w_gk2'], w['b_gk'], nb, tt, keep_rows,
        cast_ws)
    kv_t = extra[:2] if keep_rows else []
    if cast_ws:
        w = {**w, **dict(zip(_LATE_WEIGHTS, extra[len(kv_t):]))}
    if first_chunk:
        ya = _attn_prompt(qa, ka, va, w['bias_rows'])
        k_keep, v_keep = (jnp.transpose(a.reshape(nbt, H_A, HD_A, keep_rows), (0, 3, 1, 2))
                          for a in kv_t)
        yb, s_new = _gla(qb, kb, vb, gb, log_a, w['g_gla'], None, nbt, CHUNK, 4)
    else:
        k_cache, v_cache = cache
        ya = _attn_sample(qa, ka, va, k_cache, v_cache, w['bias_rows'], 4)
        k_keep, v_keep = (a.astype(F32).reshape(nbt, t, H_A, HD_A) for a in (ka, va))
        yb, s_new = _gla(qb, kb, vb, gb, log_a, w['g_gla'], s_gla, 8, t, 1)
    x1 = _mixout(x, ya, yb, gate_a, gate_b, mod, w['g_post_mix'], w['w_br_a'], w['w_br_b'],
                 w['w_out'], nb, tt)
    y, tail = _ffn(x1, mod, w['g_pre_ffn'], w['g_post_ffn'], w['w_up'], w['w_down'],
                   w['w_dw'], w['b_dw'], None if first_chunk else s_conv, nb_f, tt_f)
    return (y, k_keep, v_keep, s_new.reshape(nbt, H_B, DK_B, DV_B), tail), w


def _prep_weights(w_main, w_gk1, w_gk2, b_gk, rel_bias, g_gla, w_br_a, w_br_b, w_out, w_up, w_dw,
                  b_dw, w_down, g_pre_mix, g_post_mix, g_pre_ffn, g_post_ffn):
    w_gk2p = jnp.pad(w_gk2, ((0, LANES - GK_RANK), (0, 0))).astype(BF16)
    row = lambda a: a.reshape(1, -1)
    return {
        'w_main': w_main, 'w_gk1': w_gk1, 'w_gk2': w_gk2p, 'b_gk': row(b_gk),
        'bias_rows': _bias_rows(rel_bias), 'g_gla': row(g_gla),
        'w_br_a_f32': w_br_a, 'w_br_b_f32': w_br_b, 'w_out_f32': w_out, 'w_up_f32': w_up,
        'w_down_f32': w_down, 'w_dw': w_dw, 'b_dw': row(b_dw),
        'g_pre_mix': row(g_pre_mix), 'g_post_mix': row(g_post_mix),
        'g_pre_ffn': row(g_pre_ffn), 'g_post_ffn': row(g_post_ffn),
    }


def kernel(x_prompt, x_sample, cache_k_a, cache_v_a, state_gla, state_conv, c_prompt, c_sample, w_ada, b_ada, g_pre_mix, g_post_mix, g_pre_ffn, g_post_ffn, w_in, w_gk2, b_gk, rel_bias, g_gla, w_br_a, w_br_b, w_out, w_up, w_dw, b_dw, w_down):
    depth = w_ada.shape[0]
    assert depth == 1
    bp, bs = x_prompt.shape[0], x_sample.shape[0]
    cache_rows = cache_k_a.shape[2]
    yp, ys = x_prompt, x_sample
    outs = [[] for _ in range(8)]
    for l in range(depth):
        c_all = jnp.concatenate([c_prompt, c_sample], axis=0)
        pad = (-c_all.shape[0]) % SUBLANES
        mod, w_main, w_gk1 = _prep(jnp.pad(c_all, ((0, pad), (0, 0))), w_ada[l], b_ada[l], w_in[l].T)
        w = _prep_weights(w_main, w_gk1, w_gk2[l], b_gk[l], rel_bias[l], g_gla[l], w_br_a[l],
                          w_br_b[l], w_out[l], w_up[l], w_dw[l], b_dw[l], w_down[l], g_pre_mix[l],
                          g_post_mix[l], g_pre_ffn[l], g_post_ffn[l])
        mod_p = mod[:bp].reshape(bp, 1, 6 * D_MODEL)
        mod_s = mod[bp:bp + bs].reshape(bs, 1, 6 * D_MODEL)
        (yp, kp, vp, gp, cp), w = _layer(yp, mod_p, None, None, None, w, True)
        to_t = lambda c: jnp.transpose(c, (0, 2, 3, 1)).reshape(bs, W_A, cache_rows)
        cache = (to_t(cache_k_a[l]), to_t(cache_v_a[l]))
        s0 = state_gla[l].reshape(bs, H_B // 2, 2 * DK_B, DV_B)
        (ys, kn, vn, gn, cn), _ = _layer(ys, mod_s, cache, s0, state_conv[l], w, False)
        for lst, a in zip(outs, (kp, vp, gp, cp, kn, vn, gn, cn)):
            lst.append(a)
    return (yp, ys) + tuple(jnp.stack(lst) for lst in outs)
```

```python
import functools

import jax
import jax.numpy as jnp
import numpy as np
from jax import lax
from jax.experimental import pallas as pl
from jax.experimental.pallas import tpu as pltpu

D_MODEL = 1024
CHUNK = 64
BAND_CHUNKS = 8
WINDOW_A = BAND_CHUNKS * CHUNK
H_A = 8
HD_A = 64
MAX_REL = 128
H_B = 4
DK_B = 64
DV_B = 128
GK_RANK = 16
GK_NORM = 16.0
GLA_SUB = 16
D_FF = 2816
CONV_W = 3
EPS = 1e-6
NEG_INF = -1e30

W_A = H_A * HD_A
W_BK = H_B * DK_B
W_BV = H_B * DV_B

LANES = 128
SUBLANES = 8
VMEM_LIMIT = 60 * 1024 * 1024

ATTN_QB = 256
ATTN_KB = 3 * ATTN_QB
BIAS_ROW = 1024
FFN_FT = 256
FFN_NF = D_FF // FFN_FT

BF16 = jnp.bfloat16
F32 = jnp.float32


def _params(sem):
    return pltpu.CompilerParams(dimension_semantics=sem, vmem_limit_bytes=VMEM_LIMIT)


def _full_spec(shape):
    nd = len(shape)
    return pl.BlockSpec(shape, lambda *_: (0,) * nd, pipeline_mode=pl.Buffered(1))


def _dot(a, b):
    return jnp.dot(a, b, preferred_element_type=F32)


def _dot_nt(a, b):
    return lax.dot_general(a, b, (((1,), (1,)), ((), ())), preferred_element_type=F32)


def _dot_tn(a, b):
    return lax.dot_general(a, b, (((0,), (0,)), ((), ())), preferred_element_type=F32)


def _sigmoid(x):
    return 1.0 / (1.0 + jnp.exp(-x))


def _rms_scale(x):
    return lax.rsqrt(jnp.mean(x * x, axis=-1, keepdims=True) + EPS)


_GK_LO = 3 * W_A + 2 * W_BK + 2 * W_BV


PREP_STEPS = 4


def _prep_kernel(c_ref, wa_ref, ba_ref, wi_ref, mod_ref, main_ref, gk_ref):
    c = c_ref[...]
    s = (c * _sigmoid(c)).astype(BF16)
    mod_ref[...] = _dot(s, wa_ref[...].astype(BF16)) + ba_ref[...]
    main_ref[:_GK_LO, :] = wi_ref[:_GK_LO, :].astype(BF16)
    main_ref[_GK_LO:, :] = wi_ref[_GK_LO + GK_RANK:, :].astype(BF16)
    gk_ref[:GK_RANK, :] = wi_ref[_GK_LO:_GK_LO + GK_RANK, :].astype(BF16)
    gk_ref[GK_RANK:, :] = jnp.zeros((LANES - GK_RANK, gk_ref.shape[1]), BF16)


def _prep(c_all, w_ada, b_ada, w_in_t):
    rows = c_all.shape[0]
    n = w_ada.shape[1]
    d_in, d = w_in_t.shape
    tn, cols = n // PREP_STEPS, d // PREP_STEPS
    return pl.pallas_call(
        _prep_kernel,
        grid=(PREP_STEPS,),
        in_specs=[pl.BlockSpec((rows, D_MODEL), lambda j: (0, 0)),
                  pl.BlockSpec((D_MODEL, tn), lambda j: (0, j)),
                  pl.BlockSpec((1, tn), lambda j: (0, j)),
                  pl.BlockSpec((d_in, cols), lambda j: (0, j))],
        out_specs=[pl.BlockSpec((rows, tn), lambda j: (0, j)),
                   pl.BlockSpec((d_in - GK_RANK, cols), lambda j: (0, j)),
                   pl.BlockSpec((LANES, cols), lambda j: (0, j))],
        out_shape=[jax.ShapeDtypeStruct((rows, n), F32),
                   jax.ShapeDtypeStruct((d_in - GK_RANK, d), BF16),
                   jax.ShapeDtypeStruct((LANES, d), BF16)],
        compiler_params=_params(("parallel",)),
        name="prep",
    )(c_all, w_ada, b_ada.reshape(1, n), w_in_t)


_IN_GROUPS = (W_A, W_A, W_A, W_BK, W_BK, W_BV, W_BV, D_MODEL, D_MODEL)


def _inproj_kernel(*refs, n_cast, n_kv_t):
    (x_ref, shift_ref, scale_ref, g_ref, wm_ref, wg1_ref, wg2_ref, bgk_ref), refs = refs[:8], refs[8:]
    cast_in, refs = refs[:n_cast], refs[n_cast:]
    (qa_ref, ka_ref, va_ref, qb_ref, kb_ref, vb_ref, gb_ref, ga_ref, gtb_ref, la_ref) = refs[:10]
    kv_t_refs, cast_out = refs[10:10 + n_kv_t], refs[10 + n_kv_t:]
    for w_ref, o_ref in zip(cast_in, cast_out):
        o_ref[...] = w_ref[...].astype(o_ref.dtype)
    nb, tt, _ = x_ref.shape
    x = x_ref[...]
    h = (x * _rms_scale(x)) * (g_ref[...] * (1.0 + scale_ref[...])) + shift_ref[...]
    hb = h.reshape(nb * tt, D_MODEL).astype(BF16)
    if kv_t_refs:
        @pl.when(pl.program_id(1) == pl.num_programs(1) - 1)
        def _():
            rows = kv_t_refs[0].shape[2]
            newest = hb[nb * tt - rows:, :]
            for o_ref, lo in zip(kv_t_refs, (W_A, 2 * W_A)):
                o_ref[0] = _dot_nt(wm_ref[lo:lo + W_A, :], newest)
    outs = (qa_ref, ka_ref, va_ref, qb_ref, kb_ref, vb_ref, gb_ref, ga_ref, gtb_ref)
    lo = 0
    for o_ref, w in zip(outs, _IN_GROUPS):
        z = _dot_nt(hb, wm_ref[lo:lo + w, :])
        o_ref[...] = z.reshape(nb, tt, w).astype(o_ref.dtype)
        lo += w
    gk_low = _dot_nt(hb, wg1_ref[...]).astype(BF16)
    gk = _dot(gk_low, wg2_ref[...]) + bgk_ref[...]
    log_a = (jnp.minimum(gk, 0.0) - jnp.log1p(jnp.exp(-jnp.abs(gk)))) / GK_NORM
    la_ref[...] = log_a.reshape(nb, tt, W_BK)


def _inproj(x, mod, g_pre, wm, wg1, wg2, bgk, nb, tt, keep_rows, cast_ws=()):
    nbt, t, _ = x.shape
    grid = (nbt // nb, t // tt)
    steps = grid[0] * grid[1]
    cast_specs = [pl.BlockSpec((w.shape[0] // steps, w.shape[1]),
                               lambda b, i: (b * grid[1] + i, 0)) for w in cast_ws]
    tok = lambda w: pl.BlockSpec((nb, tt, w), lambda b, i: (b, i, 0))
    mod_spec = lambda col: pl.BlockSpec((nb, 1, D_MODEL), lambda b, i, col=col: (b, 0, col))
    widths = _IN_GROUPS + (W_BK,)
    dtypes = (BF16, BF16, BF16, F32, F32, BF16, BF16, BF16, BF16, F32)
    out_specs = [tok(w) for w in widths]
    out_shape = [jax.ShapeDtypeStruct((nbt, t, w), dt) for w, dt in zip(widths, dtypes)]
    if keep_rows:
        assert nb == 1 and keep_rows <= tt
        out_specs += [pl.BlockSpec((1, W_A, keep_rows), lambda b, i: (b, 0, 0))] * 2
        out_shape += [jax.ShapeDtypeStruct((nbt, W_A, keep_rows), F32)] * 2
    return pl.pallas_call(
        functools.partial(_inproj_kernel, n_cast=len(cast_ws), n_kv_t=2 if keep_rows else 0),
        grid=grid,
        in_specs=[tok(D_MODEL), mod_spec(0), mod_spec(1), _full_spec((1, D_MODEL)),
                  _full_spec(wm.shape), _full_spec(wg1.shape), _full_spec(wg2.shape),
                  _full_spec((1, W_BK))] + cast_specs,
        out_specs=out_specs + cast_specs,
        out_shape=out_shape + [jax.ShapeDtypeStruct(w.shape, BF16) for w in cast_ws],
        compiler_params=_params(("arbitrary", "arbitrary")),
        name="inproj",
    )(x, mod, mod, g_pre, wm, wg1, wg2, bgk, *cast_ws)


def _head_masks():
    lane = lax.broadcasted_iota(jnp.int32, (1, LANES), 1)
    first = lane < HD_A
    return first, jnp.logical_not(first)


def _toeplitz_bias(row_ref, h, rows):
    rb = jnp.broadcast_to(row_ref[h], (rows, BIAS_ROW))
    return pltpu.roll(rb, 0, 1, stride=1, stride_axis=0)


ATTN_RB = 32
ATTN_WIN = 640


ATTN_SUB = 4


def _attn_prompt_kernel(*refs):
    q_ref, refs = refs[0], refs[1:]
    k_refs, v_refs = refs[:ATTN_SUB + 2], refs[ATTN_SUB + 2:2 * ATTN_SUB + 4]
    row_ref, o_ref, bias_ref, s_ref, p_ref = refs[2 * ATTN_SUB + 4:]
    i = pl.program_id(1)

    @pl.when((pl.program_id(0) == 0) & (i == 0))
    def _():
        qc = lax.broadcasted_iota(jnp.int32, (ATTN_QB, ATTN_KB), 0) // CHUNK
        col = lax.broadcasted_iota(jnp.int32, (ATTN_QB, ATTN_KB), 1)
        kc = col // CHUNK - BAND_CHUNKS
        valid = (kc <= qc) & (kc >= qc - BAND_CHUNKS)
        for h in range(H_A):
            t = _toeplitz_bias(row_ref, h, ATTN_QB)
            band = jnp.where(valid, t[:, :ATTN_KB], NEG_INF)
            bias_ref[0, h] = jnp.where(col >= 2 * ATTN_QB, band, NEG_INF)
            bias_ref[1, h] = jnp.where(col >= ATTN_QB, band, NEG_INF)
            bias_ref[2, h] = band
        p_ref[...] = jnp.zeros_like(p_ref)

    masks = _head_masks()

    def pair_rows(refs, p):
        sl = slice(p * LANES, (p + 1) * LANES)
        return jnp.concatenate([r[0, :, sl] for r in refs], axis=0)

    units = [(sub, h) for sub in range(ATTN_SUB) for h in range(H_A)]

    def scores(u):
        sub, h = units[u]
        p, hh = divmod(h, 2)
        qp = q_ref[0, sub * ATTN_QB:(sub + 1) * ATTN_QB, p * LANES:(p + 1) * LANES] * BF16(HD_A ** -0.5)
        qm = jnp.where(masks[hh], qp, jnp.zeros_like(qp))
        s_ref[u % 2] = _dot_nt(qm, pair_rows(k_refs[sub:sub + 3], p))

    scores(0)
    o_first = None
    for u, (sub, h) in enumerate(units):
        if u + 1 < len(units):
            scores(u + 1)
        slot = u % 2
        var = jnp.minimum(ATTN_SUB * i + sub, 2)
        sums = []
        for r in range(ATTN_QB // ATTN_RB):
            rows = slice(r * ATTN_RB, (r + 1) * ATTN_RB)
            lo = 0 if r * ATTN_RB < ATTN_QB // 2 else ATTN_KB - ATTN_WIN
            sb = s_ref[slot, rows, lo:lo + ATTN_WIN] + bias_ref[var, h, rows, lo:lo + ATTN_WIN]
            e = jnp.exp(sb - jnp.max(sb, axis=-1, keepdims=True))
            sums.append(jnp.sum(e, axis=-1, keepdims=True))
            p_ref[slot, rows, lo:lo + ATTN_WIN] = e.astype(BF16)
        p, hh = divmod(h, 2)
        o = _dot(p_ref[slot], pair_rows(v_refs[sub:sub + 3], p)) / jnp.concatenate(sums, axis=0)
        if hh == 0:
            o_first = o
        else:
            o_ref[0, sub * ATTN_QB:(sub + 1) * ATTN_QB, p * LANES:(p + 1) * LANES] = jnp.where(
                masks[0], o_first, o).astype(o_ref.dtype)


def _attn_prompt(q, k, v, bias_rows):
    b, t, _ = q.shape
    tok = pl.BlockSpec((1, ATTN_SUB * ATTN_QB, W_A), lambda bb, i: (bb, i, 0))
    blk = lambda j: pl.BlockSpec(
        (1, ATTN_QB, W_A), lambda bb, i, j=j: (bb, jnp.maximum(ATTN_SUB * i + j - 2, 0), 0))
    kv_specs = [blk(j) for j in range(ATTN_SUB + 2)]
    return pl.pallas_call(
        _attn_prompt_kernel,
        grid=(b, t // (ATTN_SUB * ATTN_QB)),
        in_specs=[tok] + kv_specs + kv_specs + [_full_spec(bias_rows.shape)],
        out_specs=tok,
        out_shape=jax.ShapeDtypeStruct((b, t, W_A), BF16),
        scratch_shapes=[pltpu.VMEM((3, H_A, ATTN_QB, ATTN_KB), F32),
                        pltpu.VMEM((2, ATTN_QB, ATTN_KB), F32),
                        pltpu.VMEM((2, ATTN_QB, ATTN_KB), BF16)],
        compiler_params=_params(("arbitrary", "arbitrary")),
        name="attn_prompt",
    )(q, *([k] * (ATTN_SUB + 2)), *([v] * (ATTN_SUB + 2)), bias_rows)


def _attn_sample_kernel(q_ref, kn_ref, vn_ref, kc_ref, vc_ref, row_ref, o_ref, bc_ref, bn_ref):
    nb, s_len, _ = q_ref.shape
    w = kc_ref.shape[2]

    @pl.when(pl.program_id(0) == 0)
    def _():
        for h in range(H_A):
            t = _toeplitz_bias(row_ref, h, s_len)
            bc_ref[h * s_len:(h + 1) * s_len, :] = t[:, :w]
            bn_ref[h * s_len:(h + 1) * s_len, :] = t[:, w:w + s_len]

    lane_head = lax.broadcasted_iota(jnp.int32, (1, W_A), 1) // HD_A
    row_head = lax.broadcasted_iota(jnp.int32, (H_A * s_len, 1), 0) // s_len
    own_head = row_head == lane_head
    for b in range(nb):
        q = q_ref[b] * BF16(HD_A ** -0.5)
        qs = jnp.concatenate([q] * H_A, axis=0)
        qs = jnp.where(own_head, qs, jnp.zeros_like(qs))
        sc = _dot(qs, kc_ref[b].astype(BF16)) + bc_ref[...]
        sn = _dot_nt(qs, kn_ref[b]) + bn_ref[...]
        m = jnp.maximum(jnp.max(sc, axis=-1, keepdims=True), jnp.max(sn, axis=-1, keepdims=True))
        ec = jnp.exp(sc - m)
        en = jnp.exp(sn - m)
        l = jnp.sum(ec, axis=-1, keepdims=True) + jnp.sum(en, axis=-1, keepdims=True)
        full = (_dot_nt(ec.astype(BF16), vc_ref[b].astype(BF16))
                + _dot(en.astype(BF16), vn_ref[b])) / l
        o = full[:s_len]
        for h in range(1, H_A):
            o = jnp.where(lane_head == h, full[h * s_len:(h + 1) * s_len], o)
        o_ref[b] = o.astype(o_ref.dtype)


def _attn_sample(q, kn, vn, kc, vc, bias_rows, nb):
    b, s, _ = q.shape
    w = kc.shape[2]
    assert w == WINDOW_A
    new = pl.BlockSpec((nb, s, W_A), lambda i: (i, 0, 0))
    cache = pl.BlockSpec((nb, W_A, w), lambda i: (i, 0, 0))
    return pl.pallas_call(
        _attn_sample_kernel,
        grid=(b // nb,),
        in_specs=[new, new, new, cache, cache, _full_spec(bias_rows.shape)],
        out_specs=new,
        out_shape=jax.ShapeDtypeStruct((b, s, W_A), BF16),
        scratch_shapes=[pltpu.VMEM((H_A * s, w), F32), pltpu.VMEM((H_A * s, s), F32)],
        compiler_params=_params(("arbitrary",)),
        name="attn_sample",
    )(q, kn, vn, kc, vc, bias_rows)


def _gla_kernel(*refs, has_init, c):
    if has_init:
        q_ref, k_ref, v_ref, g_ref, la_ref, gg_ref, s0_ref, y_ref, so_ref, st_ref = refs
    else:
        q_ref, k_ref, v_ref, g_ref, la_ref, gg_ref, y_ref, so_ref, st_ref = refs
    nb, blk, _ = q_ref.shape
    nsub = c // GLA_SUB
    npair = H_B // 2
    j = pl.program_id(1)

    @pl.when(j == 0)
    def _():
        if has_init:
            st_ref[...] = s0_ref[...]
        else:
            st_ref[...] = jnp.zeros_like(st_ref)

    r2 = lax.broadcasted_iota(jnp.int32, (2 * c, c), 0)
    s2 = lax.broadcasted_iota(jnp.int32, (2 * c, c), 1)
    t2 = r2 & (c - 1)
    same_sub = (s2 // GLA_SUB) == (t2 // GLA_SUB)
    sum_mat = jnp.where((s2 <= t2) & ((r2 < c) | same_sub), 1.0, 0.0).astype(BF16)

    row = lax.broadcasted_iota(jnp.int32, (c, W_BK), 0)
    row2 = lax.broadcasted_iota(jnp.int32, (2 * c, LANES), 0)
    lane2 = lax.broadcasted_iota(jnp.int32, (2 * c, LANES), 1)
    own = (lane2 < DK_B) == (row2 < c)
    sub_of_row2 = (row2 & (c - 1)) // GLA_SUB
    tril2 = lane2 <= (row2 & (c - 1))
    eye = (lax.broadcasted_iota(jnp.int32, (LANES, LANES), 0)
           == lax.broadcasted_iota(jnp.int32, (LANES, LANES), 1))
    scale = DK_B ** -0.5
    zeros_k = jnp.zeros((LANES - c, nsub * LANES), BF16)
    zeros_v = jnp.zeros((LANES - c, DV_B), BF16)

    for ci in range(blk // c):
        rows = slice(ci * c, (ci + 1) * c)
        prep = []
        for b in range(nb):
            la = la_ref[b, rows]
            la_hi = la.astype(BF16)
            rest = la - la_hi.astype(F32)
            la_mid = rest.astype(BF16)
            la_lo = (rest - la_mid.astype(F32)).astype(BF16)
            sums = _dot(sum_mat, la_hi) + _dot(sum_mat, la_mid) + _dot(sum_mat, la_lo)
            cum, cum_sub = sums[:c], sums[c:]
            cum_end = cum[c - 1:c, :]
            qf = q_ref[b, rows]
            kf = k_ref[b, rows]
            q_sub = qf * jnp.exp(cum_sub) * scale
            q_in = (qf * jnp.exp(cum) * scale).astype(BF16)
            k_end = (kf * jnp.exp(cum_end - cum)).astype(BF16)
            k_sub = []
            for i in range(nsub):
                ref_i = cum[i * GLA_SUB - 1:i * GLA_SUB, :] if i else jnp.zeros((1, W_BK), F32)
                k_i = jnp.where(row < (i + 1) * GLA_SUB, kf * jnp.exp(ref_i - cum), 0.0)
                k_sub.append(k_i.astype(BF16))
            prep.append((q_sub, q_in, k_end, k_sub, cum_end))

        att = {}
        for b in range(nb):
            q_sub, _, _, k_sub, _ = prep[b]
            for p in range(npair):
                sl = slice(p * LANES, (p + 1) * LANES)
                k_stack = jnp.concatenate(
                    [jnp.concatenate([k_i[:, sl] for k_i in k_sub], axis=1), zeros_k], axis=0)
                q2 = jnp.where(own, jnp.concatenate([q_sub[:, sl], q_sub[:, sl]], axis=0), 0.0)
                q_stack = jnp.concatenate(
                    [jnp.where(sub_of_row2 == i, q2, 0.0) for i in range(nsub)], axis=1).astype(BF16)
                a = _dot_nt(q_stack, k_stack)
                att[b, p] = jnp.where(tril2, a, 0.0).astype(BF16)

        for b in range(nb):
            q_in = prep[b][1]
            for p in range(npair):
                sl = slice(p * LANES, (p + 1) * LANES)
                st_b = st_ref[b, p].astype(BF16)
                q_in2 = jnp.where(own, jnp.concatenate([q_in[:, sl], q_in[:, sl]], axis=0),
                                  jnp.zeros((), BF16))
                for hh in range(2):
                    h = 2 * p + hh
                    hs = slice(h * DV_B, (h + 1) * DV_B)
                    hrows = slice(hh * c, (hh + 1) * c)
                    lhs = jnp.concatenate([att[b, p][hrows], q_in2[hrows]], axis=1)
                    rhs = jnp.concatenate([v_ref[b, rows, hs], zeros_v, st_b], axis=0)
                    o = _dot(lhs, rhs)
                    gate = g_ref[b, rows, hs].astype(F32)
                    y = (o * _rms_scale(o)) * gg_ref[...] * (gate * _sigmoid(gate))
                    y_ref[b, rows, hs] = y.astype(y_ref.dtype)

        for b in range(nb):
            k_end, cum_end = prep[b][2], prep[b][4]
            for p in range(npair):
                sl = slice(p * LANES, (p + 1) * LANES)
                k2 = jnp.where(own, jnp.concatenate([k_end[:, sl], k_end[:, sl]], axis=0),
                               jnp.zeros((), BF16))
                v2 = jnp.concatenate([v_ref[b, rows, 2 * p * DV_B:(2 * p + 1) * DV_B],
                                      v_ref[b, rows, (2 * p + 1) * DV_B:(2 * p + 2) * DV_B]], axis=0)
                upd = _dot_tn(k2, v2)
                dec = jnp.exp(jnp.sum(jnp.where(eye, cum_end[:, sl], 0.0), axis=1, keepdims=True))
                st_ref[b, p] = st_ref[b, p] * dec + upd

    @pl.when(j == pl.num_programs(1) - 1)
    def _():
        so_ref[...] = st_ref[...]


def _gla(q, k, v, g, la, g_gla, s0, nb, c, chunks_per_step):
    nbt, t, _ = q.shape
    blk = c * chunks_per_step
    tok = lambda w: pl.BlockSpec((nb, blk, w), lambda b, j: (b, j, 0))
    st_spec = pl.BlockSpec((nb, H_B // 2, 2 * DK_B, DV_B), lambda b, j: (b, 0, 0, 0))
    in_specs = [tok(W_BK), tok(W_BK), tok(W_BV), tok(W_BV), tok(W_BK), _full_spec((1, DV_B))]
    args = [q, k, v, g, la, g_gla]
    if s0 is not None:
        in_specs.append(st_spec)
        args.append(s0)
    return pl.pallas_call(
        functools.partial(_gla_kernel, has_init=s0 is not None, c=c),
        grid=(nbt // nb, t // blk),
        in_specs=in_specs,
        out_specs=[tok(W_BV), st_spec],
        out_shape=[jax.ShapeDtypeStruct((nbt, t, W_BV), BF16),
                   jax.ShapeDtypeStruct((nbt, H_B // 2, 2 * DK_B, DV_B), F32)],
        scratch_shapes=[pltpu.VMEM((nb, H_B // 2, DV_B, 2 * DK_B), F32)],
        compiler_params=_params(("parallel", "arbitrary")),
        name="gla",
    )(*args)


def _mixout_kernel(x_ref, ya_ref, yb_ref, ga_ref, gb_ref, gm_ref, gp_ref, wa_ref, wb_ref, wo_ref,
                   o_ref):
    nb, tt, _ = x_ref.shape
    if nb == 1:
        halves = [(slice(None), slice(s * tt // 2, (s + 1) * tt // 2)) for s in range(2)]
        nbh, tth = nb, tt // 2
    else:
        halves = [(slice(s * nb // 2, (s + 1) * nb // 2), slice(None)) for s in range(2)]
        nbh, tth = nb // 2, tt
    m = nbh * tth
    merged = []
    for bs, ts in halves:
        a = _dot(ya_ref[bs, ts, :].reshape(m, W_A), wa_ref[...])
        b = _dot(yb_ref[bs, ts, :].reshape(m, W_BV), wb_ref[...])
        ga = _sigmoid(ga_ref[bs, ts, :].reshape(m, D_MODEL).astype(F32))
        gb = _sigmoid(gb_ref[bs, ts, :].reshape(m, D_MODEL).astype(F32))
        merged.append((ga * a + gb * b).astype(BF16))
    for (bs, ts), mg in zip(halves, merged):
        mo = _dot(mg, wo_ref[...])
        n = ((mo * _rms_scale(mo)) * gp_ref[...]).reshape(nbh, tth, D_MODEL)
        o_ref[bs, ts, :] = x_ref[bs, ts, :] + gm_ref[bs] * n


MIXOUT_BUFFERS = 3


def _mixout_piped(x, ya, yb, ga, gb, mod, g_post, wa, wb, wo, nb, tt):
    nbt, t, _ = x.shape
    deep = pl.Buffered(MIXOUT_BUFFERS)
    tok_in = lambda w: pl.BlockSpec((nb, tt, w), lambda b, i: (b, i, 0), pipeline_mode=deep)
    gate_spec = pl.BlockSpec((nb, 1, D_MODEL), lambda b, i: (b, 0, 2))
    out_spec = pl.BlockSpec((nb, tt, D_MODEL), lambda b, i: (b, i, 0))

    def outer(x_hbm, ya_hbm, yb_hbm, ga_hbm, gb_hbm, mod_hbm, gp_ref, wa_ref, wb_ref, wo_ref, o_hbm):
        def body(x_ref, ya_ref, yb_ref, ga_ref, gb_ref, gm_ref, o_ref):
            _mixout_kernel(x_ref, ya_ref, yb_ref, ga_ref, gb_ref, gm_ref, gp_ref, wa_ref, wb_ref,
                           wo_ref, o_ref)

        pltpu.emit_pipeline(
            body,
            grid=(nbt // nb, t // tt),
            in_specs=[tok_in(D_MODEL), tok_in(W_A), tok_in(W_BV), tok_in(D_MODEL), tok_in(D_MODEL),
                      gate_spec],
            out_specs=[out_spec],
        )(x_hbm, ya_hbm, yb_hbm, ga_hbm, gb_hbm, mod_hbm, o_hbm)

    hbm = pl.BlockSpec(memory_space=pl.ANY)
    vmem = pl.BlockSpec(memory_space=pltpu.VMEM)
    return pl.pallas_call(
        outer,
        in_specs=[hbm] * 6 + [vmem] * 4,
        out_specs=hbm,
        out_shape=jax.ShapeDtypeStruct(x.shape, F32),
        compiler_params=pltpu.CompilerParams(vmem_limit_bytes=VMEM_LIMIT),
        name="mixout_piped",
    )(x, ya, yb, ga, gb, mod, g_post, wa, wb, wo)


def _mixout(x, ya, yb, ga, gb, mod, g_post, wa, wb, wo, nb, tt):
    nbt, t, _ = x.shape
    tok = lambda w: pl.BlockSpec((nb, tt, w), lambda b, i: (b, i, 0))
    return pl.pallas_call(
        _mixout_kernel,
        grid=(nbt // nb, t // tt),
        in_specs=[tok(D_MODEL), tok(W_A), tok(W_BV), tok(D_MODEL), tok(D_MODEL),
                  pl.BlockSpec((nb, 1, D_MODEL), lambda b, i: (b, 0, 2)),
                  _full_spec((1, D_MODEL)), _full_spec(wa.shape), _full_spec(wb.shape),
                  _full_spec(wo.shape)],
        out_specs=tok(D_MODEL),
        out_shape=jax.ShapeDtypeStruct(x.shape, F32),
        compiler_params=_params(("parallel", "parallel")),
        name="mixout",
    )(x, ya, yb, ga, gb, mod, g_post, wa, wb, wo)


def _gelu_tanh(x):
    c = float(np.sqrt(2.0 / np.pi))
    half = 0.5 * x
    return half + half * jnp.tanh(x * (c + (0.044715 * c) * (x * x)))


def _ffn_kernel(*refs, has_state):
    if has_state:
        (x_ref, shift_ref, scale_ref, gate_ref, gpre_ref, gpost_ref, wu_ref, wd_ref,
         wdw_ref, bdw_ref, prev_ref, o_ref, tail_ref, h_ref, act_ref) = refs
    else:
        (x_ref, shift_ref, scale_ref, gate_ref, gpre_ref, gpost_ref, wu_ref, wd_ref,
         wdw_ref, bdw_ref, o_ref, tail_ref, h_ref, act_ref, prev_ref) = refs

        @pl.when(pl.program_id(1) == 0)
        def _():
            prev_ref[...] = jnp.zeros_like(prev_ref)

    nb, tt, _ = x_ref.shape
    m = nb * tt
    x = x_ref[...]
    h = (x * _rms_scale(x)) * (gpre_ref[...] * (1.0 + scale_ref[...])) + shift_ref[...]
    h_ref[...] = h.reshape(m, D_MODEL).astype(BF16)
    ridx = lax.broadcasted_iota(jnp.int32, (nb, SUBLANES, FFN_FT), 1)

    def conv(u, lanes):
        u3 = u.reshape(nb, tt, FFN_FT)
        r1 = pltpu.roll(u, 1, 0).reshape(nb, tt, FFN_FT)
        r2 = pltpu.roll(u, 2, 0).reshape(nb, tt, FFN_FT)

        def taps(u_m2, u_m1, u_0):
            y = bdw_ref[:, lanes] + wdw_ref[0:1, lanes] * u_m2
            y = y + wdw_ref[1:2, lanes] * u_m1
            return y + wdw_ref[2:3, lanes] * u_0

        p2, p1 = prev_ref[:, 0:1, lanes], prev_ref[:, 1:2, lanes]
        h_m1 = jnp.where(ridx == 0, p1, r1[:, :SUBLANES])
        h_m2 = jnp.where(ridx == 0, p2, jnp.where(ridx == 1, p1, r2[:, :SUBLANES]))
        y = jnp.concatenate([taps(h_m2, h_m1, u3[:, :SUBLANES]),
                             taps(r2[:, SUBLANES:], r1[:, SUBLANES:], u3[:, SUBLANES:])], axis=1)
        tail = u3[:, tt - (CONV_W - 1):, :]
        tail_ref[:, :, lanes] = tail
        if not has_state:
            prev_ref[:, :, lanes] = tail
        return y.reshape(m, FFN_FT)

    def up(f):
        hb = h_ref[...]
        lo = f * FFN_FT
        return (_dot(hb, wu_ref[:, lo:lo + FFN_FT]),
                _dot(hb, wu_ref[:, D_FF + lo:D_FF + lo + FFN_FT]))

    ua, ug = up(0)
    for f in range(FFN_NF):
        if f + 1 < FFN_NF:
            ua_next, ug_next = up(f + 1)
        cols = slice(f * FFN_FT, (f + 1) * FFN_FT)
        ya = conv(ua, cols)
        yg = conv(ug, slice(D_FF + f * FFN_FT, D_FF + (f + 1) * FFN_FT))
        act_ref[:, cols] = (_gelu_tanh(ya) * yg).astype(BF16)
        if f + 1 < FFN_NF:
            ua, ug = ua_next, ug_next
    yf = _dot(act_ref[...], wd_ref[...])
    n = ((yf * _rms_scale(yf)) * gpost_ref[...]).reshape(nb, tt, D_MODEL)
    o_ref[...] = x_ref[...] + gate_ref[...] * n


def _ffn(x, mod, g_pre, g_post, wu, wd, w_dw, b_dw, state, nb, tt):
    nbt, t, _ = x.shape
    m = nb * tt
    tok = pl.BlockSpec((nb, tt, D_MODEL), lambda b, i: (b, i, 0))
    mod_spec = lambda col: pl.BlockSpec((nb, 1, D_MODEL), lambda b, i, col=col: (b, 0, col))
    tail_spec = pl.BlockSpec((nb, CONV_W - 1, 2 * D_FF), lambda b, i: (b, 0, 0))
    in_specs = [tok, mod_spec(3), mod_spec(4), mod_spec(5), _full_spec((1, D_MODEL)),
                _full_spec((1, D_MODEL)), _full_spec(wu.shape),
                _full_spec(wd.shape), _full_spec(w_dw.shape), _full_spec(b_dw.shape)]
    args = [x, mod, mod, mod, g_pre, g_post, wu, wd, w_dw, b_dw]
    scratch = [pltpu.VMEM((m, D_MODEL), BF16), pltpu.VMEM((m, D_FF), BF16)]
    if state is not None:
        in_specs.append(tail_spec)
        args.append(state)
    else:
        scratch.append(pltpu.VMEM((nb, CONV_W - 1, 2 * D_FF), F32))
    return pl.pallas_call(
        functools.partial(_ffn_kernel, has_state=state is not None),
        grid=(nbt // nb, t // tt),
        in_specs=in_specs,
        out_specs=[tok, tail_spec],
        out_shape=[jax.ShapeDtypeStruct(x.shape, F32),
                   jax.ShapeDtypeStruct((nbt, CONV_W - 1, 2 * D_FF), F32)],
        scratch_shapes=scratch,
        compiler_params=_params(("parallel", "arbitrary")),
        name="ffn",
    )(*args)


def _bias_rows(rel_bias):
    assert BIAS_ROW >= ATTN_KB + ATTN_QB - 1 and WINDOW_A == ATTN_KB - ATTN_QB
    far_pos = jnp.broadcast_to(rel_bias[:, -1:], (H_A, BIAS_ROW))
    far_neg = jnp.broadcast_to(rel_bias[:, :1], (H_A, BIAS_ROW))
    n_mid = 2 * MAX_REL + 1
    n_lo = ATTN_KB + 1 - (WINDOW_A - MAX_REL) - n_mid
    rows = jnp.concatenate([far_pos[:, :WINDOW_A - MAX_REL], rel_bias[:, ::-1], far_neg[:, :n_lo],
                            far_pos[:, :BIAS_ROW - ATTN_KB - 1]], axis=1)
    return rows.astype(F32).reshape(H_A, 1, BIAS_ROW)


_LATE_WEIGHTS = ('w_br_a', 'w_br_b', 'w_out', 'w_up', 'w_down')


def _layer(x, mod, cache, s_gla, s_conv, w, first_chunk):
    nbt, t, _ = x.shape
    if first_chunk:
        nb, tt = 1, 1024
        nb_f, tt_f = 1, 1024
    else:
        nb, tt = 512 // t, t
        nb_f, tt_f = 1024 // t, t
    keep_rows = min(WINDOW_A, t) if first_chunk else 0
    cast_ws = () if _LATE_WEIGHTS[0] in w else tuple(w[k + '_f32'] for k in _LATE_WEIGHTS)
    qa, ka, va, qb, kb, vb, gb, gate_a, gate_b, log_a, *extra = _inproj(
        x, mod, w['g_pre_mix'], w['w_main'], w['w_gk1'], w['w_gk2'], w['b_gk'], nb, tt, keep_rows,
        cast_ws)
    kv_t = extra[:2] if keep_rows else []
    if cast_ws:
        w = {**w, **dict(zip(_LATE_WEIGHTS, extra[len(kv_t):]))}
    if first_chunk:
        ya = _attn_prompt(qa, ka, va, w['bias_rows'])
        k_keep, v_keep = (jnp.transpose(a.reshape(nbt, H_A, HD_A, keep_rows), (0, 3, 1, 2))
                          for a in kv_t)
        yb, s_new = _gla(qb, kb, vb, gb, log_a, w['g_gla'], None, nbt, CHUNK, 4)
    else:
        k_cache, v_cache = cache
        ya = _attn_sample(qa, ka, va, k_cache, v_cache, w['bias_rows'], 4)
        k_keep, v_keep = (a.astype(F32).reshape(nbt, t, H_A, HD_A) for a in (ka, va))
        yb, s_new = _gla(qb, kb, vb, gb, log_a, w['g_gla'], s_gla, 8, t, 1)
    mix = _mixout_piped if first_chunk else _mixout
    x1 = mix(x, ya, yb, gate_a, gate_b, mod, w['g_post_mix'], w['w_br_a'], w['w_br_b'],
             w['w_out'], nb, tt)
    y, tail = _ffn(x1, mod, w['g_pre_ffn'], w['g_post_ffn'], w['w_up'], w['w_down'],
                   w['w_dw'], w['b_dw'], None if first_chunk else s_conv, nb_f, tt_f)
    return (y, k_keep, v_keep, s_new.reshape(nbt, H_B, DK_B, DV_B), tail), w


def _prep_weights(w_main, w_gk1, w_gk2, b_gk, rel_bias, g_gla, w_br_a, w_br_b, w_out, w_up, w_dw,
                  b_dw, w_down, g_pre_mix, g_post_mix, g_pre_ffn, g_post_ffn):
    w_gk2p = jnp.pad(w_gk2, ((0, LANES - GK_RANK), (0, 0))).astype(BF16)
    row = lambda a: a.reshape(1, -1)
    return {
        'w_main': w_main, 'w_gk1': w_gk1, 'w_gk2': w_gk2p, 'b_gk': row(b_gk),
        'bias_rows': _bias_rows(rel_bias), 'g_gla': row(g_gla),
        'w_br_a_f32': w_br_a, 'w_br_b_f32': w_br_b, 'w_out_f32': w_out, 'w_up_f32': w_up,
        'w_down_f32': w_down, 'w_dw': w_dw, 'b_dw': row(b_dw),
        'g_pre_mix': row(g_pre_mix), 'g_post_mix': row(g_post_mix),
        'g_pre_ffn': row(g_pre_ffn), 'g_post_ffn': row(g_post_ffn),
    }


def kernel(x_prompt, x_sample, cache_k_a, cache_v_a, state_gla, state_conv, c_prompt, c_sample, w_ada, b_ada, g_pre_mix, g_post_mix, g_pre_ffn, g_post_ffn, w_in, w_gk2, b_gk, rel_bias, g_gla, w_br_a, w_br_b, w_out, w_up, w_dw, b_dw, w_down):
    depth = w_ada.shape[0]
    assert depth == 1
    bp, bs = x_prompt.shape[0], x_sample.shape[0]
    cache_rows = cache_k_a.shape[2]
    yp, ys = x_prompt, x_sample
    outs = [[] for _ in range(8)]
    for l in range(depth):
        c_all = jnp.concatenate([c_prompt, c_sample], axis=0)
        pad = (-c_all.shape[0]) % SUBLANES
        mod, w_main, w_gk1 = _prep(jnp.pad(c_all, ((0, pad), (0, 0))), w_ada[l], b_ada[l], w_in[l].T)
        w = _prep_weights(w_main, w_gk1, w_gk2[l], b_gk[l], rel_bias[l], g_gla[l], w_br_a[l],
                          w_br_b[l], w_out[l], w_up[l], w_dw[l], b_dw[l], w_down[l], g_pre_mix[l],
                          g_post_mix[l], g_pre_ffn[l], g_post_ffn[l])
        mod_p = mod[:bp].reshape(bp, 1, 6 * D_MODEL)
        mod_s = mod[bp:bp + bs].reshape(bs, 1, 6 * D_MODEL)
        (yp, kp, vp, gp, cp), w = _layer(yp, mod_p, None, None, None, w, True)
        to_t = lambda c: jnp.transpose(c, (0, 2, 3, 1)).reshape(bs, W_A, cache_rows)
        cache = (to_t(cache_k_a[l]), to_t(cache_v_a[l]))
        s0 = state_gla[l].reshape(bs, H_B // 2, 2 * DK_B, DV_B)
        (ys, kn, vn, gn, cn), _ = _layer(ys, mod_s, cache, s0, state_conv[l], w, False)
        for lst, a in zip(outs, (kp, vp, gp, cp, kn, vn, gn, cn)):
            lst.append(a)
    return (yp, ys) + tuple(jnp.stack(lst) for lst in outs)
```
